```python
import math
import jax, jax.numpy as jnp
from jax import lax
import numpy as np

D_MODEL = 1024
BATCH = 4
SEQ = 8192
DEPTH = 1
DEC_BATCH = 16
DEC_SEQ = 16
PAST_LEN = 2048

CHUNK = 64
D_MIX = D_MODEL
D_ATTN = D_MIX // 2
D_SSM = D_MIX - D_ATTN
HEAD_DIM = 64
N_HEADS = D_ATTN // HEAD_DIM
N_KV_HEADS = 2
Q_PER_KV = N_HEADS // N_KV_HEADS
KV_W = N_KV_HEADS * HEAD_DIM
WINDOW = 128
N_WIN_CHUNKS = WINDOW // CHUNK
ROPE_THETA = 10000.0
SSM_GROUP = 16
N_SSM_GROUPS = D_SSM // SSM_GROUP
SSM_STATE = 64
D_FF = ((8 * D_MODEL // 3 + 127) // 128) * 128
CONV_W = 3
EPS = 1e-6
D_IN = D_ATTN + 2 * KV_W + D_SSM

kernel_name = "hymba_swa_s5_convffn_stream_step"


def rmsnorm(x, g):
    xf = x.astype(jnp.float32)
    y = xf * lax.rsqrt(jnp.mean(xf * xf, axis=-1, keepdims=True) + EPS) * g.astype(jnp.float32)
    return y.astype(x.dtype)


def rope(x, pos):
    half = HEAD_DIM // 2
    inv = ROPE_THETA ** (-jnp.arange(half, dtype=jnp.float32) / half)
    ang = pos.astype(jnp.float32)[:, None] * inv[None, :]
    cos = jnp.cos(ang)[:, None, :]
    sin = jnp.sin(ang)[:, None, :]
    xf = x.astype(jnp.float32)
    x1, x2 = xf[..., :half], xf[..., half:]
    return jnp.concatenate([x1 * cos - x2 * sin, x2 * cos + x1 * sin], axis=-1).astype(x.dtype)


def sink_softmax(s, mask, sink):
    s = jnp.where(mask, s, -jnp.inf)
    m = jnp.maximum(jnp.max(s, axis=-1, keepdims=True), sink)
    e = jnp.exp(s - m)
    return e / (jnp.sum(e, axis=-1, keepdims=True) + jnp.exp(sink - m))


def window_attention_prompt(q, k, v, sinks):
    B, L = q.shape[:2]
    nb = L // CHUNK
    qb = q.reshape(B, nb, CHUNK, N_KV_HEADS, Q_PER_KV, HEAD_DIM)
    pad = N_WIN_CHUNKS * CHUNK

    def band(t):
        tp = jnp.pad(t, ((0, 0), (pad, 0), (0, 0), (0, 0)))
        tp = tp.reshape(B, nb + N_WIN_CHUNKS, CHUNK, N_KV_HEADS, HEAD_DIM)
        return jnp.concatenate([tp[:, j:j + nb] for j in range(N_WIN_CHUNKS + 1)], axis=2)

    kb, vb = band(k), band(v)
    kpos = (jnp.arange(nb)[:, None] - N_WIN_CHUNKS) * CHUNK + jnp.arange((N_WIN_CHUNKS + 1) * CHUNK)[None, :]
    mask = (kpos >= 0)[:, None, None, None, :]
    s = jnp.einsum('bnqgrd,bnkgd->bngrqk', qb, kb).astype(jnp.float32) * (HEAD_DIM ** -0.5)
    sink = sinks.astype(jnp.float32).reshape(N_KV_HEADS, Q_PER_KV)[:, :, None, None]
    p = sink_softmax(s, mask, sink)
    o = jnp.einsum('bngrqk,bnkgd->bnqgrd', p.astype(v.dtype), vb)
    return o.reshape(B, L, D_ATTN)


def window_attention_step(q, k_all, v_all, qpos, kpos, sinks):
    B, S = q.shape[:2]
    qg = q.reshape(B, S, N_KV_HEADS, Q_PER_KV, HEAD_DIM)
    s = jnp.einsum('bqgrd,bkgd->bgrqk', qg, k_all).astype(jnp.float32) * (HEAD_DIM ** -0.5)
    qc = qpos // CHUNK
    kc = kpos // CHUNK
    mask = (kc[None, :] <= qc[:, None]) & (kc[None, :] >= qc[:, None] - N_WIN_CHUNKS) & (kpos[None, :] >= 0)
    sink = sinks.astype(jnp.float32).reshape(N_KV_HEADS, Q_PER_KV)[:, :, None, None]
    p = sink_softmax(s, mask, sink)
    o = jnp.einsum('bgrqk,bkgd->bqgrd', p.astype(v_all.dtype), v_all)
    return o.reshape(B, S, D_ATTN)


def s5_glu(u, h0, A_re, A_im, log_dt, B_re, B_im, C_re, C_im, Dskip, w_glu):
    Bsz, L = u.shape[:2]
    f32 = jnp.float32
    uf = u.astype(f32).reshape(Bsz, L, N_SSM_GROUPS, SSM_GROUP)
    A = lax.complex(A_re.astype(f32), A_im.astype(f32))
    dtA = jnp.exp(log_dt.astype(f32))[:, None] * A
    A_bar = jnp.exp(dtA)
    B_bar = ((A_bar - 1.0) / A)[:, :, None] * lax.complex(B_re.astype(f32), B_im.astype(f32))
    bu = jnp.einsum('blgc,gpc->blgp', uf.astype(jnp.complex64), B_bar)
    if h0 is not None:
        h_prev = lax.complex(h0[0].astype(f32), h0[1].astype(f32))
        bu = bu.at[:, 0].add(A_bar[None] * h_prev)
    a = jnp.broadcast_to(A_bar, (1, L, N_SSM_GROUPS, SSM_STATE))

    def combine(e1, e2):
        a1, b1 = e1
        a2, b2 = e2
        return a1 * a2, a2 * b1 + b2

    _, h = lax.associative_scan(combine, (a, bu), axis=1)
    C = lax.complex(C_re.astype(f32), C_im.astype(f32))
    y = jnp.einsum('gcp,blgp->blgc', C, h).real + Dskip.astype(f32)[None, None] * uf
    z = jax.nn.gelu(y.reshape(Bsz, L, D_SSM))
    out = z * jax.nn.sigmoid(z @ w_glu.astype(f32))
    h_last = h[:, -1]
    return out.astype(u.dtype), h_last.real, h_last.imag


def conv_ffn(x, prev, w_up, conv_w, conv_b, w_down):
    B, L = x.shape[:2]
    up = x @ w_up
    if prev is None:
        prev = jnp.zeros((B, CONV_W - 1, up.shape[-1]), up.dtype)
    up_p = jnp.concatenate([prev.astype(up.dtype), up], axis=1)
    c = sum(up_p[:, j:j + L] * conv_w[j] for j in range(CONV_W)) + conv_b
    gate, val = jnp.split(c, 2, axis=-1)
    return (jax.nn.silu(gate) * val) @ w_down, up_p[:, -(CONV_W - 1):]


def layer(x, start, lw, kv_prev, ssm_prev, conv_prev):
    B, L = x.shape[:2]
    pos = start + jnp.arange(L)
    h = rmsnorm(x, lw['norm1'])
    proj = h @ lw['w_in']
    q, k, v, u = jnp.split(proj, [D_ATTN, D_ATTN + KV_W, D_ATTN + 2 * KV_W], axis=-1)
    q = rope(q.reshape(B, L, N_HEADS, HEAD_DIM), pos)
    k = rope(k.reshape(B, L, N_KV_HEADS, HEAD_DIM), pos)
    v = v.reshape(B, L, N_KV_HEADS, HEAD_DIM)
    if kv_prev is None:
        a = window_attention_prompt(q, k, v, lw['sinks'])
        new_k, new_v = k[:, -WINDOW:], v[:, -WINDOW:]
    else:
        ck, cv = kv_prev
        n_buf = ck.shape[1]
        k_all = jnp.concatenate([ck.astype(k.dtype), k], axis=1)
        v_all = jnp.concatenate([cv.astype(v.dtype), v], axis=1)
        kpos = start - n_buf + jnp.arange(n_buf + L)
        a = window_attention_step(q, k_all, v_all, pos, kpos, lw['sinks'])
        new_k, new_v = k_all[:, -n_buf:], v_all[:, -n_buf:]
    s, h_re, h_im = s5_glu(u, ssm_prev, lw['A_re'], lw['A_im'], lw['log_dt'], lw['B_re'], lw['B_im'],
                           lw['C_re'], lw['C_im'], lw['D'], lw['w_glu'])
    merged = jnp.concatenate([rmsnorm(a, lw['onorm_a']), rmsnorm(s, lw['onorm_s'])], axis=-1)
    x = x + merged @ lw['w_out']
    f, new_conv = conv_ffn(rmsnorm(x, lw['norm2']), conv_prev, lw['w_up'], lw['conv_w'], lw['conv_b'], lw['w_down'])
    x = x + f
    return x, new_k, new_v, h_re, h_im, new_conv


def setup_inputs(seed: int = 0) -> dict:
    key = jax.random.key(seed)
    ks = jax.random.split(key, 32)
    f32 = jnp.float32
    n = lambda k, shape, scale: jax.random.normal(k, shape, f32) * scale
    kv_rows = min(WINDOW, PAST_LEN)
    log_dt = jax.random.uniform(ks[10], (DEPTH, N_SSM_GROUPS), f32, math.log(1e-3), math.log(1e-1))
    a_im = jnp.broadcast_to(math.pi * jnp.arange(SSM_STATE, dtype=f32), (DEPTH, N_SSM_GROUPS, SSM_STATE))
    return {
        "x_prompt": n(ks[0], (BATCH, SEQ, D_MODEL), 1.0),
        "x_sample": n(ks[1], (DEC_BATCH, DEC_SEQ, D_MODEL), 1.0),
        "cache_k": n(ks[2], (DEPTH, DEC_BATCH, kv_rows, N_KV_HEADS, HEAD_DIM), 1.0),
        "cache_v": n(ks[3], (DEPTH, DEC_BATCH, kv_rows, N_KV_HEADS, HEAD_DIM), 1.0),
        "state_ssm_re": n(ks[4], (DEPTH, DEC_BATCH, N_SSM_GROUPS, SSM_STATE), 0.5),
        "state_ssm_im": n(ks[5], (DEPTH, DEC_BATCH, N_SSM_GROUPS, SSM_STATE), 0.5),
        "state_conv": n(ks[6], (DEPTH, DEC_BATCH, CONV_W - 1, 2 * D_FF), 1.0),
        "norm1_g": 1.0 + n(ks[7], (DEPTH, D_MODEL), 0.02),
        "w_in": n(ks[8], (DEPTH, D_MODEL, D_IN), D_MODEL ** -0.5),
        "attn_sinks": n(ks[9], (DEPTH, N_HEADS), 0.5),
        "ssm_A_re": -0.5 + n(ks[11], (DEPTH, N_SSM_GROUPS, SSM_STATE), 0.01),
        "ssm_A_im": a_im + n(ks[12], (DEPTH, N_SSM_GROUPS, SSM_STATE), 0.01),
        "ssm_log_dt": log_dt,
        "ssm_B_re": n(ks[13], (DEPTH, N_SSM_GROUPS, SSM_STATE, SSM_GROUP), (2 * SSM_GROUP) ** -0.5),
        "ssm_B_im": n(ks[14], (DEPTH, N_SSM_GROUPS, SSM_STATE, SSM_GROUP), (2 * SSM_GROUP) ** -0.5),
        "ssm_C_re": n(ks[15], (DEPTH, N_SSM_GROUPS, SSM_GROUP, SSM_STATE), (2 * SSM_STATE) ** -0.5),
        "ssm_C_im": n(ks[16], (DEPTH, N_SSM_GROUPS, SSM_GROUP, SSM_STATE), (2 * SSM_STATE) ** -0.5),
        "ssm_D": n(ks[17], (DEPTH, N_SSM_GROUPS, SSM_GROUP), 1.0),
        "w_glu": n(ks[18], (DEPTH, D_SSM, D_SSM), D_SSM ** -0.5),
        "onorm_attn_g": 1.0 + n(ks[19], (DEPTH, D_ATTN), 0.02),
        "onorm_ssm_g": 1.0 + n(ks[20], (DEPTH, D_SSM), 0.02),
        "w_out": n(ks[21], (DEPTH, D_MIX, D_MODEL), D_MIX ** -0.5),
        "norm2_g": 1.0 + n(ks[22], (DEPTH, D_MODEL), 0.02),
        "w_up": n(ks[23], (DEPTH, D_MODEL, 2 * D_FF), D_MODEL ** -0.5),
        "conv_w": n(ks[24], (DEPTH, CONV_W, 2 * D_FF), CONV_W ** -0.5),
        "conv_b": n(ks[25], (DEPTH, 2 * D_FF), 0.01),
        "w_down": n(ks[26], (DEPTH, D_FF, D_MODEL), D_FF ** -0.5),
        "final_g": 1.0 + n(ks[27], (D_MODEL,), 0.02),
    }


def reference(x_prompt, x_sample, cache_k, cache_v, state_ssm_re, state_ssm_im, state_conv,
              norm1_g, w_in, attn_sinks, ssm_A_re, ssm_A_im, ssm_log_dt, ssm_B_re, ssm_B_im,
              ssm_C_re, ssm_C_im, ssm_D, w_glu, onorm_attn_g, onorm_ssm_g, w_out, norm2_g,
              w_up, conv_w, conv_b, w_down, final_g):
    xp, xs = x_prompt, x_sample
    kp_l, vp_l, rp_l, ip_l, cp_l = [], [], [], [], []
    ks_l, vs_l, rs_l, is_l, cs_l = [], [], [], [], []
    for l in range(DEPTH):
        lw = dict(norm1=norm1_g[l], w_in=w_in[l], sinks=attn_sinks[l], A_re=ssm_A_re[l], A_im=ssm_A_im[l],
                  log_dt=ssm_log_dt[l], B_re=ssm_B_re[l], B_im=ssm_B_im[l], C_re=ssm_C_re[l],
                  C_im=ssm_C_im[l], D=ssm_D[l], w_glu=w_glu[l], onorm_a=onorm_attn_g[l],
                  onorm_s=onorm_ssm_g[l], w_out=w_out[l], norm2=norm2_g[l], w_up=w_up[l],
                  conv_w=conv_w[l], conv_b=conv_b[l], w_down=w_down[l])
        xp, kp, vp, rp, ip, cp = layer(xp, 0, lw, None, None, None)
        xs, ks_, vs_, rs_, is_, cs_ = layer(xs, PAST_LEN, lw, (cache_k[l], cache_v[l]),
                                            (state_ssm_re[l], state_ssm_im[l]), state_conv[l])
        kp_l.append(kp); vp_l.append(vp); rp_l.append(rp); ip_l.append(ip); cp_l.append(cp)
        ks_l.append(ks_); vs_l.append(vs_); rs_l.append(rs_); is_l.append(is_); cs_l.append(cs_)
    y_prompt = rmsnorm(xp, final_g)
    y_sample = rmsnorm(xs, final_g)
    return (y_prompt, y_sample,
            jnp.stack(kp_l), jnp.stack(vp_l), jnp.stack(rp_l), jnp.stack(ip_l), jnp.stack(cp_l),
            jnp.stack(ks_l), jnp.stack(vs_l), jnp.stack(rs_l), jnp.stack(is_l), jnp.stack(cs_l))
```

```python
import functools
import math

import jax
import jax.numpy as jnp
from jax import lax
from jax.experimental import pallas as pl
from jax.experimental.pallas import tpu as pltpu

F32 = jnp.float32
BF16 = jnp.bfloat16

D_MODEL = 1024
CHUNK = 64
D_ATTN = 512
D_SSM = 512
HEAD_DIM = 64
N_HEADS = 8
N_KV_HEADS = 2
KV_W = N_KV_HEADS * HEAD_DIM
WINDOW = 128
N_WIN_CHUNKS = WINDOW // CHUNK
ROPE_THETA = 10000.0
SSM_GROUP = 16
N_GROUPS = D_SSM // SSM_GROUP
SSM_STATE = 64
D_FF = 2816
CONV_W = 3
EPS = 1e-6
D_IN = D_ATTN + 2 * KV_W + D_SSM
PAST_LEN = 2048

SSD_T = 16
LANES = 128
SUBLANES = 8
GROUPS_PER_VREG = LANES // SSM_GROUP
SSD_W = SSD_T * SSM_GROUP
STATE_W = 2 * SSM_STATE

TB_IN = 512
TB_OUT = 512
NC_SSD = 64
FF_TILE = 256
VMEM_LIMIT = 56 * 1024 * 1024


def _log2(n):
    assert n > 0 and n & (n - 1) == 0, n
    return n.bit_length() - 1


def _pdiv(x, n):
    return lax.shift_right_arithmetic(x, jnp.int32(_log2(n)))


def _pmod(x, n):
    return x & (n - 1)


def _rms(x, g):
    return x * lax.rsqrt(jnp.mean(x * x, axis=-1, keepdims=True) + EPS) * g


def _rope(xb, cos, sin_signed, first_half):
    partner = jnp.where(first_half, pltpu.roll(xb, LANES - HEAD_DIM // 2, 1), pltpu.roll(xb, HEAD_DIM // 2, 1))
    return xb * cos + partner * sin_signed


def _group_transpose8(xs, lane_group):
    xs = list(xs)
    for d in (4, 2, 1):
        keep = (lane_group & d) == 0
        nxt = list(xs)
        for i in range(GROUPS_PER_VREG):
            if i & d:
                continue
            a, b = xs[i], xs[i | d]
            nxt[i] = jnp.where(keep, a, pltpu.roll(b, SSM_GROUP * d, 1))
            nxt[i | d] = jnp.where(keep, pltpu.roll(a, LANES - SSM_GROUP * d, 1), b)
        xs = nxt
    return xs


def _row_sel(s, nk, strided):
    if strided:
        return pl.ds(s, nk, stride=SSD_T)
    return pl.ds(s * nk, nk)


def _to_groups(u_ref, put, nk, strided):
    lane_group = _pdiv(lax.broadcasted_iota(jnp.int32, (nk, LANES), 1), SSM_GROUP)
    for j in range(D_SSM // LANES):
        for half in range(SSD_T // GROUPS_PER_VREG):
            xs = [u_ref[j, _row_sel(GROUPS_PER_VREG * half + sl, nk, strided), :]
                  for sl in range(GROUPS_PER_VREG)]
            ws = _group_transpose8(xs, lane_group)
            for gq in range(GROUPS_PER_VREG):
                put(GROUPS_PER_VREG * j + gq, half, ws[gq])


def _from_groups(get, y_ref, nk, strided):
    lane_group = _pdiv(lax.broadcasted_iota(jnp.int32, (nk, LANES), 1), SSM_GROUP)
    for j in range(D_SSM // LANES):
        for half in range(SSD_T // GROUPS_PER_VREG):
            ws = [get(GROUPS_PER_VREG * j + gq, half) for gq in range(GROUPS_PER_VREG)]
            xs = _group_transpose8(ws, lane_group)
            for sl in range(GROUPS_PER_VREG):
                y_ref[j, _row_sel(GROUPS_PER_VREG * half + sl, nk, strided), :] = xs[sl]


def _cmul(re_full, im_signed, z):
    return re_full * z + im_signed * pltpu.roll(z, SSM_STATE, z.ndim - 1)


def _in_proj(x, g1_ref, win_ref, cos_ref, sin_ref):
    rows = x.shape[0]
    hn = _rms(x, g1_ref[...]).astype(BF16)
    proj = jnp.dot(hn, win_ref[...], preferred_element_type=F32)
    cos = cos_ref[...]
    sin = sin_ref[...]
    lane = lax.broadcasted_iota(jnp.int32, (rows, LANES), 1)
    first_half = _pmod(lane, HEAD_DIM) < (HEAD_DIM // 2)
    scale = HEAD_DIM ** -0.5
    qs = [_rope(proj[:, LANES * j:LANES * (j + 1)], cos, sin, first_half) * scale for j in range(D_ATTN // LANES)]
    k = _rope(proj[:, D_ATTN:D_ATTN + KV_W], cos, sin, first_half)
    v = proj[:, D_ATTN + KV_W:D_ATTN + 2 * KV_W]
    u = proj[:, D_ATTN + 2 * KV_W:]
    return qs, k, v, u


def _kv_variants(k, v):
    lane = lax.broadcasted_iota(jnp.int32, k.shape, 1)
    lo = lane < HEAD_DIM
    out = []
    for t in (k, v):
        tr = pltpu.roll(t, HEAD_DIM, 1)
        zero = jnp.zeros_like(t)
        out += [jnp.where(lo, t, zero), jnp.where(lo, zero, tr), jnp.where(lo, tr, zero), jnp.where(lo, zero, t)]
    return [o.astype(BF16) for o in out]


def _attend(a_bf, kvar, vvar, valid, sink_col):
    s = lax.dot_general(a_bf, kvar, (((1,), (1,)), ((), ())), preferred_element_type=F32)
    for cond in valid:
        s = jnp.where(cond, s, -jnp.inf)
    m = jnp.maximum(jnp.max(s, axis=-1, keepdims=True), sink_col)
    e = jnp.exp(s - m)
    den = jnp.sum(e, axis=-1, keepdims=True) + jnp.exp(sink_col - m)
    o = jnp.dot(e.astype(BF16), vvar, preferred_element_type=F32)
    return o * (1.0 / den)


def _prompt_in_kernel(sinks_ref, x_ref, g1_ref, win_ref, cos_ref, sin_ref,
                      a_ref, ug_ref, kp_ref, vp_ref, q_s, kv_s, u_s):
    tb = x_ref.shape[1]
    nk = tb // SSD_T
    blk = pl.program_id(1)
    qs, k, v, u = _in_proj(x_ref[0], g1_ref, win_ref, cos_ref, sin_ref)
    for j, qb in enumerate(qs):
        q_s[:, LANES * j:LANES * (j + 1)] = qb
    for j in range(D_SSM // LANES):
        u_s[j] = u[:, LANES * j:LANES * (j + 1)]

    @pl.when(blk == 0)
    def _():
        kv_s[:, 0:WINDOW, :] = jnp.zeros((8, WINDOW, LANES), BF16)

    @pl.when(blk > 0)
    def _():
        kv_s[:, 0:WINDOW, :] = kv_s[:, tb:tb + WINDOW, :]

    for idx, arr in enumerate(_kv_variants(k, v)):
        kv_s[idx, WINDOW:WINDOW + tb, :] = arr

    @pl.when(blk == pl.num_programs(1) - 1)
    def _():
        kp_ref[0] = k[tb - WINDOW:, :]
        vp_ref[0] = v[tb - WINDOW:, :]

    nkeys = WINDOW + CHUNK
    col = lax.broadcasted_iota(jnp.int32, (2 * CHUNK, nkeys), 1)
    row = lax.broadcasted_iota(jnp.int32, (2 * CHUNK, 1), 0)

    def chunk_body(i, carry):
        r0 = pl.multiple_of(i * CHUNK, CHUNK)
        valid = [col >= (WINDOW - r0 - blk * tb)]
        for g in range(N_KV_HEADS):
            qa = q_s[pl.ds(r0, CHUNK), 2 * LANES * g:2 * LANES * g + LANES]
            qb = q_s[pl.ds(r0, CHUNK), 2 * LANES * g + LANES:2 * LANES * (g + 1)]
            a_bf = jnp.concatenate([qa, qb], axis=0).astype(BF16)
            acc = jnp.zeros((2 * CHUNK, LANES), F32)
            for par in range(2):
                h0 = 4 * g + par
                sink_col = jnp.where(row < CHUNK, sinks_ref[h0], sinks_ref[h0 + 2])
                kvar = kv_s[2 * g + par, pl.ds(r0, nkeys), :]
                vvar = kv_s[4 + 2 * g + par, pl.ds(r0, nkeys), :]
                acc = acc + _attend(a_bf, kvar, vvar, valid, sink_col)
            a_ref[0, pl.ds(r0, CHUNK), 2 * LANES * g:2 * LANES * g + LANES] = acc[:CHUNK]
            a_ref[0, pl.ds(r0, CHUNK), 2 * LANES * g + LANES:2 * LANES * (g + 1)] = acc[CHUNK:]
        return carry

    lax.fori_loop(0, tb // CHUNK, chunk_body, 0)

    def put(g, half, val):
        ug_ref[g, 0, :, LANES * half:LANES * (half + 1)] = val.astype(BF16)

    _to_groups(u_s, put, nk, strided=True)


def _prompt_ssd_kernel(ug_ref, m_ref, wd_ref, r_ref, tab_ref, abar_ref,
                       yg_ref, sfin_ref, d_s, sp_s, carry_s):
    nb = ug_ref.shape[1]
    nc = ug_ref.shape[2]
    step = pl.program_id(0)

    @pl.when(step == 0)
    def _():
        carry_s[...] = jnp.zeros(carry_s.shape, F32)

    def state_in(g, c):
        ub = ug_ref[g].reshape(nb * nc, SSD_W)
        d_s[g] = jnp.dot(ub, wd_ref[g], preferred_element_type=F32)
        return c

    lax.fori_loop(0, N_GROUPS, state_in, 0)

    pw = [tab_ref[i] for i in range(8)]
    are, aim = abar_ref[0][:, None, :], abar_ref[1][:, None, :]
    kio = lax.broadcasted_iota(jnp.int32, (N_GROUPS, nc, STATE_W), 1)
    for b in range(nb):
        dg = d_s[:, b * nc:(b + 1) * nc, :]
        s_in = carry_s[b]
        dsh = jnp.where(kio == 0, s_in, pltpu.roll(dg, 1, 1))
        c = jnp.zeros((N_GROUPS, 1, STATE_W), F32)
        for j in range(nc // SUBLANES):
            x = dsh[:, SUBLANES * j:SUBLANES * (j + 1), :]
            for lvl in range(3):
                xs = pltpu.roll(x, 1 << lvl, 1)
                x = x + _cmul(pw[2 * lvl], pw[2 * lvl + 1], xs)
            cb = jnp.broadcast_to(c, x.shape)
            h = x + _cmul(pw[6], pw[7], cb)
            sp_s[:, b * nc + SUBLANES * j:b * nc + SUBLANES * (j + 1), :] = h
            c = h[:, SUBLANES - 1:SUBLANES, :]
        carry_s[b] = _cmul(are, aim, c) + dg[:, nc - 1:nc, :]

    def readout(g, c):
        ub = ug_ref[g].reshape(nb * nc, SSD_W)
        y = jnp.dot(ub, m_ref[g], preferred_element_type=F32)
        y = y + jnp.dot(sp_s[g].astype(BF16), r_ref[g], preferred_element_type=F32)
        yg_ref[g] = y.reshape(nb, nc, SSD_W)
        return c

    lax.fori_loop(0, N_GROUPS, readout, 0)
    sfin_ref[...] = carry_s[...]


def _out_kernel(*refs, strided, cs, has_prev):
    if has_prev:
        x_ref, a_ref, yg_ref, prev_ref = refs[:4]
        refs = refs[4:]
    else:
        x_ref, a_ref, yg_ref = refs[:3]
        prev_ref = None
        refs = refs[3:]
    (wglu_ref, ga_ref, gs_ref, wout_ref, g2_ref, wup_ref, cw_ref, cb_ref, wdown_ref, gf_ref,
     y_ref, cst_ref, ys_s, carry_s, eg_s, ev_s) = refs
    tb = x_ref.shape[1]
    nk = tb // SSD_T
    pad = carry_s.shape[0]
    blk = pl.program_id(1)

    @pl.when(blk == 0)
    def _():
        if has_prev:
            carry_s[...] = prev_ref[0]
        else:
            carry_s[...] = jnp.zeros(carry_s.shape, F32)

    _from_groups(lambda g, half: yg_ref[g, 0, :, LANES * half:LANES * (half + 1)], ys_s, nk, strided)
    yv = jnp.concatenate([ys_s[j] for j in range(D_SSM // LANES)], axis=-1)
    z = 0.5 * yv * (1.0 + jnp.tanh(math.sqrt(2.0 / math.pi) * (yv + 0.044715 * (yv * yv * yv))))
    gate = jnp.dot(z.astype(BF16), wglu_ref[...], preferred_element_type=F32)
    s_out = z * jax.nn.sigmoid(gate)
    na = _rms(a_ref[0], ga_ref[...]).astype(BF16)
    ns = _rms(s_out, gs_ref[...]).astype(BF16)
    x1 = x_ref[0] + jnp.dot(na, wout_ref[0:D_ATTN, :], preferred_element_type=F32)
    x1 = x1 + jnp.dot(ns, wout_ref[D_ATTN:, :], preferred_element_type=F32)
    h2 = _rms(x1, g2_ref[...]).astype(BF16)

    acc = jnp.zeros((tb, D_MODEL), F32)
    for j in range(D_FF // FF_TILE):
        parts = []
        for half, ext in ((0, eg_s), (1, ev_s)):
            c0 = half * D_FF + j * FF_TILE
            up = jnp.dot(h2, wup_ref[:, c0:c0 + FF_TILE], preferred_element_type=F32)
            ext[0:pad, :] = carry_s[:, c0:c0 + FF_TILE]
            ext[pad:pad + tb, :] = up
            carry_s[:, c0:c0 + FF_TILE] = ext[tb:tb + pad, :]
            conv = (ext[pad - 2 * cs:pad - 2 * cs + tb, :] * cw_ref[0:1, c0:c0 + FF_TILE]
                    + ext[pad - cs:pad - cs + tb, :] * cw_ref[1:2, c0:c0 + FF_TILE]
                    + up * cw_ref[2:3, c0:c0 + FF_TILE] + cb_ref[:, c0:c0 + FF_TILE])
            parts.append(conv)
        cg, cv = parts
        act = (cg * jax.nn.sigmoid(cg) * cv).astype(BF16)
        acc = acc + jnp.dot(act, wdown_ref[j * FF_TILE:(j + 1) * FF_TILE, :], preferred_element_type=F32)
    y_ref[0] = _rms(x1 + acc, gf_ref[...])
    cst_ref[0] = carry_s[...]


def _decode_in_kernel(sinks_ref, x_ref, g1_ref, win_ref, cos_ref, sin_ref, kc_ref, vc_ref, h0_ref,
                      m_ref, wd_ref, r_ref, abar_ref,
                      a_ref, kn_ref, vn_ref, yg_ref, sn_ref, u_s, ug_s, *, n_streams, n_steps, n_cache):
    rows = n_streams * n_steps
    qs, k, v, u = _in_proj(x_ref[...], g1_ref, win_ref, cos_ref, sin_ref)
    kn_ref[...] = k
    vn_ref[...] = v
    for j in range(D_SSM // LANES):
        u_s[j] = u[:, LANES * j:LANES * (j + 1)]
    ncache_rows = n_streams * n_cache
    nkeys = ncache_rows + rows
    kall = jnp.concatenate([kc_ref[...], k], axis=0)
    vall = jnp.concatenate([vc_ref[...], v], axis=0)
    variants = _kv_variants(kall, vall)

    colv = lax.broadcasted_iota(jnp.int32, (1, nkeys), 1)
    is_new = colv >= ncache_rows
    cnew = colv - ncache_rows
    k_stream = jnp.where(is_new, _pmod(cnew, n_streams), _pdiv(colv, n_cache))
    k_pos = jnp.where(is_new, PAST_LEN + _pdiv(cnew, n_streams), PAST_LEN - n_cache + _pmod(colv, n_cache))
    k_chunk = _pdiv(k_pos, CHUNK)
    rowv = _pmod(lax.broadcasted_iota(jnp.int32, (2 * rows, 1), 0), rows)
    q_stream = _pmod(rowv, n_streams)
    q_chunk = _pdiv(PAST_LEN + _pdiv(rowv, n_streams), CHUNK)
    ok_col = jnp.where(k_pos >= 0, k_stream, -1)
    d_chunk = q_chunk - k_chunk
    valid = [q_stream == ok_col,
             lax.bitcast_convert_type(d_chunk, jnp.uint32) <= jnp.uint32(N_WIN_CHUNKS)]
    top = lax.broadcasted_iota(jnp.int32, (2 * rows, 1), 0) < rows

    for g in range(N_KV_HEADS):
        a_bf = jnp.concatenate([qs[2 * g], qs[2 * g + 1]], axis=0).astype(BF16)
        acc = jnp.zeros((2 * rows, LANES), F32)
        for par in range(2):
            h0 = 4 * g + par
            sink_col = jnp.where(top, sinks_ref[h0], sinks_ref[h0 + 2])
            acc = acc + _attend(a_bf, variants[2 * g + par], variants[4 + 2 * g + par], valid, sink_col)
        a_ref[:, 2 * LANES * g:2 * LANES * g + LANES] = acc[:rows]
        a_ref[:, 2 * LANES * g + LANES:2 * LANES * (g + 1)] = acc[rows:]

    def put(g, half, val):
        ug_s[g, :, LANES * half:LANES * (half + 1)] = val

    _to_groups(u_s, put, n_streams, strided=False)
    def group_body(g, c):
        ub = ug_s[g].astype(BF16)
        h0g = h0_ref[g]
        d = jnp.dot(ub, wd_ref[g], preferred_element_type=F32)
        sn_ref[g] = _cmul(abar_ref[0, pl.ds(g, 1), :], abar_ref[1, pl.ds(g, 1), :], h0g) + d
        y = jnp.dot(ub, m_ref[g], preferred_element_type=F32)
        yg_ref[g] = y + jnp.dot(h0g.astype(BF16), r_ref[g], preferred_element_type=F32)
        return c

    lax.fori_loop(0, N_GROUPS, group_body, 0)


def _ssd_tables(a_re, a_im, log_dt, b_re, b_im, c_re, c_im, d_skip):
    hp = lax.Precision.HIGHEST
    t = SSD_T
    dt = jnp.exp(log_dt)[:, None]
    lr, li = dt * a_re, dt * a_im
    n = jnp.arange(0, SUBLANES * t + 1, dtype=F32)[:, None, None]
    mag = jnp.exp(n * lr)
    pr, pi = mag * jnp.cos(n * li), mag * jnp.sin(n * li)
    den = a_re * a_re + a_im * a_im
    nr, ni = pr[1] - 1.0, pi[1]
    fr, fi = (nr * a_re + ni * a_im) / den, (ni * a_re - nr * a_im) / den
    bbr = fr[:, :, None] * b_re - fi[:, :, None] * b_im
    bbi = fr[:, :, None] * b_im + fi[:, :, None] * b_re
    er = c_re[None] * pr[:t + 1, :, None, :] - c_im[None] * pi[:t + 1, :, None, :]
    ei = c_re[None] * pi[:t + 1, :, None, :] + c_im[None] * pr[:t + 1, :, None, :]
    kt = (jnp.einsum('ngcp,gpd->ngcd', er[:t], bbr, precision=hp)
          - jnp.einsum('ngcp,gpd->ngcd', ei[:t], bbi, precision=hp))
    kt = kt.at[0].add(d_skip[:, :, None] * jnp.eye(SSM_GROUP, dtype=F32)[None])
    s_idx = jnp.arange(t)[:, None]
    t_idx = jnp.arange(t)[None, :]
    lag = t_idx - s_idx
    m = jnp.where((lag >= 0)[:, :, None, None, None], kt[jnp.maximum(lag, 0)], 0.0)
    m_mat = m.transpose(2, 0, 4, 1, 3).reshape(N_GROUPS, SSD_W, SSD_W)
    prr, pir = pr[t - 1 - jnp.arange(t)], pi[t - 1 - jnp.arange(t)]
    wdr = prr[..., None] * bbr[None] - pir[..., None] * bbi[None]
    wdi = prr[..., None] * bbi[None] + pir[..., None] * bbr[None]
    to_rows = lambda w: w.transpose(1, 0, 3, 2).reshape(N_GROUPS, SSD_W, SSM_STATE)
    wd_mat = jnp.concatenate([to_rows(wdr), to_rows(wdi)], axis=-1)
    to_cols = lambda e: e[1:t + 1].transpose(1, 3, 0, 2).reshape(N_GROUPS, SSM_STATE, SSD_W)
    r_mat = jnp.concatenate([to_cols(er), -to_cols(ei)], axis=1)

    def full(idx):
        return (jnp.concatenate([pr[idx], pr[idx]], -1), jnp.concatenate([-pi[idx], pi[idx]], -1))

    sub = jnp.arange(SUBLANES)
    tabs = []
    for lvl in range(3):
        d = 1 << lvl
        fre, fim = full(t * d)
        keep = (sub >= d)[None, :, None]
        tabs += [jnp.where(keep, fre[:, None, :], 0.0), jnp.where(keep, fim[:, None, :], 0.0)]
    qre, qim = full(t * (sub + 1))
    tabs += [qre.transpose(1, 0, 2), qim.transpose(1, 0, 2)]
    tab = jnp.stack(tabs)
    abar = jnp.stack(full(t))
    return m_mat.astype(BF16), wd_mat.astype(BF16), r_mat.astype(BF16), tab, abar


def _rope_tables(pos):
    half = HEAD_DIM // 2
    inv = ROPE_THETA ** (-jnp.arange(half, dtype=F32) / half)
    ang = pos.astype(F32)[:, None] * inv[None, :]
    cos, sin = jnp.cos(ang), jnp.sin(ang)
    cos_t = jnp.concatenate([cos, cos, cos, cos], axis=-1)
    sin_t = jnp.concatenate([-sin, sin, -sin, sin], axis=-1)
    return cos_t, sin_t


def _const(shape):
    nd = len(shape)
    return pl.BlockSpec(shape, lambda *_: (0,) * nd, pipeline_mode=pl.Buffered(1))


def _whole(shape):
    nd = len(shape)
    return pl.BlockSpec(shape, lambda *_: (0,) * nd)


def _out_call(x, a, yg, prev, w, *, tb, strided, cs, pad):
    nb, length, _ = x.shape
    nk = tb // SSD_T
    grid = (nb, length // tb)
    in_specs = [
        pl.BlockSpec((1, tb, D_MODEL), lambda b, i: (b, i, 0)),
        pl.BlockSpec((1, tb, D_ATTN), lambda b, i: (b, i, 0)),
        pl.BlockSpec((N_GROUPS, 1, nk, SSD_W), lambda b, i: (0, b, i, 0)),
    ]
    args = [x, a, yg]
    if prev is not None:
        in_specs.append(pl.BlockSpec((1, pad, 2 * D_FF), lambda b, i: (b, 0, 0)))
        args.append(prev)
    weights = [w['w_glu'], w['onorm_a'], w['onorm_s'], w['w_out'], w['norm2'], w['w_up'], w['conv_w'],
               w['conv_b'], w['w_down'], w['final_g']]
    in_specs += [_const(t.shape) for t in weights]
    kern = functools.partial(_out_kernel, strided=strided, cs=cs, has_prev=prev is not None)
    return pl.pallas_call(
        kern,
        grid=grid,
        in_specs=in_specs,
        out_specs=[pl.BlockSpec((1, tb, D_MODEL), lambda b, i: (b, i, 0)),
                   pl.BlockSpec((1, pad, 2 * D_FF), lambda b, i: (b, 0, 0))],
        out_shape=[jax.ShapeDtypeStruct((nb, length, D_MODEL), F32),
                   jax.ShapeDtypeStruct((nb, pad, 2 * D_FF), F32)],
        scratch_shapes=[pltpu.VMEM((D_SSM // LANES, tb, LANES), F32), pltpu.VMEM((pad, 2 * D_FF), F32),
                        pltpu.VMEM((pad + tb, FF_TILE), F32), pltpu.VMEM((pad + tb, FF_TILE), F32)],
        compiler_params=pltpu.CompilerParams(dimension_semantics=("arbitrary", "arbitrary"),
                                             vmem_limit_bytes=VMEM_LIMIT),
        name="layer_out",
    )(*args, *weights)


def kernel(x_prompt, x_sample, cache_k, cache_v, state_ssm_re, state_ssm_im, state_conv, norm1_g, w_in, attn_sinks, ssm_A_re, ssm_A_im, ssm_log_dt, ssm_B_re, ssm_B_im, ssm_C_re, ssm_C_im, ssm_D, w_glu, onorm_attn_g, onorm_ssm_g, w_out, norm2_g, w_up, conv_w, conv_b, w_down, final_g):
    assert norm1_g.shape[0] == 1, "one layer"
    nb, length, _ = x_prompt.shape
    ns, nt, _ = x_sample.shape
    n_cache = cache_k.shape[2]
    assert nt == SSD_T and length % TB_IN == 0 and length % TB_OUT == 0 and (length // SSD_T) % NC_SSD == 0

    m_mat, wd_mat, r_mat, tab, abar = _ssd_tables(ssm_A_re[0], ssm_A_im[0], ssm_log_dt[0], ssm_B_re[0],
                                                  ssm_B_im[0], ssm_C_re[0], ssm_C_im[0], ssm_D[0])
    sinks = attn_sinks[0]
    g1 = norm1_g
    win = w_in[0].astype(BF16)
    w = dict(w_glu=w_glu[0].astype(BF16), onorm_a=onorm_attn_g, onorm_s=onorm_ssm_g, w_out=w_out[0].astype(BF16),
             norm2=norm2_g, w_up=w_up[0].astype(BF16), conv_w=conv_w[0], conv_b=conv_b,
             w_down=w_down[0].astype(BF16), final_g=final_g[None, :])
    smem = pl.BlockSpec(memory_space=pltpu.SMEM)

    cos_p, sin_p = _rope_tables(jnp.arange(length))
    nk = TB_IN // SSD_T
    a_p, ug_p, k_p, v_p = pl.pallas_call(
        _prompt_in_kernel,
        grid=(nb, length // TB_IN),
        in_specs=[smem,
                  pl.BlockSpec((1, TB_IN, D_MODEL), lambda b, i: (b, i, 0)),
                  _const((1, D_MODEL)), _const((D_MODEL, D_IN)),
                  pl.BlockSpec((TB_IN, LANES), lambda b, i: (i, 0)),
                  pl.BlockSpec((TB_IN, LANES), lambda b, i: (i, 0))],
        out_specs=[pl.BlockSpec((1, TB_IN, D_ATTN), lambda b, i: (b, i, 0)),
                   pl.BlockSpec((N_GROUPS, 1, nk, SSD_W), lambda b, i: (0, b, i, 0)),
                   pl.BlockSpec((1, WINDOW, KV_W), lambda b, i: (b, 0, 0)),
                   pl.BlockSpec((1, WINDOW, KV_W), lambda b, i: (b, 0, 0))],
        out_shape=[jax.ShapeDtypeStruct((nb, length, D_ATTN), F32),
                   jax.ShapeDtypeStruct((N_GROUPS, nb, length // SSD_T, SSD_W), BF16),
                   jax.ShapeDtypeStruct((nb, WINDOW, KV_W), F32),
                   jax.ShapeDtypeStruct((nb, WINDOW, KV_W), F32)],
        scratch_shapes=[pltpu.VMEM((TB_IN, D_ATTN), F32), pltpu.VMEM((8, WINDOW + TB_IN, LANES), BF16),
                        pltpu.VMEM((D_SSM // LANES, TB_IN, LANES), F32)],
        compiler_params=pltpu.CompilerParams(dimension_semantics=("arbitrary", "arbitrary"),
                                             vmem_limit_bytes=VMEM_LIMIT),
        name="prompt_in",
    )(sinks, x_prompt, g1, win, cos_p, sin_p)

    yg_p, sfin_p = pl.pallas_call(
        _prompt_ssd_kernel,
        grid=(length // SSD_T // NC_SSD,),
        in_specs=[pl.BlockSpec((N_GROUPS, nb, NC_SSD, SSD_W), lambda i: (0, 0, i, 0)),
                  _const(m_mat.shape), _const(wd_mat.shape), _const(r_mat.shape), _const(tab.shape),
                  _const(abar.shape)],
        out_specs=[pl.BlockSpec((N_GROUPS, nb, NC_SSD, SSD_W), lambda i: (0, 0, i, 0)),
                   pl.BlockSpec((nb, N_GROUPS, 1, STATE_W), lambda i: (0, 0, 0, 0))],
        out_shape=[jax.ShapeDtypeStruct((N_GROUPS, nb, length // SSD_T, SSD_W), F32),
                   jax.ShapeDtypeStruct((nb, N_GROUPS, 1, STATE_W), F32)],
        scratch_shapes=[pltpu.VMEM((N_GROUPS, nb * NC_SSD, STATE_W), F32),
                        pltpu.VMEM((N_GROUPS, nb * NC_SSD, STATE_W), F32),
                        pltpu.VMEM((nb, N_GROUPS, 1, STATE_W), F32)],
        compiler_params=pltpu.CompilerParams(dimension_semantics=("arbitrary",), vmem_limit_bytes=VMEM_LIMIT),
        name="prompt_ssd",
    )(ug_p, m_mat, wd_mat, r_mat, tab, abar)

    y_p, cst_p = _out_call(x_prompt, a_p, yg_p, None, w, tb=TB_OUT, strided=True, cs=1, pad=SUBLANES)

    rows = ns * nt
    xs = x_sample.transpose(1, 0, 2).reshape(rows, D_MODEL)
    cos_s, sin_s = _rope_tables(PAST_LEN + jnp.arange(rows) // ns)
    kc = cache_k[0].reshape(ns * n_cache, KV_W)
    vc = cache_v[0].reshape(ns * n_cache, KV_W)
    h0 = jnp.concatenate([state_ssm_re[0], state_ssm_im[0]], axis=-1).transpose(1, 0, 2)
    dec = functools.partial(_decode_in_kernel, n_streams=ns, n_steps=nt, n_cache=n_cache)
    dec_in = [xs, g1, win, cos_s, sin_s, kc, vc, h0, m_mat, wd_mat, r_mat, abar]
    a_s, kn, vn, yg_s, sn = pl.pallas_call(
        dec,
        grid=(1,),
        in_specs=[smem] + [_const(t.shape) for t in dec_in],
        out_specs=[_whole((rows, D_ATTN)), _whole((rows, KV_W)), _whole((rows, KV_W)),
                   _whole((N_GROUPS, ns, SSD_W)), _whole((N_GROUPS, ns, STATE_W))],
        out_shape=[jax.ShapeDtypeStruct((rows, D_ATTN), F32), jax.ShapeDtypeStruct((rows, KV_W), F32),
                   jax.ShapeDtypeStruct((rows, KV_W), F32), jax.ShapeDtypeStruct((N_GROUPS, ns, SSD_W), F32),
                   jax.ShapeDtypeStruct((N_GROUPS, ns, STATE_W), F32)],
        scratch_shapes=[pltpu.VMEM((D_SSM // LANES, rows, LANES), F32), pltpu.VMEM((N_GROUPS, ns, SSD_W), F32)],
        compiler_params=pltpu.CompilerParams(dimension_semantics=("arbitrary",), vmem_limit_bytes=VMEM_LIMIT),
        name="decode_in",
    )(sinks, *dec_in)

    cs = ns
    prev = state_conv[0].transpose(1, 0, 2).reshape(1, (CONV_W - 1) * cs, 2 * D_FF)
    y_s, cst_s = _out_call(xs[None], a_s[None], yg_s[:, None], prev, w, tb=rows, strided=False, cs=cs,
                           pad=(CONV_W - 1) * cs)

    kv5 = lambda t, n: t.reshape(1, n, -1, N_KV_HEADS, HEAD_DIM)
    y_sample = y_s.reshape(nt, ns, D_MODEL).transpose(1, 0, 2)
    kn_b = kn.reshape(nt, ns, KV_W).transpose(1, 0, 2)
    vn_b = vn.reshape(nt, ns, KV_W).transpose(1, 0, 2)
    k_new = jnp.concatenate([cache_k[0].reshape(ns, n_cache, KV_W), kn_b], axis=1)[:, -n_cache:]
    v_new = jnp.concatenate([cache_v[0].reshape(ns, n_cache, KV_W), vn_b], axis=1)[:, -n_cache:]
    conv_p = cst_p[:, SUBLANES - (CONV_W - 1):, :][None]
    conv_s = cst_s.reshape(CONV_W - 1, ns, 2 * D_FF).transpose(1, 0, 2)[None]
    sn_b = sn.transpose(1, 0, 2)
    return (y_p, y_sample,
            kv5(k_p, nb), kv5(v_p, nb),
            sfin_p[None, :, :, 0, :SSM_STATE], sfin_p[None, :, :, 0, SSM_STATE:], conv_p,
            kv5(k_new, ns), kv5(v_new, ns),
            sn_b[None, :, :, :SSM_STATE], sn_b[None, :, :, SSM_STATE:], conv_s)
```

```python
import functools
import math

import jax
import jax.numpy as jnp
from jax import lax
from jax.experimental import pallas as pl
from jax.experimental.pallas import tpu as pltpu

F32 = jnp.float32
BF16 = jnp.bfloat16

D_MODEL = 1024
CHUNK = 64
D_ATTN = 512
D_SSM = 512
HEAD_DIM = 64
N_HEADS = 8
N_KV_HEADS = 2
KV_W = N_KV_HEADS * HEAD_DIM
WINDOW = 128
N_WIN_CHUNKS = WINDOW // CHUNK
ROPE_THETA = 10000.0
SSM_GROUP = 16
N_GROUPS = D_SSM // SSM_GROUP
SSM_STATE = 64
D_FF = 2816
CONV_W = 3
EPS = 1e-6
D_IN = D_ATTN + 2 * KV_W + D_SSM
PAST_LEN = 2048

SSD_T = 16
LANES = 128
SUBLANES = 8
GROUPS_PER_VREG = LANES // SSM_GROUP
SSD_W = SSD_T * SSM_GROUP
STATE_W = 2 * SSM_STATE

TB_IN = 512
TB_OUT = 512
NC_SSD = 64
FF_TILE = 256
VMEM_LIMIT = 56 * 1024 * 1024


def _log2(n):
    assert n > 0 and n & (n - 1) == 0, n
    return n.bit_length() - 1


def _pdiv(x, n):
    return lax.shift_right_arithmetic(x, jnp.int32(_log2(n)))


def _pmod(x, n):
    return x & (n - 1)


def _rms(x, g):
    return x * lax.rsqrt(jnp.mean(x * x, axis=-1, keepdims=True) + EPS) * g


def _rope(xb, cos, sin_signed, first_half):
    partner = jnp.where(first_half, pltpu.roll(xb, LANES - HEAD_DIM // 2, 1), pltpu.roll(xb, HEAD_DIM // 2, 1))
    return xb * cos + partner * sin_signed


def _group_transpose8(xs, lane_group):
    xs = list(xs)
    for d in (4, 2, 1):
        keep = (lane_group & d) == 0
        nxt = list(xs)
        for i in range(GROUPS_PER_VREG):
            if i & d:
                continue
            a, b = xs[i], xs[i | d]
            nxt[i] = jnp.where(keep, a, pltpu.roll(b, SSM_GROUP * d, 1))
            nxt[i | d] = jnp.where(keep, pltpu.roll(a, LANES - SSM_GROUP * d, 1), b)
        xs = nxt
    return xs


def _row_sel(s, nk, strided):
    if strided:
        return pl.ds(s, nk, stride=SSD_T)
    return pl.ds(s * nk, nk)


def _to_groups(u_ref, put, nk, strided):
    lane_group = _pdiv(lax.broadcasted_iota(jnp.int32, (nk, LANES), 1), SSM_GROUP)
    for j in range(D_SSM // LANES):
        for half in range(SSD_T // GROUPS_PER_VREG):
            xs = [u_ref[j, _row_sel(GROUPS_PER_VREG * half + sl, nk, strided), :]
                  for sl in range(GROUPS_PER_VREG)]
            ws = _group_transpose8(xs, lane_group)
            for gq in range(GROUPS_PER_VREG):
                put(GROUPS_PER_VREG * j + gq, half, ws[gq])


def _from_groups(get, y_ref, nk, strided):
    lane_group = _pdiv(lax.broadcasted_iota(jnp.int32, (nk, LANES), 1), SSM_GROUP)
    for j in range(D_SSM // LANES):
        for half in range(SSD_T // GROUPS_PER_VREG):
            ws = [get(GROUPS_PER_VREG * j + gq, half) for gq in range(GROUPS_PER_VREG)]
            xs = _group_transpose8(ws, lane_group)
            for sl in range(GROUPS_PER_VREG):
                y_ref[j, _row_sel(GROUPS_PER_VREG * half + sl, nk, strided), :] = xs[sl]


def _cmul(re_full, im_signed, z):
    return re_full * z + im_signed * pltpu.roll(z, SSM_STATE, z.ndim - 1)


def _in_proj(x, g1_ref, win_ref, cos_ref, sin_ref):
    rows = x.shape[0]
    hn = _rms(x, g1_ref[...]).astype(BF16)
    proj = jnp.dot(hn, win_ref[...], preferred_element_type=F32)
    cos = cos_ref[...]
    sin = sin_ref[...]
    lane = lax.broadcasted_iota(jnp.int32, (rows, LANES), 1)
    first_half = _pmod(lane, HEAD_DIM) < (HEAD_DIM // 2)
    scale = HEAD_DIM ** -0.5
    qs = [_rope(proj[:, LANES * j:LANES * (j + 1)], cos, sin, first_half) * scale for j in range(D_ATTN // LANES)]
    k = _rope(proj[:, D_ATTN:D_ATTN + KV_W], cos, sin, first_half)
    v = proj[:, D_ATTN + KV_W:D_ATTN + 2 * KV_W]
    u = proj[:, D_ATTN + 2 * KV_W:]
    return qs, k, v, u


def _kv_variants(k, v):
    lane = lax.broadcasted_iota(jnp.int32, k.shape, 1)
    lo = lane < HEAD_DIM
    out = []
    for t in (k, v):
        tr = pltpu.roll(t, HEAD_DIM, 1)
        zero = jnp.zeros_like(t)
        out += [jnp.where(lo, t, zero), jnp.where(lo, zero, tr), jnp.where(lo, tr, zero), jnp.where(lo, zero, t)]
    return [o.astype(BF16) for o in out]


def _k_variants(k):
    lo = lax.broadcasted_iota(jnp.int32, k.shape, 1) < HEAD_DIM
    kr = pltpu.roll(k, HEAD_DIM, 1)
    zero = jnp.zeros_like(k)
    out = [jnp.where(lo, k, zero), jnp.where(lo, zero, kr), jnp.where(lo, kr, zero), jnp.where(lo, zero, k)]
    return [o.astype(BF16) for o in out]


def _attend_t(a_bf, kwin, vt_win, valid, sink_row):
    s = lax.dot_general(kwin, a_bf, (((1,), (1,)), ((), ())), preferred_element_type=F32)
    if valid is not None:
        s = jnp.where(valid, s, -jnp.inf)
    m = jnp.maximum(jnp.max(s, axis=0, keepdims=True), sink_row)
    e = jnp.exp(s - m)
    den = jnp.sum(e, axis=0, keepdims=True) + jnp.exp(sink_row - m)
    o = jnp.dot(vt_win, e.astype(BF16), preferred_element_type=F32)
    return o * (1.0 / den)


def _attend(a_bf, kvar, vvar, valid, sink_col):
    s = lax.dot_general(a_bf, kvar, (((1,), (1,)), ((), ())), preferred_element_type=F32)
    for cond in valid:
        s = jnp.where(cond, s, -jnp.inf)
    m = jnp.maximum(jnp.max(s, axis=-1, keepdims=True), sink_col)
    e = jnp.exp(s - m)
    den = jnp.sum(e, axis=-1, keepdims=True) + jnp.exp(sink_col - m)
    o = jnp.dot(e.astype(BF16), vvar, preferred_element_type=F32)
    return o * (1.0 / den)


def _prompt_in_kernel(sinks_ref, x_ref, g1_ref, win_ref, cos_ref, sin_ref,
                      a_ref, ug_ref, kp_ref, vp_ref, q_s, k_s, vb_s, vt_s, u_s):
    tb = x_ref.shape[1]
    nk = tb // SSD_T
    blk = pl.program_id(1)
    qs, k, v, u = _in_proj(x_ref[0], g1_ref, win_ref, cos_ref, sin_ref)
    for j, qb in enumerate(qs):
        q_s[:, LANES * j:LANES * (j + 1)] = qb
    for j in range(D_SSM // LANES):
        u_s[j] = u[:, LANES * j:LANES * (j + 1)]

    nbuf = WINDOW + tb

    @pl.when(blk == 0)
    def _():
        k_s[:, 0:WINDOW, :] = jnp.zeros((4, WINDOW, LANES), BF16)
        vb_s[0:WINDOW, :] = jnp.zeros((WINDOW, LANES), F32)

    @pl.when(blk > 0)
    def _():
        k_s[:, 0:WINDOW, :] = k_s[:, tb:nbuf, :]
        vb_s[0:WINDOW, :] = vb_s[tb:nbuf, :]

    for idx, arr in enumerate(_k_variants(k)):
        k_s[idx, WINDOW:nbuf, :] = arr
    vb_s[WINDOW:nbuf, :] = v

    @pl.when(blk == pl.num_programs(1) - 1)
    def _():
        kp_ref[0] = k[tb - WINDOW:, :]
        vp_ref[0] = v[tb - WINDOW:, :]

    vt0 = vb_s[...].T
    vt1 = pltpu.roll(vt0, nbuf - CHUNK, 1)
    zrows = jnp.zeros((HEAD_DIM, nbuf), F32)
    for sh, vt in ((0, vt0), (1, vt1)):
        for g in range(N_KV_HEADS):
            rows = vt[HEAD_DIM * g:HEAD_DIM * (g + 1), :]
            vt_s[sh, g, 0] = jnp.concatenate([rows, zrows], axis=0).astype(BF16)
            vt_s[sh, g, 1] = jnp.concatenate([zrows, rows], axis=0).astype(BF16)

    nkeys = WINDOW + CHUNK
    key_row = lax.broadcasted_iota(jnp.int32, (nkeys, 1), 0)
    q_lane = lax.broadcasted_iota(jnp.int32, (1, LANES), 1)
    for i in range(tb // CHUNK):
        r0 = i * CHUNK
        sh = i % 2
        base = r0 - sh * CHUNK
        valid = key_row >= (WINDOW - r0 - blk * tb) if i < N_WIN_CHUNKS else None
        for g in range(N_KV_HEADS):
            qa = q_s[r0:r0 + CHUNK, 2 * LANES * g:2 * LANES * g + LANES]
            qb = q_s[r0:r0 + CHUNK, 2 * LANES * g + LANES:2 * LANES * (g + 1)]
            a_bf = jnp.concatenate([qa, qb], axis=0).astype(BF16)
            acc = None
            for par in range(2):
                h0 = 4 * g + par
                sink_row = jnp.where(q_lane < CHUNK, sinks_ref[h0], sinks_ref[h0 + 2])
                o = _attend_t(a_bf, k_s[2 * g + par, r0:r0 + nkeys, :],
                              vt_s[sh, g, par, :, base:base + nkeys], valid, sink_row)
                acc = o if acc is None else acc + o
            at = acc.T
            a_ref[0, r0:r0 + CHUNK, 2 * LANES * g:2 * LANES * g + LANES] = at[:CHUNK]
            a_ref[0, r0:r0 + CHUNK, 2 * LANES * g + LANES:2 * LANES * (g + 1)] = at[CHUNK:]

    def put(g, half, val):
        ug_ref[g, 0, :, LANES * half:LANES * (half + 1)] = val.astype(BF16)

    _to_groups(u_s, put, nk, strided=True)


def _prompt_ssd_kernel(ug_ref, m_ref, wd_ref, r_ref, tab_ref, abar_ref,
                       yg_ref, sfin_ref, d_s, sp_s, carry_s):
    nb = ug_ref.shape[1]
    nc = ug_ref.shape[2]
    step = pl.program_id(0)

    @pl.when(step == 0)
    def _():
        carry_s[...] = jnp.zeros(carry_s.shape, F32)

    def state_in(g, c):
        ub = ug_ref[g].reshape(nb * nc, SSD_W)
        d_s[g] = jnp.dot(ub, wd_ref[g], preferred_element_type=F32)
        return c

    lax.fori_loop(0, N_GROUPS, state_in, 0)

    pw = [tab_ref[i] for i in range(8)]
    are, aim = abar_ref[0][:, None, :], abar_ref[1][:, None, :]
    kio = lax.broadcasted_iota(jnp.int32, (N_GROUPS, nc, STATE_W), 1)
    for b in range(nb):
        dg = d_s[:, b * nc:(b + 1) * nc, :]
        s_in = carry_s[b]
        dsh = jnp.where(kio == 0, s_in, pltpu.roll(dg, 1, 1))
        c = jnp.zeros((N_GROUPS, 1, STATE_W), F32)
        for j in range(nc // SUBLANES):
            x = dsh[:, SUBLANES * j:SUBLANES * (j + 1), :]
            for lvl in range(3):
                xs = pltpu.roll(x, 1 << lvl, 1)
                x = x + _cmul(pw[2 * lvl], pw[2 * lvl + 1], xs)
            cb = jnp.broadcast_to(c, x.shape)
            h = x + _cmul(pw[6], pw[7], cb)
            sp_s[:, b * nc + SUBLANES * j:b * nc + SUBLANES * (j + 1), :] = h
            c = h[:, SUBLANES - 1:SUBLANES, :]
        carry_s[b] = _cmul(are, aim, c) + dg[:, nc - 1:nc, :]

    def readout(g, c):
        ub = ug_ref[g].reshape(nb * nc, SSD_W)
        y = jnp.dot(ub, m_ref[g], preferred_element_type=F32)
        y = y + jnp.dot(sp_s[g].astype(BF16), r_ref[g], preferred_element_type=F32)
        yg_ref[g] = y.reshape(nb, nc, SSD_W)
        return c

    lax.fori_loop(0, N_GROUPS, readout, 0)
    sfin_ref[...] = carry_s[...]


def _out_kernel(*refs, strided, cs, has_prev):
    if has_prev:
        x_ref, a_ref, yg_ref, prev_ref = refs[:4]
        refs = refs[4:]
    else:
        x_ref, a_ref, yg_ref = refs[:3]
        prev_ref = None
        refs = refs[3:]
    (wglu_ref, ga_ref, gs_ref, wout_ref, g2_ref, wup_ref, cw_ref, cb_ref, wdown_ref, gf_ref,
     y_ref, cst_ref, ys_s, carry_s, eg_s, ev_s) = refs
    tb = x_ref.shape[1]
    nk = tb // SSD_T
    pad = carry_s.shape[0]
    blk = pl.program_id(1)

    @pl.when(blk == 0)
    def _():
        if has_prev:
            carry_s[...] = prev_ref[0]
        else:
            carry_s[...] = jnp.zeros(carry_s.shape, F32)

    _from_groups(lambda g, half: yg_ref[g, 0, :, LANES * half:LANES * (half + 1)], ys_s, nk, strided)
    yv = jnp.concatenate([ys_s[j] for j in range(D_SSM // LANES)], axis=-1)
    z = 0.5 * yv * (1.0 + jnp.tanh(math.sqrt(2.0 / math.pi) * (yv + 0.044715 * (yv * yv * yv))))
    gate = jnp.dot(z.astype(BF16), wglu_ref[...], preferred_element_type=F32)
    s_out = z * jax.nn.sigmoid(gate)
    na = _rms(a_ref[0], ga_ref[...]).astype(BF16)
    ns = _rms(s_out, gs_ref[...]).astype(BF16)
    x1 = x_ref[0] + jnp.dot(na, wout_ref[0:D_ATTN, :], preferred_element_type=F32)
    x1 = x1 + jnp.dot(ns, wout_ref[D_ATTN:, :], preferred_element_type=F32)
    h2 = _rms(x1, g2_ref[...]).astype(BF16)

    acc = jnp.zeros((tb, D_MODEL), F32)
    for j in range(D_FF // FF_TILE):
        parts = []
        for half, ext in ((0, eg_s), (1, ev_s)):
            c0 = half * D_FF + j * FF_TILE
            up = jnp.dot(h2, wup_ref[:, c0:c0 + FF_TILE], preferred_element_type=F32)
            ext[0:pad, :] = carry_s[:, c0:c0 + FF_TILE]
            ext[pad:pad + tb, :] = up
            carry_s[:, c0:c0 + FF_TILE] = ext[tb:tb + pad, :]
            conv = (ext[pad - 2 * cs:pad - 2 * cs + tb, :] * cw_ref[0:1, c0:c0 + FF_TILE]
                    + ext[pad - cs:pad - cs + tb, :] * cw_ref[1:2, c0:c0 + FF_TILE]
                    + up * cw_ref[2:3, c0:c0 + FF_TILE] + cb_ref[:, c0:c0 + FF_TILE])
            parts.append(conv)
        cg, cv = parts
        act = (cg * jax.nn.sigmoid(cg) * cv).astype(BF16)
        acc = acc + jnp.dot(act, wdown_ref[j * FF_TILE:(j + 1) * FF_TILE, :], preferred_element_type=F32)
    y_ref[0] = _rms(x1 + acc, gf_ref[...])
    cst_ref[0] = carry_s[...]


def _decode_in_kernel(sinks_ref, x_ref, g1_ref, win_ref, cos_ref, sin_ref, kc_ref, vc_ref, h0_ref,
                      m_ref, wd_ref, r_ref, abar_ref,
                      a_ref, kn_ref, vn_ref, yg_ref, sn_ref, u_s, ug_s, *, n_streams, n_steps, n_cache):
    rows = n_streams * n_steps
    qs, k, v, u = _in_proj(x_ref[...], g1_ref, win_ref, cos_ref, sin_ref)
    kn_ref[...] = k
    vn_ref[...] = v
    for j in range(D_SSM // LANES):
        u_s[j] = u[:, LANES * j:LANES * (j + 1)]
    ncache_rows = n_streams * n_cache
    nkeys = ncache_rows + rows
    kall = jnp.concatenate([kc_ref[...], k], axis=0)
    vall = jnp.concatenate([vc_ref[...], v], axis=0)
    variants = _kv_variants(kall, vall)

    colv = lax.broadcasted_iota(jnp.int32, (1, nkeys), 1)
    is_new = colv >= ncache_rows
    cnew = colv - ncache_rows
    k_stream = jnp.where(is_new, _pmod(cnew, n_streams), _pdiv(colv, n_cache))
    k_pos = jnp.where(is_new, PAST_LEN + _pdiv(cnew, n_streams), PAST_LEN - n_cache + _pmod(colv, n_cache))
    k_chunk = _pdiv(k_pos, CHUNK)
    rowv = _pmod(lax.broadcasted_iota(jnp.int32, (2 * rows, 1), 0), rows)
    q_stream = _pmod(rowv, n_streams)
    q_chunk = _pdiv(PAST_LEN + _pdiv(rowv, n_streams), CHUNK)
    ok_col = jnp.where(k_pos >= 0, k_stream, -1)
    d_chunk = q_chunk - k_chunk
    valid = [q_stream == ok_col,
             lax.bitcast_convert_type(d_chunk, jnp.uint32) <= jnp.uint32(N_WIN_CHUNKS)]
    top = lax.broadcasted_iota(jnp.int32, (2 * rows, 1), 0) < rows

    for g in range(N_KV_HEADS):
        a_bf = jnp.concatenate([qs[2 * g], qs[2 * g + 1]], axis=0).astype(BF16)
        acc = jnp.zeros((2 * rows, LANES), F32)
        for par in range(2):
            h0 = 4 * g + par
            sink_col = jnp.where(top, sinks_ref[h0], sinks_ref[h0 + 2])
            acc = acc + _attend(a_bf, variants[2 * g + par], variants[4 + 2 * g + par], valid, sink_col)
        a_ref[:, 2 * LANES * g:2 * LANES * g + LANES] = acc[:rows]
        a_ref[:, 2 * LANES * g + LANES:2 * LANES * (g + 1)] = acc[rows:]

    def put(g, half, val):
        ug_s[g, :, LANES * half:LANES * (half + 1)] = val

    _to_groups(u_s, put, n_streams, strided=False)
    def group_body(g, c):
        ub = ug_s[g].astype(BF16)
        h0g = h0_ref[g]
        d = jnp.dot(ub, wd_ref[g], preferred_element_type=F32)
        sn_ref[g] = _cmul(abar_ref[0, pl.ds(g, 1), :], abar_ref[1, pl.ds(g, 1), :], h0g) + d
        y = jnp.dot(ub, m_ref[g], preferred_element_type=F32)
        yg_ref[g] = y + jnp.dot(h0g.astype(BF16), r_ref[g], preferred_element_type=F32)
        return c

    lax.fori_loop(0, N_GROUPS, group_body, 0)


def _ssd_tables(a_re, a_im, log_dt, b_re, b_im, c_re, c_im, d_skip):
    hp = lax.Precision.HIGHEST
    t = SSD_T
    dt = jnp.exp(log_dt)[:, None]
    lr, li = dt * a_re, dt * a_im
    n = jnp.arange(0, SUBLANES * t + 1, dtype=F32)[:, None, None]
    mag = jnp.exp(n * lr)
    pr, pi = mag * jnp.cos(n * li), mag * jnp.sin(n * li)
    den = a_re * a_re + a_im * a_im
    nr, ni = pr[1] - 1.0, pi[1]
    fr, fi = (nr * a_re + ni * a_im) / den, (ni * a_re - nr * a_im) / den
    bbr = fr[:, :, None] * b_re - fi[:, :, None] * b_im
    bbi = fr[:, :, None] * b_im + fi[:, :, None] * b_re
    er = c_re[None] * pr[:t + 1, :, None, :] - c_im[None] * pi[:t + 1, :, None, :]
    ei = c_re[None] * pi[:t + 1, :, None, :] + c_im[None] * pr[:t + 1, :, None, :]
    kt = (jnp.einsum('ngcp,gpd->ngcd', er[:t], bbr, precision=hp)
          - jnp.einsum('ngcp,gpd->ngcd', ei[:t], bbi, precision=hp))
    kt = kt.at[0].add(d_skip[:, :, None] * jnp.eye(SSM_GROUP, dtype=F32)[None])
    s_idx = jnp.arange(t)[:, None]
    t_idx = jnp.arange(t)[None, :]
    lag = t_idx - s_idx
    m = jnp.where((lag >= 0)[:, :, None, None, None], kt[jnp.maximum(lag, 0)], 0.0)
    m_mat = m.transpose(2, 0, 4, 1, 3).reshape(N_GROUPS, SSD_W, SSD_W)
    prr, pir = pr[t - 1 - jnp.arange(t)], pi[t - 1 - jnp.arange(t)]
    wdr = prr[..., None] * bbr[None] - pir[..., None] * bbi[None]
    wdi = prr[..., None] * bbi[None] + pir[..., None] * bbr[None]
    to_rows = lambda w: w.transpose(1, 0, 3, 2).reshape(N_GROUPS, SSD_W, SSM_STATE)
    wd_mat = jnp.concatenate([to_rows(wdr), to_rows(wdi)], axis=-1)
    to_cols = lambda e: e[1:t + 1].transpose(1, 3, 0, 2).reshape(N_GROUPS, SSM_STATE, SSD_W)
    r_mat = jnp.concatenate([to_cols(er), -to_cols(ei)], axis=1)

    def full(idx):
        return (jnp.concatenate([pr[idx], pr[idx]], -1), jnp.concatenate([-pi[idx], pi[idx]], -1))

    sub = jnp.arange(SUBLANES)
    tabs = []
    for lvl in range(3):
        d = 1 << lvl
        fre, fim = full(t * d)
        keep = (sub >= d)[None, :, None]
        tabs += [jnp.where(keep, fre[:, None, :], 0.0), jnp.where(keep, fim[:, None, :], 0.0)]
    qre, qim = full(t * (sub + 1))
    tabs += [qre.transpose(1, 0, 2), qim.transpose(1, 0, 2)]
    tab = jnp.stack(tabs)
    abar = jnp.stack(full(t))
    return m_mat.astype(BF16), wd_mat.astype(BF16), r_mat.astype(BF16), tab, abar


def _rope_tables(pos):
    half = HEAD_DIM // 2
    inv = ROPE_THETA ** (-jnp.arange(half, dtype=F32) / half)
    ang = pos.astype(F32)[:, None] * inv[None, :]
    cos, sin = jnp.cos(ang), jnp.sin(ang)
    cos_t = jnp.concatenate([cos, cos, cos, cos], axis=-1)
    sin_t = jnp.concatenate([-sin, sin, -sin, sin], axis=-1)
    return cos_t, sin_t


def _const(shape):
    nd = len(shape)
    return pl.BlockSpec(shape, lambda *_: (0,) * nd, pipeline_mode=pl.Buffered(1))


def _whole(shape):
    nd = len(shape)
    return pl.BlockSpec(shape, lambda *_: (0,) * nd)


def _out_call(x, a, yg, prev, w, *, tb, strided, cs, pad):
    nb, length, _ = x.shape
    nk = tb // SSD_T
    grid = (nb, length // tb)
    in_specs = [
        pl.BlockSpec((1, tb, D_MODEL), lambda b, i: (b, i, 0)),
        pl.BlockSpec((1, tb, D_ATTN), lambda b, i: (b, i, 0)),
        pl.BlockSpec((N_GROUPS, 1, nk, SSD_W), lambda b, i: (0, b, i, 0)),
    ]
    args = [x, a, yg]
    if prev is not None:
        in_specs.append(pl.BlockSpec((1, pad, 2 * D_FF), lambda b, i: (b, 0, 0)))
        args.append(prev)
    weights = [w['w_glu'], w['onorm_a'], w['onorm_s'], w['w_out'], w['norm2'], w['w_up'], w['conv_w'],
               w['conv_b'], w['w_down'], w['final_g']]
    in_specs += [_const(t.shape) for t in weights]
    kern = functools.partial(_out_kernel, strided=strided, cs=cs, has_prev=prev is not None)
    return pl.pallas_call(
        kern,
        grid=grid,
        in_specs=in_specs,
        out_specs=[pl.BlockSpec((1, tb, D_MODEL), lambda b, i: (b, i, 0)),
                   pl.BlockSpec((1, pad, 2 * D_FF), lambda b, i: (b, 0, 0))],
        out_shape=[jax.ShapeDtypeStruct((nb, length, D_MODEL), F32),
                   jax.ShapeDtypeStruct((nb, pad, 2 * D_FF), F32)],
        scratch_shapes=[pltpu.VMEM((D_SSM // LANES, tb, LANES), F32), pltpu.VMEM((pad, 2 * D_FF), F32),
                        pltpu.VMEM((pad + tb, FF_TILE), F32), pltpu.VMEM((pad + tb, FF_TILE), F32)],
        compiler_params=pltpu.CompilerParams(dimension_semantics=("arbitrary", "arbitrary"),
                                             vmem_limit_bytes=VMEM_LIMIT),
        name="layer_out",
    )(*args, *weights)


def kernel(x_prompt, x_sample, cache_k, cache_v, state_ssm_re, state_ssm_im, state_conv, norm1_g, w_in, attn_sinks, ssm_A_re, ssm_A_im, ssm_log_dt, ssm_B_re, ssm_B_im, ssm_C_re, ssm_C_im, ssm_D, w_glu, onorm_attn_g, onorm_ssm_g, w_out, norm2_g, w_up, conv_w, conv_b, w_down, final_g):
    assert norm1_g.shape[0] == 1, "one layer"
    nb, length, _ = x_prompt.shape
    ns, nt, _ = x_sample.shape
    n_cache = cache_k.shape[2]
    assert nt == SSD_T and length % TB_IN == 0 and length % TB_OUT == 0 and (length // SSD_T) % NC_SSD == 0

    m_mat, wd_mat, r_mat, tab, abar = _ssd_tables(ssm_A_re[0], ssm_A_im[0], ssm_log_dt[0], ssm_B_re[0],
                                                  ssm_B_im[0], ssm_C_re[0], ssm_C_im[0], ssm_D[0])
    sinks = attn_sinks[0]
    g1 = norm1_g
    win = w_in[0].astype(BF16)
    w = dict(w_glu=w_glu[0].astype(BF16), onorm_a=onorm_attn_g, onorm_s=onorm_ssm_g, w_out=w_out[0].astype(BF16),
             norm2=norm2_g, w_up=w_up[0].astype(BF16), conv_w=conv_w[0], conv_b=conv_b,
             w_down=w_down[0].astype(BF16), final_g=final_g[None, :])
    smem = pl.BlockSpec(memory_space=pltpu.SMEM)

    cos_p, sin_p = _rope_tables(jnp.arange(length))
    nk = TB_IN // SSD_T
    a_p, ug_p, k_p, v_p = pl.pallas_call(
        _prompt_in_kernel,
        grid=(nb, length // TB_IN),
        in_specs=[smem,
                  pl.BlockSpec((1, TB_IN, D_MODEL), lambda b, i: (b, i, 0)),
                  _const((1, D_MODEL)), _const((D_MODEL, D_IN)),
                  pl.BlockSpec((TB_IN, LANES), lambda b, i: (i, 0)),
                  pl.BlockSpec((TB_IN, LANES), lambda b, i: (i, 0))],
        out_specs=[pl.BlockSpec((1, TB_IN, D_ATTN), lambda b, i: (b, i, 0)),
                   pl.BlockSpec((N_GROUPS, 1, nk, SSD_W), lambda b, i: (0, b, i, 0)),
                   pl.BlockSpec((1, WINDOW, KV_W), lambda b, i: (b, 0, 0)),
                   pl.BlockSpec((1, WINDOW, KV_W), lambda b, i: (b, 0, 0))],
        out_shape=[jax.ShapeDtypeStruct((nb, length, D_ATTN), F32),
                   jax.ShapeDtypeStruct((N_GROUPS, nb, length // SSD_T, SSD_W), BF16),
                   jax.ShapeDtypeStruct((nb, WINDOW, KV_W), F32),
                   jax.ShapeDtypeStruct((nb, WINDOW, KV_W), F32)],
        scratch_shapes=[pltpu.VMEM((TB_IN, D_ATTN), F32),
                        pltpu.VMEM((2 * N_KV_HEADS, WINDOW + TB_IN, LANES), BF16),
                        pltpu.VMEM((WINDOW + TB_IN, LANES), F32),
                        pltpu.VMEM((2, N_KV_HEADS, 2, LANES, WINDOW + TB_IN), BF16),
                        pltpu.VMEM((D_SSM // LANES, TB_IN, LANES), F32)],
        compiler_params=pltpu.CompilerParams(dimension_semantics=("arbitrary", "arbitrary"),
                                             vmem_limit_bytes=VMEM_LIMIT),
        name="prompt_in",
    )(sinks, x_prompt, g1, win, cos_p, sin_p)

    yg_p, sfin_p = pl.pallas_call(
        _prompt_ssd_kernel,
        grid=(length // SSD_T // NC_SSD,),
        in_specs=[pl.BlockSpec((N_GROUPS, nb, NC_SSD, SSD_W), lambda i: (0, 0, i, 0)),
                  _const(m_mat.shape), _const(wd_mat.shape), _const(r_mat.shape), _const(tab.shape),
                  _const(abar.shape)],
        out_specs=[pl.BlockSpec((N_GROUPS, nb, NC_SSD, SSD_W), lambda i: (0, 0, i, 0)),
                   pl.BlockSpec((nb, N_GROUPS, 1, STATE_W), lambda i: (0, 0, 0, 0))],
        out_shape=[jax.ShapeDtypeStruct((N_GROUPS, nb, length // SSD_T, SSD_W), F32),
                   jax.ShapeDtypeStruct((nb, N_GROUPS, 1, STATE_W), F32)],
        scratch_shapes=[pltpu.VMEM((N_GROUPS, nb * NC_SSD, STATE_W), F32),
                        pltpu.VMEM((N_GROUPS, nb * NC_SSD, STATE_W), F32),
                        pltpu.VMEM((nb, N_GROUPS, 1, STATE_W), F32)],
        compiler_params=pltpu.CompilerParams(dimension_semantics=("arbitrary",), vmem_limit_bytes=VMEM_LIMIT),
        name="prompt_ssd",
    )(ug_p, m_mat, wd_mat, r_mat, tab, abar)

    y_p, cst_p = _out_call(x_prompt, a_p, yg_p, None, w, tb=TB_OUT, strided=True, cs=1, pad=SUBLANES)

    rows = ns * nt
    xs = x_sample.transpose(1, 0, 2).reshape(rows, D_MODEL)
    cos_s, sin_s = _rope_tables(PAST_LEN + jnp.arange(rows) // ns)
    kc = cache_k[0].reshape(ns * n_cache, KV_W)
    vc = cache_v[0].reshape(ns * n_cache, KV_W)
    h0 = jnp.concatenate([state_ssm_re[0], state_ssm_im[0]], axis=-1).transpose(1, 0, 2)
    dec = functools.partial(_decode_in_kernel, n_streams=ns, n_steps=nt, n_cache=n_cache)
    dec_in = [xs, g1, win, cos_s, sin_s, kc, vc, h0, m_mat, wd_mat, r_mat, abar]
    a_s, kn, vn, yg_s, sn = pl.pallas_call(
        dec,
        grid=(1,),
        in_specs=[smem] + [_const(t.shape) for t in dec_in],
        out_specs=[_whole((rows, D_ATTN)), _whole((rows, KV_W)), _whole((rows, KV_W)),
                   _whole((N_GROUPS, ns, SSD_W)), _whole((N_GROUPS, ns, STATE_W))],
        out_shape=[jax.ShapeDtypeStruct((rows, D_ATTN), F32), jax.ShapeDtypeStruct((rows, KV_W), F32),
                   jax.ShapeDtypeStruct((rows, KV_W), F32), jax.ShapeDtypeStruct((N_GROUPS, ns, SSD_W), F32),
                   jax.ShapeDtypeStruct((N_GROUPS, ns, STATE_W), F32)],
        scratch_shapes=[pltpu.VMEM((D_SSM // LANES, rows, LANES), F32), pltpu.VMEM((N_GROUPS, ns, SSD_W), F32)],
        compiler_params=pltpu.CompilerParams(dimension_semantics=("arbitrary",), vmem_limit_bytes=VMEM_LIMIT),
        name="decode_in",
    )(sinks, *dec_in)

    cs = ns
    prev = state_conv[0].transpose(1, 0, 2).reshape(1, (CONV_W - 1) * cs, 2 * D_FF)
    y_s, cst_s = _out_call(xs[None], a_s[None], yg_s[:, None], prev, w, tb=rows, strided=False, cs=cs,
                           pad=(CONV_W - 1) * cs)

    kv5 = lambda t, n: t.reshape(1, n, -1, N_KV_HEADS, HEAD_DIM)
    y_sample = y_s.reshape(nt, ns, D_MODEL).transpose(1, 0, 2)
    kn_b = kn.reshape(nt, ns, KV_W).transpose(1, 0, 2)
    vn_b = vn.reshape(nt, ns, KV_W).transpose(1, 0, 2)
    k_new = jnp.concatenate([cache_k[0].reshape(ns, n_cache, KV_W), kn_b], axis=1)[:, -n_cache:]
    v_new = jnp.concatenate([cache_v[0].reshape(ns, n_cache, KV_W), vn_b], axis=1)[:, -n_cache:]
    conv_p = cst_p[:, SUBLANES - (CONV_W - 1):, :][None]
    conv_s = cst_s.reshape(CONV_W - 1, ns, 2 * D_FF).transpose(1, 0, 2)[None]
    sn_b = sn.transpose(1, 0, 2)
    return (y_p, y_sample,
            kv5(k_p, nb), kv5(v_p, nb),
            sfin_p[None, :, :, 0, :SSM_STATE], sfin_p[None, :, :, 0, SSM_STATE:], conv_p,
            kv5(k_new, ns), kv5(v_new, ns),
            sn_b[None, :, :, :SSM_STATE], sn_b[None, :, :, SSM_STATE:], conv_s)
```

```python
import functools
import math

import jax
import jax.numpy as jnp
from jax import lax
from jax.experimental import pallas as pl
from jax.experimental.pallas import tpu as pltpu

F32 = jnp.float32
BF16 = jnp.bfloat16

D_MODEL = 1024
CHUNK = 64
D_ATTN = 512
D_SSM = 512
HEAD_DIM = 64
N_HEADS = 8
N_KV_HEADS = 2
KV_W = N_KV_HEADS * HEAD_DIM
WINDOW = 128
N_WIN_CHUNKS = WINDOW // CHUNK
ROPE_THETA = 10000.0
SSM_GROUP = 16
N_GROUPS = D_SSM // SSM_GROUP
SSM_STATE = 64
D_FF = 2816
CONV_W = 3
EPS = 1e-6
D_IN = D_ATTN + 2 * KV_W + D_SSM
PAST_LEN = 2048

SSD_T = 16
LANES = 128
SUBLANES = 8
GROUPS_PER_VREG = LANES // SSM_GROUP
SSD_W = SSD_T * SSM_GROUP
STATE_W = 2 * SSM_STATE

TB_IN = 512
TB_OUT = 512
NC_SSD = 64
FF_TILE = 256
VMEM_LIMIT = 56 * 1024 * 1024


def _log2(n):
    assert n > 0 and n & (n - 1) == 0, n
    return n.bit_length() - 1


def _pdiv(x, n):
    return lax.shift_right_arithmetic(x, jnp.int32(_log2(n)))


def _pmod(x, n):
    return x & (n - 1)


def _rms(x, g):
    return x * lax.rsqrt(jnp.mean(x * x, axis=-1, keepdims=True) + EPS) * g


def _rope(xb, cos, sin_signed, first_half):
    partner = jnp.where(first_half, pltpu.roll(xb, LANES - HEAD_DIM // 2, 1), pltpu.roll(xb, HEAD_DIM // 2, 1))
    return xb * cos + partner * sin_signed


def _group_transpose8(xs, lane_group):
    xs = list(xs)
    for d in (4, 2, 1):
        keep = (lane_group & d) == 0
        nxt = list(xs)
        for i in range(GROUPS_PER_VREG):
            if i & d:
                continue
            a, b = xs[i], xs[i | d]
            nxt[i] = jnp.where(keep, a, pltpu.roll(b, SSM_GROUP * d, 1))
            nxt[i | d] = jnp.where(keep, pltpu.roll(a, LANES - SSM_GROUP * d, 1), b)
        xs = nxt
    return xs


def _row_sel(s, nk, strided):
    if strided:
        return pl.ds(s, nk, stride=SSD_T)
    return pl.ds(s * nk, nk)


def _to_groups(u_ref, put, nk, strided):
    lane_group = _pdiv(lax.broadcasted_iota(jnp.int32, (nk, LANES), 1), SSM_GROUP)
    for j in range(D_SSM // LANES):
        for half in range(SSD_T // GROUPS_PER_VREG):
            xs = [u_ref[j, _row_sel(GROUPS_PER_VREG * half + sl, nk, strided), :]
                  for sl in range(GROUPS_PER_VREG)]
            ws = _group_transpose8(xs, lane_group)
            for gq in range(GROUPS_PER_VREG):
                put(GROUPS_PER_VREG * j + gq, half, ws[gq])


def _from_groups(get, y_ref, nk, strided):
    lane_group = _pdiv(lax.broadcasted_iota(jnp.int32, (nk, LANES), 1), SSM_GROUP)
    for j in range(D_SSM // LANES):
        for half in range(SSD_T // GROUPS_PER_VREG):
            ws = [get(GROUPS_PER_VREG * j + gq, half) for gq in range(GROUPS_PER_VREG)]
            xs = _group_transpose8(ws, lane_group)
            for sl in range(GROUPS_PER_VREG):
                y_ref[j, _row_sel(GROUPS_PER_VREG * half + sl, nk, strided), :] = xs[sl]


def _cmul(re_full, im_signed, z):
    return re_full * z + im_signed * pltpu.roll(z, SSM_STATE, z.ndim - 1)


def _in_proj(x, g1_ref, win_ref, cos_ref, sin_ref):
    rows = x.shape[0]
    hn = _rms(x, g1_ref[...]).astype(BF16)
    proj = jnp.dot(hn, win_ref[...], preferred_element_type=F32)
    cos = cos_ref[...]
    sin = sin_ref[...]
    lane = lax.broadcasted_iota(jnp.int32, (rows, LANES), 1)
    first_half = _pmod(lane, HEAD_DIM) < (HEAD_DIM // 2)
    scale = HEAD_DIM ** -0.5
    qs = [_rope(proj[:, LANES * j:LANES * (j + 1)], cos, sin, first_half) * scale for j in range(D_ATTN // LANES)]
    k = _rope(proj[:, D_ATTN:D_ATTN + KV_W], cos, sin, first_half)
    v = proj[:, D_ATTN + KV_W:D_ATTN + 2 * KV_W]
    u = proj[:, D_ATTN + 2 * KV_W:]
    return qs, k, v, u


def _kv_variants(k, v):
    lane = lax.broadcasted_iota(jnp.int32, k.shape, 1)
    lo = lane < HEAD_DIM
    out = []
    for t in (k, v):
        tr = pltpu.roll(t, HEAD_DIM, 1)
        zero = jnp.zeros_like(t)
        out += [jnp.where(lo, t, zero), jnp.where(lo, zero, tr), jnp.where(lo, tr, zero), jnp.where(lo, zero, t)]
    return [o.astype(BF16) for o in out]


def _k_variants(k):
    lo = lax.broadcasted_iota(jnp.int32, k.shape, 1) < HEAD_DIM
    kr = pltpu.roll(k, HEAD_DIM, 1)
    zero = jnp.zeros_like(k)
    out = [jnp.where(lo, k, zero), jnp.where(lo, zero, kr), jnp.where(lo, kr, zero), jnp.where(lo, zero, k)]
    return [o.astype(BF16) for o in out]


def _attend_t(a_bf, kwin, vt_win, valid, sink_row):
    s = lax.dot_general(kwin, a_bf, (((1,), (1,)), ((), ())), preferred_element_type=F32)
    if valid is not None:
        s = jnp.where(valid, s, -jnp.inf)
    m = jnp.maximum(jnp.max(s, axis=0, keepdims=True), sink_row)
    e = jnp.exp(s - m)
    den = jnp.sum(e, axis=0, keepdims=True) + jnp.exp(sink_row - m)
    o = jnp.dot(vt_win, e.astype(BF16), preferred_element_type=F32)
    return o * (1.0 / den)


def _attend(a_bf, kvar, vvar, valid, sink_col):
    s = lax.dot_general(a_bf, kvar, (((1,), (1,)), ((), ())), preferred_element_type=F32)
    for cond in valid:
        s = jnp.where(cond, s, -jnp.inf)
    m = jnp.maximum(jnp.max(s, axis=-1, keepdims=True), sink_col)
    e = jnp.exp(s - m)
    den = jnp.sum(e, axis=-1, keepdims=True) + jnp.exp(sink_col - m)
    o = jnp.dot(e.astype(BF16), vvar, preferred_element_type=F32)
    return o * (1.0 / den)


def _prompt_in_kernel(sinks_ref, x_ref, g1_ref, win_ref, cos_ref, sin_ref,
                      a_ref, ug_ref, kp_ref, vp_ref, q_s, k_s, vb_s, vt_s, u_s):
    tb = x_ref.shape[1]
    nk = tb // SSD_T
    blk = pl.program_id(1)
    qs, k, v, u = _in_proj(x_ref[0], g1_ref, win_ref, cos_ref, sin_ref)
    for j, qb in enumerate(qs):
        q_s[:, LANES * j:LANES * (j + 1)] = qb
    for j in range(D_SSM // LANES):
        u_s[j] = u[:, LANES * j:LANES * (j + 1)]

    nbuf = WINDOW + tb

    @pl.when(blk == 0)
    def _():
        k_s[:, 0:WINDOW, :] = jnp.zeros((4, WINDOW, LANES), BF16)
        vb_s[0:WINDOW, :] = jnp.zeros((WINDOW, LANES), F32)

    @pl.when(blk > 0)
    def _():
        k_s[:, 0:WINDOW, :] = k_s[:, tb:nbuf, :]
        vb_s[0:WINDOW, :] = vb_s[tb:nbuf, :]

    for idx, arr in enumerate(_k_variants(k)):
        k_s[idx, WINDOW:nbuf, :] = arr
    vb_s[WINDOW:nbuf, :] = v

    @pl.when(blk == pl.num_programs(1) - 1)
    def _():
        kp_ref[0] = k[tb - WINDOW:, :]
        vp_ref[0] = v[tb - WINDOW:, :]

    vt0 = vb_s[...].T
    vt1 = pltpu.roll(vt0, nbuf - CHUNK, 1)
    zrows = jnp.zeros((HEAD_DIM, nbuf), F32)
    for sh, vt in ((0, vt0), (1, vt1)):
        for g in range(N_KV_HEADS):
            rows = vt[HEAD_DIM * g:HEAD_DIM * (g + 1), :]
            vt_s[sh, g, 0] = jnp.concatenate([rows, zrows], axis=0).astype(BF16)
            vt_s[sh, g, 1] = jnp.concatenate([zrows, rows], axis=0).astype(BF16)

    nkeys = WINDOW + CHUNK
    key_row = lax.broadcasted_iota(jnp.int32, (nkeys, 1), 0)
    q_lane = lax.broadcasted_iota(jnp.int32, (1, LANES), 1)
    for i in range(tb // CHUNK):
        r0 = i * CHUNK
        sh = i % 2
        base = r0 - sh * CHUNK
        valid = key_row >= (WINDOW - r0 - blk * tb) if i < N_WIN_CHUNKS else None
        for g in range(N_KV_HEADS):
            qa = q_s[r0:r0 + CHUNK, 2 * LANES * g:2 * LANES * g + LANES]
            qb = q_s[r0:r0 + CHUNK, 2 * LANES * g + LANES:2 * LANES * (g + 1)]
            a_bf = jnp.concatenate([qa, qb], axis=0).astype(BF16)
            acc = None
            for par in range(2):
                h0 = 4 * g + par
                sink_row = jnp.where(q_lane < CHUNK, sinks_ref[h0], sinks_ref[h0 + 2])
                o = _attend_t(a_bf, k_s[2 * g + par, r0:r0 + nkeys, :],
                              vt_s[sh, g, par, :, base:base + nkeys], valid, sink_row)
                acc = o if acc is None else acc + o
            at = acc.T
            a_ref[0, r0:r0 + CHUNK, 2 * LANES * g:2 * LANES * g + LANES] = at[:CHUNK]
            a_ref[0, r0:r0 + CHUNK, 2 * LANES * g + LANES:2 * LANES * (g + 1)] = at[CHUNK:]

    def put(g, half, val):
        ug_ref[g, 0, :, LANES * half:LANES * (half + 1)] = val.astype(BF16)

    _to_groups(u_s, put, nk, strided=True)


def _prompt_ssd_kernel(ug_ref, m_ref, wd_ref, r_ref, tab_ref, abar_ref,
                       yg_ref, sfin_ref, d_s, sp_s, carry_s):
    nb = ug_ref.shape[1]
    nc = ug_ref.shape[2]
    step = pl.program_id(0)

    @pl.when(step == 0)
    def _():
        carry_s[...] = jnp.zeros(carry_s.shape, F32)

    def state_in(g, c):
        ub = ug_ref[g].reshape(nb * nc, SSD_W)
        d_s[g] = jnp.dot(ub, wd_ref[g], preferred_element_type=F32)
        return c

    lax.fori_loop(0, N_GROUPS, state_in, 0)

    pw = [tab_ref[i] for i in range(8)]
    are, aim = abar_ref[0][:, None, :], abar_ref[1][:, None, :]
    kio = lax.broadcasted_iota(jnp.int32, (N_GROUPS, nc, STATE_W), 1)
    for b in range(nb):
        dg = d_s[:, b * nc:(b + 1) * nc, :]
        s_in = carry_s[b]
        dsh = jnp.where(kio == 0, s_in, pltpu.roll(dg, 1, 1))
        c = jnp.zeros((N_GROUPS, 1, STATE_W), F32)
        for j in range(nc // SUBLANES):
            x = dsh[:, SUBLANES * j:SUBLANES * (j + 1), :]
            for lvl in range(3):
                xs = pltpu.roll(x, 1 << lvl, 1)
                x = x + _cmul(pw[2 * lvl], pw[2 * lvl + 1], xs)
            cb = jnp.broadcast_to(c, x.shape)
            h = x + _cmul(pw[6], pw[7], cb)
            sp_s[:, b * nc + SUBLANES * j:b * nc + SUBLANES * (j + 1), :] = h
            c = h[:, SUBLANES - 1:SUBLANES, :]
        carry_s[b] = _cmul(are, aim, c) + dg[:, nc - 1:nc, :]

    def readout(g, c):
        ub = ug_ref[g].reshape(nb * nc, SSD_W)
        y = jnp.dot(ub, m_ref[g], preferred_element_type=F32)
        y = y + jnp.dot(sp_s[g].astype(BF16), r_ref[g], preferred_element_type=F32)
        yg_ref[g] = y.reshape(nb, nc, SSD_W)
        return c

    lax.fori_loop(0, N_GROUPS, readout, 0)
    sfin_ref[...] = carry_s[...]


def _out_kernel(*refs, strided, cs, has_prev):
    if has_prev:
        x_ref, a_ref, yg_ref, prev_ref = refs[:4]
        refs = refs[4:]
    else:
        x_ref, a_ref, yg_ref = refs[:3]
        prev_ref = None
        refs = refs[3:]
    (wglu_ref, ga_ref, gs_ref, wout_ref, g2_ref, wup_ref, cw_ref, cb_ref, wdown_ref, gf_ref,
     y_ref, cst_ref, ys_s, ext_s) = refs
    tb = x_ref.shape[1]
    nk = tb // SSD_T
    pad = ext_s.shape[0] - tb
    blk = pl.program_id(1)

    @pl.when(blk == 0)
    def _():
        if has_prev:
            ext_s[0:pad, :] = prev_ref[0]
        else:
            ext_s[0:pad, :] = jnp.zeros((pad, 2 * D_FF), F32)

    _from_groups(lambda g, half: yg_ref[g, 0, :, LANES * half:LANES * (half + 1)], ys_s, nk, strided)
    yv = jnp.concatenate([ys_s[j] for j in range(D_SSM // LANES)], axis=-1)
    z = 0.5 * yv * (1.0 + jnp.tanh(math.sqrt(2.0 / math.pi) * (yv + 0.044715 * (yv * yv * yv))))
    gate = jnp.dot(z.astype(BF16), wglu_ref[...], preferred_element_type=F32)
    s_out = z * jax.nn.sigmoid(gate)
    na = _rms(a_ref[0], ga_ref[...]).astype(BF16)
    ns = _rms(s_out, gs_ref[...]).astype(BF16)
    x1 = x_ref[0] + jnp.dot(na, wout_ref[0:D_ATTN, :], preferred_element_type=F32)
    x1 = x1 + jnp.dot(ns, wout_ref[D_ATTN:, :], preferred_element_type=F32)
    h2 = _rms(x1, g2_ref[...]).astype(BF16)

    def tile_cols(j, half):
        return slice(half * D_FF + j * FF_TILE, half * D_FF + (j + 1) * FF_TILE)

    def up_proj(j):
        ups = []
        for half in range(2):
            up = jnp.dot(h2, wup_ref[:, tile_cols(j, half)], preferred_element_type=F32)
            ext_s[pad:pad + tb, tile_cols(j, half)] = up
            ups.append(up)
        return ups

    n_tiles = D_FF // FF_TILE
    acc = jnp.zeros((tb, D_MODEL), F32)
    def down_proj(act, j):
        return jnp.dot(act, wdown_ref[j * FF_TILE:(j + 1) * FF_TILE, :], preferred_element_type=F32)

    ups = up_proj(0)
    act_prev = None
    for j in range(n_tiles):
        nxt = up_proj(j + 1) if j + 1 < n_tiles else None
        if act_prev is not None:
            acc = acc + down_proj(act_prev, j - 1)
        parts = []
        for half in range(2):
            cols = tile_cols(j, half)
            conv = (ext_s[pad - 2 * cs:pad - 2 * cs + tb, cols] * cw_ref[0:1, cols]
                    + ext_s[pad - cs:pad - cs + tb, cols] * cw_ref[1:2, cols]
                    + ups[half] * cw_ref[2:3, cols] + cb_ref[:, cols])
            ext_s[0:pad, cols] = ext_s[tb:tb + pad, cols]
            parts.append(conv)
        cg, cv = parts
        act_prev = (cg * jax.nn.sigmoid(cg) * cv).astype(BF16)
        ups = nxt
    acc = acc + down_proj(act_prev, n_tiles - 1)
    y_ref[0] = _rms(x1 + acc, gf_ref[...])
    cst_ref[0] = ext_s[0:pad, :]


def _decode_in_kernel(sinks_ref, x_ref, g1_ref, win_ref, cos_ref, sin_ref, kc_ref, vc_ref, h0_ref,
                      m_ref, wd_ref, r_ref, abar_ref,
                      a_ref, kn_ref, vn_ref, yg_ref, sn_ref, u_s, ug_s, *, n_streams, n_steps, n_cache):
    rows = n_streams * n_steps
    qs, k, v, u = _in_proj(x_ref[...], g1_ref, win_ref, cos_ref, sin_ref)
    kn_ref[...] = k
    vn_ref[...] = v
    for j in range(D_SSM // LANES):
        u_s[j] = u[:, LANES * j:LANES * (j + 1)]
    ncache_rows = n_streams * n_cache
    nkeys = ncache_rows + rows
    kall = jnp.concatenate([kc_ref[...], k], axis=0)
    vall = jnp.concatenate([vc_ref[...], v], axis=0)
    variants = _kv_variants(kall, vall)

    colv = lax.broadcasted_iota(jnp.int32, (1, nkeys), 1)
    is_new = colv >= ncache_rows
    cnew = colv - ncache_rows
    k_stream = jnp.where(is_new, _pmod(cnew, n_streams), _pdiv(colv, n_cache))
    k_pos = jnp.where(is_new, PAST_LEN + _pdiv(cnew, n_streams), PAST_LEN - n_cache + _pmod(colv, n_cache))
    k_chunk = _pdiv(k_pos, CHUNK)
    rowv = _pmod(lax.broadcasted_iota(jnp.int32, (2 * rows, 1), 0), rows)
    q_stream = _pmod(rowv, n_streams)
    q_chunk = _pdiv(PAST_LEN + _pdiv(rowv, n_streams), CHUNK)
    ok_col = jnp.where(k_pos >= 0, k_stream, -1)
    d_chunk = q_chunk - k_chunk
    valid = [q_stream == ok_col,
             lax.bitcast_convert_type(d_chunk, jnp.uint32) <= jnp.uint32(N_WIN_CHUNKS)]
    top = lax.broadcasted_iota(jnp.int32, (2 * rows, 1), 0) < rows

    for g in range(N_KV_HEADS):
        a_bf = jnp.concatenate([qs[2 * g], qs[2 * g + 1]], axis=0).astype(BF16)
        acc = jnp.zeros((2 * rows, LANES), F32)
        for par in range(2):
            h0 = 4 * g + par
            sink_col = jnp.where(top, sinks_ref[h0], sinks_ref[h0 + 2])
            acc = acc + _attend(a_bf, variants[2 * g + par], variants[4 + 2 * g + par], valid, sink_col)
        a_ref[:, 2 * LANES * g:2 * LANES * g + LANES] = acc[:rows]
        a_ref[:, 2 * LANES * g + LANES:2 * LANES * (g + 1)] = acc[rows:]

    def put(g, half, val):
        ug_s[g, :, LANES * half:LANES * (half + 1)] = val

    _to_groups(u_s, put, n_streams, strided=False)
    def group_body(g, c):
        ub = ug_s[g].astype(BF16)
        h0g = h0_ref[g]
        d = jnp.dot(ub, wd_ref[g], preferred_element_type=F32)
        sn_ref[g] = _cmul(abar_ref[0, pl.ds(g, 1), :], abar_ref[1, pl.ds(g, 1), :], h0g) + d
        y = jnp.dot(ub, m_ref[g], preferred_element_type=F32)
        yg_ref[g] = y + jnp.dot(h0g.astype(BF16), r_ref[g], preferred_element_type=F32)
        return c

    lax.fori_loop(0, N_GROUPS, group_body, 0)


def _ssd_tables(a_re, a_im, log_dt, b_re, b_im, c_re, c_im, d_skip):
    hp = lax.Precision.HIGHEST
    t = SSD_T
    dt = jnp.exp(log_dt)[:, None]
    lr, li = dt * a_re, dt * a_im
    n = jnp.arange(0, SUBLANES * t + 1, dtype=F32)[:, None, None]
    mag = jnp.exp(n * lr)
    pr, pi = mag * jnp.cos(n * li), mag * jnp.sin(n * li)
    den = a_re * a_re + a_im * a_im
    nr, ni = pr[1] - 1.0, pi[1]
    fr, fi = (nr * a_re + ni * a_im) / den, (ni * a_re - nr * a_im) / den
    bbr = fr[:, :, None] * b_re - fi[:, :, None] * b_im
    bbi = fr[:, :, None] * b_im + fi[:, :, None] * b_re
    er = c_re[None] * pr[:t + 1, :, None, :] - c_im[None] * pi[:t + 1, :, None, :]
    ei = c_re[None] * pi[:t + 1, :, None, :] + c_im[None] * pr[:t + 1, :, None, :]
    kt = (jnp.einsum('ngcp,gpd->ngcd', er[:t], bbr, precision=hp)
          - jnp.einsum('ngcp,gpd->ngcd', ei[:t], bbi, precision=hp))
    kt = kt.at[0].add(d_skip[:, :, None] * jnp.eye(SSM_GROUP, dtype=F32)[None])
    s_idx = jnp.arange(t)[:, None]
    t_idx = jnp.arange(t)[None, :]
    lag = t_idx - s_idx
    m = jnp.where((lag >= 0)[:, :, None, None, None], kt[jnp.maximum(lag, 0)], 0.0)
    m_mat = m.transpose(2, 0, 4, 1, 3).reshape(N_GROUPS, SSD_W, SSD_W)
    prr, pir = pr[t - 1 - jnp.arange(t)], pi[t - 1 - jnp.arange(t)]
    wdr = prr[..., None] * bbr[None] - pir[..., None] * bbi[None]
    wdi = prr[..., None] * bbi[None] + pir[..., None] * bbr[None]
    to_rows = lambda w: w.transpose(1, 0, 3, 2).reshape(N_GROUPS, SSD_W, SSM_STATE)
    wd_mat = jnp.concatenate([to_rows(wdr), to_rows(wdi)], axis=-1)
    to_cols = lambda e: e[1:t + 1].transpose(1, 3, 0, 2).reshape(N_GROUPS, SSM_STATE, SSD_W)
    r_mat = jnp.concatenate([to_cols(er), -to_cols(ei)], axis=1)

    def full(idx):
        return (jnp.concatenate([pr[idx], pr[idx]], -1), jnp.concatenate([-pi[idx], pi[idx]], -1))

    sub = jnp.arange(SUBLANES)
    tabs = []
    for lvl in range(3):
        d = 1 << lvl
        fre, fim = full(t * d)
        keep = (sub >= d)[None, :, None]
        tabs += [jnp.where(keep, fre[:, None, :], 0.0), jnp.where(keep, fim[:, None, :], 0.0)]
    qre, qim = full(t * (sub + 1))
    tabs += [qre.transpose(1, 0, 2), qim.transpose(1, 0, 2)]
    tab = jnp.stack(tabs)
    abar = jnp.stack(full(t))
    return m_mat.astype(BF16), wd_mat.astype(BF16), r_mat.astype(BF16), tab, abar


def _rope_tables(pos):
    half = HEAD_DIM // 2
    inv = ROPE_THETA ** (-jnp.arange(half, dtype=F32) / half)
    ang = pos.astype(F32)[:, None] * inv[None, :]
    cos, sin = jnp.cos(ang), jnp.sin(ang)
    cos_t = jnp.concatenate([cos, cos, cos, cos], axis=-1)
    sin_t = jnp.concatenate([-sin, sin, -sin, sin], axis=-1)
    return cos_t, sin_t


def _const(shape):
    nd = len(shape)
    return pl.BlockSpec(shape, lambda *_: (0,) * nd, pipeline_mode=pl.Buffered(1))


def _whole(shape):
    nd = len(shape)
    return pl.BlockSpec(shape, lambda *_: (0,) * nd)


def _out_call(x, a, yg, prev, w, *, tb, strided, cs, pad):
    nb, length, _ = x.shape
    nk = tb // SSD_T
    grid = (nb, length // tb)
    in_specs = [
        pl.BlockSpec((1, tb, D_MODEL), lambda b, i: (b, i, 0)),
        pl.BlockSpec((1, tb, D_ATTN), lambda b, i: (b, i, 0)),
        pl.BlockSpec((N_GROUPS, 1, nk, SSD_W), lambda b, i: (0, b, i, 0)),
    ]
    args = [x, a, yg]
    if prev is not None:
        in_specs.append(pl.BlockSpec((1, pad, 2 * D_FF), lambda b, i: (b, 0, 0)))
        args.append(prev)
    weights = [w['w_glu'], w['onorm_a'], w['onorm_s'], w['w_out'], w['norm2'], w['w_up'], w['conv_w'],
               w['conv_b'], w['w_down'], w['final_g']]
    in_specs += [_const(t.shape) for t in weights]
    kern = functools.partial(_out_kernel, strided=strided, cs=cs, has_prev=prev is not None)
    return pl.pallas_call(
        kern,
        grid=grid,
        in_specs=in_specs,
        out_specs=[pl.BlockSpec((1, tb, D_MODEL), lambda b, i: (b, i, 0)),
                   pl.BlockSpec((1, pad, 2 * D_FF), lambda b, i: (b, 0, 0))],
        out_shape=[jax.ShapeDtypeStruct((nb, length, D_MODEL), F32),
                   jax.ShapeDtypeStruct((nb, pad, 2 * D_FF), F32)],
        scratch_shapes=[pltpu.VMEM((D_SSM // LANES, tb, LANES), F32), pltpu.VMEM((pad + tb, 2 * D_FF), F32)],
        compiler_params=pltpu.CompilerParams(dimension_semantics=("arbitrary", "arbitrary"),
                                             vmem_limit_bytes=VMEM_LIMIT),
        name="layer_out",
    )(*args, *weights)


def kernel(x_prompt, x_sample, cache_k, cache_v, state_ssm_re, state_ssm_im, state_conv, norm1_g, w_in, attn_sinks, ssm_A_re, ssm_A_im, ssm_log_dt, ssm_B_re, ssm_B_im, ssm_C_re, ssm_C_im, ssm_D, w_glu, onorm_attn_g, onorm_ssm_g, w_out, norm2_g, w_up, conv_w, conv_b, w_down, final_g):
    assert norm1_g.shape[0] == 1, "one layer"
    nb, length, _ = x_prompt.shape
    ns, nt, _ = x_sample.shape
    n_cache = cache_k.shape[2]
    assert nt == SSD_T and length % TB_IN == 0 and length % TB_OUT == 0 and (length // SSD_T) % NC_SSD == 0

    m_mat, wd_mat, r_mat, tab, abar = _ssd_tables(ssm_A_re[0], ssm_A_im[0], ssm_log_dt[0], ssm_B_re[0],
                                                  ssm_B_im[0], ssm_C_re[0], ssm_C_im[0], ssm_D[0])
    sinks = attn_sinks[0]
    g1 = norm1_g
    win = w_in[0].astype(BF16)
    w = dict(w_glu=w_glu[0].astype(BF16), onorm_a=onorm_attn_g, onorm_s=onorm_ssm_g, w_out=w_out[0].astype(BF16),
             norm2=norm2_g, w_up=w_up[0].astype(BF16), conv_w=conv_w[0], conv_b=conv_b,
             w_down=w_down[0].astype(BF16), final_g=final_g[None, :])
    smem = pl.BlockSpec(memory_space=pltpu.SMEM)

    cos_p, sin_p = _rope_tables(jnp.arange(length))
    nk = TB_IN // SSD_T
    a_p, ug_p, k_p, v_p = pl.pallas_call(
        _prompt_in_kernel,
        grid=(nb, length // TB_IN),
        in_specs=[smem,
                  pl.BlockSpec((1, TB_IN, D_MODEL), lambda b, i: (b, i, 0)),
                  _const((1, D_MODEL)), _const((D_MODEL, D_IN)),
                  pl.BlockSpec((TB_IN, LANES), lambda b, i: (i, 0)),
                  pl.BlockSpec((TB_IN, LANES), lambda b, i: (i, 0))],
        out_specs=[pl.BlockSpec((1, TB_IN, D_ATTN), lambda b, i: (b, i, 0)),
                   pl.BlockSpec((N_GROUPS, 1, nk, SSD_W), lambda b, i: (0, b, i, 0)),
                   pl.BlockSpec((1, WINDOW, KV_W), lambda b, i: (b, 0, 0)),
                   pl.BlockSpec((1, WINDOW, KV_W), lambda b, i: (b, 0, 0))],
        out_shape=[jax.ShapeDtypeStruct((nb, length, D_ATTN), F32),
                   jax.ShapeDtypeStruct((N_GROUPS, nb, length // SSD_T, SSD_W), BF16),
                   jax.ShapeDtypeStruct((nb, WINDOW, KV_W), F32),
                   jax.ShapeDtypeStruct((nb, WINDOW, KV_W), F32)],
        scratch_shapes=[pltpu.VMEM((TB_IN, D_ATTN), F32),
                        pltpu.VMEM((2 * N_KV_HEADS, WINDOW + TB_IN, LANES), BF16),
                        pltpu.VMEM((WINDOW + TB_IN, LANES), F32),
                        pltpu.VMEM((2, N_KV_HEADS, 2, LANES, WINDOW + TB_IN), BF16),
                        pltpu.VMEM((D_SSM // LANES, TB_IN, LANES), F32)],
        compiler_params=pltpu.CompilerParams(dimension_semantics=("arbitrary", "arbitrary"),
                                             vmem_limit_bytes=VMEM_LIMIT),
        name="prompt_in",
    )(sinks, x_prompt, g1, win, cos_p, sin_p)

    yg_p, sfin_p = pl.pallas_call(
        _prompt_ssd_kernel,
        grid=(length // SSD_T // NC_SSD,),
        in_specs=[pl.BlockSpec((N_GROUPS, nb, NC_SSD, SSD_W), lambda i: (0, 0, i, 0)),
                  _const(m_mat.shape), _const(wd_mat.shape), _const(r_mat.shape), _const(tab.shape),
                  _const(abar.shape)],
        out_specs=[pl.BlockSpec((N_GROUPS, nb, NC_SSD, SSD_W), lambda i: (0, 0, i, 0)),
                   pl.BlockSpec((nb, N_GROUPS, 1, STATE_W), lambda i: (0, 0, 0, 0))],
        out_shape=[jax.ShapeDtypeStruct((N_GROUPS, nb, length // SSD_T, SSD_W), F32),
                   jax.ShapeDtypeStruct((nb, N_GROUPS, 1, STATE_W), F32)],
        scratch_shapes=[pltpu.VMEM((N_GROUPS, nb * NC_SSD, STATE_W), F32),
                        pltpu.VMEM((N_GROUPS, nb * NC_SSD, STATE_W), F32),
                        pltpu.VMEM((nb, N_GROUPS, 1, STATE_W), F32)],
        compiler_params=pltpu.CompilerParams(dimension_semantics=("arbitrary",), vmem_limit_bytes=VMEM_LIMIT),
        name="prompt_ssd",
    )(ug_p, m_mat, wd_mat, r_mat, tab, abar)

    y_p, cst_p = _out_call(x_prompt, a_p, yg_p, None, w, tb=TB_OUT, strided=True, cs=1, pad=SUBLANES)

    rows = ns * nt
    xs = x_sample.transpose(1, 0, 2).reshape(rows, D_MODEL)
    cos_s, sin_s = _rope_tables(PAST_LEN + jnp.arange(rows) // ns)
    kc = cache_k[0].reshape(ns * n_cache, KV_W)
    vc = cache_v[0].reshape(ns * n_cache, KV_W)
    h0 = jnp.concatenate([state_ssm_re[0], state_ssm_im[0]], axis=-1).transpose(1, 0, 2)
    dec = functools.partial(_decode_in_kernel, n_streams=ns, n_steps=nt, n_cache=n_cache)
    dec_in = [xs, g1, win, cos_s, sin_s, kc, vc, h0, m_mat, wd_mat, r_mat, abar]
    a_s, kn, vn, yg_s, sn = pl.pallas_call(
        dec,
        grid=(1,),
        in_specs=[smem] + [_const(t.shape) for t in dec_in],
        out_specs=[_whole((rows, D_ATTN)), _whole((rows, KV_W)), _whole((rows, KV_W)),
                   _whole((N_GROUPS, ns, SSD_W)), _whole((N_GROUPS, ns, STATE_W))],
        out_shape=[jax.ShapeDtypeStruct((rows, D_ATTN), F32), jax.ShapeDtypeStruct((rows, KV_W), F32),
                   jax.ShapeDtypeStruct((rows, KV_W), F32), jax.ShapeDtypeStruct((N_GROUPS, ns, SSD_W), F32),
                   jax.ShapeDtypeStruct((N_GROUPS, ns, STATE_W), F32)],
        scratch_shapes=[pltpu.VMEM((D_SSM // LANES, rows, LANES), F32), pltpu.VMEM((N_GROUPS, ns, SSD_W), F32)],
        compiler_params=pltpu.CompilerParams(dimension_semantics=("arbitrary",), vmem_limit_bytes=VMEM_LIMIT),
        name="decode_in",
    )(sinks, *dec_in)

    cs = ns
    prev = state_conv[0].transpose(1, 0, 2).reshape(1, (CONV_W - 1) * cs, 2 * D_FF)
    y_s, cst_s = _out_call(xs[None], a_s[None], yg_s[:, None], prev, w, tb=rows, strided=False, cs=cs,
                           pad=(CONV_W - 1) * cs)

    kv5 = lambda t, n: t.reshape(1, n, -1, N_KV_HEADS, HEAD_DIM)
    y_sample = y_s.reshape(nt, ns, D_MODEL).transpose(1, 0, 2)
    kn_b = kn.reshape(nt, ns, KV_W).transpose(1, 0, 2)
    vn_b = vn.reshape(nt, ns, KV_W).transpose(1, 0, 2)
    k_new = jnp.concatenate([cache_k[0].reshape(ns, n_cache, KV_W), kn_b], axis=1)[:, -n_cache:]
    v_new = jnp.concatenate([cache_v[0].reshape(ns, n_cache, KV_W), vn_b], axis=1)[:, -n_cache:]
    conv_p = cst_p[:, SUBLANES - (CONV_W - 1):, :][None]
    conv_s = cst_s.reshape(CONV_W - 1, ns, 2 * D_FF).transpose(1, 0, 2)[None]
    sn_b = sn.transpose(1, 0, 2)
    return (y_p, y_sample,
            kv5(k_p, nb), kv5(v_p, nb),
            sfin_p[None, :, :, 0, :SSM_STATE], sfin_p[None, :, :, 0, SSM_STATE:], conv_p,
            kv5(k_new, ns), kv5(v_new, ns),
            sn_b[None, :, :, :SSM_STATE], sn_b[None, :, :, SSM_STATE:], conv_s)
```

```python
import functools
import math

import jax
import jax.numpy as jnp
from jax import lax
from jax.experimental import pallas as pl
from jax.experimental.pallas import tpu as pltpu

F32 = jnp.float32
BF16 = jnp.bfloat16

D_MODEL = 1024
CHUNK = 64
D_ATTN = 512
D_SSM = 512
HEAD_DIM = 64
N_HEADS = 8
N_KV_HEADS = 2
KV_W = N_KV_HEADS * HEAD_DIM
WINDOW = 128
N_WIN_CHUNKS = WINDOW // CHUNK
ROPE_THETA = 10000.0
SSM_GROUP = 16
N_GROUPS = D_SSM // SSM_GROUP
SSM_STATE = 64
D_FF = 2816
CONV_W = 3
EPS = 1e-6
D_IN = D_ATTN + 2 * KV_W + D_SSM
PAST_LEN = 2048

SSD_T = 16
LANES = 128
SUBLANES = 8
GROUPS_PER_VREG = LANES // SSM_GROUP
SSD_W = SSD_T * SSM_GROUP
STATE_W = 2 * SSM_STATE

TB_IN = 512
TB_OUT = 512
NC_SSD = 64
FF_TILE = 256
GROUP_UNROLL = 8
VMEM_LIMIT = 56 * 1024 * 1024


def _log2(n):
    assert n > 0 and n & (n - 1) == 0, n
    return n.bit_length() - 1


def _pdiv(x, n):
    return lax.shift_right_arithmetic(x, jnp.int32(_log2(n)))


def _pmod(x, n):
    return x & (n - 1)


def _rms(x, g):
    return x * lax.rsqrt(jnp.mean(x * x, axis=-1, keepdims=True) + EPS) * g


def _rope(xb, cos, sin_signed, first_half):
    partner = jnp.where(first_half, pltpu.roll(xb, LANES - HEAD_DIM // 2, 1), pltpu.roll(xb, HEAD_DIM // 2, 1))
    return xb * cos + partner * sin_signed


def _group_transpose8(xs, lane_group):
    xs = list(xs)
    for d in (4, 2, 1):
        keep = (lane_group & d) == 0
        nxt = list(xs)
        for i in range(GROUPS_PER_VREG):
            if i & d:
                continue
            a, b = xs[i], xs[i | d]
            nxt[i] = jnp.where(keep, a, pltpu.roll(b, SSM_GROUP * d, 1))
            nxt[i | d] = jnp.where(keep, pltpu.roll(a, LANES - SSM_GROUP * d, 1), b)
        xs = nxt
    return xs


def _row_sel(s, nk, strided):
    if strided:
        return pl.ds(s, nk, stride=SSD_T)
    return pl.ds(s * nk, nk)


def _to_groups(u_ref, put, nk, strided):
    lane_group = _pdiv(lax.broadcasted_iota(jnp.int32, (nk, LANES), 1), SSM_GROUP)
    for j in range(D_SSM // LANES):
        for half in range(SSD_T // GROUPS_PER_VREG):
            xs = [u_ref[j, _row_sel(GROUPS_PER_VREG * half + sl, nk, strided), :]
                  for sl in range(GROUPS_PER_VREG)]
            ws = _group_transpose8(xs, lane_group)
            for gq in range(GROUPS_PER_VREG):
                put(GROUPS_PER_VREG * j + gq, half, ws[gq])


def _from_groups(get, y_ref, nk, strided):
    lane_group = _pdiv(lax.broadcasted_iota(jnp.int32, (nk, LANES), 1), SSM_GROUP)
    for j in range(D_SSM // LANES):
        for half in range(SSD_T // GROUPS_PER_VREG):
            ws = [get(GROUPS_PER_VREG * j + gq, half) for gq in range(GROUPS_PER_VREG)]
            xs = _group_transpose8(ws, lane_group)
            for sl in range(GROUPS_PER_VREG):
                y_ref[j, _row_sel(GROUPS_PER_VREG * half + sl, nk, strided), :] = xs[sl]


def _cmul(re_full, im_signed, z):
    return re_full * z + im_signed * pltpu.roll(z, SSM_STATE, z.ndim - 1)


def _in_proj(x, g1_ref, win_ref, cos_ref, sin_ref):
    rows = x.shape[0]
    hn = _rms(x, g1_ref[...]).astype(BF16)
    proj = jnp.dot(hn, win_ref[...], preferred_element_type=F32)
    cos = cos_ref[...]
    sin = sin_ref[...]
    lane = lax.broadcasted_iota(jnp.int32, (rows, LANES), 1)
    first_half = _pmod(lane, HEAD_DIM) < (HEAD_DIM // 2)
    scale = HEAD_DIM ** -0.5
    qs = [_rope(proj[:, LANES * j:LANES * (j + 1)], cos, sin, first_half) * scale for j in range(D_ATTN // LANES)]
    k = _rope(proj[:, D_ATTN:D_ATTN + KV_W], cos, sin, first_half)
    v = proj[:, D_ATTN + KV_W:D_ATTN + 2 * KV_W]
    u = proj[:, D_ATTN + 2 * KV_W:]
    return qs, k, v, u


def _kv_variants(k, v):
    lane = lax.broadcasted_iota(jnp.int32, k.shape, 1)
    lo = lane < HEAD_DIM
    out = []
    for t in (k, v):
        tr = pltpu.roll(t, HEAD_DIM, 1)
        zero = jnp.zeros_like(t)
        out += [jnp.where(lo, t, zero), jnp.where(lo, zero, tr), jnp.where(lo, tr, zero), jnp.where(lo, zero, t)]
    return [o.astype(BF16) for o in out]


def _k_variants(k):
    lo = lax.broadcasted_iota(jnp.int32, k.shape, 1) < HEAD_DIM
    kr = pltpu.roll(k, HEAD_DIM, 1)
    zero = jnp.zeros_like(k)
    out = [jnp.where(lo, k, zero), jnp.where(lo, zero, kr), jnp.where(lo, kr, zero), jnp.where(lo, zero, k)]
    return [o.astype(BF16) for o in out]


def _attend_t(a_bf, kwin, vt_win, valid, sink_row):
    s = lax.dot_general(kwin, a_bf, (((1,), (1,)), ((), ())), preferred_element_type=F32)
    if valid is not None:
        s = jnp.where(valid, s, -jnp.inf)
    m = jnp.maximum(jnp.max(s, axis=0, keepdims=True), sink_row)
    e = jnp.exp(s - m)
    den = jnp.sum(e, axis=0, keepdims=True) + jnp.exp(sink_row - m)
    o = jnp.dot(vt_win, e.astype(BF16), preferred_element_type=F32)
    return o * (1.0 / den)


def _attend(a_bf, kvar, vvar, valid, sink_col):
    s = lax.dot_general(a_bf, kvar, (((1,), (1,)), ((), ())), preferred_element_type=F32)
    for cond in valid:
        s = jnp.where(cond, s, -jnp.inf)
    m = jnp.maximum(jnp.max(s, axis=-1, keepdims=True), sink_col)
    e = jnp.exp(s - m)
    den = jnp.sum(e, axis=-1, keepdims=True) + jnp.exp(sink_col - m)
    o = jnp.dot(e.astype(BF16), vvar, preferred_element_type=F32)
    return o * (1.0 / den)


def _prompt_in_kernel(sinks_ref, x_ref, g1_ref, win_ref, cos_ref, sin_ref,
                      a_ref, ug_ref, kp_ref, vp_ref, q_s, k_s, vb_s, vt_s, u_s):
    tb = x_ref.shape[1]
    nk = tb // SSD_T
    blk = pl.program_id(1)
    qs, k, v, u = _in_proj(x_ref[0], g1_ref, win_ref, cos_ref, sin_ref)
    for j, qb in enumerate(qs):
        q_s[:, LANES * j:LANES * (j + 1)] = qb
    for j in range(D_SSM // LANES):
        u_s[j] = u[:, LANES * j:LANES * (j + 1)]

    nbuf = WINDOW + tb

    @pl.when(blk == 0)
    def _():
        k_s[:, 0:WINDOW, :] = jnp.zeros((4, WINDOW, LANES), BF16)
        vb_s[0:WINDOW, :] = jnp.zeros((WINDOW, LANES), F32)

    @pl.when(blk > 0)
    def _():
        k_s[:, 0:WINDOW, :] = k_s[:, tb:nbuf, :]
        vb_s[0:WINDOW, :] = vb_s[tb:nbuf, :]

    for idx, arr in enumerate(_k_variants(k)):
        k_s[idx, WINDOW:nbuf, :] = arr
    vb_s[WINDOW:nbuf, :] = v

    @pl.when(blk == pl.num_programs(1) - 1)
    def _():
        kp_ref[0] = k[tb - WINDOW:, :]
        vp_ref[0] = v[tb - WINDOW:, :]

    vt0 = vb_s[...].T
    vt1 = pltpu.roll(vt0, nbuf - CHUNK, 1)
    zrows = jnp.zeros((HEAD_DIM, nbuf), F32)
    for sh, vt in ((0, vt0), (1, vt1)):
        for g in range(N_KV_HEADS):
            rows = vt[HEAD_DIM * g:HEAD_DIM * (g + 1), :]
            vt_s[sh, g, 0] = jnp.concatenate([rows, zrows], axis=0).astype(BF16)
            vt_s[sh, g, 1] = jnp.concatenate([zrows, rows], axis=0).astype(BF16)

    nkeys = WINDOW + CHUNK
    key_row = lax.broadcasted_iota(jnp.int32, (nkeys, 1), 0)
    q_lane = lax.broadcasted_iota(jnp.int32, (1, LANES), 1)
    for i in range(tb // CHUNK):
        r0 = i * CHUNK
        sh = i % 2
        base = r0 - sh * CHUNK
        valid = key_row >= (WINDOW - r0 - blk * tb) if i < N_WIN_CHUNKS else None
        for g in range(N_KV_HEADS):
            qa = q_s[r0:r0 + CHUNK, 2 * LANES * g:2 * LANES * g + LANES]
            qb = q_s[r0:r0 + CHUNK, 2 * LANES * g + LANES:2 * LANES * (g + 1)]
            a_bf = jnp.concatenate([qa, qb], axis=0).astype(BF16)
            acc = None
            for par in range(2):
                h0 = 4 * g + par
                sink_row = jnp.where(q_lane < CHUNK, sinks_ref[h0], sinks_ref[h0 + 2])
                o = _attend_t(a_bf, k_s[2 * g + par, r0:r0 + nkeys, :],
                              vt_s[sh, g, par, :, base:base + nkeys], valid, sink_row)
                acc = o if acc is None else acc + o
            at = acc.T
            a_ref[0, r0:r0 + CHUNK, 2 * LANES * g:2 * LANES * g + LANES] = at[:CHUNK]
            a_ref[0, r0:r0 + CHUNK, 2 * LANES * g + LANES:2 * LANES * (g + 1)] = at[CHUNK:]

    def put(g, half, val):
        ug_ref[g, 0, :, LANES * half:LANES * (half + 1)] = val.astype(BF16)

    _to_groups(u_s, put, nk, strided=True)


def _prompt_ssd_kernel(ug_ref, m_ref, wd_ref, r_ref, tab_ref, abar_ref,
                       yg_ref, sfin_ref, d_s, sp_s, carry_s):
    nb = ug_ref.shape[1]
    nc = ug_ref.shape[2]
    step = pl.program_id(0)

    @pl.when(step == 0)
    def _():
        carry_s[...] = jnp.zeros(carry_s.shape, F32)

    def state_in(g, c):
        ub = ug_ref[g].reshape(nb * nc, SSD_W)
        d_s[g] = jnp.dot(ub, wd_ref[g], preferred_element_type=F32)
        return c

    lax.fori_loop(0, N_GROUPS, state_in, 0, unroll=GROUP_UNROLL)

    pw = [tab_ref[i] for i in range(8)]
    are, aim = abar_ref[0][:, None, :], abar_ref[1][:, None, :]
    kio = lax.broadcasted_iota(jnp.int32, (N_GROUPS, nc, STATE_W), 1)
    for b in range(nb):
        dg = d_s[:, b * nc:(b + 1) * nc, :]
        s_in = carry_s[b]
        dsh = jnp.where(kio == 0, s_in, pltpu.roll(dg, 1, 1))
        c = jnp.zeros((N_GROUPS, 1, STATE_W), F32)
        for j in range(nc // SUBLANES):
            x = dsh[:, SUBLANES * j:SUBLANES * (j + 1), :]
            for lvl in range(3):
                xs = pltpu.roll(x, 1 << lvl, 1)
                x = x + _cmul(pw[2 * lvl], pw[2 * lvl + 1], xs)
            cb = jnp.broadcast_to(c, x.shape)
            h = x + _cmul(pw[6], pw[7], cb)
            sp_s[:, b * nc + SUBLANES * j:b * nc + SUBLANES * (j + 1), :] = h
            c = h[:, SUBLANES - 1:SUBLANES, :]
        carry_s[b] = _cmul(are, aim, c) + dg[:, nc - 1:nc, :]

    def readout(g, c):
        ub = ug_ref[g].reshape(nb * nc, SSD_W)
        y = jnp.dot(ub, m_ref[g], preferred_element_type=F32)
        y = y + lax.dot_general(sp_s[g].astype(BF16), r_ref[g], (((1,), (1,)), ((), ())),
                                preferred_element_type=F32)
        yg_ref[g] = y.reshape(nb, nc, SSD_W)
        return c

    lax.fori_loop(0, N_GROUPS, readout, 0, unroll=GROUP_UNROLL)
    sfin_ref[...] = carry_s[...]


def _out_kernel(*refs, strided, cs, has_prev):
    if has_prev:
        x_ref, a_ref, yg_ref, prev_ref = refs[:4]
        refs = refs[4:]
    else:
        x_ref, a_ref, yg_ref = refs[:3]
        prev_ref = None
        refs = refs[3:]
    (wglu_ref, ga_ref, gs_ref, wout_ref, g2_ref, wup_ref, cw_ref, cb_ref, wdown_ref, gf_ref,
     y_ref, cst_ref, ys_s, ext_s) = refs
    tb = x_ref.shape[1]
    nk = tb // SSD_T
    pad = ext_s.shape[0] - tb
    blk = pl.program_id(1)

    @pl.when(blk == 0)
    def _():
        if has_prev:
            ext_s[0:pad, :] = prev_ref[0]
        else:
            ext_s[0:pad, :] = jnp.zeros((pad, 2 * D_FF), F32)

    _from_groups(lambda g, half: yg_ref[g, 0, :, LANES * half:LANES * (half + 1)], ys_s, nk, strided)
    yv = jnp.concatenate([ys_s[j] for j in range(D_SSM // LANES)], axis=-1)
    z = 0.5 * yv * (1.0 + jnp.tanh(math.sqrt(2.0 / math.pi) * (yv + 0.044715 * (yv * yv * yv))))
    gate = jnp.dot(z.astype(BF16), wglu_ref[...], preferred_element_type=F32)
    s_out = z * jax.nn.sigmoid(gate)
    na = _rms(a_ref[0], ga_ref[...]).astype(BF16)
    ns = _rms(s_out, gs_ref[...]).astype(BF16)
    x1 = x_ref[0] + jnp.dot(na, wout_ref[0:D_ATTN, :], preferred_element_type=F32)
    x1 = x1 + jnp.dot(ns, wout_ref[D_ATTN:, :], preferred_element_type=F32)
    h2 = _rms(x1, g2_ref[...]).astype(BF16)

    def tile_cols(j, half):
        return slice(half * D_FF + j * FF_TILE, half * D_FF + (j + 1) * FF_TILE)

    def up_proj(j):
        ups = []
        for half in range(2):
            up = jnp.dot(h2, wup_ref[:, tile_cols(j, half)], preferred_element_type=F32)
            ext_s[pad:pad + tb, tile_cols(j, half)] = up
            ups.append(up)
        return ups

    n_tiles = D_FF // FF_TILE
    acc = jnp.zeros((tb, D_MODEL), F32)
    def down_proj(act, j):
        return jnp.dot(act, wdown_ref[j * FF_TILE:(j + 1) * FF_TILE, :], preferred_element_type=F32)

    ups = up_proj(0)
    act_prev = None
    for j in range(n_tiles):
        nxt = up_proj(j + 1) if j + 1 < n_tiles else None
        if act_prev is not None:
            acc = acc + down_proj(act_prev, j - 1)
        parts = []
        for half in range(2):
            cols = tile_cols(j, half)
            conv = (ext_s[pad - 2 * cs:pad - 2 * cs + tb, cols] * cw_ref[0:1, cols]
                    + ext_s[pad - cs:pad - cs + tb, cols] * cw_ref[1:2, cols]
                    + ups[half] * cw_ref[2:3, cols] + cb_ref[:, cols])
            ext_s[0:pad, cols] = ext_s[tb:tb + pad, cols]
            parts.append(conv)
        cg, cv = parts
        act_prev = (cg * jax.nn.sigmoid(cg) * cv).astype(BF16)
        ups = nxt
    acc = acc + down_proj(act_prev, n_tiles - 1)
    y_ref[0] = _rms(x1 + acc, gf_ref[...])
    cst_ref[0] = ext_s[0:pad, :]


def _decode_in_kernel(sinks_ref, x_ref, g1_ref, win_ref, cos_ref, sin_ref, kc_ref, vc_ref, h0_ref,
                      m_ref, wd_ref, r_ref, abar_ref,
                      a_ref, kn_ref, vn_ref, yg_ref, sn_ref, u_s, ug_s, *, n_streams, n_steps, n_cache):
    rows = n_streams * n_steps
    qs, k, v, u = _in_proj(x_ref[...], g1_ref, win_ref, cos_ref, sin_ref)
    kn_ref[...] = k
    vn_ref[...] = v
    for j in range(D_SSM // LANES):
        u_s[j] = u[:, LANES * j:LANES * (j + 1)]
    ncache_rows = n_streams * n_cache
    nkeys = ncache_rows + rows
    kall = jnp.concatenate([kc_ref[...], k], axis=0)
    vall = jnp.concatenate([vc_ref[...], v], axis=0)
    variants = _kv_variants(kall, vall)

    colv = lax.broadcasted_iota(jnp.int32, (1, nkeys), 1)
    is_new = colv >= ncache_rows
    cnew = colv - ncache_rows
    k_stream = jnp.where(is_new, _pmod(cnew, n_streams), _pdiv(colv, n_cache))
    k_pos = jnp.where(is_new, PAST_LEN + _pdiv(cnew, n_streams), PAST_LEN - n_cache + _pmod(colv, n_cache))
    k_chunk = _pdiv(k_pos, CHUNK)
    rowv = _pmod(lax.broadcasted_iota(jnp.int32, (2 * rows, 1), 0), rows)
    q_stream = _pmod(rowv, n_streams)
    q_chunk = _pdiv(PAST_LEN + _pdiv(rowv, n_streams), CHUNK)
    ok_col = jnp.where(k_pos >= 0, k_stream, -1)
    d_chunk = q_chunk - k_chunk
    valid = [q_stream == ok_col,
             lax.bitcast_convert_type(d_chunk, jnp.uint32) <= jnp.uint32(N_WIN_CHUNKS)]
    top = lax.broadcasted_iota(jnp.int32, (2 * rows, 1), 0) < rows

    for g in range(N_KV_HEADS):
        a_bf = jnp.concatenate([qs[2 * g], qs[2 * g + 1]], axis=0).astype(BF16)
        acc = jnp.zeros((2 * rows, LANES), F32)
        for par in range(2):
            h0 = 4 * g + par
            sink_col = jnp.where(top, sinks_ref[h0], sinks_ref[h0 + 2])
            acc = acc + _attend(a_bf, variants[2 * g + par], variants[4 + 2 * g + par], valid, sink_col)
        a_ref[:, 2 * LANES * g:2 * LANES * g + LANES] = acc[:rows]
        a_ref[:, 2 * LANES * g + LANES:2 * LANES * (g + 1)] = acc[rows:]

    def put(g, half, val):
        ug_s[g, :, LANES * half:LANES * (half + 1)] = val

    _to_groups(u_s, put, n_streams, strided=False)
    def group_body(g, c):
        ub = ug_s[g].astype(BF16)
        h0g = h0_ref[g]
        d = jnp.dot(ub, wd_ref[g], preferred_element_type=F32)
        sn_ref[g] = _cmul(abar_ref[0, pl.ds(g, 1), :], abar_ref[1, pl.ds(g, 1), :], h0g) + d
        y = jnp.dot(ub, m_ref[g], preferred_element_type=F32)
        yg_ref[g] = y + lax.dot_general(h0g.astype(BF16), r_ref[g], (((1,), (1,)), ((), ())),
                                        preferred_element_type=F32)
        return c

    lax.fori_loop(0, N_GROUPS, group_body, 0, unroll=GROUP_UNROLL)


def _ssd_tables(a_re, a_im, log_dt, b_re, b_im, c_re, c_im, d_skip):
    t = SSD_T
    dt = jnp.exp(log_dt)[:, None]
    lr, li = dt * a_re, dt * a_im

    def powers(n):
        n = n.astype(F32)[None, :, None]
        mag = jnp.exp(n * lr[:, None, :])
        return mag * jnp.cos(n * li[:, None, :]), mag * jnp.sin(n * li[:, None, :])

    def cmul4(xr, xi, pr, pi):
        xr, xi, pr, pi = xr[:, None], xi[:, None], pr[:, :, None], pi[:, :, None]
        return xr * pr - xi * pi, xr * pi + xi * pr

    def rows(re, im):
        return jnp.concatenate([re, im], axis=-1).reshape(N_GROUPS, SSD_W, STATE_W)

    steps = jnp.arange(t)
    ar, ai = powers(jnp.ones((1,)))
    ar, ai = ar[:, 0], ai[:, 0]
    den = a_re * a_re + a_im * a_im
    nr, ni = ar - 1.0, ai
    fr, fi = (nr * a_re + ni * a_im) / den, (ni * a_re - nr * a_im) / den
    btr, bti = b_re.transpose(0, 2, 1), b_im.transpose(0, 2, 1)
    bbr = fr[:, None, :] * btr - fi[:, None, :] * bti
    bbi = fr[:, None, :] * bti + fi[:, None, :] * btr

    xr, xi = cmul4(bbr, bbi, *powers(-steps))
    yr, yi = cmul4(c_re, c_im, *powers(steps))
    m = jnp.einsum('gik,gjk->gij', rows(xr, -xi), rows(yr, yi), precision=lax.Precision.HIGHEST)
    idx = jnp.arange(SSD_W)
    causal = (idx[None, :] // SSM_GROUP) >= (idx[:, None] // SSM_GROUP)
    skip = jnp.tile(d_skip, (1, t))[:, None, :] * jnp.eye(SSD_W, dtype=F32)[None]
    m_mat = jnp.where(causal[None], m, 0.0) + skip
    wd_mat = rows(*cmul4(bbr, bbi, *powers(t - 1 - steps)))
    er, ei = cmul4(c_re, c_im, *powers(steps + 1))
    rt_mat = rows(er, -ei)

    def full(n):
        pr, pi = powers(n)
        return jnp.concatenate([pr, pr], -1), jnp.concatenate([-pi, pi], -1)

    sub = jnp.arange(SUBLANES)
    tabs = []
    for lvl in range(3):
        d = 1 << lvl
        fre, fim = full(jnp.full((SUBLANES,), t * d))
        keep = (sub >= d)[None, :, None]
        tabs += [jnp.where(keep, fre, 0.0), jnp.where(keep, fim, 0.0)]
    tabs += list(full(t * (sub + 1)))
    tab = jnp.stack(tabs)
    are, aim = full(jnp.full((1,), t))
    abar = jnp.stack([are[:, 0], aim[:, 0]])
    return m_mat.astype(BF16), wd_mat.astype(BF16), rt_mat.astype(BF16), tab, abar


def _rope_tables(pos):
    half = HEAD_DIM // 2
    inv = ROPE_THETA ** (-jnp.arange(half, dtype=F32) / half)
    ang = pos.astype(F32)[:, None] * inv[None, :]
    cos, sin = jnp.cos(ang), jnp.sin(ang)
    cos_t = jnp.concatenate([cos, cos, cos, cos], axis=-1)
    sin_t = jnp.concatenate([-sin, sin, -sin, sin], axis=-1)
    return cos_t, sin_t


def _const(shape):
    nd = len(shape)
    return pl.BlockSpec(shape, lambda *_: (0,) * nd, pipeline_mode=pl.Buffered(1))


def _whole(shape):
    nd = len(shape)
    return pl.BlockSpec(shape, lambda *_: (0,) * nd)


def _out_call(x, a, yg, prev, w, *, tb, strided, cs, pad):
    nb, length, _ = x.shape
    nk = tb // SSD_T
    grid = (nb, length // tb)
    in_specs = [
        pl.BlockSpec((1, tb, D_MODEL), lambda b, i: (b, i, 0)),
        pl.BlockSpec((1, tb, D_ATTN), lambda b, i: (b, i, 0)),
        pl.BlockSpec((N_GROUPS, 1, nk, SSD_W), lambda b, i: (0, b, i, 0)),
    ]
    args = [x, a, yg]
    if prev is not None:
        in_specs.append(pl.BlockSpec((1, pad, 2 * D_FF), lambda b, i: (b, 0, 0)))
        args.append(prev)
    weights = [w['w_glu'], w['onorm_a'], w['onorm_s'], w['w_out'], w['norm2'], w['w_up'], w['conv_w'],
               w['conv_b'], w['w_down'], w['final_g']]
    in_specs += [_const(t.shape) for t in weights]
    kern = functools.partial(_out_kernel, strided=strided, cs=cs, has_prev=prev is not None)
    return pl.pallas_call(
        kern,
        grid=grid,
        in_specs=in_specs,
        out_specs=[pl.BlockSpec((1, tb, D_MODEL), lambda b, i: (b, i, 0)),
                   pl.BlockSpec((1, pad, 2 * D_FF), lambda b, i: (b, 0, 0))],
        out_shape=[jax.ShapeDtypeStruct((nb, length, D_MODEL), F32),
                   jax.ShapeDtypeStruct((nb, pad, 2 * D_FF), F32)],
        scratch_shapes=[pltpu.VMEM((D_SSM // LANES, tb, LANES), F32), pltpu.VMEM((pad + tb, 2 * D_FF), F32)],
        compiler_params=pltpu.CompilerParams(dimension_semantics=("arbitrary", "arbitrary"),
                                             vmem_limit_bytes=VMEM_LIMIT),
        name="layer_out",
    )(*args, *weights)


def kernel(x_prompt, x_sample, cache_k, cache_v, state_ssm_re, state_ssm_im, state_conv, norm1_g, w_in, attn_sinks, ssm_A_re, ssm_A_im, ssm_log_dt, ssm_B_re, ssm_B_im, ssm_C_re, ssm_C_im, ssm_D, w_glu, onorm_attn_g, onorm_ssm_g, w_out, norm2_g, w_up, conv_w, conv_b, w_down, final_g):
    assert norm1_g.shape[0] == 1, "one layer"
    nb, length, _ = x_prompt.shape
    ns, nt, _ = x_sample.shape
    n_cache = cache_k.shape[2]
    assert nt == SSD_T and length % TB_IN == 0 and length % TB_OUT == 0 and (length // SSD_T) % NC_SSD == 0

    m_mat, wd_mat, r_mat, tab, abar = _ssd_tables(ssm_A_re[0], ssm_A_im[0], ssm_log_dt[0], ssm_B_re[0],
                                                  ssm_B_im[0], ssm_C_re[0], ssm_C_im[0], ssm_D[0])
    sinks = attn_sinks[0]
    g1 = norm1_g
    win = w_in[0].astype(BF16)
    w = dict(w_glu=w_glu[0].astype(BF16), onorm_a=onorm_attn_g, onorm_s=onorm_ssm_g, w_out=w_out[0].astype(BF16),
             norm2=norm2_g, w_up=w_up[0].astype(BF16), conv_w=conv_w[0], conv_b=conv_b,
             w_down=w_down[0].astype(BF16), final_g=final_g[None, :])
    smem = pl.BlockSpec(memory_space=pltpu.SMEM)

    cos_p, sin_p = _rope_tables(jnp.arange(length))
    nk = TB_IN // SSD_T
    a_p, ug_p, k_p, v_p = pl.pallas_call(
        _prompt_in_kernel,
        grid=(nb, length // TB_IN),
        in_specs=[smem,
                  pl.BlockSpec((1, TB_IN, D_MODEL), lambda b, i: (b, i, 0)),
                  _const((1, D_MODEL)), _const((D_MODEL, D_IN)),
                  pl.BlockSpec((TB_IN, LANES), lambda b, i: (i, 0)),
                  pl.BlockSpec((TB_IN, LANES), lambda b, i: (i, 0))],
        out_specs=[pl.BlockSpec((1, TB_IN, D_ATTN), lambda b, i: (b, i, 0)),
                   pl.BlockSpec((N_GROUPS, 1, nk, SSD_W), lambda b, i: (0, b, i, 0)),
                   pl.BlockSpec((1, WINDOW, KV_W), lambda b, i: (b, 0, 0)),
                   pl.BlockSpec((1, WINDOW, KV_W), lambda b, i: (b, 0, 0))],
        out_shape=[jax.ShapeDtypeStruct((nb, length, D_ATTN), F32),
                   jax.ShapeDtypeStruct((N_GROUPS, nb, length // SSD_T, SSD_W), BF16),
                   jax.ShapeDtypeStruct((nb, WINDOW, KV_W), F32),
                   jax.ShapeDtypeStruct((nb, WINDOW, KV_W), F32)],
        scratch_shapes=[pltpu.VMEM((TB_IN, D_ATTN), F32),
                        pltpu.VMEM((2 * N_KV_HEADS, WINDOW + TB_IN, LANES), BF16),
                        pltpu.VMEM((WINDOW + TB_IN, LANES), F32),
                        pltpu.VMEM((2, N_KV_HEADS, 2, LANES, WINDOW + TB_IN), BF16),
                        pltpu.VMEM((D_SSM // LANES, TB_IN, LANES), F32)],
        compiler_params=pltpu.CompilerParams(dimension_semantics=("arbitrary", "arbitrary"),
                                             vmem_limit_bytes=VMEM_LIMIT),
        name="prompt_in",
    )(sinks, x_prompt, g1, win, cos_p, sin_p)

    yg_p, sfin_p = pl.pallas_call(
        _prompt_ssd_kernel,
        grid=(length // SSD_T // NC_SSD,),
        in_specs=[pl.BlockSpec((N_GROUPS, nb, NC_SSD, SSD_W), lambda i: (0, 0, i, 0)),
                  _const(m_mat.shape), _const(wd_mat.shape), _const(r_mat.shape), _const(tab.shape),
                  _const(abar.shape)],
        out_specs=[pl.BlockSpec((N_GROUPS, nb, NC_SSD, SSD_W), lambda i: (0, 0, i, 0)),
                   pl.BlockSpec((nb, N_GROUPS, 1, STATE_W), lambda i: (0, 0, 0, 0))],
        out_shape=[jax.ShapeDtypeStruct((N_GROUPS, nb, length // SSD_T, SSD_W), F32),
                   jax.ShapeDtypeStruct((nb, N_GROUPS, 1, STATE_W), F32)],
        scratch_shapes=[pltpu.VMEM((N_GROUPS, nb * NC_SSD, STATE_W), F32),
                        pltpu.VMEM((N_GROUPS, nb * NC_SSD, STATE_W), F32),
                        pltpu.VMEM((nb, N_GROUPS, 1, STATE_W), F32)],
        compiler_params=pltpu.CompilerParams(dimension_semantics=("arbitrary",), vmem_limit_bytes=VMEM_LIMIT),
        name="prompt_ssd",
    )(ug_p, m_mat, wd_mat, r_mat, tab, abar)

    y_p, cst_p = _out_call(x_prompt, a_p, yg_p, None, w, tb=TB_OUT, strided=True, cs=1, pad=SUBLANES)

    rows = ns * nt
    xs = x_sample.transpose(1, 0, 2).reshape(rows, D_MODEL)
    cos_s, sin_s = _rope_tables(PAST_LEN + jnp.arange(rows) // ns)
    kc = cache_k[0].reshape(ns * n_cache, KV_W)
    vc = cache_v[0].reshape(ns * n_cache, KV_W)
    h0 = jnp.concatenate([state_ssm_re[0], state_ssm_im[0]], axis=-1).transpose(1, 0, 2)
    dec = functools.partial(_decode_in_kernel, n_streams=ns, n_steps=nt, n_cache=n_cache)
    dec_in = [xs, g1, win, cos_s, sin_s, kc, vc, h0, m_mat, wd_mat, r_mat, abar]
    a_s, kn, vn, yg_s, sn = pl.pallas_call(
        dec,
        grid=(1,),
        in_specs=[smem] + [_const(t.shape) for t in dec_in],
        out_specs=[_whole((rows, D_ATTN)), _whole((rows, KV_W)), _whole((rows, KV_W)),
                   _whole((N_GROUPS, ns, SSD_W)), _whole((N_GROUPS, ns, STATE_W))],
        out_shape=[jax.ShapeDtypeStruct((rows, D_ATTN), F32), jax.ShapeDtypeStruct((rows, KV_W), F32),
                   jax.ShapeDtypeStruct((rows, KV_W), F32), jax.ShapeDtypeStruct((N_GROUPS, ns, SSD_W), F32),
                   jax.ShapeDtypeStruct((N_GROUPS, ns, STATE_W), F32)],
        scratch_shapes=[pltpu.VMEM((D_SSM // LANES, rows, LANES), F32), pltpu.VMEM((N_GROUPS, ns, SSD_W), F32)],
        compiler_params=pltpu.CompilerParams(dimension_semantics=("arbitrary",), vmem_limit_bytes=VMEM_LIMIT),
        name="decode_in",
    )(sinks, *dec_in)

    cs = ns
    prev = state_conv[0].transpose(1, 0, 2).reshape(1, (CONV_W - 1) * cs, 2 * D_FF)
    y_s, cst_s = _out_call(xs[None], a_s[None], yg_s[:, None], prev, w, tb=rows, strided=False, cs=cs,
                           pad=(CONV_W - 1) * cs)

    kv5 = lambda t, n: t.reshape(1, n, -1, N_KV_HEADS, HEAD_DIM)
    y_sample = y_s.reshape(nt, ns, D_MODEL).transpose(1, 0, 2)
    kn_b = kn.reshape(nt, ns, KV_W).transpose(1, 0, 2)
    vn_b = vn.reshape(nt, ns, KV_W).transpose(1, 0, 2)
    k_new = jnp.concatenate([cache_k[0].reshape(ns, n_cache, KV_W), kn_b], axis=1)[:, -n_cache:]
    v_new = jnp.concatenate([cache_v[0].reshape(ns, n_cache, KV_W), vn_b], axis=1)[:, -n_cache:]
    conv_p = cst_p[:, SUBLANES - (CONV_W - 1):, :][None]
    conv_s = cst_s.reshape(CONV_W - 1, ns, 2 * D_FF).transpose(1, 0, 2)[None]
    sn_b = sn.transpose(1, 0, 2)
    return (y_p, y_sample,
            kv5(k_p, nb), kv5(v_p, nb),
            sfin_p[None, :, :, 0, :SSM_STATE], sfin_p[None, :, :, 0, SSM_STATE:], conv_p,
            kv5(k_new, ns), kv5(v_new, ns),
            sn_b[None, :, :, :SSM_STATE], sn_b[None, :, :, SSM_STATE:], conv_s)
```

```python
import functools
import math

import jax
import jax.numpy as jnp
from jax import lax
from jax.experimental import pallas as pl
from jax.experimental.pallas import tpu as pltpu

F32 = jnp.float32
BF16 = jnp.bfloat16

D_MODEL = 1024
CHUNK = 64
D_ATTN = 512
D_SSM = 512
HEAD_DIM = 64
N_HEADS = 8
N_KV_HEADS = 2
KV_W = N_KV_HEADS * HEAD_DIM
WINDOW = 128
N_WIN_CHUNKS = WINDOW // CHUNK
ROPE_THETA = 10000.0
SSM_GROUP = 16
N_GROUPS = D_SSM // SSM_GROUP
SSM_STATE = 64
D_FF = 2816
CONV_W = 3
EPS = 1e-6
D_IN = D_ATTN + 2 * KV_W + D_SSM
PAST_LEN = 2048

SSD_T = 16
LANES = 128
SUBLANES = 8
GROUPS_PER_VREG = LANES // SSM_GROUP
SSD_W = SSD_T * SSM_GROUP
STATE_W = 2 * SSM_STATE

TB_IN = 512
TB_OUT = 512
SUB_OUT = 256
NC_SSD = 64
FF_TILE = 256
GROUP_UNROLL = 8
VMEM_LIMIT = 56 * 1024 * 1024


def _log2(n):
    assert n > 0 and n & (n - 1) == 0, n
    return n.bit_length() - 1


def _pdiv(x, n):
    return lax.shift_right_arithmetic(x, jnp.int32(_log2(n)))


def _pmod(x, n):
    return x & (n - 1)


def _rms(x, g):
    return x * lax.rsqrt(jnp.mean(x * x, axis=-1, keepdims=True) + EPS) * g


def _rope(xb, cos, sin_signed, first_half):
    partner = jnp.where(first_half, pltpu.roll(xb, LANES - HEAD_DIM // 2, 1), pltpu.roll(xb, HEAD_DIM // 2, 1))
    return xb * cos + partner * sin_signed


def _group_transpose8(xs, lane_group):
    xs = list(xs)
    for d in (4, 2, 1):
        keep = (lane_group & d) == 0
        nxt = list(xs)
        for i in range(GROUPS_PER_VREG):
            if i & d:
                continue
            a, b = xs[i], xs[i | d]
            nxt[i] = jnp.where(keep, a, pltpu.roll(b, SSM_GROUP * d, 1))
            nxt[i | d] = jnp.where(keep, pltpu.roll(a, LANES - SSM_GROUP * d, 1), b)
        xs = nxt
    return xs


def _row_sel(s, nk, strided):
    if strided:
        return pl.ds(s, nk, stride=SSD_T)
    return pl.ds(s * nk, nk)


def _to_groups(u_ref, put, nk, strided):
    lane_group = _pdiv(lax.broadcasted_iota(jnp.int32, (nk, LANES), 1), SSM_GROUP)
    for j in range(D_SSM // LANES):
        for half in range(SSD_T // GROUPS_PER_VREG):
            xs = [u_ref[j, _row_sel(GROUPS_PER_VREG * half + sl, nk, strided), :]
                  for sl in range(GROUPS_PER_VREG)]
            ws = _group_transpose8(xs, lane_group)
            for gq in range(GROUPS_PER_VREG):
                put(GROUPS_PER_VREG * j + gq, half, ws[gq].astype(BF16))


def _from_groups(get, y_ref, nk, strided):
    lane_group = _pdiv(lax.broadcasted_iota(jnp.int32, (nk, LANES), 1), SSM_GROUP)
    for j in range(D_SSM // LANES):
        for half in range(SSD_T // GROUPS_PER_VREG):
            ws = [get(GROUPS_PER_VREG * j + gq, half).astype(F32) for gq in range(GROUPS_PER_VREG)]
            xs = _group_transpose8(ws, lane_group)
            for sl in range(GROUPS_PER_VREG):
                y_ref[j, _row_sel(GROUPS_PER_VREG * half + sl, nk, strided), :] = xs[sl]


def _cmul(re_full, im_signed, z):
    return re_full * z + im_signed * pltpu.roll(z, SSM_STATE, z.ndim - 1)


def _in_proj(x, g1_ref, win_ref, cos_ref, sin_ref):
    rows = x.shape[0]
    hn = _rms(x, g1_ref[...]).astype(BF16)
    proj = jnp.dot(hn, win_ref[...], preferred_element_type=F32)
    cos = cos_ref[...]
    sin = sin_ref[...]
    lane = lax.broadcasted_iota(jnp.int32, (rows, LANES), 1)
    first_half = _pmod(lane, HEAD_DIM) < (HEAD_DIM // 2)
    scale = HEAD_DIM ** -0.5
    qs = [_rope(proj[:, LANES * j:LANES * (j + 1)], cos, sin, first_half) * scale for j in range(D_ATTN // LANES)]
    k = _rope(proj[:, D_ATTN:D_ATTN + KV_W], cos, sin, first_half)
    v = proj[:, D_ATTN + KV_W:D_ATTN + 2 * KV_W]
    u = proj[:, D_ATTN + 2 * KV_W:]
    return qs, k, v, u


def _kv_variants(k, v):
    lane = lax.broadcasted_iota(jnp.int32, k.shape, 1)
    lo = lane < HEAD_DIM
    out = []
    for t in (k, v):
        tr = pltpu.roll(t, HEAD_DIM, 1)
        zero = jnp.zeros_like(t)
        out += [jnp.where(lo, t, zero), jnp.where(lo, zero, tr), jnp.where(lo, tr, zero), jnp.where(lo, zero, t)]
    return [o.astype(BF16) for o in out]


def _k_variants(k):
    lo = lax.broadcasted_iota(jnp.int32, k.shape, 1) < HEAD_DIM
    kr = pltpu.roll(k, HEAD_DIM, 1)
    zero = jnp.zeros_like(k)
    out = [jnp.where(lo, k, zero), jnp.where(lo, zero, kr), jnp.where(lo, kr, zero), jnp.where(lo, zero, k)]
    return [o.astype(BF16) for o in out]


def _attend_t(a_bf, kwin, vt_win, valid, sink_row):
    s = lax.dot_general(kwin, a_bf, (((1,), (1,)), ((), ())), preferred_element_type=F32)
    if valid is not None:
        s = jnp.where(valid, s, -jnp.inf)
    m = jnp.maximum(jnp.max(s, axis=0, keepdims=True), sink_row)
    e = jnp.exp(s - m)
    den = jnp.sum(e, axis=0, keepdims=True) + jnp.exp(sink_row - m)
    o = jnp.dot(vt_win, e.astype(BF16), preferred_element_type=F32)
    return o * (1.0 / den)


def _attend(a_bf, kvar, vvar, valid, sink_col):
    s = lax.dot_general(a_bf, kvar, (((1,), (1,)), ((), ())), preferred_element_type=F32)
    for cond in valid:
        s = jnp.where(cond, s, -jnp.inf)
    m = jnp.maximum(jnp.max(s, axis=-1, keepdims=True), sink_col)
    e = jnp.exp(s - m)
    den = jnp.sum(e, axis=-1, keepdims=True) + jnp.exp(sink_col - m)
    o = jnp.dot(e.astype(BF16), vvar, preferred_element_type=F32)
    return o * (1.0 / den)


def _prompt_in_kernel(sinks_ref, x_ref, g1_ref, win_ref, cos_ref, sin_ref,
                      a_ref, ug_ref, kp_ref, vp_ref, q_s, k_s, vb_s, vt_s, u_s):
    tb = x_ref.shape[1]
    nk = tb // SSD_T
    blk = pl.program_id(1)
    qs, k, v, u = _in_proj(x_ref[0], g1_ref, win_ref, cos_ref, sin_ref)
    for j, qb in enumerate(qs):
        q_s[:, LANES * j:LANES * (j + 1)] = qb
    for j in range(D_SSM // LANES):
        u_s[j] = u[:, LANES * j:LANES * (j + 1)]

    nbuf = WINDOW + tb

    @pl.when(blk == 0)
    def _():
        k_s[:, 0:WINDOW, :] = jnp.zeros((4, WINDOW, LANES), BF16)
        vb_s[0:WINDOW, :] = jnp.zeros((WINDOW, LANES), F32)

    @pl.when(blk > 0)
    def _():
        k_s[:, 0:WINDOW, :] = k_s[:, tb:nbuf, :]
        vb_s[0:WINDOW, :] = vb_s[tb:nbuf, :]

    for idx, arr in enumerate(_k_variants(k)):
        k_s[idx, WINDOW:nbuf, :] = arr
    vb_s[WINDOW:nbuf, :] = v

    @pl.when(blk == pl.num_programs(1) - 1)
    def _():
        kp_ref[0] = k[tb - WINDOW:, :]
        vp_ref[0] = v[tb - WINDOW:, :]

    vt0 = vb_s[...].T
    vt1 = pltpu.roll(vt0, nbuf - CHUNK, 1)
    zrows = jnp.zeros((HEAD_DIM, nbuf), F32)
    for sh, vt in ((0, vt0), (1, vt1)):
        for g in range(N_KV_HEADS):
            rows = vt[HEAD_DIM * g:HEAD_DIM * (g + 1), :]
            vt_s[sh, g, 0] = jnp.concatenate([rows, zrows], axis=0).astype(BF16)
            vt_s[sh, g, 1] = jnp.concatenate([zrows, rows], axis=0).astype(BF16)

    nkeys = WINDOW + CHUNK
    key_row = lax.broadcasted_iota(jnp.int32, (nkeys, 1), 0)
    q_lane = lax.broadcasted_iota(jnp.int32, (1, LANES), 1)
    for i in range(tb // CHUNK):
        r0 = i * CHUNK
        sh = i % 2
        base = r0 - sh * CHUNK
        valid = key_row >= (WINDOW - r0 - blk * tb) if i < N_WIN_CHUNKS else None
        for g in range(N_KV_HEADS):
            qa = q_s[r0:r0 + CHUNK, 2 * LANES * g:2 * LANES * g + LANES]
            qb = q_s[r0:r0 + CHUNK, 2 * LANES * g + LANES:2 * LANES * (g + 1)]
            a_bf = jnp.concatenate([qa, qb], axis=0).astype(BF16)
            acc = None
            for par in range(2):
                h0 = 4 * g + par
                sink_row = jnp.where(q_lane < CHUNK, sinks_ref[h0], sinks_ref[h0 + 2])
                o = _attend_t(a_bf, k_s[2 * g + par, r0:r0 + nkeys, :],
                              vt_s[sh, g, par, :, base:base + nkeys], valid, sink_row)
                acc = o if acc is None else acc + o
            at = acc.T
            a_ref[0, r0:r0 + CHUNK, 2 * LANES * g:2 * LANES * g + LANES] = at[:CHUNK]
            a_ref[0, r0:r0 + CHUNK, 2 * LANES * g + LANES:2 * LANES * (g + 1)] = at[CHUNK:]

    def put(g, half, val):
        ug_ref[g, 0, :, LANES * half:LANES * (half + 1)] = val

    _to_groups(u_s, put, nk, strided=True)


def _prompt_ssd_kernel(ug_ref, m_ref, wd_ref, r_ref, tab_ref, abar_ref,
                       yg_ref, sfin_ref, d_s, sp_s, carry_s):
    nb = ug_ref.shape[1]
    nc = ug_ref.shape[2]
    step = pl.program_id(0)

    @pl.when(step == 0)
    def _():
        carry_s[...] = jnp.zeros(carry_s.shape, F32)

    def state_in(g, c):
        ub = ug_ref[g].reshape(nb * nc, SSD_W)
        d_s[g] = jnp.dot(ub, wd_ref[g], preferred_element_type=F32)
        return c

    lax.fori_loop(0, N_GROUPS, state_in, 0, unroll=GROUP_UNROLL)

    pw = [tab_ref[i] for i in range(8)]
    are, aim = abar_ref[0][:, None, :], abar_ref[1][:, None, :]
    kio = lax.broadcasted_iota(jnp.int32, (N_GROUPS, nc, STATE_W), 1)
    for b in range(nb):
        dg = d_s[:, b * nc:(b + 1) * nc, :]
        s_in = carry_s[b]
        dsh = jnp.where(kio == 0, s_in, pltpu.roll(dg, 1, 1))
        c = jnp.zeros((N_GROUPS, 1, STATE_W), F32)
        for j in range(nc // SUBLANES):
            x = dsh[:, SUBLANES * j:SUBLANES * (j + 1), :]
            for lvl in range(3):
                xs = pltpu.roll(x, 1 << lvl, 1)
                x = x + _cmul(pw[2 * lvl], pw[2 * lvl + 1], xs)
            cb = jnp.broadcast_to(c, x.shape)
            h = x + _cmul(pw[6], pw[7], cb)
            sp_s[:, b * nc + SUBLANES * j:b * nc + SUBLANES * (j + 1), :] = h
            c = h[:, SUBLANES - 1:SUBLANES, :]
        carry_s[b] = _cmul(are, aim, c) + dg[:, nc - 1:nc, :]

    def readout(g, c):
        ub = ug_ref[g].reshape(nb * nc, SSD_W)
        y = jnp.dot(ub, m_ref[g], preferred_element_type=F32)
        y = y + lax.dot_general(sp_s[g].astype(BF16), r_ref[g], (((1,), (1,)), ((), ())),
                                preferred_element_type=F32)
        yg_ref[g] = y.reshape(nb, nc, SSD_W).astype(BF16)
        return c

    lax.fori_loop(0, N_GROUPS, readout, 0, unroll=GROUP_UNROLL)
    sfin_ref[...] = carry_s[...]


def _out_kernel(*refs, strided, cs, has_prev):
    if has_prev:
        x_ref, a_ref, yg_ref, prev_ref = refs[:4]
        refs = refs[4:]
    else:
        x_ref, a_ref, yg_ref = refs[:3]
        prev_ref = None
        refs = refs[3:]
    (wglu_ref, ga_ref, gs_ref, wout_ref, g2_ref, wup_ref, cw_ref, cb_ref, wdown_ref, gf_ref,
     y_ref, cst_ref, ys_s, ext_s) = refs
    tb = x_ref.shape[1]
    n_sub, _, sb, _ = ys_s.shape
    nks = sb // SSD_T
    pad = ext_s.shape[0] - sb
    blk = pl.program_id(1)
    assert n_sub * sb == tb

    @pl.when(blk == 0)
    def _():
        if has_prev:
            ext_s[0:pad, :] = prev_ref[0]
        else:
            ext_s[0:pad, :] = jnp.zeros((pad, 2 * D_FF), F32)

    def mix(i):
        rows = slice(i * sb, (i + 1) * sb)
        _from_groups(lambda g, half: yg_ref[g, 0, i * nks:(i + 1) * nks, LANES * half:LANES * (half + 1)],
                     ys_s.at[i], nks, strided)
        yv = jnp.concatenate([ys_s[i, j] for j in range(D_SSM // LANES)], axis=-1)
        z = 0.5 * yv * (1.0 + jnp.tanh(math.sqrt(2.0 / math.pi) * (yv + 0.044715 * (yv * yv * yv))))
        gate = jnp.dot(z.astype(BF16), wglu_ref[...], preferred_element_type=F32)
        s_out = z * jax.nn.sigmoid(gate)
        na = _rms(a_ref[0, rows, :], ga_ref[...]).astype(BF16)
        ns = _rms(s_out, gs_ref[...]).astype(BF16)
        x1 = x_ref[0, rows, :] + jnp.dot(na, wout_ref[0:D_ATTN, :], preferred_element_type=F32)
        x1 = x1 + jnp.dot(ns, wout_ref[D_ATTN:, :], preferred_element_type=F32)
        return x1, _rms(x1, g2_ref[...]).astype(BF16)

    def tile_cols(j, half):
        return slice(half * D_FF + j * FF_TILE, half * D_FF + (j + 1) * FF_TILE)

    def conv_ffn(i, x1, h2):
        def up_proj(j):
            ups = []
            for half in range(2):
                up = jnp.dot(h2, wup_ref[:, tile_cols(j, half)], preferred_element_type=F32)
                ext_s[pad:pad + sb, tile_cols(j, half)] = up
                ups.append(up)
            return ups

        def down_proj(act, j):
            return jnp.dot(act, wdown_ref[j * FF_TILE:(j + 1) * FF_TILE, :], preferred_element_type=F32)

        n_tiles = D_FF // FF_TILE
        acc = jnp.zeros((sb, D_MODEL), F32)
        ups = up_proj(0)
        act_prev = None
        for j in range(n_tiles):
            nxt = up_proj(j + 1) if j + 1 < n_tiles else None
            if act_prev is not None:
                acc = acc + down_proj(act_prev, j - 1)
            parts = []
            for half in range(2):
                cols = tile_cols(j, half)
                conv = (ext_s[pad - 2 * cs:pad - 2 * cs + sb, cols] * cw_ref[0:1, cols]
                        + ext_s[pad - cs:pad - cs + sb, cols] * cw_ref[1:2, cols]
                        + ups[half] * cw_ref[2:3, cols] + cb_ref[:, cols])
                ext_s[0:pad, cols] = ext_s[sb:sb + pad, cols]
                parts.append(conv)
            cg, cv = parts
            act_prev = (cg * jax.nn.sigmoid(cg) * cv).astype(BF16)
            ups = nxt
        acc = acc + down_proj(act_prev, n_tiles - 1)
        y_ref[0, i * sb:(i + 1) * sb, :] = _rms(x1 + acc, gf_ref[...])

    mixed = mix(0)
    for i in range(n_sub):
        nxt_mixed = mix(i + 1) if i + 1 < n_sub else None
        conv_ffn(i, *mixed)
        mixed = nxt_mixed
    cst_ref[0] = ext_s[0:pad, :]


def _decode_in_kernel(sinks_ref, x_ref, g1_ref, win_ref, cos_ref, sin_ref, kc_ref, vc_ref, h0_ref,
                      m_ref, wd_ref, r_ref, abar_ref,
                      a_ref, kn_ref, vn_ref, yg_ref, sn_ref, u_s, ug_s, *, n_streams, n_steps, n_cache):
    rows = n_streams * n_steps
    qs, k, v, u = _in_proj(x_ref[...], g1_ref, win_ref, cos_ref, sin_ref)
    kn_ref[...] = k
    vn_ref[...] = v
    for j in range(D_SSM // LANES):
        u_s[j] = u[:, LANES * j:LANES * (j + 1)]
    ncache_rows = n_streams * n_cache
    nkeys = ncache_rows + rows
    kall = jnp.concatenate([kc_ref[...], k], axis=0)
    vall = jnp.concatenate([vc_ref[...], v], axis=0)
    variants = _kv_variants(kall, vall)

    colv = lax.broadcasted_iota(jnp.int32, (1, nkeys), 1)
    is_new = colv >= ncache_rows
    cnew = colv - ncache_rows
    k_stream = jnp.where(is_new, _pmod(cnew, n_streams), _pdiv(colv, n_cache))
    k_pos = jnp.where(is_new, PAST_LEN + _pdiv(cnew, n_streams), PAST_LEN - n_cache + _pmod(colv, n_cache))
    k_chunk = _pdiv(k_pos, CHUNK)
    rowv = _pmod(lax.broadcasted_iota(jnp.int32, (2 * rows, 1), 0), rows)
    q_stream = _pmod(rowv, n_streams)
    q_chunk = _pdiv(PAST_LEN + _pdiv(rowv, n_streams), CHUNK)
    ok_col = jnp.where(k_pos >= 0, k_stream, -1)
    d_chunk = q_chunk - k_chunk
    valid = [q_stream == ok_col,
             lax.bitcast_convert_type(d_chunk, jnp.uint32) <= jnp.uint32(N_WIN_CHUNKS)]
    top = lax.broadcasted_iota(jnp.int32, (2 * rows, 1), 0) < rows

    for g in range(N_KV_HEADS):
        a_bf = jnp.concatenate([qs[2 * g], qs[2 * g + 1]], axis=0).astype(BF16)
        acc = jnp.zeros((2 * rows, LANES), F32)
        for par in range(2):
            h0 = 4 * g + par
            sink_col = jnp.where(top, sinks_ref[h0], sinks_ref[h0 + 2])
            acc = acc + _attend(a_bf, variants[2 * g + par], variants[4 + 2 * g + par], valid, sink_col)
        a_ref[:, 2 * LANES * g:2 * LANES * g + LANES] = acc[:rows]
        a_ref[:, 2 * LANES * g + LANES:2 * LANES * (g + 1)] = acc[rows:]

    def put(g, half, val):
        ug_s[g, :, LANES * half:LANES * (half + 1)] = val

    _to_groups(u_s, put, n_streams, strided=False)
    def group_body(g, c):
        ub = ug_s[g]
        h0g = h0_ref[g]
        d = jnp.dot(ub, wd_ref[g], preferred_element_type=F32)
        sn_ref[g] = _cmul(abar_ref[0, pl.ds(g, 1), :], abar_ref[1, pl.ds(g, 1), :], h0g) + d
        y = jnp.dot(ub, m_ref[g], preferred_element_type=F32)
        y = y + lax.dot_general(h0g.astype(BF16), r_ref[g], (((1,), (1,)), ((), ())),
                                preferred_element_type=F32)
        yg_ref[g] = y.astype(BF16)
        return c

    lax.fori_loop(0, N_GROUPS, group_body, 0, unroll=GROUP_UNROLL)


def _ssd_tables(a_re, a_im, log_dt, b_re, b_im, c_re, c_im, d_skip):
    t = SSD_T
    dt = jnp.exp(log_dt)[:, None]
    lr, li = dt * a_re, dt * a_im

    def powers(n):
        n = n.astype(F32)[None, :, None]
        mag = jnp.exp(n * lr[:, None, :])
        return mag * jnp.cos(n * li[:, None, :]), mag * jnp.sin(n * li[:, None, :])

    def cmul4(xr, xi, pr, pi):
        xr, xi, pr, pi = xr[:, None], xi[:, None], pr[:, :, None], pi[:, :, None]
        return xr * pr - xi * pi, xr * pi + xi * pr

    def rows(re, im):
        return jnp.concatenate([re, im], axis=-1).reshape(N_GROUPS, SSD_W, STATE_W)

    steps = jnp.arange(t)
    ar, ai = powers(jnp.ones((1,)))
    ar, ai = ar[:, 0], ai[:, 0]
    den = a_re * a_re + a_im * a_im
    nr, ni = ar - 1.0, ai
    fr, fi = (nr * a_re + ni * a_im) / den, (ni * a_re - nr * a_im) / den
    btr, bti = b_re.transpose(0, 2, 1), b_im.transpose(0, 2, 1)
    bbr = fr[:, None, :] * btr - fi[:, None, :] * bti
    bbi = fr[:, None, :] * bti + fi[:, None, :] * btr

    xr, xi = cmul4(bbr, bbi, *powers(-steps))
    yr, yi = cmul4(c_re, c_im, *powers(steps))
    m = jnp.einsum('gik,gjk->gij', rows(xr, -xi), rows(yr, yi), precision=lax.Precision.HIGHEST)
    idx = jnp.arange(SSD_W)
    causal = (idx[None, :] // SSM_GROUP) >= (idx[:, None] // SSM_GROUP)
    skip = jnp.tile(d_skip, (1, t))[:, None, :] * jnp.eye(SSD_W, dtype=F32)[None]
    m_mat = jnp.where(causal[None], m, 0.0) + skip
    wd_mat = rows(*cmul4(bbr, bbi, *powers(t - 1 - steps)))
    er, ei = cmul4(c_re, c_im, *powers(steps + 1))
    rt_mat = rows(er, -ei)

    def full(n):
        pr, pi = powers(n)
        return jnp.concatenate([pr, pr], -1), jnp.concatenate([-pi, pi], -1)

    sub = jnp.arange(SUBLANES)
    tabs = []
    for lvl in range(3):
        d = 1 << lvl
        fre, fim = full(jnp.full((SUBLANES,), t * d))
        keep = (sub >= d)[None, :, None]
        tabs += [jnp.where(keep, fre, 0.0), jnp.where(keep, fim, 0.0)]
    tabs += list(full(t * (sub + 1)))
    tab = jnp.stack(tabs)
    are, aim = full(jnp.full((1,), t))
    abar = jnp.stack([are[:, 0], aim[:, 0]])
    return m_mat.astype(BF16), wd_mat.astype(BF16), rt_mat.astype(BF16), tab, abar


def _rope_tables(pos):
    half = HEAD_DIM // 2
    inv = ROPE_THETA ** (-jnp.arange(half, dtype=F32) / half)
    ang = pos.astype(F32)[:, None] * inv[None, :]
    cos, sin = jnp.cos(ang), jnp.sin(ang)
    cos_t = jnp.concatenate([cos, cos, cos, cos], axis=-1)
    sin_t = jnp.concatenate([-sin, sin, -sin, sin], axis=-1)
    return cos_t, sin_t


def _const(shape):
    nd = len(shape)
    return pl.BlockSpec(shape, lambda *_: (0,) * nd, pipeline_mode=pl.Buffered(1))


def _whole(shape):
    nd = len(shape)
    return pl.BlockSpec(shape, lambda *_: (0,) * nd)


def _out_call(x, a, yg, prev, w, *, tb, sb, strided, cs, pad):
    nb, length, _ = x.shape
    nk = tb // SSD_T
    grid = (nb, length // tb)
    in_specs = [
        pl.BlockSpec((1, tb, D_MODEL), lambda b, i: (b, i, 0)),
        pl.BlockSpec((1, tb, D_ATTN), lambda b, i: (b, i, 0)),
        pl.BlockSpec((N_GROUPS, 1, nk, SSD_W), lambda b, i: (0, b, i, 0)),
    ]
    args = [x, a, yg]
    if prev is not None:
        in_specs.append(pl.BlockSpec((1, pad, 2 * D_FF), lambda b, i: (b, 0, 0)))
        args.append(prev)
    weights = [w['w_glu'], w['onorm_a'], w['onorm_s'], w['w_out'], w['norm2'], w['w_up'], w['conv_w'],
               w['conv_b'], w['w_down'], w['final_g']]
    in_specs += [_const(t.shape) for t in weights]
    kern = functools.partial(_out_kernel, strided=strided, cs=cs, has_prev=prev is not None)
    return pl.pallas_call(
        kern,
        grid=grid,
        in_specs=in_specs,
        out_specs=[pl.BlockSpec((1, tb, D_MODEL), lambda b, i: (b, i, 0)),
                   pl.BlockSpec((1, pad, 2 * D_FF), lambda b, i: (b, 0, 0))],
        out_shape=[jax.ShapeDtypeStruct((nb, length, D_MODEL), F32),
                   jax.ShapeDtypeStruct((nb, pad, 2 * D_FF), F32)],
        scratch_shapes=[pltpu.VMEM((tb // sb, D_SSM // LANES, sb, LANES), F32),
                        pltpu.VMEM((pad + sb, 2 * D_FF), F32)],
        compiler_params=pltpu.CompilerParams(dimension_semantics=("arbitrary", "arbitrary"),
                                             vmem_limit_bytes=VMEM_LIMIT),
        name="layer_out",
    )(*args, *weights)


def kernel(x_prompt, x_sample, cache_k, cache_v, state_ssm_re, state_ssm_im, state_conv, norm1_g, w_in, attn_sinks, ssm_A_re, ssm_A_im, ssm_log_dt, ssm_B_re, ssm_B_im, ssm_C_re, ssm_C_im, ssm_D, w_glu, onorm_attn_g, onorm_ssm_g, w_out, norm2_g, w_up, conv_w, conv_b, w_down, final_g):
    assert norm1_g.shape[0] == 1, "one layer"
    nb, length, _ = x_prompt.shape
    ns, nt, _ = x_sample.shape
    n_cache = cache_k.shape[2]
    assert nt == SSD_T and length % TB_IN == 0 and length % TB_OUT == 0 and (length // SSD_T) % NC_SSD == 0

    m_mat, wd_mat, r_mat, tab, abar = _ssd_tables(ssm_A_re[0], ssm_A_im[0], ssm_log_dt[0], ssm_B_re[0],
                                                  ssm_B_im[0], ssm_C_re[0], ssm_C_im[0], ssm_D[0])
    sinks = attn_sinks[0]
    g1 = norm1_g
    win = w_in[0].astype(BF16)
    w = dict(w_glu=w_glu[0].astype(BF16), onorm_a=onorm_attn_g, onorm_s=onorm_ssm_g, w_out=w_out[0].astype(BF16),
             norm2=norm2_g, w_up=w_up[0].astype(BF16), conv_w=conv_w[0], conv_b=conv_b,
             w_down=w_down[0].astype(BF16), final_g=final_g[None, :])
    smem = pl.BlockSpec(memory_space=pltpu.SMEM)

    cos_p, sin_p = _rope_tables(jnp.arange(length))
    nk = TB_IN // SSD_T
    a_p, ug_p, k_p, v_p = pl.pallas_call(
        _prompt_in_kernel,
        grid=(nb, length // TB_IN),
        in_specs=[smem,
                  pl.BlockSpec((1, TB_IN, D_MODEL), lambda b, i: (b, i, 0)),
                  _const((1, D_MODEL)), _const((D_MODEL, D_IN)),
                  pl.BlockSpec((TB_IN, LANES), lambda b, i: (i, 0)),
                  pl.BlockSpec((TB_IN, LANES), lambda b, i: (i, 0))],
        out_specs=[pl.BlockSpec((1, TB_IN, D_ATTN), lambda b, i: (b, i, 0)),
                   pl.BlockSpec((N_GROUPS, 1, nk, SSD_W), lambda b, i: (0, b, i, 0)),
                   pl.BlockSpec((1, WINDOW, KV_W), lambda b, i: (b, 0, 0)),
                   pl.BlockSpec((1, WINDOW, KV_W), lambda b, i: (b, 0, 0))],
        out_shape=[jax.ShapeDtypeStruct((nb, length, D_ATTN), F32),
                   jax.ShapeDtypeStruct((N_GROUPS, nb, length // SSD_T, SSD_W), BF16),
                   jax.ShapeDtypeStruct((nb, WINDOW, KV_W), F32),
                   jax.ShapeDtypeStruct((nb, WINDOW, KV_W), F32)],
        scratch_shapes=[pltpu.VMEM((TB_IN, D_ATTN), F32),
                        pltpu.VMEM((2 * N_KV_HEADS, WINDOW + TB_IN, LANES), BF16),
                        pltpu.VMEM((WINDOW + TB_IN, LANES), F32),
                        pltpu.VMEM((2, N_KV_HEADS, 2, LANES, WINDOW + TB_IN), BF16),
                        pltpu.VMEM((D_SSM // LANES, TB_IN, LANES), F32)],
        compiler_params=pltpu.CompilerParams(dimension_semantics=("arbitrary", "arbitrary"),
                                             vmem_limit_bytes=VMEM_LIMIT),
        name="prompt_in",
    )(sinks, x_prompt, g1, win, cos_p, sin_p)

    yg_p, sfin_p = pl.pallas_call(
        _prompt_ssd_kernel,
        grid=(length // SSD_T // NC_SSD,),
        in_specs=[pl.BlockSpec((N_GROUPS, nb, NC_SSD, SSD_W), lambda i: (0, 0, i, 0)),
                  _const(m_mat.shape), _const(wd_mat.shape), _const(r_mat.shape), _const(tab.shape),
                  _const(abar.shape)],
        out_specs=[pl.BlockSpec((N_GROUPS, nb, NC_SSD, SSD_W), lambda i: (0, 0, i, 0)),
                   pl.BlockSpec((nb, N_GROUPS, 1, STATE_W), lambda i: (0, 0, 0, 0))],
        out_shape=[jax.ShapeDtypeStruct((N_GROUPS, nb, length // SSD_T, SSD_W), BF16),
                   jax.ShapeDtypeStruct((nb, N_GROUPS, 1, STATE_W), F32)],
        scratch_shapes=[pltpu.VMEM((N_GROUPS, nb * NC_SSD, STATE_W), F32),
                        pltpu.VMEM((N_GROUPS, nb * NC_SSD, STATE_W), F32),
                        pltpu.VMEM((nb, N_GROUPS, 1, STATE_W), F32)],
        compiler_params=pltpu.CompilerParams(dimension_semantics=("arbitrary",), vmem_limit_bytes=VMEM_LIMIT),
        name="prompt_ssd",
    )(ug_p, m_mat, wd_mat, r_mat, tab, abar)

    y_p, cst_p = _out_call(x_prompt, a_p, yg_p, None, w, tb=TB_OUT, sb=SUB_OUT, strided=True, cs=1, pad=SUBLANES)

    rows = ns * nt
    xs = x_sample.transpose(1, 0, 2).reshape(rows, D_MODEL)
    cos_s, sin_s = _rope_tables(PAST_LEN + jnp.arange(rows) // ns)
    kc = cache_k[0].reshape(ns * n_cache, KV_W)
    vc = cache_v[0].reshape(ns * n_cache, KV_W)
    h0 = jnp.concatenate([state_ssm_re[0], state_ssm_im[0]], axis=-1).transpose(1, 0, 2)
    dec = functools.partial(_decode_in_kernel, n_streams=ns, n_steps=nt, n_cache=n_cache)
    dec_in = [xs, g1, win, cos_s, sin_s, kc, vc, h0, m_mat, wd_mat, r_mat, abar]
    a_s, kn, vn, yg_s, sn = pl.pallas_call(
        dec,
        grid=(1,),
        in_specs=[smem] + [_const(t.shape) for t in dec_in],
        out_specs=[_whole((rows, D_ATTN)), _whole((rows, KV_W)), _whole((rows, KV_W)),
                   _whole((N_GROUPS, ns, SSD_W)), _whole((N_GROUPS, ns, STATE_W))],
        out_shape=[jax.ShapeDtypeStruct((rows, D_ATTN), F32), jax.ShapeDtypeStruct((rows, KV_W), F32),
                   jax.ShapeDtypeStruct((rows, KV_W), F32), jax.ShapeDtypeStruct((N_GROUPS, ns, SSD_W), BF16),
                   jax.ShapeDtypeStruct((N_GROUPS, ns, STATE_W), F32)],
        scratch_shapes=[pltpu.VMEM((D_SSM // LANES, rows, LANES), F32), pltpu.VMEM((N_GROUPS, ns, SSD_W), BF16)],
        compiler_params=pltpu.CompilerParams(dimension_semantics=("arbitrary",), vmem_limit_bytes=VMEM_LIMIT),
        name="decode_in",
    )(sinks, *dec_in)

    cs = ns
    prev = state_conv[0].transpose(1, 0, 2).reshape(1, (CONV_W - 1) * cs, 2 * D_FF)
    y_s, cst_s = _out_call(xs[None], a_s[None], yg_s[:, None], prev, w, tb=rows, sb=rows, strided=False, cs=cs,
                           pad=(CONV_W - 1) * cs)

    kv5 = lambda t, n: t.reshape(1, n, -1, N_KV_HEADS, HEAD_DIM)
    y_sample = y_s.reshape(nt, ns, D_MODEL).transpose(1, 0, 2)
    kn_b = kn.reshape(nt, ns, KV_W).transpose(1, 0, 2)
    vn_b = vn.reshape(nt, ns, KV_W).transpose(1, 0, 2)
    k_new = jnp.concatenate([cache_k[0].reshape(ns, n_cache, KV_W), kn_b], axis=1)[:, -n_cache:]
    v_new = jnp.concatenate([cache_v[0].reshape(ns, n_cache, KV_W), vn_b], axis=1)[:, -n_cache:]
    conv_p = cst_p[:, SUBLANES - (CONV_W - 1):, :][None]
    conv_s = cst_s.reshape(CONV_W - 1, ns, 2 * D_FF).transpose(1, 0, 2)[None]
    sn_b = sn.transpose(1, 0, 2)
    return (y_p, y_sample,
            kv5(k_p, nb), kv5(v_p, nb),
            sfin_p[None, :, :, 0, :SSM_STATE], sfin_p[None, :, :, 0, SSM_STATE:], conv_p,
            kv5(k_new, ns), kv5(v_new, ns),
            sn_b[None, :, :, :SSM_STATE], sn_b[None, :, :, SSM_STATE:], conv_s)
```

```python
import functools
import math

import numpy as np
import jax
import jax.numpy as jnp
from jax import lax
from jax.experimental import pallas as pl
from jax.experimental.pallas import tpu as pltpu

F32 = jnp.float32
BF16 = jnp.bfloat16

D_MODEL = 1024
CHUNK = 64
D_ATTN = 512
D_SSM = 512
HEAD_DIM = 64
N_HEADS = 8
N_KV_HEADS = 2
KV_W = N_KV_HEADS * HEAD_DIM
WINDOW = 128
N_WIN_CHUNKS = WINDOW // CHUNK
ROPE_THETA = 10000.0
SSM_GROUP = 16
N_GROUPS = D_SSM // SSM_GROUP
SSM_STATE = 64
D_FF = 2816
CONV_W = 3
EPS = 1e-6
D_IN = D_ATTN + 2 * KV_W + D_SSM
PAST_LEN = 2048

SSD_T = 16
LANES = 128
SUBLANES = 8
GROUPS_PER_VREG = LANES // SSM_GROUP
SSD_W = SSD_T * SSM_GROUP
STATE_W = 2 * SSM_STATE

TB_IN = 512
TB_OUT = 512
SUB_OUT = 256
NC_SSD = 64
FF_TILE = 256
GROUP_UNROLL = 8
VMEM_LIMIT = 56 * 1024 * 1024


def _log2(n):
    assert n > 0 and n & (n - 1) == 0, n
    return n.bit_length() - 1


def _pdiv(x, n):
    return lax.shift_right_arithmetic(x, jnp.int32(_log2(n)))


def _pmod(x, n):
    return x & (n - 1)


def _rms(x, g):
    return x * lax.rsqrt(jnp.mean(x * x, axis=-1, keepdims=True) + EPS) * g


def _rope(xb, cos, sin_signed, first_half):
    partner = jnp.where(first_half, pltpu.roll(xb, LANES - HEAD_DIM // 2, 1), pltpu.roll(xb, HEAD_DIM // 2, 1))
    return xb * cos + partner * sin_signed


def _group_transpose8(xs, lane_group):
    xs = list(xs)
    for d in (4, 2, 1):
        keep = (lane_group & d) == 0
        nxt = list(xs)
        for i in range(GROUPS_PER_VREG):
            if i & d:
                continue
            a, b = xs[i], xs[i | d]
            nxt[i] = jnp.where(keep, a, pltpu.roll(b, SSM_GROUP * d, 1))
            nxt[i | d] = jnp.where(keep, pltpu.roll(a, LANES - SSM_GROUP * d, 1), b)
        xs = nxt
    return xs


def _row_sel(s, nk, strided):
    if strided:
        return pl.ds(s, nk, stride=SSD_T)
    return pl.ds(s * nk, nk)


def _to_groups(u_ref, put, nk, strided):
    lane_group = _pdiv(lax.broadcasted_iota(jnp.int32, (nk, LANES), 1), SSM_GROUP)
    for j in range(D_SSM // LANES):
        for half in range(SSD_T // GROUPS_PER_VREG):
            xs = [u_ref[j, _row_sel(GROUPS_PER_VREG * half + sl, nk, strided), :]
                  for sl in range(GROUPS_PER_VREG)]
            ws = _group_transpose8(xs, lane_group)
            for gq in range(GROUPS_PER_VREG):
                put(GROUPS_PER_VREG * j + gq, half, ws[gq].astype(BF16))


def _from_groups(get, y_ref, nk, strided):
    lane_group = _pdiv(lax.broadcasted_iota(jnp.int32, (nk, LANES), 1), SSM_GROUP)
    for j in range(D_SSM // LANES):
        for half in range(SSD_T // GROUPS_PER_VREG):
            ws = [get(GROUPS_PER_VREG * j + gq, half).astype(F32) for gq in range(GROUPS_PER_VREG)]
            xs = _group_transpose8(ws, lane_group)
            for sl in range(GROUPS_PER_VREG):
                y_ref[j, _row_sel(GROUPS_PER_VREG * half + sl, nk, strided), :] = xs[sl]


def _cmul(re_full, im_signed, z):
    return re_full * z + im_signed * pltpu.roll(z, SSM_STATE, z.ndim - 1)


def _in_proj(x, g1_ref, win_ref, cos_ref, sin_ref):
    rows = x.shape[0]
    hn = _rms(x, g1_ref[...]).astype(BF16)
    proj = jnp.dot(hn, win_ref[...], preferred_element_type=F32)
    cos = cos_ref[...]
    sin = sin_ref[...]
    lane = lax.broadcasted_iota(jnp.int32, (rows, LANES), 1)
    first_half = _pmod(lane, HEAD_DIM) < (HEAD_DIM // 2)
    scale = HEAD_DIM ** -0.5
    qs = [_rope(proj[:, LANES * j:LANES * (j + 1)], cos, sin, first_half) * scale for j in range(D_ATTN // LANES)]
    k = _rope(proj[:, D_ATTN:D_ATTN + KV_W], cos, sin, first_half)
    v = proj[:, D_ATTN + KV_W:D_ATTN + 2 * KV_W]
    u = proj[:, D_ATTN + 2 * KV_W:]
    return qs, k, v, u


def _kv_variants(k, v):
    lane = lax.broadcasted_iota(jnp.int32, k.shape, 1)
    lo = lane < HEAD_DIM
    out = []
    for t in (k, v):
        tr = pltpu.roll(t, HEAD_DIM, 1)
        zero = jnp.zeros_like(t)
        out += [jnp.where(lo, t, zero), jnp.where(lo, zero, tr), jnp.where(lo, tr, zero), jnp.where(lo, zero, t)]
    return [o.astype(BF16) for o in out]


def _k_variants(k):
    lo = lax.broadcasted_iota(jnp.int32, k.shape, 1) < HEAD_DIM
    kr = pltpu.roll(k, HEAD_DIM, 1)
    zero = jnp.zeros_like(k)
    out = [jnp.where(lo, k, zero), jnp.where(lo, zero, kr), jnp.where(lo, kr, zero), jnp.where(lo, zero, k)]
    return [o.astype(BF16) for o in out]


def _attend_t(a_bf, kwin, vt_win, valid, sink_row):
    s = lax.dot_general(kwin, a_bf, (((1,), (1,)), ((), ())), preferred_element_type=F32)
    if valid is not None:
        s = jnp.where(valid, s, -jnp.inf)
    m = jnp.maximum(jnp.max(s, axis=0, keepdims=True), sink_row)
    e = jnp.exp(s - m)
    den = jnp.sum(e, axis=0, keepdims=True) + jnp.exp(sink_row - m)
    o = jnp.dot(vt_win, e.astype(BF16), preferred_element_type=F32)
    return o * (1.0 / den)


def _attend(a_bf, kvar, vvar, valid, sink_col):
    s = lax.dot_general(a_bf, kvar, (((1,), (1,)), ((), ())), preferred_element_type=F32)
    for cond in valid:
        s = jnp.where(cond, s, -jnp.inf)
    m = jnp.maximum(jnp.max(s, axis=-1, keepdims=True), sink_col)
    e = jnp.exp(s - m)
    den = jnp.sum(e, axis=-1, keepdims=True) + jnp.exp(sink_col - m)
    o = jnp.dot(e.astype(BF16), vvar, preferred_element_type=F32)
    return o * (1.0 / den)


def _prompt_in_kernel(sinks_ref, x_ref, g1_ref, win_ref, cos_ref, sin_ref,
                      a_ref, ug_ref, kp_ref, vp_ref, q_s, k_s, vb_s, vt_s, u_s):
    tb = x_ref.shape[1]
    nk = tb // SSD_T
    blk = pl.program_id(1)
    qs, k, v, u = _in_proj(x_ref[0], g1_ref, win_ref, cos_ref, sin_ref)
    for j, qb in enumerate(qs):
        q_s[:, LANES * j:LANES * (j + 1)] = qb
    for j in range(D_SSM // LANES):
        u_s[j] = u[:, LANES * j:LANES * (j + 1)]

    nbuf = WINDOW + tb

    @pl.when(blk == 0)
    def _():
        k_s[:, 0:WINDOW, :] = jnp.zeros((4, WINDOW, LANES), BF16)
        vb_s[0:WINDOW, :] = jnp.zeros((WINDOW, LANES), F32)

    @pl.when(blk > 0)
    def _():
        k_s[:, 0:WINDOW, :] = k_s[:, tb:nbuf, :]
        vb_s[0:WINDOW, :] = vb_s[tb:nbuf, :]

    for idx, arr in enumerate(_k_variants(k)):
        k_s[idx, WINDOW:nbuf, :] = arr
    vb_s[WINDOW:nbuf, :] = v

    @pl.when(blk == pl.num_programs(1) - 1)
    def _():
        kp_ref[0] = k[tb - WINDOW:, :]
        vp_ref[0] = v[tb - WINDOW:, :]

    vt0 = vb_s[...].T
    vt1 = pltpu.roll(vt0, nbuf - CHUNK, 1)
    zrows = jnp.zeros((HEAD_DIM, nbuf), F32)
    for sh, vt in ((0, vt0), (1, vt1)):
        for g in range(N_KV_HEADS):
            rows = vt[HEAD_DIM * g:HEAD_DIM * (g + 1), :]
            vt_s[sh, g, 0] = jnp.concatenate([rows, zrows], axis=0).astype(BF16)
            vt_s[sh, g, 1] = jnp.concatenate([zrows, rows], axis=0).astype(BF16)

    nkeys = WINDOW + CHUNK
    key_row = lax.broadcasted_iota(jnp.int32, (nkeys, 1), 0)
    q_lane = lax.broadcasted_iota(jnp.int32, (1, LANES), 1)
    for i in range(tb // CHUNK):
        r0 = i * CHUNK
        sh = i % 2
        base = r0 - sh * CHUNK
        valid = key_row >= (WINDOW - r0 - blk * tb) if i < N_WIN_CHUNKS else None
        for g in range(N_KV_HEADS):
            qa = q_s[r0:r0 + CHUNK, 2 * LANES * g:2 * LANES * g + LANES]
            qb = q_s[r0:r0 + CHUNK, 2 * LANES * g + LANES:2 * LANES * (g + 1)]
            a_bf = jnp.concatenate([qa, qb], axis=0).astype(BF16)
            acc = None
            for par in range(2):
                h0 = 4 * g + par
                sink_row = jnp.where(q_lane < CHUNK, sinks_ref[h0], sinks_ref[h0 + 2])
                o = _attend_t(a_bf, k_s[2 * g + par, r0:r0 + nkeys, :],
                              vt_s[sh, g, par, :, base:base + nkeys], valid, sink_row)
                acc = o if acc is None else acc + o
            at = acc.T
            a_ref[0, r0:r0 + CHUNK, 2 * LANES * g:2 * LANES * g + LANES] = at[:CHUNK]
            a_ref[0, r0:r0 + CHUNK, 2 * LANES * g + LANES:2 * LANES * (g + 1)] = at[CHUNK:]

    def put(g, half, val):
        ug_ref[g, 0, :, LANES * half:LANES * (half + 1)] = val

    _to_groups(u_s, put, nk, strided=True)


def _prompt_ssd_kernel(ug_ref, m_ref, wd_ref, r_ref, tab_ref, abar_ref,
                       yg_ref, sfin_ref, d_s, sp_s, carry_s):
    nb = ug_ref.shape[1]
    nc = ug_ref.shape[2]
    step = pl.program_id(0)

    @pl.when(step == 0)
    def _():
        carry_s[...] = jnp.zeros(carry_s.shape, F32)

    def state_in(g, c):
        ub = ug_ref[g].reshape(nb * nc, SSD_W)
        d_s[g] = jnp.dot(ub, wd_ref[g], preferred_element_type=F32)
        return c

    lax.fori_loop(0, N_GROUPS, state_in, 0, unroll=GROUP_UNROLL)

    pw = [tab_ref[i] for i in range(8)]
    are, aim = abar_ref[0][:, None, :], abar_ref[1][:, None, :]
    kio = lax.broadcasted_iota(jnp.int32, (N_GROUPS, nc, STATE_W), 1)
    for b in range(nb):
        dg = d_s[:, b * nc:(b + 1) * nc, :]
        s_in = carry_s[b]
        dsh = jnp.where(kio == 0, s_in, pltpu.roll(dg, 1, 1))
        c = jnp.zeros((N_GROUPS, 1, STATE_W), F32)
        for j in range(nc // SUBLANES):
            x = dsh[:, SUBLANES * j:SUBLANES * (j + 1), :]
            for lvl in range(3):
                xs = pltpu.roll(x, 1 << lvl, 1)
                x = x + _cmul(pw[2 * lvl], pw[2 * lvl + 1], xs)
            cb = jnp.broadcast_to(c, x.shape)
            h = x + _cmul(pw[6], pw[7], cb)
            sp_s[:, b * nc + SUBLANES * j:b * nc + SUBLANES * (j + 1), :] = h
            c = h[:, SUBLANES - 1:SUBLANES, :]
        carry_s[b] = _cmul(are, aim, c) + dg[:, nc - 1:nc, :]

    def readout(g, c):
        ub = ug_ref[g].reshape(nb * nc, SSD_W)
        y = jnp.dot(ub, m_ref[g], preferred_element_type=F32)
        y = y + lax.dot_general(sp_s[g].astype(BF16), r_ref[g], (((1,), (1,)), ((), ())),
                                preferred_element_type=F32)
        yg_ref[g] = y.reshape(nb, nc, SSD_W).astype(BF16)
        return c

    lax.fori_loop(0, N_GROUPS, readout, 0, unroll=GROUP_UNROLL)
    sfin_ref[...] = carry_s[...]


def _out_kernel(*refs, strided, cs, has_prev):
    if has_prev:
        x_ref, a_ref, yg_ref, prev_ref = refs[:4]
        refs = refs[4:]
    else:
        x_ref, a_ref, yg_ref = refs[:3]
        prev_ref = None
        refs = refs[3:]
    (wglu_ref, ga_ref, gs_ref, wout_ref, g2_ref, wup_ref, cw_ref, cb_ref, wdown_ref, gf_ref,
     y_ref, cst_ref, ys_s, ext_s, px_s, py_s) = refs
    tb = x_ref.shape[1]
    n_sub, _, sb, _ = ys_s.shape
    nks = sb // SSD_T
    planes = strided
    pr = sb // SUBLANES
    pad = SUBLANES if planes else ext_s.shape[0] - sb
    blk = pl.program_id(1)
    assert n_sub * sb == tb and (not planes or (cs == 1 and not has_prev))

    @pl.when(blk == 0)
    def _():
        if planes:
            ext_s[:, 0:pad, :] = jnp.zeros((2, pad, 2 * D_FF), F32)
        elif has_prev:
            ext_s[0:pad, :] = prev_ref[0]
        else:
            ext_s[0:pad, :] = jnp.zeros((pad, 2 * D_FF), F32)

    def mix(i):
        rows = slice(i * sb, (i + 1) * sb)
        _from_groups(lambda g, half: yg_ref[g, 0, i * nks:(i + 1) * nks, LANES * half:LANES * (half + 1)],
                     ys_s.at[i], nks, strided)
        yv = jnp.concatenate([ys_s[i, j] for j in range(D_SSM // LANES)], axis=-1)
        z = 0.5 * yv * (1.0 + jnp.tanh(math.sqrt(2.0 / math.pi) * (yv + 0.044715 * (yv * yv * yv))))
        gate = jnp.dot(z.astype(BF16), wglu_ref[...], preferred_element_type=F32)
        s_out = z * jax.nn.sigmoid(gate)
        na = _rms(a_ref[0, rows, :], ga_ref[...]).astype(BF16)
        ns = _rms(s_out, gs_ref[...]).astype(BF16)
        x1 = x_ref[0, rows, :] + jnp.dot(na, wout_ref[0:D_ATTN, :], preferred_element_type=F32)
        return x1 + jnp.dot(ns, wout_ref[D_ATTN:, :], preferred_element_type=F32)

    def tile_cols(j, half):
        return slice(half * D_FF + j * FF_TILE, half * D_FF + (j + 1) * FF_TILE)

    def to_planes(x):
        for c in range(D_MODEL // LANES):
            px_s[c] = x[:, LANES * c:LANES * (c + 1)]
        return jnp.concatenate(
            [jnp.concatenate([px_s[c, pl.ds(r, pr, stride=SUBLANES), :] for r in range(SUBLANES)], axis=0)
             for c in range(D_MODEL // LANES)], axis=1)

    def store_rows(i, y):
        rows = slice(i * sb, (i + 1) * sb)
        if not planes:
            y_ref[0, rows, :] = y
            return
        for c in range(D_MODEL // LANES):
            for r in range(SUBLANES):
                py_s[c, pl.ds(r, pr, stride=SUBLANES), :] = y[r * pr:(r + 1) * pr, LANES * c:LANES * (c + 1)]
            y_ref[0, rows, LANES * c:LANES * (c + 1)] = py_s[c]

    def time_shifts(up, cols):
        if not planes:
            ext_s[pad:pad + sb, cols] = up
            sh2 = ext_s[pad - 2 * cs:pad - 2 * cs + sb, cols]
            sh1 = ext_s[pad - cs:pad - cs + sb, cols]
            ext_s[0:pad, cols] = ext_s[sb:sb + pad, cols]
            return sh2, sh1
        for p in range(2):
            ext_s[p, pad:pad + pr, cols] = up[(6 + p) * pr:(7 + p) * pr, :]
        s6 = ext_s[0, pad - 1:pad - 1 + pr, cols]
        s7 = ext_s[1, pad - 1:pad - 1 + pr, cols]
        ext_s[:, 0:pad, cols] = ext_s[:, pr:pr + pad, cols]
        sh1 = jnp.concatenate([s7, up[:7 * pr, :]], axis=0)
        sh2 = jnp.concatenate([s6, s7, up[:6 * pr, :]], axis=0)
        return sh2, sh1

    def conv_ffn(i, x1):
        if planes:
            x1 = to_planes(x1)
        h2 = _rms(x1, g2_ref[...]).astype(BF16)

        def up_proj(j):
            return [jnp.dot(h2, wup_ref[:, tile_cols(j, half)], preferred_element_type=F32) for half in range(2)]

        def down_proj(act, j):
            return jnp.dot(act, wdown_ref[j * FF_TILE:(j + 1) * FF_TILE, :], preferred_element_type=F32)

        n_tiles = D_FF // FF_TILE
        acc = jnp.zeros((sb, D_MODEL), F32)
        ups = up_proj(0)
        act_prev = None
        for j in range(n_tiles):
            nxt = up_proj(j + 1) if j + 1 < n_tiles else None
            if act_prev is not None:
                acc = acc + down_proj(act_prev, j - 1)
            parts = []
            for half in range(2):
                cols = tile_cols(j, half)
                sh2, sh1 = time_shifts(ups[half], cols)
                parts.append(sh2 * cw_ref[0:1, cols] + sh1 * cw_ref[1:2, cols]
                             + ups[half] * cw_ref[2:3, cols] + cb_ref[:, cols])
            cg, cv = parts
            act_prev = (cg * jax.nn.sigmoid(cg) * cv).astype(BF16)
            ups = nxt
        acc = acc + down_proj(act_prev, n_tiles - 1)
        store_rows(i, _rms(x1 + acc, gf_ref[...]))

    mixed = mix(0)
    for i in range(n_sub):
        nxt_mixed = mix(i + 1) if i + 1 < n_sub else None
        conv_ffn(i, mixed)
        mixed = nxt_mixed
    if planes:
        cst_ref[0] = ext_s[:, 0:pad, :]
    else:
        cst_ref[0] = ext_s[0:pad, :]


def _decode_in_kernel(sinks_ref, x_ref, g1_ref, win_ref, cos_ref, sin_ref, kc_ref, vc_ref, h0_ref,
                      m_ref, wd_ref, r_ref, abar_ref,
                      a_ref, kn_ref, vn_ref, yg_ref, sn_ref, u_s, ug_s, *, n_streams, n_steps, n_cache):
    rows = n_streams * n_steps
    qs, k, v, u = _in_proj(x_ref[...], g1_ref, win_ref, cos_ref, sin_ref)
    kn_ref[...] = k
    vn_ref[...] = v
    for j in range(D_SSM // LANES):
        u_s[j] = u[:, LANES * j:LANES * (j + 1)]
    ncache_rows = n_streams * n_cache
    nkeys = ncache_rows + rows
    kall = jnp.concatenate([kc_ref[...], k], axis=0)
    vall = jnp.concatenate([vc_ref[...], v], axis=0)
    variants = _kv_variants(kall, vall)

    colv = lax.broadcasted_iota(jnp.int32, (1, nkeys), 1)
    is_new = colv >= ncache_rows
    cnew = colv - ncache_rows
    k_stream = jnp.where(is_new, _pmod(cnew, n_streams), _pdiv(colv, n_cache))
    k_pos = jnp.where(is_new, PAST_LEN + _pdiv(cnew, n_streams), PAST_LEN - n_cache + _pmod(colv, n_cache))
    k_chunk = _pdiv(k_pos, CHUNK)
    rowv = _pmod(lax.broadcasted_iota(jnp.int32, (2 * rows, 1), 0), rows)
    q_stream = _pmod(rowv, n_streams)
    q_chunk = _pdiv(PAST_LEN + _pdiv(rowv, n_streams), CHUNK)
    ok_col = jnp.where(k_pos >= 0, k_stream, -1)
    d_chunk = q_chunk - k_chunk
    valid = [q_stream == ok_col,
             lax.bitcast_convert_type(d_chunk, jnp.uint32) <= jnp.uint32(N_WIN_CHUNKS)]
    top = lax.broadcasted_iota(jnp.int32, (2 * rows, 1), 0) < rows

    for g in range(N_KV_HEADS):
        a_bf = jnp.concatenate([qs[2 * g], qs[2 * g + 1]], axis=0).astype(BF16)
        acc = jnp.zeros((2 * rows, LANES), F32)
        for par in range(2):
            h0 = 4 * g + par
            sink_col = jnp.where(top, sinks_ref[h0], sinks_ref[h0 + 2])
            acc = acc + _attend(a_bf, variants[2 * g + par], variants[4 + 2 * g + par], valid, sink_col)
        a_ref[:, 2 * LANES * g:2 * LANES * g + LANES] = acc[:rows]
        a_ref[:, 2 * LANES * g + LANES:2 * LANES * (g + 1)] = acc[rows:]

    def put(g, half, val):
        ug_s[g, :, LANES * half:LANES * (half + 1)] = val

    _to_groups(u_s, put, n_streams, strided=False)
    def group_body(g, c):
        ub = ug_s[g]
        h0g = h0_ref[g]
        d = jnp.dot(ub, wd_ref[g], preferred_element_type=F32)
        sn_ref[g] = _cmul(abar_ref[0, pl.ds(g, 1), :], abar_ref[1, pl.ds(g, 1), :], h0g) + d
        y = jnp.dot(ub, m_ref[g], preferred_element_type=F32)
        y = y + lax.dot_general(h0g.astype(BF16), r_ref[g], (((1,), (1,)), ((), ())),
                                preferred_element_type=F32)
        yg_ref[g] = y.astype(BF16)
        return c

    lax.fori_loop(0, N_GROUPS, group_body, 0, unroll=GROUP_UNROLL)


def _ssd_tables(a_re, a_im, log_dt, b_re, b_im, c_re, c_im, d_skip):
    t = SSD_T
    dt = jnp.exp(log_dt)[:, None]
    lr, li = dt * a_re, dt * a_im

    exps = list(range(1 - t, t + 1)) + [t * k for k in range(2, SUBLANES + 1)]
    n_all = jnp.asarray(exps, F32)[None, :, None]
    mag_all = jnp.exp(n_all * lr[:, None, :])
    pr_all, pi_all = mag_all * jnp.cos(n_all * li[:, None, :]), mag_all * jnp.sin(n_all * li[:, None, :])

    def powers(n):
        sel = jnp.asarray([exps.index(int(v)) for v in n])
        return pr_all[:, sel], pi_all[:, sel]

    def cmul4(xr, xi, pr, pi):
        xr, xi, pr, pi = xr[:, None], xi[:, None], pr[:, :, None], pi[:, :, None]
        return xr * pr - xi * pi, xr * pi + xi * pr

    def rows(re, im):
        return jnp.concatenate([re, im], axis=-1).reshape(N_GROUPS, SSD_W, STATE_W)

    steps = np.arange(t)
    ar, ai = powers([1])
    ar, ai = ar[:, 0], ai[:, 0]
    den = a_re * a_re + a_im * a_im
    nr, ni = ar - 1.0, ai
    fr, fi = (nr * a_re + ni * a_im) / den, (ni * a_re - nr * a_im) / den
    btr, bti = b_re.transpose(0, 2, 1), b_im.transpose(0, 2, 1)
    bbr = fr[:, None, :] * btr - fi[:, None, :] * bti
    bbi = fr[:, None, :] * bti + fi[:, None, :] * btr

    xr, xi = cmul4(bbr, bbi, *powers(-steps))
    yr, yi = cmul4(c_re, c_im, *powers(steps))
    m = jnp.einsum('gik,gjk->gij', rows(xr, -xi), rows(yr, yi), precision=lax.Precision.HIGHEST)
    idx = jnp.arange(SSD_W)
    causal = (idx[None, :] // SSM_GROUP) >= (idx[:, None] // SSM_GROUP)
    skip = jnp.tile(d_skip, (1, t))[:, None, :] * jnp.eye(SSD_W, dtype=F32)[None]
    m_mat = jnp.where(causal[None], m, 0.0) + skip
    wd_mat = rows(*cmul4(bbr, bbi, *powers(t - 1 - steps)))
    er, ei = cmul4(c_re, c_im, *powers(steps + 1))
    rt_mat = rows(er, -ei)

    def full(n):
        pr, pi = powers(n)
        return jnp.concatenate([pr, pr], -1), jnp.concatenate([-pi, pi], -1)

    sub = np.arange(SUBLANES)
    tabs = []
    for lvl in range(3):
        d = 1 << lvl
        fre, fim = full([t * d] * SUBLANES)
        keep = jnp.asarray(sub >= d)[None, :, None]
        tabs += [jnp.where(keep, fre, 0.0), jnp.where(keep, fim, 0.0)]
    tabs += list(full(t * (sub + 1)))
    tab = jnp.stack(tabs)
    are, aim = full([t])
    abar = jnp.stack([are[:, 0], aim[:, 0]])
    return m_mat.astype(BF16), wd_mat.astype(BF16), rt_mat.astype(BF16), tab, abar


def _rope_tables(pos):
    half = HEAD_DIM // 2
    inv = ROPE_THETA ** (-jnp.arange(half, dtype=F32) / half)
    ang = pos.astype(F32)[:, None] * inv[None, :]
    cos, sin = jnp.cos(ang), jnp.sin(ang)
    cos_t = jnp.concatenate([cos, cos, cos, cos], axis=-1)
    sin_t = jnp.concatenate([-sin, sin, -sin, sin], axis=-1)
    return cos_t, sin_t


def _const(shape):
    nd = len(shape)
    return pl.BlockSpec(shape, lambda *_: (0,) * nd, pipeline_mode=pl.Buffered(1))


def _whole(shape):
    nd = len(shape)
    return pl.BlockSpec(shape, lambda *_: (0,) * nd)


def _out_call(x, a, yg, prev, w, *, tb, sb, strided, cs, pad):
    nb, length, _ = x.shape
    nk = tb // SSD_T
    grid = (nb, length // tb)
    in_specs = [
        pl.BlockSpec((1, tb, D_MODEL), lambda b, i: (b, i, 0)),
        pl.BlockSpec((1, tb, D_ATTN), lambda b, i: (b, i, 0)),
        pl.BlockSpec((N_GROUPS, 1, nk, SSD_W), lambda b, i: (0, b, i, 0)),
    ]
    args = [x, a, yg]
    if prev is not None:
        in_specs.append(pl.BlockSpec((1, pad, 2 * D_FF), lambda b, i: (b, 0, 0)))
        args.append(prev)
    weights = [w['w_glu'], w['onorm_a'], w['onorm_s'], w['w_out'], w['norm2'], w['w_up'], w['conv_w'],
               w['conv_b'], w['w_down'], w['final_g']]
    in_specs += [_const(t.shape) for t in weights]
    kern = functools.partial(_out_kernel, strided=strided, cs=cs, has_prev=prev is not None)
    if strided:
        ext_shape = (2, pad + sb // SUBLANES, 2 * D_FF)
        cst_block, cst_map = (1, 2, pad, 2 * D_FF), (lambda b, i: (b, 0, 0, 0))
    else:
        ext_shape = (pad + sb, 2 * D_FF)
        cst_block, cst_map = (1, pad, 2 * D_FF), (lambda b, i: (b, 0, 0))
    return pl.pallas_call(
        kern,
        grid=grid,
        in_specs=in_specs,
        out_specs=[pl.BlockSpec((1, tb, D_MODEL), lambda b, i: (b, i, 0)),
                   pl.BlockSpec(cst_block, cst_map)],
        out_shape=[jax.ShapeDtypeStruct((nb, length, D_MODEL), F32),
                   jax.ShapeDtypeStruct((nb,) + cst_block[1:], F32)],
        scratch_shapes=[pltpu.VMEM((tb // sb, D_SSM // LANES, sb, LANES), F32),
                        pltpu.VMEM(ext_shape, F32),
                        pltpu.VMEM((D_MODEL // LANES, sb, LANES), F32),
                        pltpu.VMEM((D_MODEL // LANES, sb, LANES), F32)],
        compiler_params=pltpu.CompilerParams(dimension_semantics=("arbitrary", "arbitrary"),
                                             vmem_limit_bytes=VMEM_LIMIT),
        name="layer_out",
    )(*args, *weights)


def kernel(x_prompt, x_sample, cache_k, cache_v, state_ssm_re, state_ssm_im, state_conv, norm1_g, w_in, attn_sinks, ssm_A_re, ssm_A_im, ssm_log_dt, ssm_B_re, ssm_B_im, ssm_C_re, ssm_C_im, ssm_D, w_glu, onorm_attn_g, onorm_ssm_g, w_out, norm2_g, w_up, conv_w, conv_b, w_down, final_g):
    assert norm1_g.shape[0] == 1, "one layer"
    nb, length, _ = x_prompt.shape
    ns, nt, _ = x_sample.shape
    n_cache = cache_k.shape[2]
    assert nt == SSD_T and length % TB_IN == 0 and length % TB_OUT == 0 and (length // SSD_T) % NC_SSD == 0

    m_mat, wd_mat, r_mat, tab, abar = _ssd_tables(ssm_A_re[0], ssm_A_im[0], ssm_log_dt[0], ssm_B_re[0],
                                                  ssm_B_im[0], ssm_C_re[0], ssm_C_im[0], ssm_D[0])
    sinks = attn_sinks[0]
    g1 = norm1_g
    win = w_in[0].astype(BF16)
    w = dict(w_glu=w_glu[0].astype(BF16), onorm_a=onorm_attn_g, onorm_s=onorm_ssm_g, w_out=w_out[0].astype(BF16),
             norm2=norm2_g, w_up=w_up[0].astype(BF16), conv_w=conv_w[0], conv_b=conv_b,
             w_down=w_down[0].astype(BF16), final_g=final_g[None, :])
    smem = pl.BlockSpec(memory_space=pltpu.SMEM)

    cos_p, sin_p = _rope_tables(jnp.arange(length))
    nk = TB_IN // SSD_T
    a_p, ug_p, k_p, v_p = pl.pallas_call(
        _prompt_in_kernel,
        grid=(nb, length // TB_IN),
        in_specs=[smem,
                  pl.BlockSpec((1, TB_IN, D_MODEL), lambda b, i: (b, i, 0)),
                  _const((1, D_MODEL)), _const((D_MODEL, D_IN)),
                  pl.BlockSpec((TB_IN, LANES), lambda b, i: (i, 0)),
                  pl.BlockSpec((TB_IN, LANES), lambda b, i: (i, 0))],
        out_specs=[pl.BlockSpec((1, TB_IN, D_ATTN), lambda b, i: (b, i, 0)),
                   pl.BlockSpec((N_GROUPS, 1, nk, SSD_W), lambda b, i: (0, b, i, 0)),
                   pl.BlockSpec((1, WINDOW, KV_W), lambda b, i: (b, 0, 0)),
                   pl.BlockSpec((1, WINDOW, KV_W), lambda b, i: (b, 0, 0))],
        out_shape=[jax.ShapeDtypeStruct((nb, length, D_ATTN), F32),
                   jax.ShapeDtypeStruct((N_GROUPS, nb, length // SSD_T, SSD_W), BF16),
                   jax.ShapeDtypeStruct((nb, WINDOW, KV_W), F32),
                   jax.ShapeDtypeStruct((nb, WINDOW, KV_W), F32)],
        scratch_shapes=[pltpu.VMEM((TB_IN, D_ATTN), F32),
                        pltpu.VMEM((2 * N_KV_HEADS, WINDOW + TB_IN, LANES), BF16),
                        pltpu.VMEM((WINDOW + TB_IN, LANES), F32),
                        pltpu.VMEM((2, N_KV_HEADS, 2, LANES, WINDOW + TB_IN), BF16),
                        pltpu.VMEM((D_SSM // LANES, TB_IN, LANES), F32)],
        compiler_params=pltpu.CompilerParams(dimension_semantics=("arbitrary", "arbitrary"),
                                             vmem_limit_bytes=VMEM_LIMIT),
        name="prompt_in",
    )(sinks, x_prompt, g1, win, cos_p, sin_p)

    yg_p, sfin_p = pl.pallas_call(
        _prompt_ssd_kernel,
        grid=(length // SSD_T // NC_SSD,),
        in_specs=[pl.BlockSpec((N_GROUPS, nb, NC_SSD, SSD_W), lambda i: (0, 0, i, 0)),
                  _const(m_mat.shape), _const(wd_mat.shape), _const(r_mat.shape), _const(tab.shape),
                  _const(abar.shape)],
        out_specs=[pl.BlockSpec((N_GROUPS, nb, NC_SSD, SSD_W), lambda i: (0, 0, i, 0)),
                   pl.BlockSpec((nb, N_GROUPS, 1, STATE_W), lambda i: (0, 0, 0, 0))],
        out_shape=[jax.ShapeDtypeStruct((N_GROUPS, nb, length // SSD_T, SSD_W), BF16),
                   jax.ShapeDtypeStruct((nb, N_GROUPS, 1, STATE_W), F32)],
        scratch_shapes=[pltpu.VMEM((N_GROUPS, nb * NC_SSD, STATE_W), F32),
                        pltpu.VMEM((N_GROUPS, nb * NC_SSD, STATE_W), F32),
                        pltpu.VMEM((nb, N_GROUPS, 1, STATE_W), F32)],
        compiler_params=pltpu.CompilerParams(dimension_semantics=("arbitrary",), vmem_limit_bytes=VMEM_LIMIT),
        name="prompt_ssd",
    )(ug_p, m_mat, wd_mat, r_mat, tab, abar)

    y_p, cst_p = _out_call(x_prompt, a_p, yg_p, None, w, tb=TB_OUT, sb=SUB_OUT, strided=True, cs=1, pad=SUBLANES)

    rows = ns * nt
    xs = x_sample.transpose(1, 0, 2).reshape(rows, D_MODEL)
    cos_s, sin_s = _rope_tables(PAST_LEN + jnp.arange(rows) // ns)
    kc = cache_k[0].reshape(ns * n_cache, KV_W)
    vc = cache_v[0].reshape(ns * n_cache, KV_W)
    h0 = jnp.concatenate([state_ssm_re[0], state_ssm_im[0]], axis=-1).transpose(1, 0, 2)
    dec = functools.partial(_decode_in_kernel, n_streams=ns, n_steps=nt, n_cache=n_cache)
    dec_in = [xs, g1, win, cos_s, sin_s, kc, vc, h0, m_mat, wd_mat, r_mat, abar]
    a_s, kn, vn, yg_s, sn = pl.pallas_call(
        dec,
        grid=(1,),
        in_specs=[smem] + [_const(t.shape) for t in dec_in],
        out_specs=[_whole((rows, D_ATTN)), _whole((rows, KV_W)), _whole((rows, KV_W)),
                   _whole((N_GROUPS, ns, SSD_W)), _whole((N_GROUPS, ns, STATE_W))],
        out_shape=[jax.ShapeDtypeStruct((rows, D_ATTN), F32), jax.ShapeDtypeStruct((rows, KV_W), F32),
                   jax.ShapeDtypeStruct((rows, KV_W), F32), jax.ShapeDtypeStruct((N_GROUPS, ns, SSD_W), BF16),
                   jax.ShapeDtypeStruct((N_GROUPS, ns, STATE_W), F32)],
        scratch_shapes=[pltpu.VMEM((D_SSM // LANES, rows, LANES), F32), pltpu.VMEM((N_GROUPS, ns, SSD_W), BF16)],
        compiler_params=pltpu.CompilerParams(dimension_semantics=("arbitrary",), vmem_limit_bytes=VMEM_LIMIT),
        name="decode_in",
    )(sinks, *dec_in)

    cs = ns
    prev = state_conv[0].transpose(1, 0, 2).reshape(1, (CONV_W - 1) * cs, 2 * D_FF)
    y_s, cst_s = _out_call(xs[None], a_s[None], yg_s[:, None], prev, w, tb=rows, sb=rows, strided=False, cs=cs,
                           pad=(CONV_W - 1) * cs)

    kv5 = lambda t, n: t.reshape(1, n, -1, N_KV_HEADS, HEAD_DIM)
    y_sample = y_s.reshape(nt, ns, D_MODEL).transpose(1, 0, 2)
    kn_b = kn.reshape(nt, ns, KV_W).transpose(1, 0, 2)
    vn_b = vn.reshape(nt, ns, KV_W).transpose(1, 0, 2)
    k_new = jnp.concatenate([cache_k[0].reshape(ns, n_cache, KV_W), kn_b], axis=1)[:, -n_cache:]
    v_new = jnp.concatenate([cache_v[0].reshape(ns, n_cache, KV_W), vn_b], axis=1)[:, -n_cache:]
    conv_p = cst_p[:, :, SUBLANES - 1, :][None]
    conv_s = cst_s.reshape(CONV_W - 1, ns, 2 * D_FF).transpose(1, 0, 2)[None]
    sn_b = sn.transpose(1, 0, 2)
    return (y_p, y_sample,
            kv5(k_p, nb), kv5(v_p, nb),
            sfin_p[None, :, :, 0, :SSM_STATE], sfin_p[None, :, :, 0, SSM_STATE:], conv_p,
            kv5(k_new, ns), kv5(v_new, ns),
            sn_b[None, :, :, :SSM_STATE], sn_b[None, :, :, SSM_STATE:], conv_s)
```

```python
import functools
import math

import numpy as np
import jax
import jax.numpy as jnp
from jax import lax
from jax.experimental import pallas as pl
from jax.experimental.pallas import tpu as pltpu

F32 = jnp.float32
BF16 = jnp.bfloat16

D_MODEL = 1024
CHUNK = 64
D_ATTN = 512
D_SSM = 512
HEAD_DIM = 64
N_HEADS = 8
N_KV_HEADS = 2
KV_W = N_KV_HEADS * HEAD_DIM
WINDOW = 128
N_WIN_CHUNKS = WINDOW // CHUNK
ROPE_THETA = 10000.0
SSM_GROUP = 16
N_GROUPS = D_SSM // SSM_GROUP
SSM_STATE = 64
D_FF = 2816
CONV_W = 3
EPS = 1e-6
D_IN = D_ATTN + 2 * KV_W + D_SSM
PAST_LEN = 2048

SSD_T = 16
LANES = 128
SUBLANES = 8
GROUPS_PER_VREG = LANES // SSM_GROUP
SSD_W = SSD_T * SSM_GROUP
STATE_W = 2 * SSM_STATE

TB_IN = 512
TB_OUT = 512
SUB_OUT = 256
NC_SSD = 64
FF_TILE = 256
GROUP_UNROLL = 8
TABLE_GROUPS = 8
VMEM_LIMIT = 56 * 1024 * 1024


def _log2(n):
    assert n > 0 and n & (n - 1) == 0, n
    return n.bit_length() - 1


def _pdiv(x, n):
    return lax.shift_right_arithmetic(x, jnp.int32(_log2(n)))


def _pmod(x, n):
    return x & (n - 1)


def _rms(x, g):
    return x * lax.rsqrt(jnp.mean(x * x, axis=-1, keepdims=True) + EPS) * g


def _rope(xb, cos, sin_signed, first_half):
    partner = jnp.where(first_half, pltpu.roll(xb, LANES - HEAD_DIM // 2, 1), pltpu.roll(xb, HEAD_DIM // 2, 1))
    return xb * cos + partner * sin_signed


def _group_transpose8(xs, lane_group):
    xs = list(xs)
    for d in (4, 2, 1):
        keep = (lane_group & d) == 0
        nxt = list(xs)
        for i in range(GROUPS_PER_VREG):
            if i & d:
                continue
            a, b = xs[i], xs[i | d]
            nxt[i] = jnp.where(keep, a, pltpu.roll(b, SSM_GROUP * d, 1))
            nxt[i | d] = jnp.where(keep, pltpu.roll(a, LANES - SSM_GROUP * d, 1), b)
        xs = nxt
    return xs


def _row_sel(s, nk, strided):
    if strided:
        return pl.ds(s, nk, stride=SSD_T)
    return pl.ds(s * nk, nk)


def _to_groups(u_ref, put, nk, strided):
    lane_group = _pdiv(lax.broadcasted_iota(jnp.int32, (nk, LANES), 1), SSM_GROUP)
    for j in range(D_SSM // LANES):
        for half in range(SSD_T // GROUPS_PER_VREG):
            xs = [u_ref[j, _row_sel(GROUPS_PER_VREG * half + sl, nk, strided), :]
                  for sl in range(GROUPS_PER_VREG)]
            ws = _group_transpose8(xs, lane_group)
            for gq in range(GROUPS_PER_VREG):
                put(GROUPS_PER_VREG * j + gq, half, ws[gq].astype(BF16))


def _from_groups(get, y_ref, nk, strided):
    lane_group = _pdiv(lax.broadcasted_iota(jnp.int32, (nk, LANES), 1), SSM_GROUP)
    for j in range(D_SSM // LANES):
        for half in range(SSD_T // GROUPS_PER_VREG):
            ws = [get(GROUPS_PER_VREG * j + gq, half).astype(F32) for gq in range(GROUPS_PER_VREG)]
            xs = _group_transpose8(ws, lane_group)
            for sl in range(GROUPS_PER_VREG):
                y_ref[j, _row_sel(GROUPS_PER_VREG * half + sl, nk, strided), :] = xs[sl]


def _cmul(re_full, im_signed, z):
    return re_full * z + im_signed * pltpu.roll(z, SSM_STATE, z.ndim - 1)


def _in_proj(x, g1_ref, win_ref, cos_ref, sin_ref):
    rows = x.shape[0]
    hn = _rms(x, g1_ref[...]).astype(BF16)
    proj = jnp.dot(hn, win_ref[...], preferred_element_type=F32)
    cos = cos_ref[...]
    sin = sin_ref[...]
    lane = lax.broadcasted_iota(jnp.int32, (rows, LANES), 1)
    first_half = _pmod(lane, HEAD_DIM) < (HEAD_DIM // 2)
    scale = HEAD_DIM ** -0.5
    qs = [_rope(proj[:, LANES * j:LANES * (j + 1)], cos, sin, first_half) * scale for j in range(D_ATTN // LANES)]
    k = _rope(proj[:, D_ATTN:D_ATTN + KV_W], cos, sin, first_half)
    v = proj[:, D_ATTN + KV_W:D_ATTN + 2 * KV_W]
    u = proj[:, D_ATTN + 2 * KV_W:]
    return qs, k, v, u


def _kv_variants(k, v):
    lane = lax.broadcasted_iota(jnp.int32, k.shape, 1)
    lo = lane < HEAD_DIM
    out = []
    for t in (k, v):
        tr = pltpu.roll(t, HEAD_DIM, 1)
        zero = jnp.zeros_like(t)
        out += [jnp.where(lo, t, zero), jnp.where(lo, zero, tr), jnp.where(lo, tr, zero), jnp.where(lo, zero, t)]
    return [o.astype(BF16) for o in out]


def _k_variants(k):
    lo = lax.broadcasted_iota(jnp.int32, k.shape, 1) < HEAD_DIM
    kr = pltpu.roll(k, HEAD_DIM, 1)
    zero = jnp.zeros_like(k)
    out = [jnp.where(lo, k, zero), jnp.where(lo, zero, kr), jnp.where(lo, kr, zero), jnp.where(lo, zero, k)]
    return [o.astype(BF16) for o in out]


def _attend_t(a_bf, kwin, vt_win, valid, sink_row):
    s = lax.dot_general(kwin, a_bf, (((1,), (1,)), ((), ())), preferred_element_type=F32)
    if valid is not None:
        s = jnp.where(valid, s, -jnp.inf)
    m = jnp.maximum(jnp.max(s, axis=0, keepdims=True), sink_row)
    e = jnp.exp(s - m)
    den = jnp.sum(e, axis=0, keepdims=True) + jnp.exp(sink_row - m)
    o = jnp.dot(vt_win, e.astype(BF16), preferred_element_type=F32)
    return o * (1.0 / den)


def _attend(a_bf, kvar, vvar, valid, sink_col):
    s = lax.dot_general(a_bf, kvar, (((1,), (1,)), ((), ())), preferred_element_type=F32)
    for cond in valid:
        s = jnp.where(cond, s, -jnp.inf)
    m = jnp.maximum(jnp.max(s, axis=-1, keepdims=True), sink_col)
    e = jnp.exp(s - m)
    den = jnp.sum(e, axis=-1, keepdims=True) + jnp.exp(sink_col - m)
    o = jnp.dot(e.astype(BF16), vvar, preferred_element_type=F32)
    return o * (1.0 / den)


def _prompt_in_kernel(sinks_ref, x_ref, g1_ref, win_ref, cos_ref, sin_ref,
                      a_ref, ug_ref, kp_ref, vp_ref, q_s, k_s, vb_s, vt_s, u_s):
    tb = x_ref.shape[1]
    nk = tb // SSD_T
    blk = pl.program_id(1)
    qs, k, v, u = _in_proj(x_ref[0], g1_ref, win_ref, cos_ref, sin_ref)
    for j, qb in enumerate(qs):
        q_s[:, LANES * j:LANES * (j + 1)] = qb
    for j in range(D_SSM // LANES):
        u_s[j] = u[:, LANES * j:LANES * (j + 1)]

    nbuf = WINDOW + tb

    @pl.when(blk == 0)
    def _():
        k_s[:, 0:WINDOW, :] = jnp.zeros((4, WINDOW, LANES), BF16)
        vb_s[0:WINDOW, :] = jnp.zeros((WINDOW, LANES), F32)

    @pl.when(blk > 0)
    def _():
        k_s[:, 0:WINDOW, :] = k_s[:, tb:nbuf, :]
        vb_s[0:WINDOW, :] = vb_s[tb:nbuf, :]

    for idx, arr in enumerate(_k_variants(k)):
        k_s[idx, WINDOW:nbuf, :] = arr
    vb_s[WINDOW:nbuf, :] = v

    @pl.when(blk == pl.num_programs(1) - 1)
    def _():
        kp_ref[0] = k[tb - WINDOW:, :]
        vp_ref[0] = v[tb - WINDOW:, :]

    vt0 = vb_s[...].T
    vt1 = pltpu.roll(vt0, nbuf - CHUNK, 1)
    zrows = jnp.zeros((HEAD_DIM, nbuf), F32)
    for sh, vt in ((0, vt0), (1, vt1)):
        for g in range(N_KV_HEADS):
            rows = vt[HEAD_DIM * g:HEAD_DIM * (g + 1), :]
            vt_s[sh, g, 0] = jnp.concatenate([rows, zrows], axis=0).astype(BF16)
            vt_s[sh, g, 1] = jnp.concatenate([zrows, rows], axis=0).astype(BF16)

    nkeys = WINDOW + CHUNK
    key_row = lax.broadcasted_iota(jnp.int32, (nkeys, 1), 0)
    q_lane = lax.broadcasted_iota(jnp.int32, (1, LANES), 1)
    for i in range(tb // CHUNK):
        r0 = i * CHUNK
        sh = i % 2
        base = r0 - sh * CHUNK
        valid = key_row >= (WINDOW - r0 - blk * tb) if i < N_WIN_CHUNKS else None
        for g in range(N_KV_HEADS):
            qa = q_s[r0:r0 + CHUNK, 2 * LANES * g:2 * LANES * g + LANES]
            qb = q_s[r0:r0 + CHUNK, 2 * LANES * g + LANES:2 * LANES * (g + 1)]
            a_bf = jnp.concatenate([qa, qb], axis=0).astype(BF16)
            acc = None
            for par in range(2):
                h0 = 4 * g + par
                sink_row = jnp.where(q_lane < CHUNK, sinks_ref[h0], sinks_ref[h0 + 2])
                o = _attend_t(a_bf, k_s[2 * g + par, r0:r0 + nkeys, :],
                              vt_s[sh, g, par, :, base:base + nkeys], valid, sink_row)
                acc = o if acc is None else acc + o
            at = acc.T
            a_ref[0, r0:r0 + CHUNK, 2 * LANES * g:2 * LANES * g + LANES] = at[:CHUNK]
            a_ref[0, r0:r0 + CHUNK, 2 * LANES * g + LANES:2 * LANES * (g + 1)] = at[CHUNK:]

    def put(g, half, val):
        ug_ref[g, 0, :, LANES * half:LANES * (half + 1)] = val

    _to_groups(u_s, put, nk, strided=True)


def _prompt_ssd_kernel(ug_ref, m_ref, wd_ref, r_ref, tab_ref, abar_ref,
                       yg_ref, sfin_ref, d_s, sp_s, carry_s):
    nb = ug_ref.shape[1]
    nc = ug_ref.shape[2]
    step = pl.program_id(0)

    @pl.when(step == 0)
    def _():
        carry_s[...] = jnp.zeros(carry_s.shape, F32)

    def state_in(g, c):
        ub = ug_ref[g].reshape(nb * nc, SSD_W)
        d_s[g] = jnp.dot(ub, wd_ref[g], preferred_element_type=F32)
        return c

    lax.fori_loop(0, N_GROUPS, state_in, 0, unroll=GROUP_UNROLL)

    pw = [tab_ref[i] for i in range(8)]
    are, aim = abar_ref[0], abar_ref[1]
    kio = lax.broadcasted_iota(jnp.int32, (N_GROUPS, nc, STATE_W), 1)
    for b in range(nb):
        dg = d_s[:, b * nc:(b + 1) * nc, :]
        s_in = carry_s[b]
        dsh = jnp.where(kio == 0, s_in, pltpu.roll(dg, 1, 1))
        c = jnp.zeros((N_GROUPS, 1, STATE_W), F32)
        for j in range(nc // SUBLANES):
            x = dsh[:, SUBLANES * j:SUBLANES * (j + 1), :]
            for lvl in range(3):
                xs = pltpu.roll(x, 1 << lvl, 1)
                x = x + _cmul(pw[2 * lvl], pw[2 * lvl + 1], xs)
            cb = jnp.broadcast_to(c, x.shape)
            h = x + _cmul(pw[6], pw[7], cb)
            sp_s[:, b * nc + SUBLANES * j:b * nc + SUBLANES * (j + 1), :] = h
            c = h[:, SUBLANES - 1:SUBLANES, :]
        carry_s[b] = _cmul(are, aim, c) + dg[:, nc - 1:nc, :]

    def readout(g, c):
        ub = ug_ref[g].reshape(nb * nc, SSD_W)
        y = jnp.dot(ub, m_ref[g], preferred_element_type=F32)
        y = y + lax.dot_general(sp_s[g].astype(BF16), r_ref[g], (((1,), (1,)), ((), ())),
                                preferred_element_type=F32)
        yg_ref[g] = y.reshape(nb, nc, SSD_W).astype(BF16)
        return c

    lax.fori_loop(0, N_GROUPS, readout, 0, unroll=GROUP_UNROLL)
    sfin_ref[...] = carry_s[...]


def _out_kernel(*refs, strided, cs, has_prev):
    if has_prev:
        x_ref, a_ref, yg_ref, prev_ref = refs[:4]
        refs = refs[4:]
    else:
        x_ref, a_ref, yg_ref = refs[:3]
        prev_ref = None
        refs = refs[3:]
    (wglu_ref, ga_ref, gs_ref, wout_ref, g2_ref, wup_ref, cw_ref, cb_ref, wdown_ref, gf_ref,
     y_ref, cst_ref, ys_s, ext_s, px_s, py_s) = refs
    tb = x_ref.shape[1]
    n_sub, _, sb, _ = ys_s.shape
    nks = sb // SSD_T
    planes = strided
    pr = sb // SUBLANES
    pad = SUBLANES if planes else ext_s.shape[0] - sb
    blk = pl.program_id(1)
    assert n_sub * sb == tb and (not planes or (cs == 1 and not has_prev))

    @pl.when(blk == 0)
    def _():
        if planes:
            ext_s[:, 0:pad, :] = jnp.zeros((2, pad, 2 * D_FF), F32)
        elif has_prev:
            ext_s[0:pad, :] = prev_ref[0]
        else:
            ext_s[0:pad, :] = jnp.zeros((pad, 2 * D_FF), F32)

    def mix(i):
        rows = slice(i * sb, (i + 1) * sb)
        _from_groups(lambda g, half: yg_ref[g, 0, i * nks:(i + 1) * nks, LANES * half:LANES * (half + 1)],
                     ys_s.at[i], nks, strided)
        yv = jnp.concatenate([ys_s[i, j] for j in range(D_SSM // LANES)], axis=-1)
        z = 0.5 * yv * (1.0 + jnp.tanh(math.sqrt(2.0 / math.pi) * (yv + 0.044715 * (yv * yv * yv))))
        gate = jnp.dot(z.astype(BF16), wglu_ref[...], preferred_element_type=F32)
        s_out = z * jax.nn.sigmoid(gate)
        na = _rms(a_ref[0, rows, :], ga_ref[...]).astype(BF16)
        ns = _rms(s_out, gs_ref[...]).astype(BF16)
        x1 = x_ref[0, rows, :] + jnp.dot(na, wout_ref[0:D_ATTN, :], preferred_element_type=F32)
        return x1 + jnp.dot(ns, wout_ref[D_ATTN:, :], preferred_element_type=F32)

    def tile_cols(j, half):
        return slice(half * D_FF + j * FF_TILE, half * D_FF + (j + 1) * FF_TILE)

    def to_planes(x):
        for c in range(D_MODEL // LANES):
            px_s[c] = x[:, LANES * c:LANES * (c + 1)]
        return jnp.concatenate(
            [jnp.concatenate([px_s[c, pl.ds(r, pr, stride=SUBLANES), :] for r in range(SUBLANES)], axis=0)
             for c in range(D_MODEL // LANES)], axis=1)

    def store_rows(i, y):
        rows = slice(i * sb, (i + 1) * sb)
        if not planes:
            y_ref[0, rows, :] = y
            return
        for c in range(D_MODEL // LANES):
            for r in range(SUBLANES):
                py_s[c, pl.ds(r, pr, stride=SUBLANES), :] = y[r * pr:(r + 1) * pr, LANES * c:LANES * (c + 1)]
            y_ref[0, rows, LANES * c:LANES * (c + 1)] = py_s[c]

    def time_shifts(up, cols):
        if not planes:
            ext_s[pad:pad + sb, cols] = up
            sh2 = ext_s[pad - 2 * cs:pad - 2 * cs + sb, cols]
            sh1 = ext_s[pad - cs:pad - cs + sb, cols]
            ext_s[0:pad, cols] = ext_s[sb:sb + pad, cols]
            return sh2, sh1
        for p in range(2):
            ext_s[p, pad:pad + pr, cols] = up[(6 + p) * pr:(7 + p) * pr, :]
        s6 = ext_s[0, pad - 1:pad - 1 + pr, cols]
        s7 = ext_s[1, pad - 1:pad - 1 + pr, cols]
        ext_s[:, 0:pad, cols] = ext_s[:, pr:pr + pad, cols]
        sh1 = jnp.concatenate([s7, up[:7 * pr, :]], axis=0)
        sh2 = jnp.concatenate([s6, s7, up[:6 * pr, :]], axis=0)
        return sh2, sh1

    def conv_ffn(i, x1):
        if planes:
            x1 = to_planes(x1)
        h2 = _rms(x1, g2_ref[...]).astype(BF16)

        def up_proj(j):
            return [jnp.dot(h2, wup_ref[:, tile_cols(j, half)], preferred_element_type=F32) for half in range(2)]

        def down_proj(act, j):
            return jnp.dot(act, wdown_ref[j * FF_TILE:(j + 1) * FF_TILE, :], preferred_element_type=F32)

        n_tiles = D_FF // FF_TILE
        acc = jnp.zeros((sb, D_MODEL), F32)
        ups = up_proj(0)
        act_prev = None
        for j in range(n_tiles):
            nxt = up_proj(j + 1) if j + 1 < n_tiles else None
            if act_prev is not None:
                acc = acc + down_proj(act_prev, j - 1)
            parts = []
            for half in range(2):
                cols = tile_cols(j, half)
                sh2, sh1 = time_shifts(ups[half], cols)
                parts.append(sh2 * cw_ref[0:1, cols] + sh1 * cw_ref[1:2, cols]
                             + ups[half] * cw_ref[2:3, cols] + cb_ref[:, cols])
            cg, cv = parts
            act_prev = (cg * jax.nn.sigmoid(cg) * cv).astype(BF16)
            ups = nxt
        acc = acc + down_proj(act_prev, n_tiles - 1)
        store_rows(i, _rms(x1 + acc, gf_ref[...]))

    mixed = mix(0)
    for i in range(n_sub):
        nxt_mixed = mix(i + 1) if i + 1 < n_sub else None
        conv_ffn(i, mixed)
        mixed = nxt_mixed
    if planes:
        cst_ref[0] = ext_s[:, 0:pad, :]
    else:
        cst_ref[0] = ext_s[0:pad, :]


def _decode_in_kernel(sinks_ref, x_ref, g1_ref, win_ref, cos_ref, sin_ref, kc_ref, vc_ref, h0_ref,
                      m_ref, wd_ref, r_ref, abar_ref,
                      a_ref, kn_ref, vn_ref, yg_ref, sn_ref, u_s, ug_s, *, n_streams, n_steps, n_cache):
    rows = n_streams * n_steps
    qs, k, v, u = _in_proj(x_ref[...], g1_ref, win_ref, cos_ref, sin_ref)
    kn_ref[...] = k
    vn_ref[...] = v
    for j in range(D_SSM // LANES):
        u_s[j] = u[:, LANES * j:LANES * (j + 1)]
    ncache_rows = n_streams * n_cache
    nkeys = ncache_rows + rows
    kall = jnp.concatenate([kc_ref[...], k], axis=0)
    vall = jnp.concatenate([vc_ref[...], v], axis=0)
    variants = _kv_variants(kall, vall)

    colv = lax.broadcasted_iota(jnp.int32, (1, nkeys), 1)
    is_new = colv >= ncache_rows
    cnew = colv - ncache_rows
    k_stream = jnp.where(is_new, _pmod(cnew, n_streams), _pdiv(colv, n_cache))
    k_pos = jnp.where(is_new, PAST_LEN + _pdiv(cnew, n_streams), PAST_LEN - n_cache + _pmod(colv, n_cache))
    k_chunk = _pdiv(k_pos, CHUNK)
    rowv = _pmod(lax.broadcasted_iota(jnp.int32, (2 * rows, 1), 0), rows)
    q_stream = _pmod(rowv, n_streams)
    q_chunk = _pdiv(PAST_LEN + _pdiv(rowv, n_streams), CHUNK)
    ok_col = jnp.where(k_pos >= 0, k_stream, -1)
    d_chunk = q_chunk - k_chunk
    valid = [q_stream == ok_col,
             lax.bitcast_convert_type(d_chunk, jnp.uint32) <= jnp.uint32(N_WIN_CHUNKS)]
    top = lax.broadcasted_iota(jnp.int32, (2 * rows, 1), 0) < rows

    for g in range(N_KV_HEADS):
        a_bf = jnp.concatenate([qs[2 * g], qs[2 * g + 1]], axis=0).astype(BF16)
        acc = jnp.zeros((2 * rows, LANES), F32)
        for par in range(2):
            h0 = 4 * g + par
            sink_col = jnp.where(top, sinks_ref[h0], sinks_ref[h0 + 2])
            acc = acc + _attend(a_bf, variants[2 * g + par], variants[4 + 2 * g + par], valid, sink_col)
        a_ref[:, 2 * LANES * g:2 * LANES * g + LANES] = acc[:rows]
        a_ref[:, 2 * LANES * g + LANES:2 * LANES * (g + 1)] = acc[rows:]

    def put(g, half, val):
        ug_s[g, :, LANES * half:LANES * (half + 1)] = val

    _to_groups(u_s, put, n_streams, strided=False)
    def group_body(g, c):
        ub = ug_s[g]
        h0g = h0_ref[g]
        d = jnp.dot(ub, wd_ref[g], preferred_element_type=F32)
        sn_ref[g] = _cmul(abar_ref[0, g], abar_ref[1, g], h0g) + d
        y = jnp.dot(ub, m_ref[g], preferred_element_type=F32)
        y = y + lax.dot_general(h0g.astype(BF16), r_ref[g], (((1,), (1,)), ((), ())),
                                preferred_element_type=F32)
        yg_ref[g] = y.astype(BF16)
        return c

    lax.fori_loop(0, N_GROUPS, group_body, 0, unroll=GROUP_UNROLL)


def _ssd_tables(a_re, a_im, log_dt, b_re, b_im, c_re, c_im, d_skip):
    dup = lambda x: jnp.concatenate([x, x], axis=-1)
    prm = jnp.stack([dup(a_re), dup(a_im), jnp.broadcast_to(log_dt[:, None], (N_GROUPS, STATE_W))], axis=1)
    bt = jnp.concatenate([b_re.transpose(0, 2, 1), b_im.transpose(0, 2, 1)], axis=-1)
    cc = jnp.concatenate([c_re, c_im], axis=-1)
    dtile = jnp.tile(d_skip, (1, SSD_T))[:, None, :]
    gb = TABLE_GROUPS
    blk3 = lambda r, c: pl.BlockSpec((gb, r, c), lambda i: (i, 0, 0))
    return pl.pallas_call(
        _ssd_tables_kernel,
        grid=(N_GROUPS // gb,),
        in_specs=[blk3(3, STATE_W), blk3(SSM_GROUP, STATE_W), blk3(SSM_GROUP, STATE_W), blk3(1, SSD_W)],
        out_specs=[blk3(SSD_W, SSD_W), blk3(SSD_W, STATE_W), blk3(SSD_W, STATE_W),
                   pl.BlockSpec((8, gb, SUBLANES, STATE_W), lambda i: (0, i, 0, 0)),
                   pl.BlockSpec((2, gb, 1, STATE_W), lambda i: (0, i, 0, 0))],
        out_shape=[jax.ShapeDtypeStruct((N_GROUPS, SSD_W, SSD_W), BF16),
                   jax.ShapeDtypeStruct((N_GROUPS, SSD_W, STATE_W), BF16),
                   jax.ShapeDtypeStruct((N_GROUPS, SSD_W, STATE_W), BF16),
                   jax.ShapeDtypeStruct((8, N_GROUPS, SUBLANES, STATE_W), F32),
                   jax.ShapeDtypeStruct((2, N_GROUPS, 1, STATE_W), F32)],
        compiler_params=pltpu.CompilerParams(dimension_semantics=("arbitrary",)),
        name="ssd_tables",
    )(prm, bt, cc, dtile)


def _ssd_tables_kernel(prm_ref, bt_ref, cc_ref, dt_ref, m_ref, wd_ref, rt_ref, tab_ref, abar_ref):
    t = SSD_T
    lane = lax.broadcasted_iota(jnp.int32, (1, STATE_W), 1)
    sgn = jnp.where(lane >= SSM_STATE, 1.0, -1.0).astype(F32)
    conj = -sgn
    step = lax.broadcasted_iota(jnp.int32, (t, 1), 0).astype(F32)
    sub_i = lax.broadcasted_iota(jnp.int32, (SUBLANES, 1), 0)
    sub = sub_i.astype(F32)
    row = lax.broadcasted_iota(jnp.int32, (SSD_W, SSD_W), 0)
    col = lax.broadcasted_iota(jnp.int32, (SSD_W, SSD_W), 1)
    causal = _pdiv(col, SSM_GROUP) >= _pdiv(row, SSM_GROUP)
    diag = row == col
    for gi in range(prm_ref.shape[0]):
        a_re, a_im = prm_ref[gi, 0:1, :], prm_ref[gi, 1:2, :]
        dt = jnp.exp(prm_ref[gi, 2:3, :])
        lr, li = dt * a_re, dt * a_im

        def power(n):
            mag = jnp.exp(n * lr)
            return mag * jnp.cos(n * li), mag * jnp.sin(n * li) * sgn

        def times_powers(z, zs, pw):
            pr, pis = pw
            return jnp.concatenate([z * pr[n:n + 1, :] + zs * pis[n:n + 1, :] for n in range(t)], axis=0)

        ar, ais = power(jnp.ones((1, 1), F32))
        den = a_re * a_re + a_im * a_im
        nr, ni = ar - 1.0, ais * sgn
        fr, fis = (nr * a_re + ni * a_im) / den, (ni * a_re - nr * a_im) / den * sgn
        bt = bt_ref[gi]
        bb = bt * fr + pltpu.roll(bt, SSM_STATE, 1) * fis
        bbs = pltpu.roll(bb, SSM_STATE, 1)
        cc = cc_ref[gi]
        ccs = pltpu.roll(cc, SSM_STATE, 1)
        x = times_powers(bb, bbs, power(-step)) * conj
        y = times_powers(cc, ccs, power(step))
        m = lax.dot_general(x, y, (((1,), (1,)), ((), ())), precision=lax.Precision.HIGHEST,
                            preferred_element_type=F32)
        m = jnp.where(causal, m, 0.0) + jnp.where(diag, dt_ref[gi], 0.0)
        m_ref[gi] = m.astype(BF16)
        wd_ref[gi] = times_powers(bb, bbs, power(float(t - 1) - step)).astype(BF16)
        rt_ref[gi] = (times_powers(cc, ccs, power(step + 1.0)) * conj).astype(BF16)
        for lvl in range(3):
            d = 1 << lvl
            pr, pis = power(jnp.full((SUBLANES, 1), float(t * d), F32))
            tab_ref[2 * lvl, gi] = jnp.where(sub_i >= d, pr, 0.0)
            tab_ref[2 * lvl + 1, gi] = jnp.where(sub_i >= d, pis, 0.0)
        pr, pis = power(float(t) * (sub + 1.0))
        tab_ref[6, gi] = pr
        tab_ref[7, gi] = pis
        pr, pis = power(jnp.full((1, 1), float(t), F32))
        abar_ref[0, gi] = pr
        abar_ref[1, gi] = pis


def _rope_tables(pos):
    half = HEAD_DIM // 2
    inv = ROPE_THETA ** (-jnp.arange(half, dtype=F32) / half)
    ang = pos.astype(F32)[:, None] * inv[None, :]
    cos, sin = jnp.cos(ang), jnp.sin(ang)
    cos_t = jnp.concatenate([cos, cos, cos, cos], axis=-1)
    sin_t = jnp.concatenate([-sin, sin, -sin, sin], axis=-1)
    return cos_t, sin_t


def _const(shape):
    nd = len(shape)
    return pl.BlockSpec(shape, lambda *_: (0,) * nd, pipeline_mode=pl.Buffered(1))


def _whole(shape):
    nd = len(shape)
    return pl.BlockSpec(shape, lambda *_: (0,) * nd)


def _out_call(x, a, yg, prev, w, *, tb, sb, strided, cs, pad):
    nb, length, _ = x.shape
    nk = tb // SSD_T
    grid = (nb, length // tb)
    in_specs = [
        pl.BlockSpec((1, tb, D_MODEL), lambda b, i: (b, i, 0)),
        pl.BlockSpec((1, tb, D_ATTN), lambda b, i: (b, i, 0)),
        pl.BlockSpec((N_GROUPS, 1, nk, SSD_W), lambda b, i: (0, b, i, 0)),
    ]
    args = [x, a, yg]
    if prev is not None:
        in_specs.append(pl.BlockSpec((1, pad, 2 * D_FF), lambda b, i: (b, 0, 0)))
        args.append(prev)
    weights = [w['w_glu'], w['onorm_a'], w['onorm_s'], w['w_out'], w['norm2'], w['w_up'], w['conv_w'],
               w['conv_b'], w['w_down'], w['final_g']]
    in_specs += [_const(t.shape) for t in weights]
    kern = functools.partial(_out_kernel, strided=strided, cs=cs, has_prev=prev is not None)
    if strided:
        ext_shape = (2, pad + sb // SUBLANES, 2 * D_FF)
        cst_block, cst_map = (1, 2, pad, 2 * D_FF), (lambda b, i: (b, 0, 0, 0))
    else:
        ext_shape = (pad + sb, 2 * D_FF)
        cst_block, cst_map = (1, pad, 2 * D_FF), (lambda b, i: (b, 0, 0))
    return pl.pallas_call(
        kern,
        grid=grid,
        in_specs=in_specs,
        out_specs=[pl.BlockSpec((1, tb, D_MODEL), lambda b, i: (b, i, 0)),
                   pl.BlockSpec(cst_block, cst_map)],
        out_shape=[jax.ShapeDtypeStruct((nb, length, D_MODEL), F32),
                   jax.ShapeDtypeStruct((nb,) + cst_block[1:], F32)],
        scratch_shapes=[pltpu.VMEM((tb // sb, D_SSM // LANES, sb, LANES), F32),
                        pltpu.VMEM(ext_shape, F32),
                        pltpu.VMEM((D_MODEL // LANES, sb, LANES), F32),
                        pltpu.VMEM((D_MODEL // LANES, sb, LANES), F32)],
        compiler_params=pltpu.CompilerParams(dimension_semantics=("arbitrary", "arbitrary"),
                                             vmem_limit_bytes=VMEM_LIMIT),
        name="layer_out",
    )(*args, *weights)


def kernel(x_prompt, x_sample, cache_k, cache_v, state_ssm_re, state_ssm_im, state_conv, norm1_g, w_in, attn_sinks, ssm_A_re, ssm_A_im, ssm_log_dt, ssm_B_re, ssm_B_im, ssm_C_re, ssm_C_im, ssm_D, w_glu, onorm_attn_g, onorm_ssm_g, w_out, norm2_g, w_up, conv_w, conv_b, w_down, final_g):
    assert norm1_g.shape[0] == 1, "one layer"
    nb, length, _ = x_prompt.shape
    ns, nt, _ = x_sample.shape
    n_cache = cache_k.shape[2]
    assert nt == SSD_T and length % TB_IN == 0 and length % TB_OUT == 0 and (length // SSD_T) % NC_SSD == 0

    m_mat, wd_mat, r_mat, tab, abar = _ssd_tables(ssm_A_re[0], ssm_A_im[0], ssm_log_dt[0], ssm_B_re[0],
                                                  ssm_B_im[0], ssm_C_re[0], ssm_C_im[0], ssm_D[0])
    sinks = attn_sinks[0]
    g1 = norm1_g
    win = w_in[0].astype(BF16)
    w = dict(w_glu=w_glu[0].astype(BF16), onorm_a=onorm_attn_g, onorm_s=onorm_ssm_g, w_out=w_out[0].astype(BF16),
             norm2=norm2_g, w_up=w_up[0].astype(BF16), conv_w=conv_w[0], conv_b=conv_b,
             w_down=w_down[0].astype(BF16), final_g=final_g[None, :])
    smem = pl.BlockSpec(memory_space=pltpu.SMEM)

    cos_p, sin_p = _rope_tables(jnp.arange(length))
    nk = TB_IN // SSD_T
    a_p, ug_p, k_p, v_p = pl.pallas_call(
        _prompt_in_kernel,
        grid=(nb, length // TB_IN),
        in_specs=[smem,
                  pl.BlockSpec((1, TB_IN, D_MODEL), lambda b, i: (b, i, 0)),
                  _const((1, D_MODEL)), _const((D_MODEL, D_IN)),
                  pl.BlockSpec((TB_IN, LANES), lambda b, i: (i, 0)),
                  pl.BlockSpec((TB_IN, LANES), lambda b, i: (i, 0))],
        out_specs=[pl.BlockSpec((1, TB_IN, D_ATTN), lambda b, i: (b, i, 0)),
                   pl.BlockSpec((N_GROUPS, 1, nk, SSD_W), lambda b, i: (0, b, i, 0)),
                   pl.BlockSpec((1, WINDOW, KV_W), lambda b, i: (b, 0, 0)),
                   pl.BlockSpec((1, WINDOW, KV_W), lambda b, i: (b, 0, 0))],
        out_shape=[jax.ShapeDtypeStruct((nb, length, D_ATTN), F32),
                   jax.ShapeDtypeStruct((N_GROUPS, nb, length // SSD_T, SSD_W), BF16),
                   jax.ShapeDtypeStruct((nb, WINDOW, KV_W), F32),
                   jax.ShapeDtypeStruct((nb, WINDOW, KV_W), F32)],
        scratch_shapes=[pltpu.VMEM((TB_IN, D_ATTN), F32),
                        pltpu.VMEM((2 * N_KV_HEADS, WINDOW + TB_IN, LANES), BF16),
                        pltpu.VMEM((WINDOW + TB_IN, LANES), F32),
                        pltpu.VMEM((2, N_KV_HEADS, 2, LANES, WINDOW + TB_IN), BF16),
                        pltpu.VMEM((D_SSM // LANES, TB_IN, LANES), F32)],
        compiler_params=pltpu.CompilerParams(dimension_semantics=("arbitrary", "arbitrary"),
                                             vmem_limit_bytes=VMEM_LIMIT),
        name="prompt_in",
    )(sinks, x_prompt, g1, win, cos_p, sin_p)

    yg_p, sfin_p = pl.pallas_call(
        _prompt_ssd_kernel,
        grid=(length // SSD_T // NC_SSD,),
        in_specs=[pl.BlockSpec((N_GROUPS, nb, NC_SSD, SSD_W), lambda i: (0, 0, i, 0)),
                  _const(m_mat.shape), _const(wd_mat.shape), _const(r_mat.shape), _const(tab.shape),
                  _const(abar.shape)],
        out_specs=[pl.BlockSpec((N_GROUPS, nb, NC_SSD, SSD_W), lambda i: (0, 0, i, 0)),
                   pl.BlockSpec((nb, N_GROUPS, 1, STATE_W), lambda i: (0, 0, 0, 0))],
        out_shape=[jax.ShapeDtypeStruct((N_GROUPS, nb, length // SSD_T, SSD_W), BF16),
                   jax.ShapeDtypeStruct((nb, N_GROUPS, 1, STATE_W), F32)],
        scratch_shapes=[pltpu.VMEM((N_GROUPS, nb * NC_SSD, STATE_W), F32),
                        pltpu.VMEM((N_GROUPS, nb * NC_SSD, STATE_W), F32),
                        pltpu.VMEM((nb, N_GROUPS, 1, STATE_W), F32)],
        compiler_params=pltpu.CompilerParams(dimension_semantics=("arbitrary",), vmem_limit_bytes=VMEM_LIMIT),
        name="prompt_ssd",
    )(ug_p, m_mat, wd_mat, r_mat, tab, abar)

    y_p, cst_p = _out_call(x_prompt, a_p, yg_p, None, w, tb=TB_OUT, sb=SUB_OUT, strided=True, cs=1, pad=SUBLANES)

    rows = ns * nt
    xs = x_sample.transpose(1, 0, 2).reshape(rows, D_MODEL)
    cos_s, sin_s = _rope_tables(PAST_LEN + jnp.arange(rows) // ns)
    kc = cache_k[0].reshape(ns * n_cache, KV_W)
    vc = cache_v[0].reshape(ns * n_cache, KV_W)
    h0 = jnp.concatenate([state_ssm_re[0], state_ssm_im[0]], axis=-1).transpose(1, 0, 2)
    dec = functools.partial(_decode_in_kernel, n_streams=ns, n_steps=nt, n_cache=n_cache)
    dec_in = [xs, g1, win, cos_s, sin_s, kc, vc, h0, m_mat, wd_mat, r_mat, abar]
    a_s, kn, vn, yg_s, sn = pl.pallas_call(
        dec,
        grid=(1,),
        in_specs=[smem] + [_const(t.shape) for t in dec_in],
        out_specs=[_whole((rows, D_ATTN)), _whole((rows, KV_W)), _whole((rows, KV_W)),
                   _whole((N_GROUPS, ns, SSD_W)), _whole((N_GROUPS, ns, STATE_W))],
        out_shape=[jax.ShapeDtypeStruct((rows, D_ATTN), F32), jax.ShapeDtypeStruct((rows, KV_W), F32),
                   jax.ShapeDtypeStruct((rows, KV_W), F32), jax.ShapeDtypeStruct((N_GROUPS, ns, SSD_W), BF16),
                   jax.ShapeDtypeStruct((N_GROUPS, ns, STATE_W), F32)],
        scratch_shapes=[pltpu.VMEM((D_SSM // LANES, rows, LANES), F32), pltpu.VMEM((N_GROUPS, ns, SSD_W), BF16)],
        compiler_params=pltpu.CompilerParams(dimension_semantics=("arbitrary",), vmem_limit_bytes=VMEM_LIMIT),
        name="decode_in",
    )(sinks, *dec_in)

    cs = ns
    prev = state_conv[0].transpose(1, 0, 2).reshape(1, (CONV_W - 1) * cs, 2 * D_FF)
    y_s, cst_s = _out_call(xs[None], a_s[None], yg_s[:, None], prev, w, tb=rows, sb=rows, strided=False, cs=cs,
                           pad=(CONV_W - 1) * cs)

    kv5 = lambda t, n: t.reshape(1, n, -1, N_KV_HEADS, HEAD_DIM)
    y_sample = y_s.reshape(nt, ns, D_MODEL).transpose(1, 0, 2)
    kn_b = kn.reshape(nt, ns, KV_W).transpose(1, 0, 2)
    vn_b = vn.reshape(nt, ns, KV_W).transpose(1, 0, 2)
    k_new = jnp.concatenate([cache_k[0].reshape(ns, n_cache, KV_W), kn_b], axis=1)[:, -n_cache:]
    v_new = jnp.concatenate([cache_v[0].reshape(ns, n_cache, KV_W), vn_b], axis=1)[:, -n_cache:]
    conv_p = cst_p[:, :, SUBLANES - 1, :][None]
    conv_s = cst_s.reshape(CONV_W - 1, ns, 2 * D_FF).transpose(1, 0, 2)[None]
    sn_b = sn.transpose(1, 0, 2)
    return (y_p, y_sample,
            kv5(k_p, nb), kv5(v_p, nb),
            sfin_p[None, :, :, 0, :SSM_STATE], sfin_p[None, :, :, 0, SSM_STATE:], conv_p,
            kv5(k_new, ns), kv5(v_new, ns),
            sn_b[None, :, :, :SSM_STATE], sn_b[None, :, :, SSM_STATE:], conv_s)
```

```python
import functools
import math

import numpy as np
import jax
import jax.numpy as jnp
from jax import lax
from jax.experimental import pallas as pl
from jax.experimental.pallas import tpu as pltpu

F32 = jnp.float32
BF16 = jnp.bfloat16

D_MODEL = 1024
CHUNK = 64
D_ATTN = 512
D_SSM = 512
HEAD_DIM = 64
N_HEADS = 8
N_KV_HEADS = 2
KV_W = N_KV_HEADS * HEAD_DIM
WINDOW = 128
N_WIN_CHUNKS = WINDOW // CHUNK
ROPE_THETA = 10000.0
SSM_GROUP = 16
N_GROUPS = D_SSM // SSM_GROUP
SSM_STATE = 64
D_FF = 2816
CONV_W = 3
EPS = 1e-6
D_IN = D_ATTN + 2 * KV_W + D_SSM
PAST_LEN = 2048

SSD_T = 16
LANES = 128
SUBLANES = 8
GROUPS_PER_VREG = LANES // SSM_GROUP
SSD_W = SSD_T * SSM_GROUP
STATE_W = 2 * SSM_STATE

TB_IN = 512
TB_OUT = 512
SUB_OUT = 256
NC_SSD = 64
FF_TILE = 256
GROUP_UNROLL = 8
TABLE_GROUPS = 8
VMEM_LIMIT = 56 * 1024 * 1024


def _log2(n):
    assert n > 0 and n & (n - 1) == 0, n
    return n.bit_length() - 1


def _pdiv(x, n):
    return lax.shift_right_arithmetic(x, jnp.int32(_log2(n)))


def _pmod(x, n):
    return x & (n - 1)


def _rms(x, g):
    return x * lax.rsqrt(jnp.mean(x * x, axis=-1, keepdims=True) + EPS) * g


def _rope(xb, cos, sin_signed, first_half):
    partner = jnp.where(first_half, pltpu.roll(xb, LANES - HEAD_DIM // 2, 1), pltpu.roll(xb, HEAD_DIM // 2, 1))
    return xb * cos + partner * sin_signed


def _group_transpose8(xs, lane_group):
    xs = list(xs)
    for d in (4, 2, 1):
        keep = (lane_group & d) == 0
        nxt = list(xs)
        for i in range(GROUPS_PER_VREG):
            if i & d:
                continue
            a, b = xs[i], xs[i | d]
            nxt[i] = jnp.where(keep, a, pltpu.roll(b, SSM_GROUP * d, 1))
            nxt[i | d] = jnp.where(keep, pltpu.roll(a, LANES - SSM_GROUP * d, 1), b)
        xs = nxt
    return xs


def _row_sel(s, nk, strided):
    if strided:
        return pl.ds(s, nk, stride=SSD_T)
    return pl.ds(s * nk, nk)


def _to_groups(u_ref, put, nk, strided):
    lane_group = _pdiv(lax.broadcasted_iota(jnp.int32, (nk, LANES), 1), SSM_GROUP)
    for j in range(D_SSM // LANES):
        for half in range(SSD_T // GROUPS_PER_VREG):
            xs = [u_ref[j, _row_sel(GROUPS_PER_VREG * half + sl, nk, strided), :]
                  for sl in range(GROUPS_PER_VREG)]
            ws = _group_transpose8(xs, lane_group)
            for gq in range(GROUPS_PER_VREG):
                put(GROUPS_PER_VREG * j + gq, half, ws[gq].astype(BF16))


def _from_groups(get, y_ref, nk, strided):
    lane_group = _pdiv(lax.broadcasted_iota(jnp.int32, (nk, LANES), 1), SSM_GROUP)
    for j in range(D_SSM // LANES):
        for half in range(SSD_T // GROUPS_PER_VREG):
            ws = [get(GROUPS_PER_VREG * j + gq, half).astype(F32) for gq in range(GROUPS_PER_VREG)]
            xs = _group_transpose8(ws, lane_group)
            for sl in range(GROUPS_PER_VREG):
                y_ref[j, _row_sel(GROUPS_PER_VREG * half + sl, nk, strided), :] = xs[sl]


def _cmul(re_full, im_signed, z):
    return re_full * z + im_signed * pltpu.roll(z, SSM_STATE, z.ndim - 1)


def _in_proj(x, g1_ref, win_ref, cos_ref, sin_ref):
    rows = x.shape[0]
    hn = _rms(x, g1_ref[...]).astype(BF16)
    proj = jnp.dot(hn, win_ref[...], preferred_element_type=F32)
    cos = cos_ref[...]
    sin = sin_ref[...]
    lane = lax.broadcasted_iota(jnp.int32, (rows, LANES), 1)
    first_half = _pmod(lane, HEAD_DIM) < (HEAD_DIM // 2)
    scale = HEAD_DIM ** -0.5
    qs = [_rope(proj[:, LANES * j:LANES * (j + 1)], cos, sin, first_half) * scale for j in range(D_ATTN // LANES)]
    k = _rope(proj[:, D_ATTN:D_ATTN + KV_W], cos, sin, first_half)
    v = proj[:, D_ATTN + KV_W:D_ATTN + 2 * KV_W]
    u = proj[:, D_ATTN + 2 * KV_W:]
    return qs, k, v, u


def _kv_variants(k, v):
    lane = lax.broadcasted_iota(jnp.int32, k.shape, 1)
    lo = lane < HEAD_DIM
    out = []
    for t in (k, v):
        tr = pltpu.roll(t, HEAD_DIM, 1)
        zero = jnp.zeros_like(t)
        out += [jnp.where(lo, t, zero), jnp.where(lo, zero, tr), jnp.where(lo, tr, zero), jnp.where(lo, zero, t)]
    return [o.astype(BF16) for o in out]


def _k_variants(k):
    lo = lax.broadcasted_iota(jnp.int32, k.shape, 1) < HEAD_DIM
    kr = pltpu.roll(k, HEAD_DIM, 1)
    zero = jnp.zeros_like(k)
    out = [jnp.where(lo, k, zero), jnp.where(lo, zero, kr), jnp.where(lo, kr, zero), jnp.where(lo, zero, k)]
    return [o.astype(BF16) for o in out]


def _softmax_pv_t(s, vt_win, sink_row):
    m = jnp.maximum(jnp.max(s, axis=0, keepdims=True), sink_row)
    e = jnp.exp(s - m)
    den = jnp.sum(e, axis=0, keepdims=True) + jnp.exp(sink_row - m)
    o = jnp.dot(vt_win, e.astype(BF16), preferred_element_type=F32)
    return o * (1.0 / den)


def _attend(a_bf, kvar, vvar, valid, sink_col):
    s = lax.dot_general(a_bf, kvar, (((1,), (1,)), ((), ())), preferred_element_type=F32)
    for cond in valid:
        s = jnp.where(cond, s, -jnp.inf)
    m = jnp.maximum(jnp.max(s, axis=-1, keepdims=True), sink_col)
    e = jnp.exp(s - m)
    den = jnp.sum(e, axis=-1, keepdims=True) + jnp.exp(sink_col - m)
    o = jnp.dot(e.astype(BF16), vvar, preferred_element_type=F32)
    return o * (1.0 / den)


def _prompt_in_kernel(sinks_ref, x_ref, g1_ref, win_ref, cos_ref, sin_ref,
                      a_ref, ug_ref, kp_ref, vp_ref, q_s, k_s, vb_s, vt_s, u_s):
    tb = x_ref.shape[1]
    nk = tb // SSD_T
    blk = pl.program_id(1)
    qs, k, v, u = _in_proj(x_ref[0], g1_ref, win_ref, cos_ref, sin_ref)
    for j, qb in enumerate(qs):
        q_s[:, LANES * j:LANES * (j + 1)] = qb
    for j in range(D_SSM // LANES):
        u_s[j] = u[:, LANES * j:LANES * (j + 1)]

    nbuf = WINDOW + tb

    @pl.when(blk == 0)
    def _():
        k_s[:, 0:WINDOW, :] = jnp.zeros((4, WINDOW, LANES), BF16)
        vb_s[0:WINDOW, :] = jnp.zeros((WINDOW, LANES), F32)

    @pl.when(blk > 0)
    def _():
        k_s[:, 0:WINDOW, :] = k_s[:, tb:nbuf, :]
        vb_s[0:WINDOW, :] = vb_s[tb:nbuf, :]

    for idx, arr in enumerate(_k_variants(k)):
        k_s[idx, WINDOW:nbuf, :] = arr
    vb_s[WINDOW:nbuf, :] = v

    @pl.when(blk == pl.num_programs(1) - 1)
    def _():
        kp_ref[0] = k[tb - WINDOW:, :]
        vp_ref[0] = v[tb - WINDOW:, :]

    vt0 = vb_s[...].T
    vt1 = pltpu.roll(vt0, nbuf - CHUNK, 1)
    zrows = jnp.zeros((HEAD_DIM, nbuf), F32)
    for sh, vt in ((0, vt0), (1, vt1)):
        for g in range(N_KV_HEADS):
            rows = vt[HEAD_DIM * g:HEAD_DIM * (g + 1), :]
            vt_s[sh, g, 0] = jnp.concatenate([rows, zrows], axis=0).astype(BF16)
            vt_s[sh, g, 1] = jnp.concatenate([zrows, rows], axis=0).astype(BF16)

    nkeys = WINDOW + CHUNK
    key_row = lax.broadcasted_iota(jnp.int32, (nkeys, 1), 0)
    q_lane = lax.broadcasted_iota(jnp.int32, (1, LANES), 1)
    n_chunks = tb // CHUNK

    def scores(i):
        r0 = i * CHUNK
        valid = key_row >= (WINDOW - r0 - blk * tb) if i < N_WIN_CHUNKS else None
        out = []
        for g in range(N_KV_HEADS):
            qa = q_s[r0:r0 + CHUNK, 2 * LANES * g:2 * LANES * g + LANES]
            qb = q_s[r0:r0 + CHUNK, 2 * LANES * g + LANES:2 * LANES * (g + 1)]
            a_bf = jnp.concatenate([qa, qb], axis=0).astype(BF16)
            for par in range(2):
                s = lax.dot_general(k_s[2 * g + par, r0:r0 + nkeys, :], a_bf, (((1,), (1,)), ((), ())),
                                    preferred_element_type=F32)
                out.append(s if valid is None else jnp.where(valid, s, -jnp.inf))
        return out

    pending = scores(0)
    for i in range(n_chunks):
        nxt = scores(i + 1) if i + 1 < n_chunks else None
        r0 = i * CHUNK
        sh = i % 2
        base = r0 - sh * CHUNK
        for g in range(N_KV_HEADS):
            acc = None
            for par in range(2):
                h0 = 4 * g + par
                sink_row = jnp.where(q_lane < CHUNK, sinks_ref[h0], sinks_ref[h0 + 2])
                o = _softmax_pv_t(pending[2 * g + par], vt_s[sh, g, par, :, base:base + nkeys], sink_row)
                acc = o if acc is None else acc + o
            at = acc.T
            a_ref[0, r0:r0 + CHUNK, 2 * LANES * g:2 * LANES * g + LANES] = at[:CHUNK]
            a_ref[0, r0:r0 + CHUNK, 2 * LANES * g + LANES:2 * LANES * (g + 1)] = at[CHUNK:]
        pending = nxt

    def put(g, half, val):
        ug_ref[g, 0, :, LANES * half:LANES * (half + 1)] = val

    _to_groups(u_s, put, nk, strided=True)


def _prompt_ssd_kernel(ug_ref, m_ref, wd_ref, r_ref, tab_ref, abar_ref,
                       yg_ref, sfin_ref, d_s, sp_s, carry_s):
    nb = ug_ref.shape[1]
    nc = ug_ref.shape[2]
    step = pl.program_id(0)

    @pl.when(step == 0)
    def _():
        carry_s[...] = jnp.zeros(carry_s.shape, F32)

    def state_in(g, c):
        ub = ug_ref[g].reshape(nb * nc, SSD_W)
        d_s[g] = jnp.dot(ub, wd_ref[g], preferred_element_type=F32)
        return c

    lax.fori_loop(0, N_GROUPS, state_in, 0, unroll=GROUP_UNROLL)

    pw = [tab_ref[i] for i in range(8)]
    are, aim = abar_ref[0], abar_ref[1]
    kio = lax.broadcasted_iota(jnp.int32, (N_GROUPS, nc, STATE_W), 1)
    for b in range(nb):
        dg = d_s[:, b * nc:(b + 1) * nc, :]
        s_in = carry_s[b]
        dsh = jnp.where(kio == 0, s_in, pltpu.roll(dg, 1, 1))
        c = jnp.zeros((N_GROUPS, 1, STATE_W), F32)
        for j in range(nc // SUBLANES):
            x = dsh[:, SUBLANES * j:SUBLANES * (j + 1), :]
            for lvl in range(3):
                xs = pltpu.roll(x, 1 << lvl, 1)
                x = x + _cmul(pw[2 * lvl], pw[2 * lvl + 1], xs)
            cb = jnp.broadcast_to(c, x.shape)
            h = x + _cmul(pw[6], pw[7], cb)
            sp_s[:, b * nc + SUBLANES * j:b * nc + SUBLANES * (j + 1), :] = h
            c = h[:, SUBLANES - 1:SUBLANES, :]
        carry_s[b] = _cmul(are, aim, c) + dg[:, nc - 1:nc, :]

    def readout(g, c):
        ub = ug_ref[g].reshape(nb * nc, SSD_W)
        y = jnp.dot(ub, m_ref[g], preferred_element_type=F32)
        y = y + lax.dot_general(sp_s[g].astype(BF16), r_ref[g], (((1,), (1,)), ((), ())),
                                preferred_element_type=F32)
        yg_ref[g] = y.reshape(nb, nc, SSD_W).astype(BF16)
        return c

    lax.fori_loop(0, N_GROUPS, readout, 0, unroll=GROUP_UNROLL)
    sfin_ref[...] = carry_s[...]


def _out_kernel(*refs, strided, cs, has_prev):
    if has_prev:
        x_ref, a_ref, yg_ref, prev_ref = refs[:4]
        refs = refs[4:]
    else:
        x_ref, a_ref, yg_ref = refs[:3]
        prev_ref = None
        refs = refs[3:]
    (wglu_ref, ga_ref, gs_ref, wout_ref, g2_ref, wup_ref, cw_ref, cb_ref, wdown_ref, gf_ref,
     y_ref, cst_ref, ys_s, ext_s, px_s, py_s) = refs
    tb = x_ref.shape[1]
    n_sub, _, sb, _ = ys_s.shape
    nks = sb // SSD_T
    planes = strided
    pr = sb // SUBLANES
    pad = SUBLANES if planes else ext_s.shape[0] - sb
    blk = pl.program_id(1)
    assert n_sub * sb == tb and (not planes or (cs == 1 and not has_prev))

    @pl.when(blk == 0)
    def _():
        if planes:
            ext_s[:, 0:pad, :] = jnp.zeros((2, pad, 2 * D_FF), F32)
        elif has_prev:
            ext_s[0:pad, :] = prev_ref[0]
        else:
            ext_s[0:pad, :] = jnp.zeros((pad, 2 * D_FF), F32)

    def mix(i):
        rows = slice(i * sb, (i + 1) * sb)
        _from_groups(lambda g, half: yg_ref[g, 0, i * nks:(i + 1) * nks, LANES * half:LANES * (half + 1)],
                     ys_s.at[i], nks, strided)
        yv = jnp.concatenate([ys_s[i, j] for j in range(D_SSM // LANES)], axis=-1)
        z = 0.5 * yv * (1.0 + jnp.tanh(math.sqrt(2.0 / math.pi) * (yv + 0.044715 * (yv * yv * yv))))
        gate = jnp.dot(z.astype(BF16), wglu_ref[...], preferred_element_type=F32)
        s_out = z * jax.nn.sigmoid(gate)
        na = _rms(a_ref[0, rows, :], ga_ref[...]).astype(BF16)
        ns = _rms(s_out, gs_ref[...]).astype(BF16)
        x1 = x_ref[0, rows, :] + jnp.dot(na, wout_ref[0:D_ATTN, :], preferred_element_type=F32)
        return x1 + jnp.dot(ns, wout_ref[D_ATTN:, :], preferred_element_type=F32)

    def tile_cols(j, half):
        return slice(half * D_FF + j * FF_TILE, half * D_FF + (j + 1) * FF_TILE)

    def to_planes(x):
        for c in range(D_MODEL // LANES):
            px_s[c] = x[:, LANES * c:LANES * (c + 1)]
        return jnp.concatenate(
            [jnp.concatenate([px_s[c, pl.ds(r, pr, stride=SUBLANES), :] for r in range(SUBLANES)], axis=0)
             for c in range(D_MODEL // LANES)], axis=1)

    def store_rows(i, y):
        rows = slice(i * sb, (i + 1) * sb)
        if not planes:
            y_ref[0, rows, :] = y
            return
        for c in range(D_MODEL // LANES):
            for r in range(SUBLANES):
                py_s[c, pl.ds(r, pr, stride=SUBLANES), :] = y[r * pr:(r + 1) * pr, LANES * c:LANES * (c + 1)]
            y_ref[0, rows, LANES * c:LANES * (c + 1)] = py_s[c]

    def time_shifts(up, cols):
        if not planes:
            ext_s[pad:pad + sb, cols] = up
            sh2 = ext_s[pad - 2 * cs:pad - 2 * cs + sb, cols]
            sh1 = ext_s[pad - cs:pad - cs + sb, cols]
            ext_s[0:pad, cols] = ext_s[sb:sb + pad, cols]
            return sh2, sh1
        for p in range(2):
            ext_s[p, pad:pad + pr, cols] = up[(6 + p) * pr:(7 + p) * pr, :]
        s6 = ext_s[0, pad - 1:pad - 1 + pr, cols]
        s7 = ext_s[1, pad - 1:pad - 1 + pr, cols]
        ext_s[:, 0:pad, cols] = ext_s[:, pr:pr + pad, cols]
        sh1 = jnp.concatenate([s7, up[:7 * pr, :]], axis=0)
        sh2 = jnp.concatenate([s6, s7, up[:6 * pr, :]], axis=0)
        return sh2, sh1

    def conv_ffn(i, x1):
        if planes:
            x1 = to_planes(x1)
        h2 = _rms(x1, g2_ref[...]).astype(BF16)

        def up_proj(j):
            return [jnp.dot(h2, wup_ref[:, tile_cols(j, half)], preferred_element_type=F32) for half in range(2)]

        def down_proj(act, j):
            return jnp.dot(act, wdown_ref[j * FF_TILE:(j + 1) * FF_TILE, :], preferred_element_type=F32)

        n_tiles = D_FF // FF_TILE
        acc = jnp.zeros((sb, D_MODEL), F32)
        ups = up_proj(0)
        act_prev = None
        for j in range(n_tiles):
            nxt = up_proj(j + 1) if j + 1 < n_tiles else None
            if act_prev is not None:
                acc = acc + down_proj(act_prev, j - 1)
            parts = []
            for half in range(2):
                cols = tile_cols(j, half)
                sh2, sh1 = time_shifts(ups[half], cols)
                parts.append(sh2 * cw_ref[0:1, cols] + sh1 * cw_ref[1:2, cols]
                             + ups[half] * cw_ref[2:3, cols] + cb_ref[:, cols])
            cg, cv = parts
            act_prev = (cg * jax.nn.sigmoid(cg) * cv).astype(BF16)
            ups = nxt
        acc = acc + down_proj(act_prev, n_tiles - 1)
        store_rows(i, _rms(x1 + acc, gf_ref[...]))

    mixed = mix(0)
    for i in range(n_sub):
        nxt_mixed = mix(i + 1) if i + 1 < n_sub else None
        conv_ffn(i, mixed)
        mixed = nxt_mixed
    if planes:
        cst_ref[0] = ext_s[:, 0:pad, :]
    else:
        cst_ref[0] = ext_s[0:pad, :]


def _decode_in_kernel(sinks_ref, x_ref, g1_ref, win_ref, cos_ref, sin_ref, kc_ref, vc_ref, h0_ref,
                      m_ref, wd_ref, r_ref, abar_ref,
                      a_ref, kn_ref, vn_ref, yg_ref, sn_ref, u_s, ug_s, *, n_streams, n_steps, n_cache):
    rows = n_streams * n_steps
    qs, k, v, u = _in_proj(x_ref[...], g1_ref, win_ref, cos_ref, sin_ref)
    kn_ref[...] = k
    vn_ref[...] = v
    for j in range(D_SSM // LANES):
        u_s[j] = u[:, LANES * j:LANES * (j + 1)]
    ncache_rows = n_streams * n_cache
    nkeys = ncache_rows + rows
    kall = jnp.concatenate([kc_ref[...], k], axis=0)
    vall = jnp.concatenate([vc_ref[...], v], axis=0)
    variants = _kv_variants(kall, vall)

    colv = lax.broadcasted_iota(jnp.int32, (1, nkeys), 1)
    is_new = colv >= ncache_rows
    cnew = colv - ncache_rows
    k_stream = jnp.where(is_new, _pmod(cnew, n_streams), _pdiv(colv, n_cache))
    k_pos = jnp.where(is_new, PAST_LEN + _pdiv(cnew, n_streams), PAST_LEN - n_cache + _pmod(colv, n_cache))
    k_chunk = _pdiv(k_pos, CHUNK)
    rowv = _pmod(lax.broadcasted_iota(jnp.int32, (2 * rows, 1), 0), rows)
    q_stream = _pmod(rowv, n_streams)
    q_chunk = _pdiv(PAST_LEN + _pdiv(rowv, n_streams), CHUNK)
    ok_col = jnp.where(k_pos >= 0, k_stream, -1)
    d_chunk = q_chunk - k_chunk
    valid = [q_stream == ok_col,
             lax.bitcast_convert_type(d_chunk, jnp.uint32) <= jnp.uint32(N_WIN_CHUNKS)]
    top = lax.broadcasted_iota(jnp.int32, (2 * rows, 1), 0) < rows

    for g in range(N_KV_HEADS):
        a_bf = jnp.concatenate([qs[2 * g], qs[2 * g + 1]], axis=0).astype(BF16)
        acc = jnp.zeros((2 * rows, LANES), F32)
        for par in range(2):
            h0 = 4 * g + par
            sink_col = jnp.where(top, sinks_ref[h0], sinks_ref[h0 + 2])
            acc = acc + _attend(a_bf, variants[2 * g + par], variants[4 + 2 * g + par], valid, sink_col)
        a_ref[:, 2 * LANES * g:2 * LANES * g + LANES] = acc[:rows]
        a_ref[:, 2 * LANES * g + LANES:2 * LANES * (g + 1)] = acc[rows:]

    def put(g, half, val):
        ug_s[g, :, LANES * half:LANES * (half + 1)] = val

    _to_groups(u_s, put, n_streams, strided=False)
    def group_body(g, c):
        ub = ug_s[g]
        h0g = h0_ref[g]
        d = jnp.dot(ub, wd_ref[g], preferred_element_type=F32)
        sn_ref[g] = _cmul(abar_ref[0, g], abar_ref[1, g], h0g) + d
        y = jnp.dot(ub, m_ref[g], preferred_element_type=F32)
        y = y + lax.dot_general(h0g.astype(BF16), r_ref[g], (((1,), (1,)), ((), ())),
                                preferred_element_type=F32)
        yg_ref[g] = y.astype(BF16)
        return c

    lax.fori_loop(0, N_GROUPS, group_body, 0, unroll=GROUP_UNROLL)


def _ssd_tables(a_re, a_im, log_dt, b_re, b_im, c_re, c_im, d_skip):
    dup = lambda x: jnp.concatenate([x, x], axis=-1)
    prm = jnp.stack([dup(a_re), dup(a_im), jnp.broadcast_to(log_dt[:, None], (N_GROUPS, STATE_W))], axis=1)
    bt = jnp.concatenate([b_re.transpose(0, 2, 1), b_im.transpose(0, 2, 1)], axis=-1)
    cc = jnp.concatenate([c_re, c_im], axis=-1)
    dtile = jnp.tile(d_skip, (1, SSD_T))[:, None, :]
    gb = TABLE_GROUPS
    blk3 = lambda r, c: pl.BlockSpec((gb, r, c), lambda i: (i, 0, 0))
    return pl.pallas_call(
        _ssd_tables_kernel,
        grid=(N_GROUPS // gb,),
        in_specs=[blk3(3, STATE_W), blk3(SSM_GROUP, STATE_W), blk3(SSM_GROUP, STATE_W), blk3(1, SSD_W)],
        out_specs=[blk3(SSD_W, SSD_W), blk3(SSD_W, STATE_W), blk3(SSD_W, STATE_W),
                   pl.BlockSpec((8, gb, SUBLANES, STATE_W), lambda i: (0, i, 0, 0)),
                   pl.BlockSpec((2, gb, 1, STATE_W), lambda i: (0, i, 0, 0))],
        out_shape=[jax.ShapeDtypeStruct((N_GROUPS, SSD_W, SSD_W), BF16),
                   jax.ShapeDtypeStruct((N_GROUPS, SSD_W, STATE_W), BF16),
                   jax.ShapeDtypeStruct((N_GROUPS, SSD_W, STATE_W), BF16),
                   jax.ShapeDtypeStruct((8, N_GROUPS, SUBLANES, STATE_W), F32),
                   jax.ShapeDtypeStruct((2, N_GROUPS, 1, STATE_W), F32)],
        compiler_params=pltpu.CompilerParams(dimension_semantics=("arbitrary",)),
        name="ssd_tables",
    )(prm, bt, cc, dtile)


def _ssd_tables_kernel(prm_ref, bt_ref, cc_ref, dt_ref, m_ref, wd_ref, rt_ref, tab_ref, abar_ref):
    t = SSD_T
    lane = lax.broadcasted_iota(jnp.int32, (1, STATE_W), 1)
    sgn = jnp.where(lane >= SSM_STATE, 1.0, -1.0).astype(F32)
    conj = -sgn
    step = lax.broadcasted_iota(jnp.int32, (t, 1), 0).astype(F32)
    sub_i = lax.broadcasted_iota(jnp.int32, (SUBLANES, 1), 0)
    sub = sub_i.astype(F32)
    row = lax.broadcasted_iota(jnp.int32, (SSD_W, SSD_W), 0)
    col = lax.broadcasted_iota(jnp.int32, (SSD_W, SSD_W), 1)
    causal = _pdiv(col, SSM_GROUP) >= _pdiv(row, SSM_GROUP)
    diag = row == col
    for gi in range(prm_ref.shape[0]):
        a_re, a_im = prm_ref[gi, 0:1, :], prm_ref[gi, 1:2, :]
        dt = jnp.exp(prm_ref[gi, 2:3, :])
        lr, li = dt * a_re, dt * a_im

        def power(n):
            mag = jnp.exp(n * lr)
            return mag * jnp.cos(n * li), mag * jnp.sin(n * li) * sgn

        def times_powers(z, zs, pw):
            pr, pis = pw
            return jnp.concatenate([z * pr[n:n + 1, :] + zs * pis[n:n + 1, :] for n in range(t)], axis=0)

        ar, ais = power(jnp.ones((1, 1), F32))
        den = a_re * a_re + a_im * a_im
        nr, ni = ar - 1.0, ais * sgn
        fr, fis = (nr * a_re + ni * a_im) / den, (ni * a_re - nr * a_im) / den * sgn
        bt = bt_ref[gi]
        bb = bt * fr + pltpu.roll(bt, SSM_STATE, 1) * fis
        bbs = pltpu.roll(bb, SSM_STATE, 1)
        cc = cc_ref[gi]
        ccs = pltpu.roll(cc, SSM_STATE, 1)
        x = times_powers(bb, bbs, power(-step)) * conj
        y = times_powers(cc, ccs, power(step))
        m = lax.dot_general(x, y, (((1,), (1,)), ((), ())), precision=lax.Precision.HIGHEST,
                            preferred_element_type=F32)
        m = jnp.where(causal, m, 0.0) + jnp.where(diag, dt_ref[gi], 0.0)
        m_ref[gi] = m.astype(BF16)
        wd_ref[gi] = times_powers(bb, bbs, power(float(t - 1) - step)).astype(BF16)
        rt_ref[gi] = (times_powers(cc, ccs, power(step + 1.0)) * conj).astype(BF16)
        for lvl in range(3):
            d = 1 << lvl
            pr, pis = power(jnp.full((SUBLANES, 1), float(t * d), F32))
            tab_ref[2 * lvl, gi] = jnp.where(sub_i >= d, pr, 0.0)
            tab_ref[2 * lvl + 1, gi] = jnp.where(sub_i >= d, pis, 0.0)
        pr, pis = power(float(t) * (sub + 1.0))
        tab_ref[6, gi] = pr
        tab_ref[7, gi] = pis
        pr, pis = power(jnp.full((1, 1), float(t), F32))
        abar_ref[0, gi] = pr
        abar_ref[1, gi] = pis


def _rope_tables(pos):
    half = HEAD_DIM // 2
    inv = ROPE_THETA ** (-jnp.arange(half, dtype=F32) / half)
    ang = pos.astype(F32)[:, None] * inv[None, :]
    cos, sin = jnp.cos(ang), jnp.sin(ang)
    cos_t = jnp.concatenate([cos, cos, cos, cos], axis=-1)
    sin_t = jnp.concatenate([-sin, sin, -sin, sin], axis=-1)
    return cos_t, sin_t


def _const(shape):
    nd = len(shape)
    return pl.BlockSpec(shape, lambda *_: (0,) * nd, pipeline_mode=pl.Buffered(1))


def _whole(shape):
    nd = len(shape)
    return pl.BlockSpec(shape, lambda *_: (0,) * nd)


def _out_call(x, a, yg, prev, w, *, tb, sb, strided, cs, pad):
    nb, length, _ = x.shape
    nk = tb // SSD_T
    grid = (nb, length // tb)
    in_specs = [
        pl.BlockSpec((1, tb, D_MODEL), lambda b, i: (b, i, 0)),
        pl.BlockSpec((1, tb, D_ATTN), lambda b, i: (b, i, 0)),
        pl.BlockSpec((N_GROUPS, 1, nk, SSD_W), lambda b, i: (0, b, i, 0)),
    ]
    args = [x, a, yg]
    if prev is not None:
        in_specs.append(pl.BlockSpec((1, pad, 2 * D_FF), lambda b, i: (b, 0, 0)))
        args.append(prev)
    weights = [w['w_glu'], w['onorm_a'], w['onorm_s'], w['w_out'], w['norm2'], w['w_up'], w['conv_w'],
               w['conv_b'], w['w_down'], w['final_g']]
    in_specs += [_const(t.shape) for t in weights]
    kern = functools.partial(_out_kernel, strided=strided, cs=cs, has_prev=prev is not None)
    if strided:
        ext_shape = (2, pad + sb // SUBLANES, 2 * D_FF)
        cst_block, cst_map = (1, 2, pad, 2 * D_FF), (lambda b, i: (b, 0, 0, 0))
    else:
        ext_shape = (pad + sb, 2 * D_FF)
        cst_block, cst_map = (1, pad, 2 * D_FF), (lambda b, i: (b, 0, 0))
    return pl.pallas_call(
        kern,
        grid=grid,
        in_specs=in_specs,
        out_specs=[pl.BlockSpec((1, tb, D_MODEL), lambda b, i: (b, i, 0)),
                   pl.BlockSpec(cst_block, cst_map)],
        out_shape=[jax.ShapeDtypeStruct((nb, length, D_MODEL), F32),
                   jax.ShapeDtypeStruct((nb,) + cst_block[1:], F32)],
        scratch_shapes=[pltpu.VMEM((tb // sb, D_SSM // LANES, sb, LANES), F32),
                        pltpu.VMEM(ext_shape, F32),
                        pltpu.VMEM((D_MODEL // LANES, sb, LANES), F32),
                        pltpu.VMEM((D_MODEL // LANES, sb, LANES), F32)],
        compiler_params=pltpu.CompilerParams(dimension_semantics=("arbitrary", "arbitrary"),
                                             vmem_limit_bytes=VMEM_LIMIT),
        name="layer_out",
    )(*args, *weights)


def kernel(x_prompt, x_sample, cache_k, cache_v, state_ssm_re, state_ssm_im, state_conv, norm1_g, w_in, attn_sinks, ssm_A_re, ssm_A_im, ssm_log_dt, ssm_B_re, ssm_B_im, ssm_C_re, ssm_C_im, ssm_D, w_glu, onorm_attn_g, onorm_ssm_g, w_out, norm2_g, w_up, conv_w, conv_b, w_down, final_g):
    assert norm1_g.shape[0] == 1, "one layer"
    nb, length, _ = x_prompt.shape
    ns, nt, _ = x_sample.shape
    n_cache = cache_k.shape[2]
    assert nt == SSD_T and length % TB_IN == 0 and length % TB_OUT == 0 and (length // SSD_T) % NC_SSD == 0

    m_mat, wd_mat, r_mat, tab, abar = _ssd_tables(ssm_A_re[0], ssm_A_im[0], ssm_log_dt[0], ssm_B_re[0],
                                                  ssm_B_im[0], ssm_C_re[0], ssm_C_im[0], ssm_D[0])
    sinks = attn_sinks[0]
    g1 = norm1_g
    win = w_in[0].astype(BF16)
    w = dict(w_glu=w_glu[0].astype(BF16), onorm_a=onorm_attn_g, onorm_s=onorm_ssm_g, w_out=w_out[0].astype(BF16),
             norm2=norm2_g, w_up=w_up[0].astype(BF16), conv_w=conv_w[0], conv_b=conv_b,
             w_down=w_down[0].astype(BF16), final_g=final_g[None, :])
    smem = pl.BlockSpec(memory_space=pltpu.SMEM)

    cos_p, sin_p = _rope_tables(jnp.arange(length))
    nk = TB_IN // SSD_T
    a_p, ug_p, k_p, v_p = pl.pallas_call(
        _prompt_in_kernel,
        grid=(nb, length // TB_IN),
        in_specs=[smem,
                  pl.BlockSpec((1, TB_IN, D_MODEL), lambda b, i: (b, i, 0)),
                  _const((1, D_MODEL)), _const((D_MODEL, D_IN)),
                  pl.BlockSpec((TB_IN, LANES), lambda b, i: (i, 0)),
                  pl.BlockSpec((TB_IN, LANES), lambda b, i: (i, 0))],
        out_specs=[pl.BlockSpec((1, TB_IN, D_ATTN), lambda b, i: (b, i, 0)),
                   pl.BlockSpec((N_GROUPS, 1, nk, SSD_W), lambda b, i: (0, b, i, 0)),
                   pl.BlockSpec((1, WINDOW, KV_W), lambda b, i: (b, 0, 0)),
                   pl.BlockSpec((1, WINDOW, KV_W), lambda b, i: (b, 0, 0))],
        out_shape=[jax.ShapeDtypeStruct((nb, length, D_ATTN), F32),
                   jax.ShapeDtypeStruct((N_GROUPS, nb, length // SSD_T, SSD_W), BF16),
                   jax.ShapeDtypeStruct((nb, WINDOW, KV_W), F32),
                   jax.ShapeDtypeStruct((nb, WINDOW, KV_W), F32)],
        scratch_shapes=[pltpu.VMEM((TB_IN, D_ATTN), F32),
                        pltpu.VMEM((2 * N_KV_HEADS, WINDOW + TB_IN, LANES), BF16),
                        pltpu.VMEM((WINDOW + TB_IN, LANES), F32),
                        pltpu.VMEM((2, N_KV_HEADS, 2, LANES, WINDOW + TB_IN), BF16),
                        pltpu.VMEM((D_SSM // LANES, TB_IN, LANES), F32)],
        compiler_params=pltpu.CompilerParams(dimension_semantics=("arbitrary", "arbitrary"),
                                             vmem_limit_bytes=VMEM_LIMIT),
        name="prompt_in",
    )(sinks, x_prompt, g1, win, cos_p, sin_p)

    yg_p, sfin_p = pl.pallas_call(
        _prompt_ssd_kernel,
        grid=(length // SSD_T // NC_SSD,),
        in_specs=[pl.BlockSpec((N_GROUPS, nb, NC_SSD, SSD_W), lambda i: (0, 0, i, 0)),
                  _const(m_mat.shape), _const(wd_mat.shape), _const(r_mat.shape), _const(tab.shape),
                  _const(abar.shape)],
        out_specs=[pl.BlockSpec((N_GROUPS, nb, NC_SSD, SSD_W), lambda i: (0, 0, i, 0)),
                   pl.BlockSpec((nb, N_GROUPS, 1, STATE_W), lambda i: (0, 0, 0, 0))],
        out_shape=[jax.ShapeDtypeStruct((N_GROUPS, nb, length // SSD_T, SSD_W), BF16),
                   jax.ShapeDtypeStruct((nb, N_GROUPS, 1, STATE_W), F32)],
        scratch_shapes=[pltpu.VMEM((N_GROUPS, nb * NC_SSD, STATE_W), F32),
                        pltpu.VMEM((N_GROUPS, nb * NC_SSD, STATE_W), F32),
                        pltpu.VMEM((nb, N_GROUPS, 1, STATE_W), F32)],
        compiler_params=pltpu.CompilerParams(dimension_semantics=("arbitrary",), vmem_limit_bytes=VMEM_LIMIT),
        name="prompt_ssd",
    )(ug_p, m_mat, wd_mat, r_mat, tab, abar)

    y_p, cst_p = _out_call(x_prompt, a_p, yg_p, None, w, tb=TB_OUT, sb=SUB_OUT, strided=True, cs=1, pad=SUBLANES)

    rows = ns * nt
    xs = x_sample.transpose(1, 0, 2).reshape(rows, D_MODEL)
    cos_s, sin_s = _rope_tables(PAST_LEN + jnp.arange(rows) // ns)
    kc = cache_k[0].reshape(ns * n_cache, KV_W)
    vc = cache_v[0].reshape(ns * n_cache, KV_W)
    h0 = jnp.concatenate([state_ssm_re[0], state_ssm_im[0]], axis=-1).transpose(1, 0, 2)
    dec = functools.partial(_decode_in_kernel, n_streams=ns, n_steps=nt, n_cache=n_cache)
    dec_in = [xs, g1, win, cos_s, sin_s, kc, vc, h0, m_mat, wd_mat, r_mat, abar]
    a_s, kn, vn, yg_s, sn = pl.pallas_call(
        dec,
        grid=(1,),
        in_specs=[smem] + [_const(t.shape) for t in dec_in],
        out_specs=[_whole((rows, D_ATTN)), _whole((rows, KV_W)), _whole((rows, KV_W)),
                   _whole((N_GROUPS, ns, SSD_W)), _whole((N_GROUPS, ns, STATE_W))],
        out_shape=[jax.ShapeDtypeStruct((rows, D_ATTN), F32), jax.ShapeDtypeStruct((rows, KV_W), F32),
                   jax.ShapeDtypeStruct((rows, KV_W), F32), jax.ShapeDtypeStruct((N_GROUPS, ns, SSD_W), BF16),
                   jax.ShapeDtypeStruct((N_GROUPS, ns, STATE_W), F32)],
        scratch_shapes=[pltpu.VMEM((D_SSM // LANES, rows, LANES), F32), pltpu.VMEM((N_GROUPS, ns, SSD_W), BF16)],
        compiler_params=pltpu.CompilerParams(dimension_semantics=("arbitrary",), vmem_limit_bytes=VMEM_LIMIT),
        name="decode_in",
    )(sinks, *dec_in)

    cs = ns
    prev = state_conv[0].transpose(1, 0, 2).reshape(1, (CONV_W - 1) * cs, 2 * D_FF)
    y_s, cst_s = _out_call(xs[None], a_s[None], yg_s[:, None], prev, w, tb=rows, sb=rows, strided=False, cs=cs,
                           pad=(CONV_W - 1) * cs)

    kv5 = lambda t, n: t.reshape(1, n, -1, N_KV_HEADS, HEAD_DIM)
    y_sample = y_s.reshape(nt, ns, D_MODEL).transpose(1, 0, 2)
    kn_b = kn.reshape(nt, ns, KV_W).transpose(1, 0, 2)
    vn_b = vn.reshape(nt, ns, KV_W).transpose(1, 0, 2)
    k_new = jnp.concatenate([cache_k[0].reshape(ns, n_cache, KV_W), kn_b], axis=1)[:, -n_cache:]
    v_new = jnp.concatenate([cache_v[0].reshape(ns, n_cache, KV_W), vn_b], axis=1)[:, -n_cache:]
    conv_p = cst_p[:, :, SUBLANES - 1, :][None]
    conv_s = cst_s.reshape(CONV_W - 1, ns, 2 * D_FF).transpose(1, 0, 2)[None]
    sn_b = sn.transpose(1, 0, 2)
    return (y_p, y_sample,
            kv5(k_p, nb), kv5(v_p, nb),
            sfin_p[None, :, :, 0, :SSM_STATE], sfin_p[None, :, :, 0, SSM_STATE:], conv_p,
            kv5(k_new, ns), kv5(v_new, ns),
            sn_b[None, :, :, :SSM_STATE], sn_b[None, :, :, SSM_STATE:], conv_s)
```

```python
import functools
import math

import numpy as np
import jax
import jax.numpy as jnp
from jax import lax
from jax.experimental import pallas as pl
from jax.experimental.pallas import tpu as pltpu

F32 = jnp.float32
BF16 = jnp.bfloat16

D_MODEL = 1024
CHUNK = 64
D_ATTN = 512
D_SSM = 512
HEAD_DIM = 64
N_HEADS = 8
N_KV_HEADS = 2
KV_W = N_KV_HEADS * HEAD_DIM
WINDOW = 128
N_WIN_CHUNKS = WINDOW // CHUNK
ROPE_THETA = 10000.0
SSM_GROUP = 16
N_GROUPS = D_SSM // SSM_GROUP
SSM_STATE = 64
D_FF = 2816
CONV_W = 3
EPS = 1e-6
D_IN = D_ATTN + 2 * KV_W + D_SSM
PAST_LEN = 2048

SSD_T = 16
LANES = 128
SUBLANES = 8
GROUPS_PER_VREG = LANES // SSM_GROUP
SSD_W = SSD_T * SSM_GROUP
STATE_W = 2 * SSM_STATE

TB_IN = 1024
SUB_IN = 512
TB_OUT = 512
SUB_OUT = 256
NC_SSD = 64
FF_TILE = 256
GROUP_UNROLL = 8
TABLE_GROUPS = 8
VMEM_LIMIT = 56 * 1024 * 1024


def _log2(n):
    assert n > 0 and n & (n - 1) == 0, n
    return n.bit_length() - 1


def _pdiv(x, n):
    return lax.shift_right_arithmetic(x, jnp.int32(_log2(n)))


def _pmod(x, n):
    return x & (n - 1)


def _rms(x, g):
    return x * lax.rsqrt(jnp.mean(x * x, axis=-1, keepdims=True) + EPS) * g


def _rope(xb, cos, sin_signed, first_half):
    partner = jnp.where(first_half, pltpu.roll(xb, LANES - HEAD_DIM // 2, 1), pltpu.roll(xb, HEAD_DIM // 2, 1))
    return xb * cos + partner * sin_signed


def _group_transpose8(xs, lane_group):
    xs = list(xs)
    for d in (4, 2, 1):
        keep = (lane_group & d) == 0
        nxt = list(xs)
        for i in range(GROUPS_PER_VREG):
            if i & d:
                continue
            a, b = xs[i], xs[i | d]
            nxt[i] = jnp.where(keep, a, pltpu.roll(b, SSM_GROUP * d, 1))
            nxt[i | d] = jnp.where(keep, pltpu.roll(a, LANES - SSM_GROUP * d, 1), b)
        xs = nxt
    return xs


def _row_sel(s, nk, strided, row0=0):
    if strided:
        return pl.ds(row0 + s, nk, stride=SSD_T)
    return pl.ds(row0 + s * nk, nk)


def _to_groups(u_ref, put, nk, strided, row0=0):
    lane_group = _pdiv(lax.broadcasted_iota(jnp.int32, (nk, LANES), 1), SSM_GROUP)
    for j in range(D_SSM // LANES):
        for half in range(SSD_T // GROUPS_PER_VREG):
            xs = [u_ref[j, _row_sel(GROUPS_PER_VREG * half + sl, nk, strided, row0), :]
                  for sl in range(GROUPS_PER_VREG)]
            ws = _group_transpose8(xs, lane_group)
            for gq in range(GROUPS_PER_VREG):
                put(GROUPS_PER_VREG * j + gq, half, ws[gq].astype(BF16))


def _from_groups(get, y_ref, nk, strided):
    lane_group = _pdiv(lax.broadcasted_iota(jnp.int32, (nk, LANES), 1), SSM_GROUP)
    for j in range(D_SSM // LANES):
        for half in range(SSD_T // GROUPS_PER_VREG):
            ws = [get(GROUPS_PER_VREG * j + gq, half).astype(F32) for gq in range(GROUPS_PER_VREG)]
            xs = _group_transpose8(ws, lane_group)
            for sl in range(GROUPS_PER_VREG):
                y_ref[j, _row_sel(GROUPS_PER_VREG * half + sl, nk, strided), :] = xs[sl]


def _cmul(re_full, im_signed, z):
    return re_full * z + im_signed * pltpu.roll(z, SSM_STATE, z.ndim - 1)


def _in_proj(x, g1_ref, win_ref, cos_ref, sin_ref):
    rows = x.shape[0]
    hn = _rms(x, g1_ref[...]).astype(BF16)
    proj = jnp.dot(hn, win_ref[...], preferred_element_type=F32)
    cos = cos_ref[...]
    sin = sin_ref[...]
    lane = lax.broadcasted_iota(jnp.int32, (rows, LANES), 1)
    first_half = _pmod(lane, HEAD_DIM) < (HEAD_DIM // 2)
    scale = HEAD_DIM ** -0.5
    qs = [_rope(proj[:, LANES * j:LANES * (j + 1)], cos, sin, first_half) * scale for j in range(D_ATTN // LANES)]
    k = _rope(proj[:, D_ATTN:D_ATTN + KV_W], cos, sin, first_half)
    v = proj[:, D_ATTN + KV_W:D_ATTN + 2 * KV_W]
    u = proj[:, D_ATTN + 2 * KV_W:]
    return qs, k, v, u


def _kv_variants(k, v):
    lane = lax.broadcasted_iota(jnp.int32, k.shape, 1)
    lo = lane < HEAD_DIM
    out = []
    for t in (k, v):
        tr = pltpu.roll(t, HEAD_DIM, 1)
        zero = jnp.zeros_like(t)
        out += [jnp.where(lo, t, zero), jnp.where(lo, zero, tr), jnp.where(lo, tr, zero), jnp.where(lo, zero, t)]
    return [o.astype(BF16) for o in out]


def _k_variants(k):
    lo = lax.broadcasted_iota(jnp.int32, k.shape, 1) < HEAD_DIM
    kr = pltpu.roll(k, HEAD_DIM, 1)
    zero = jnp.zeros_like(k)
    out = [jnp.where(lo, k, zero), jnp.where(lo, zero, kr), jnp.where(lo, kr, zero), jnp.where(lo, zero, k)]
    return [o.astype(BF16) for o in out]


def _softmax_pv_t(s, vt_win, sink_row):
    m = jnp.maximum(jnp.max(s, axis=0, keepdims=True), sink_row)
    e = jnp.exp(s - m)
    den = jnp.sum(e, axis=0, keepdims=True) + jnp.exp(sink_row - m)
    o = jnp.dot(vt_win, e.astype(BF16), preferred_element_type=F32)
    return o * (1.0 / den)


def _attend(a_bf, kvar, vvar, valid, sink_col):
    s = lax.dot_general(a_bf, kvar, (((1,), (1,)), ((), ())), preferred_element_type=F32)
    for cond in valid:
        s = jnp.where(cond, s, -jnp.inf)
    m = jnp.maximum(jnp.max(s, axis=-1, keepdims=True), sink_col)
    e = jnp.exp(s - m)
    den = jnp.sum(e, axis=-1, keepdims=True) + jnp.exp(sink_col - m)
    o = jnp.dot(e.astype(BF16), vvar, preferred_element_type=F32)
    return o * (1.0 / den)


def _prompt_in_kernel(sinks_ref, x_ref, g1_ref, win_ref, cos_ref, sin_ref,
                      a_ref, ug_ref, kp_ref, vp_ref, q_s, k_s, vb_s, vt_s, u_s, *, sb):
    tb = x_ref.shape[1]
    n_sub = tb // sb
    nks = sb // SSD_T
    blk = pl.program_id(1)
    nbuf = WINDOW + tb
    cps = sb // CHUNK
    n_chunks = tb // CHUNK

    @pl.when(blk == 0)
    def _():
        k_s[:, 0:WINDOW, :] = jnp.zeros((4, WINDOW, LANES), BF16)
        vb_s[0:WINDOW, :] = jnp.zeros((WINDOW, LANES), F32)

    @pl.when(blk > 0)
    def _():
        k_s[:, 0:WINDOW, :] = k_s[:, tb:nbuf, :]
        vb_s[0:WINDOW, :] = vb_s[tb:nbuf, :]

    lane = lax.broadcasted_iota(jnp.int32, (sb, LANES), 1)
    first_half = _pmod(lane, HEAD_DIM) < (HEAD_DIM // 2)
    scale = HEAD_DIM ** -0.5

    def project(s):
        rows = slice(s * sb, (s + 1) * sb)
        brows = slice(WINDOW + s * sb, WINDOW + (s + 1) * sb)
        state = {}

        def piece(c0):
            if 'hn' not in state:
                state['hn'] = _rms(x_ref[0, rows, :], g1_ref[...]).astype(BF16)
                state['cos'], state['sin'] = cos_ref[rows, :], sin_ref[rows, :]
            return jnp.dot(state['hn'], win_ref[:, c0:c0 + 2 * LANES], preferred_element_type=F32)

        def q_piece(j):
            def run():
                p = piece(2 * LANES * j)
                for h in range(2):
                    q_s[rows, LANES * (2 * j + h):LANES * (2 * j + h + 1)] = _rope(
                        p[:, LANES * h:LANES * (h + 1)], state['cos'], state['sin'], first_half) * scale
            return run

        def kv_piece():
            p = piece(D_ATTN)
            k = _rope(p[:, :KV_W], state['cos'], state['sin'], first_half)
            v = p[:, KV_W:]
            for idx, arr in enumerate(_k_variants(k)):
                k_s[idx, brows, :] = arr
            vb_s[brows, :] = v
            if s == n_sub - 1:
                kp_ref[0] = k[sb - WINDOW:, :]
                vp_ref[0] = v[sb - WINDOW:, :]

        def u_piece(j):
            def run():
                p = piece(D_ATTN + 2 * KV_W + 2 * LANES * j)
                for h in range(2):
                    u_s[2 * j + h, rows, :] = p[:, LANES * h:LANES * (h + 1)]
            return run

        return [kv_piece, q_piece(0), q_piece(1), u_piece(0), u_piece(1)]

    vt_tail = {}

    def build_vt(s):
        lo = 0 if s == 0 else WINDOW + s * sb
        hi = WINDOW + (s + 1) * sb
        piece = vb_s[lo:hi, :].T
        src = piece if s == 0 else jnp.concatenate([vt_tail['v'], piece], axis=1)
        lo1 = lo if s == 0 else lo - LANES
        shifted = pltpu.roll(src, src.shape[1] - CHUNK, 1)
        vt_tail['v'] = piece[:, piece.shape[1] - LANES:]
        for sh, arr, b0 in ((0, piece, lo), (1, shifted, lo1)):
            w = arr.shape[1]
            zrows = jnp.zeros((HEAD_DIM, w), F32)
            for g in range(N_KV_HEADS):
                rows = arr[HEAD_DIM * g:HEAD_DIM * (g + 1), :]
                vt_s[sh, g, 0, :, b0:b0 + w] = jnp.concatenate([rows, zrows], axis=0).astype(BF16)
                vt_s[sh, g, 1, :, b0:b0 + w] = jnp.concatenate([zrows, rows], axis=0).astype(BF16)

    def regroup(s):
        def put(g, half, val):
            ug_ref[g, 0, s * nks:(s + 1) * nks, LANES * half:LANES * (half + 1)] = val
        _to_groups(u_s, put, nks, strided=True, row0=s * sb)

    nkeys = WINDOW + CHUNK
    key_row = lax.broadcasted_iota(jnp.int32, (nkeys, 1), 0)
    q_lane = lax.broadcasted_iota(jnp.int32, (1, LANES), 1)

    def scores(i):
        r0 = i * CHUNK
        valid = key_row >= (WINDOW - r0 - blk * tb) if i < N_WIN_CHUNKS else None
        out = []
        for g in range(N_KV_HEADS):
            qa = q_s[r0:r0 + CHUNK, 2 * LANES * g:2 * LANES * g + LANES]
            qb = q_s[r0:r0 + CHUNK, 2 * LANES * g + LANES:2 * LANES * (g + 1)]
            a_bf = jnp.concatenate([qa, qb], axis=0).astype(BF16)
            for par in range(2):
                s = lax.dot_general(k_s[2 * g + par, r0:r0 + nkeys, :], a_bf, (((1,), (1,)), ((), ())),
                                    preferred_element_type=F32)
                out.append(s if valid is None else jnp.where(valid, s, -jnp.inf))
        return out

    for item in project(0):
        item()
    build_vt(0)
    staging = []
    for s in range(1, n_sub):
        staging += [(s, item) for item in project(s)] + [(s, functools.partial(build_vt, s))]
    regroups = [functools.partial(regroup, s) for s in range(n_sub)]
    pending = scores(0)
    for i in range(n_chunks):
        if i + 1 < n_chunks and (i + 1) % cps == 0:
            while staging and staging[0][0] <= (i + 1) // cps:
                staging.pop(0)[1]()
        nxt = scores(i + 1) if i + 1 < n_chunks else None
        if staging:
            staging.pop(0)[1]()
        elif regroups:
            regroups.pop(0)()
        r0 = i * CHUNK
        sh = i % 2
        base = r0 - sh * CHUNK
        for g in range(N_KV_HEADS):
            acc = None
            for par in range(2):
                h0 = 4 * g + par
                sink_row = jnp.where(q_lane < CHUNK, sinks_ref[h0], sinks_ref[h0 + 2])
                o = _softmax_pv_t(pending[2 * g + par], vt_s[sh, g, par, :, base:base + nkeys], sink_row)
                acc = o if acc is None else acc + o
            at = acc.T
            a_ref[0, r0:r0 + CHUNK, 2 * LANES * g:2 * LANES * g + LANES] = at[:CHUNK]
            a_ref[0, r0:r0 + CHUNK, 2 * LANES * g + LANES:2 * LANES * (g + 1)] = at[CHUNK:]
        pending = nxt
    for item in regroups:
        item()


def _prompt_ssd_kernel(ug_ref, m_ref, wd_ref, r_ref, tab_ref, abar_ref,
                       yg_ref, sfin_ref, d_s, sp_s, carry_s):
    nb = ug_ref.shape[1]
    nc = ug_ref.shape[2]
    step = pl.program_id(0)

    @pl.when(step == 0)
    def _():
        carry_s[...] = jnp.zeros(carry_s.shape, F32)

    def state_in(g, c):
        ub = ug_ref[g].reshape(nb * nc, SSD_W)
        d_s[g] = jnp.dot(ub, wd_ref[g], preferred_element_type=F32)
        return c

    lax.fori_loop(0, N_GROUPS, state_in, 0, unroll=GROUP_UNROLL)

    pw = [tab_ref[i] for i in range(8)]
    are, aim = abar_ref[0], abar_ref[1]
    kio = lax.broadcasted_iota(jnp.int32, (N_GROUPS, nc, STATE_W), 1)
    for b in range(nb):
        dg = d_s[:, b * nc:(b + 1) * nc, :]
        s_in = carry_s[b]
        dsh = jnp.where(kio == 0, s_in, pltpu.roll(dg, 1, 1))
        c = jnp.zeros((N_GROUPS, 1, STATE_W), F32)
        for j in range(nc // SUBLANES):
            x = dsh[:, SUBLANES * j:SUBLANES * (j + 1), :]
            for lvl in range(3):
                xs = pltpu.roll(x, 1 << lvl, 1)
                x = x + _cmul(pw[2 * lvl], pw[2 * lvl + 1], xs)
            cb = jnp.broadcast_to(c, x.shape)
            h = x + _cmul(pw[6], pw[7], cb)
            sp_s[:, b * nc + SUBLANES * j:b * nc + SUBLANES * (j + 1), :] = h
            c = h[:, SUBLANES - 1:SUBLANES, :]
        carry_s[b] = _cmul(are, aim, c) + dg[:, nc - 1:nc, :]

    def readout(g, c):
        ub = ug_ref[g].reshape(nb * nc, SSD_W)
        y = jnp.dot(ub, m_ref[g], preferred_element_type=F32)
        y = y + lax.dot_general(sp_s[g].astype(BF16), r_ref[g], (((1,), (1,)), ((), ())),
                                preferred_element_type=F32)
        yg_ref[g] = y.reshape(nb, nc, SSD_W).astype(BF16)
        return c

    lax.fori_loop(0, N_GROUPS, readout, 0, unroll=GROUP_UNROLL)
    sfin_ref[...] = carry_s[...]


def _out_kernel(*refs, strided, cs, has_prev):
    if has_prev:
        x_ref, a_ref, yg_ref, prev_ref = refs[:4]
        refs = refs[4:]
    else:
        x_ref, a_ref, yg_ref = refs[:3]
        prev_ref = None
        refs = refs[3:]
    (wglu_ref, ga_ref, gs_ref, wout_ref, g2_ref, wup_ref, cw_ref, cb_ref, wdown_ref, gf_ref,
     y_ref, cst_ref, ys_s, ext_s, px_s, py_s) = refs
    tb = x_ref.shape[1]
    n_sub, _, sb, _ = ys_s.shape
    nks = sb // SSD_T
    planes = strided
    pr = sb // SUBLANES
    pad = SUBLANES if planes else ext_s.shape[0] - sb
    blk = pl.program_id(1)
    assert n_sub * sb == tb and (not planes or (cs == 1 and not has_prev))

    @pl.when(blk == 0)
    def _():
        if planes:
            ext_s[:, 0:pad, :] = jnp.zeros((2, pad, 2 * D_FF), F32)
        elif has_prev:
            ext_s[0:pad, :] = prev_ref[0]
        else:
            ext_s[0:pad, :] = jnp.zeros((pad, 2 * D_FF), F32)

    def mix(i):
        rows = slice(i * sb, (i + 1) * sb)
        _from_groups(lambda g, half: yg_ref[g, 0, i * nks:(i + 1) * nks, LANES * half:LANES * (half + 1)],
                     ys_s.at[i], nks, strided)
        yv = jnp.concatenate([ys_s[i, j] for j in range(D_SSM // LANES)], axis=-1)
        z = 0.5 * yv * (1.0 + jnp.tanh(math.sqrt(2.0 / math.pi) * (yv + 0.044715 * (yv * yv * yv))))
        gate = jnp.dot(z.astype(BF16), wglu_ref[...], preferred_element_type=F32)
        s_out = z * jax.nn.sigmoid(gate)
        na = _rms(a_ref[0, rows, :], ga_ref[...]).astype(BF16)
        ns = _rms(s_out, gs_ref[...]).astype(BF16)
        x1 = x_ref[0, rows, :] + jnp.dot(na, wout_ref[0:D_ATTN, :], preferred_element_type=F32)
        return x1 + jnp.dot(ns, wout_ref[D_ATTN:, :], preferred_element_type=F32)

    def tile_cols(j, half):
        return slice(half * D_FF + j * FF_TILE, half * D_FF + (j + 1) * FF_TILE)

    def to_planes(x):
        for c in range(D_MODEL // LANES):
            px_s[c] = x[:, LANES * c:LANES * (c + 1)]
        return jnp.concatenate(
            [jnp.concatenate([px_s[c, pl.ds(r, pr, stride=SUBLANES), :] for r in range(SUBLANES)], axis=0)
             for c in range(D_MODEL // LANES)], axis=1)

    def store_rows(i, y):
        rows = slice(i * sb, (i + 1) * sb)
        if not planes:
            y_ref[0, rows, :] = y
            return
        for c in range(D_MODEL // LANES):
            for r in range(SUBLANES):
                py_s[c, pl.ds(r, pr, stride=SUBLANES), :] = y[r * pr:(r + 1) * pr, LANES * c:LANES * (c + 1)]
            y_ref[0, rows, LANES * c:LANES * (c + 1)] = py_s[c]

    def time_shifts(up, cols):
        if not planes:
            ext_s[pad:pad + sb, cols] = up
            sh2 = ext_s[pad - 2 * cs:pad - 2 * cs + sb, cols]
            sh1 = ext_s[pad - cs:pad - cs + sb, cols]
            ext_s[0:pad, cols] = ext_s[sb:sb + pad, cols]
            return sh2, sh1
        for p in range(2):
            ext_s[p, pad:pad + pr, cols] = up[(6 + p) * pr:(7 + p) * pr, :]
        s6 = ext_s[0, pad - 1:pad - 1 + pr, cols]
        s7 = ext_s[1, pad - 1:pad - 1 + pr, cols]
        ext_s[:, 0:pad, cols] = ext_s[:, pr:pr + pad, cols]
        sh1 = jnp.concatenate([s7, up[:7 * pr, :]], axis=0)
        sh2 = jnp.concatenate([s6, s7, up[:6 * pr, :]], axis=0)
        return sh2, sh1

    def conv_ffn(i, x1):
        if planes:
            x1 = to_planes(x1)
        h2 = _rms(x1, g2_ref[...]).astype(BF16)

        def up_proj(j):
            return [jnp.dot(h2, wup_ref[:, tile_cols(j, half)], preferred_element_type=F32) for half in range(2)]

        def down_proj(act, j):
            return jnp.dot(act, wdown_ref[j * FF_TILE:(j + 1) * FF_TILE, :], preferred_element_type=F32)

        n_tiles = D_FF // FF_TILE
        acc = jnp.zeros((sb, D_MODEL), F32)
        ups = up_proj(0)
        act_prev = None
        for j in range(n_tiles):
            nxt = up_proj(j + 1) if j + 1 < n_tiles else None
            if act_prev is not None:
                acc = acc + down_proj(act_prev, j - 1)
            parts = []
            for half in range(2):
                cols = tile_cols(j, half)
                sh2, sh1 = time_shifts(ups[half], cols)
                parts.append(sh2 * cw_ref[0:1, cols] + sh1 * cw_ref[1:2, cols]
                             + ups[half] * cw_ref[2:3, cols] + cb_ref[:, cols])
            cg, cv = parts
            act_prev = (cg * jax.nn.sigmoid(cg) * cv).astype(BF16)
            ups = nxt
        acc = acc + down_proj(act_prev, n_tiles - 1)
        store_rows(i, _rms(x1 + acc, gf_ref[...]))

    mixed = mix(0)
    for i in range(n_sub):
        nxt_mixed = mix(i + 1) if i + 1 < n_sub else None
        conv_ffn(i, mixed)
        mixed = nxt_mixed
    if planes:
        cst_ref[0] = ext_s[:, 0:pad, :]
    else:
        cst_ref[0] = ext_s[0:pad, :]


def _decode_in_kernel(sinks_ref, x_ref, g1_ref, win_ref, cos_ref, sin_ref, kc_ref, vc_ref, h0_ref,
                      m_ref, wd_ref, r_ref, abar_ref,
                      a_ref, kn_ref, vn_ref, yg_ref, sn_ref, u_s, ug_s, *, n_streams, n_steps, n_cache):
    rows = n_streams * n_steps
    qs, k, v, u = _in_proj(x_ref[...], g1_ref, win_ref, cos_ref, sin_ref)
    kn_ref[...] = k
    vn_ref[...] = v
    for j in range(D_SSM // LANES):
        u_s[j] = u[:, LANES * j:LANES * (j + 1)]
    ncache_rows = n_streams * n_cache
    nkeys = ncache_rows + rows
    kall = jnp.concatenate([kc_ref[...], k], axis=0)
    vall = jnp.concatenate([vc_ref[...], v], axis=0)
    variants = _kv_variants(kall, vall)

    colv = lax.broadcasted_iota(jnp.int32, (1, nkeys), 1)
    is_new = colv >= ncache_rows
    cnew = colv - ncache_rows
    k_stream = jnp.where(is_new, _pmod(cnew, n_streams), _pdiv(colv, n_cache))
    k_pos = jnp.where(is_new, PAST_LEN + _pdiv(cnew, n_streams), PAST_LEN - n_cache + _pmod(colv, n_cache))
    k_chunk = _pdiv(k_pos, CHUNK)
    rowv = _pmod(lax.broadcasted_iota(jnp.int32, (2 * rows, 1), 0), rows)
    q_stream = _pmod(rowv, n_streams)
    q_chunk = _pdiv(PAST_LEN + _pdiv(rowv, n_streams), CHUNK)
    ok_col = jnp.where(k_pos >= 0, k_stream, -1)
    d_chunk = q_chunk - k_chunk
    valid = [q_stream == ok_col,
             lax.bitcast_convert_type(d_chunk, jnp.uint32) <= jnp.uint32(N_WIN_CHUNKS)]
    top = lax.broadcasted_iota(jnp.int32, (2 * rows, 1), 0) < rows

    for g in range(N_KV_HEADS):
        a_bf = jnp.concatenate([qs[2 * g], qs[2 * g + 1]], axis=0).astype(BF16)
        acc = jnp.zeros((2 * rows, LANES), F32)
        for par in range(2):
            h0 = 4 * g + par
            sink_col = jnp.where(top, sinks_ref[h0], sinks_ref[h0 + 2])
            acc = acc + _attend(a_bf, variants[2 * g + par], variants[4 + 2 * g + par], valid, sink_col)
        a_ref[:, 2 * LANES * g:2 * LANES * g + LANES] = acc[:rows]
        a_ref[:, 2 * LANES * g + LANES:2 * LANES * (g + 1)] = acc[rows:]

    def put(g, half, val):
        ug_s[g, :, LANES * half:LANES * (half + 1)] = val

    _to_groups(u_s, put, n_streams, strided=False)
    def group_body(g, c):
        ub = ug_s[g]
        h0g = h0_ref[g]
        d = jnp.dot(ub, wd_ref[g], preferred_element_type=F32)
        sn_ref[g] = _cmul(abar_ref[0, g], abar_ref[1, g], h0g) + d
        y = jnp.dot(ub, m_ref[g], preferred_element_type=F32)
        y = y + lax.dot_general(h0g.astype(BF16), r_ref[g], (((1,), (1,)), ((), ())),
                                preferred_element_type=F32)
        yg_ref[g] = y.astype(BF16)
        return c

    lax.fori_loop(0, N_GROUPS, group_body, 0, unroll=GROUP_UNROLL)


def _ssd_tables(a_re, a_im, log_dt, b_re, b_im, c_re, c_im, d_skip):
    dup = lambda x: jnp.concatenate([x, x], axis=-1)
    prm = jnp.stack([dup(a_re), dup(a_im), jnp.broadcast_to(log_dt[:, None], (N_GROUPS, STATE_W))], axis=1)
    bt = jnp.concatenate([b_re.transpose(0, 2, 1), b_im.transpose(0, 2, 1)], axis=-1)
    cc = jnp.concatenate([c_re, c_im], axis=-1)
    dtile = jnp.tile(d_skip, (1, SSD_T))[:, None, :]
    gb = TABLE_GROUPS
    blk3 = lambda r, c: pl.BlockSpec((gb, r, c), lambda i: (i, 0, 0))
    return pl.pallas_call(
        _ssd_tables_kernel,
        grid=(N_GROUPS // gb,),
        in_specs=[blk3(3, STATE_W), blk3(SSM_GROUP, STATE_W), blk3(SSM_GROUP, STATE_W), blk3(1, SSD_W)],
        out_specs=[blk3(SSD_W, SSD_W), blk3(SSD_W, STATE_W), blk3(SSD_W, STATE_W),
                   pl.BlockSpec((8, gb, SUBLANES, STATE_W), lambda i: (0, i, 0, 0)),
                   pl.BlockSpec((2, gb, 1, STATE_W), lambda i: (0, i, 0, 0))],
        out_shape=[jax.ShapeDtypeStruct((N_GROUPS, SSD_W, SSD_W), BF16),
                   jax.ShapeDtypeStruct((N_GROUPS, SSD_W, STATE_W), BF16),
                   jax.ShapeDtypeStruct((N_GROUPS, SSD_W, STATE_W), BF16),
                   jax.ShapeDtypeStruct((8, N_GROUPS, SUBLANES, STATE_W), F32),
                   jax.ShapeDtypeStruct((2, N_GROUPS, 1, STATE_W), F32)],
        compiler_params=pltpu.CompilerParams(dimension_semantics=("arbitrary",)),
        name="ssd_tables",
    )(prm, bt, cc, dtile)


def _ssd_tables_kernel(prm_ref, bt_ref, cc_ref, dt_ref, m_ref, wd_ref, rt_ref, tab_ref, abar_ref):
    t = SSD_T
    lane = lax.broadcasted_iota(jnp.int32, (1, STATE_W), 1)
    sgn = jnp.where(lane >= SSM_STATE, 1.0, -1.0).astype(F32)
    conj = -sgn
    step = lax.broadcasted_iota(jnp.int32, (t, 1), 0).astype(F32)
    sub_i = lax.broadcasted_iota(jnp.int32, (SUBLANES, 1), 0)
    sub = sub_i.astype(F32)
    row = lax.broadcasted_iota(jnp.int32, (SSD_W, SSD_W), 0)
    col = lax.broadcasted_iota(jnp.int32, (SSD_W, SSD_W), 1)
    causal = _pdiv(col, SSM_GROUP) >= _pdiv(row, SSM_GROUP)
    diag = row == col
    for gi in range(prm_ref.shape[0]):
        a_re, a_im = prm_ref[gi, 0:1, :], prm_ref[gi, 1:2, :]
        dt = jnp.exp(prm_ref[gi, 2:3, :])
        lr, li = dt * a_re, dt * a_im

        def power(n):
            mag = jnp.exp(n * lr)
            return mag * jnp.cos(n * li), mag * jnp.sin(n * li) * sgn

        def times_powers(z, zs, pw):
            pr, pis = pw
            return jnp.concatenate([z * pr[n:n + 1, :] + zs * pis[n:n + 1, :] for n in range(t)], axis=0)

        ar, ais = power(jnp.ones((1, 1), F32))
        den = a_re * a_re + a_im * a_im
        nr, ni = ar - 1.0, ais * sgn
        fr, fis = (nr * a_re + ni * a_im) / den, (ni * a_re - nr * a_im) / den * sgn
        bt = bt_ref[gi]
        bb = bt * fr + pltpu.roll(bt, SSM_STATE, 1) * fis
        bbs = pltpu.roll(bb, SSM_STATE, 1)
        cc = cc_ref[gi]
        ccs = pltpu.roll(cc, SSM_STATE, 1)
        x = times_powers(bb, bbs, power(-step)) * conj
        y = times_powers(cc, ccs, power(step))
        m = lax.dot_general(x, y, (((1,), (1,)), ((), ())), precision=lax.Precision.HIGHEST,
                            preferred_element_type=F32)
        m = jnp.where(causal, m, 0.0) + jnp.where(diag, dt_ref[gi], 0.0)
        m_ref[gi] = m.astype(BF16)
        wd_ref[gi] = times_powers(bb, bbs, power(float(t - 1) - step)).astype(BF16)
        rt_ref[gi] = (times_powers(cc, ccs, power(step + 1.0)) * conj).astype(BF16)
        for lvl in range(3):
            d = 1 << lvl
            pr, pis = power(jnp.full((SUBLANES, 1), float(t * d), F32))
            tab_ref[2 * lvl, gi] = jnp.where(sub_i >= d, pr, 0.0)
            tab_ref[2 * lvl + 1, gi] = jnp.where(sub_i >= d, pis, 0.0)
        pr, pis = power(float(t) * (sub + 1.0))
        tab_ref[6, gi] = pr
        tab_ref[7, gi] = pis
        pr, pis = power(jnp.full((1, 1), float(t), F32))
        abar_ref[0, gi] = pr
        abar_ref[1, gi] = pis


def _rope_tables(pos):
    half = HEAD_DIM // 2
    inv = ROPE_THETA ** (-jnp.arange(half, dtype=F32) / half)
    ang = pos.astype(F32)[:, None] * inv[None, :]
    cos, sin = jnp.cos(ang), jnp.sin(ang)
    cos_t = jnp.concatenate([cos, cos, cos, cos], axis=-1)
    sin_t = jnp.concatenate([-sin, sin, -sin, sin], axis=-1)
    return cos_t, sin_t


def _const(shape):
    nd = len(shape)
    return pl.BlockSpec(shape, lambda *_: (0,) * nd, pipeline_mode=pl.Buffered(1))


def _whole(shape):
    nd = len(shape)
    return pl.BlockSpec(shape, lambda *_: (0,) * nd)


def _out_call(x, a, yg, prev, w, *, tb, sb, strided, cs, pad):
    nb, length, _ = x.shape
    nk = tb // SSD_T
    grid = (nb, length // tb)
    in_specs = [
        pl.BlockSpec((1, tb, D_MODEL), lambda b, i: (b, i, 0)),
        pl.BlockSpec((1, tb, D_ATTN), lambda b, i: (b, i, 0)),
        pl.BlockSpec((N_GROUPS, 1, nk, SSD_W), lambda b, i: (0, b, i, 0)),
    ]
    args = [x, a, yg]
    if prev is not None:
        in_specs.append(pl.BlockSpec((1, pad, 2 * D_FF), lambda b, i: (b, 0, 0)))
        args.append(prev)
    weights = [w['w_glu'], w['onorm_a'], w['onorm_s'], w['w_out'], w['norm2'], w['w_up'], w['conv_w'],
               w['conv_b'], w['w_down'], w['final_g']]
    in_specs += [_const(t.shape) for t in weights]
    kern = functools.partial(_out_kernel, strided=strided, cs=cs, has_prev=prev is not None)
    if strided:
        ext_shape = (2, pad + sb // SUBLANES, 2 * D_FF)
        cst_block, cst_map = (1, 2, pad, 2 * D_FF), (lambda b, i: (b, 0, 0, 0))
    else:
        ext_shape = (pad + sb, 2 * D_FF)
        cst_block, cst_map = (1, pad, 2 * D_FF), (lambda b, i: (b, 0, 0))
    return pl.pallas_call(
        kern,
        grid=grid,
        in_specs=in_specs,
        out_specs=[pl.BlockSpec((1, tb, D_MODEL), lambda b, i: (b, i, 0)),
                   pl.BlockSpec(cst_block, cst_map)],
        out_shape=[jax.ShapeDtypeStruct((nb, length, D_MODEL), F32),
                   jax.ShapeDtypeStruct((nb,) + cst_block[1:], F32)],
        scratch_shapes=[pltpu.VMEM((tb // sb, D_SSM // LANES, sb, LANES), F32),
                        pltpu.VMEM(ext_shape, F32),
                        pltpu.VMEM((D_MODEL // LANES, sb, LANES), F32),
                        pltpu.VMEM((D_MODEL // LANES, sb, LANES), F32)],
        compiler_params=pltpu.CompilerParams(dimension_semantics=("arbitrary", "arbitrary"),
                                             vmem_limit_bytes=VMEM_LIMIT),
        name="layer_out",
    )(*args, *weights)


def kernel(x_prompt, x_sample, cache_k, cache_v, state_ssm_re, state_ssm_im, state_conv, norm1_g, w_in, attn_sinks, ssm_A_re, ssm_A_im, ssm_log_dt, ssm_B_re, ssm_B_im, ssm_C_re, ssm_C_im, ssm_D, w_glu, onorm_attn_g, onorm_ssm_g, w_out, norm2_g, w_up, conv_w, conv_b, w_down, final_g):
    assert norm1_g.shape[0] == 1, "one layer"
    nb, length, _ = x_prompt.shape
    ns, nt, _ = x_sample.shape
    n_cache = cache_k.shape[2]
    assert nt == SSD_T and length % TB_IN == 0 and length % TB_OUT == 0 and (length // SSD_T) % NC_SSD == 0

    m_mat, wd_mat, r_mat, tab, abar = _ssd_tables(ssm_A_re[0], ssm_A_im[0], ssm_log_dt[0], ssm_B_re[0],
                                                  ssm_B_im[0], ssm_C_re[0], ssm_C_im[0], ssm_D[0])
    sinks = attn_sinks[0]
    g1 = norm1_g
    win = w_in[0].astype(BF16)
    w = dict(w_glu=w_glu[0].astype(BF16), onorm_a=onorm_attn_g, onorm_s=onorm_ssm_g, w_out=w_out[0].astype(BF16),
             norm2=norm2_g, w_up=w_up[0].astype(BF16), conv_w=conv_w[0], conv_b=conv_b,
             w_down=w_down[0].astype(BF16), final_g=final_g[None, :])
    smem = pl.BlockSpec(memory_space=pltpu.SMEM)

    cos_p, sin_p = _rope_tables(jnp.arange(length))
    nk = TB_IN // SSD_T
    a_p, ug_p, k_p, v_p = pl.pallas_call(
        functools.partial(_prompt_in_kernel, sb=SUB_IN),
        grid=(nb, length // TB_IN),
        in_specs=[smem,
                  pl.BlockSpec((1, TB_IN, D_MODEL), lambda b, i: (b, i, 0)),
                  _const((1, D_MODEL)), _const((D_MODEL, D_IN)),
                  pl.BlockSpec((TB_IN, LANES), lambda b, i: (i, 0)),
                  pl.BlockSpec((TB_IN, LANES), lambda b, i: (i, 0))],
        out_specs=[pl.BlockSpec((1, TB_IN, D_ATTN), lambda b, i: (b, i, 0)),
                   pl.BlockSpec((N_GROUPS, 1, nk, SSD_W), lambda b, i: (0, b, i, 0)),
                   pl.BlockSpec((1, WINDOW, KV_W), lambda b, i: (b, 0, 0)),
                   pl.BlockSpec((1, WINDOW, KV_W), lambda b, i: (b, 0, 0))],
        out_shape=[jax.ShapeDtypeStruct((nb, length, D_ATTN), F32),
                   jax.ShapeDtypeStruct((N_GROUPS, nb, length // SSD_T, SSD_W), BF16),
                   jax.ShapeDtypeStruct((nb, WINDOW, KV_W), F32),
                   jax.ShapeDtypeStruct((nb, WINDOW, KV_W), F32)],
        scratch_shapes=[pltpu.VMEM((TB_IN, D_ATTN), F32),
                        pltpu.VMEM((2 * N_KV_HEADS, WINDOW + TB_IN, LANES), BF16),
                        pltpu.VMEM((WINDOW + TB_IN, LANES), F32),
                        pltpu.VMEM((2, N_KV_HEADS, 2, LANES, WINDOW + TB_IN), BF16),
                        pltpu.VMEM((D_SSM // LANES, TB_IN, LANES), F32)],
        compiler_params=pltpu.CompilerParams(dimension_semantics=("arbitrary", "arbitrary"),
                                             vmem_limit_bytes=VMEM_LIMIT),
        name="prompt_in",
    )(sinks, x_prompt, g1, win, cos_p, sin_p)

    yg_p, sfin_p = pl.pallas_call(
        _prompt_ssd_kernel,
        grid=(length // SSD_T // NC_SSD,),
        in_specs=[pl.BlockSpec((N_GROUPS, nb, NC_SSD, SSD_W), lambda i: (0, 0, i, 0)),
                  _const(m_mat.shape), _const(wd_mat.shape), _const(r_mat.shape), _const(tab.shape),
                  _const(abar.shape)],
        out_specs=[pl.BlockSpec((N_GROUPS, nb, NC_SSD, SSD_W), lambda i: (0, 0, i, 0)),
                   pl.BlockSpec((nb, N_GROUPS, 1, STATE_W), lambda i: (0, 0, 0, 0))],
        out_shape=[jax.ShapeDtypeStruct((N_GROUPS, nb, length // SSD_T, SSD_W), BF16),
                   jax.ShapeDtypeStruct((nb, N_GROUPS, 1, STATE_W), F32)],
        scratch_shapes=[pltpu.VMEM((N_GROUPS, nb * NC_SSD, STATE_W), F32),
                        pltpu.VMEM((N_GROUPS, nb * NC_SSD, STATE_W), F32),
                        pltpu.VMEM((nb, N_GROUPS, 1, STATE_W), F32)],
        compiler_params=pltpu.CompilerParams(dimension_semantics=("arbitrary",), vmem_limit_bytes=VMEM_LIMIT),
        name="prompt_ssd",
    )(ug_p, m_mat, wd_mat, r_mat, tab, abar)

    y_p, cst_p = _out_call(x_prompt, a_p, yg_p, None, w, tb=TB_OUT, sb=SUB_OUT, strided=True, cs=1, pad=SUBLANES)

    rows = ns * nt
    xs = x_sample.transpose(1, 0, 2).reshape(rows, D_MODEL)
    cos_s, sin_s = _rope_tables(PAST_LEN + jnp.arange(rows) // ns)
    kc = cache_k[0].reshape(ns * n_cache, KV_W)
    vc = cache_v[0].reshape(ns * n_cache, KV_W)
    h0 = jnp.concatenate([state_ssm_re[0], state_ssm_im[0]], axis=-1).transpose(1, 0, 2)
    dec = functools.partial(_decode_in_kernel, n_streams=ns, n_steps=nt, n_cache=n_cache)
    dec_in = [xs, g1, win, cos_s, sin_s, kc, vc, h0, m_mat, wd_mat, r_mat, abar]
    a_s, kn, vn, yg_s, sn = pl.pallas_call(
        dec,
        grid=(1,),
        in_specs=[smem] + [_const(t.shape) for t in dec_in],
        out_specs=[_whole((rows, D_ATTN)), _whole((rows, KV_W)), _whole((rows, KV_W)),
                   _whole((N_GROUPS, ns, SSD_W)), _whole((N_GROUPS, ns, STATE_W))],
        out_shape=[jax.ShapeDtypeStruct((rows, D_ATTN), F32), jax.ShapeDtypeStruct((rows, KV_W), F32),
                   jax.ShapeDtypeStruct((rows, KV_W), F32), jax.ShapeDtypeStruct((N_GROUPS, ns, SSD_W), BF16),
                   jax.ShapeDtypeStruct((N_GROUPS, ns, STATE_W), F32)],
        scratch_shapes=[pltpu.VMEM((D_SSM // LANES, rows, LANES), F32), pltpu.VMEM((N_GROUPS, ns, SSD_W), BF16)],
        compiler_params=pltpu.CompilerParams(dimension_semantics=("arbitrary",), vmem_limit_bytes=VMEM_LIMIT),
        name="decode_in",
    )(sinks, *dec_in)

    cs = ns
    prev = state_conv[0].transpose(1, 0, 2).reshape(1, (CONV_W - 1) * cs, 2 * D_FF)
    y_s, cst_s = _out_call(xs[None], a_s[None], yg_s[:, None], prev, w, tb=rows, sb=rows, strided=False, cs=cs,
                           pad=(CONV_W - 1) * cs)

    kv5 = lambda t, n: t.reshape(1, n, -1, N_KV_HEADS, HEAD_DIM)
    y_sample = y_s.reshape(nt, ns, D_MODEL).transpose(1, 0, 2)
    kn_b = kn.reshape(nt, ns, KV_W).transpose(1, 0, 2)
    vn_b = vn.reshape(nt, ns, KV_W).transpose(1, 0, 2)
    k_new = jnp.concatenate([cache_k[0].reshape(ns, n_cache, KV_W), kn_b], axis=1)[:, -n_cache:]
    v_new = jnp.concatenate([cache_v[0].reshape(ns, n_cache, KV_W), vn_b], axis=1)[:, -n_cache:]
    conv_p = cst_p[:, :, SUBLANES - 1, :][None]
    conv_s = cst_s.reshape(CONV_W - 1, ns, 2 * D_FF).transpose(1, 0, 2)[None]
    sn_b = sn.transpose(1, 0, 2)
    return (y_p, y_sample,
            kv5(k_p, nb), kv5(v_p, nb),
            sfin_p[None, :, :, 0, :SSM_STATE], sfin_p[None, :, :, 0, SSM_STATE:], conv_p,
            kv5(k_new, ns), kv5(v_new, ns),
            sn_b[None, :, :, :SSM_STATE], sn_b[None, :, :, SSM_STATE:], conv_s)
```

```python
import functools
import math

import numpy as np
import jax
import jax.numpy as jnp
from jax import lax
from jax.experimental import pallas as pl
from jax.experimental.pallas import tpu as pltpu

F32 = jnp.float32
BF16 = jnp.bfloat16

D_MODEL = 1024
CHUNK = 64
D_ATTN = 512
D_SSM = 512
HEAD_DIM = 64
N_HEADS = 8
N_KV_HEADS = 2
KV_W = N_KV_HEADS * HEAD_DIM
WINDOW = 128
N_WIN_CHUNKS = WINDOW // CHUNK
ROPE_THETA = 10000.0
SSM_GROUP = 16
N_GROUPS = D_SSM // SSM_GROUP
SSM_STATE = 64
D_FF = 2816
CONV_W = 3
EPS = 1e-6
D_IN = D_ATTN + 2 * KV_W + D_SSM
PAST_LEN = 2048

SSD_T = 16
LANES = 128
SUBLANES = 8
GROUPS_PER_VREG = LANES // SSM_GROUP
SSD_W = SSD_T * SSM_GROUP
STATE_W = 2 * SSM_STATE

TB_IN = 2048
SUB_IN = 512
TB_OUT = 512
SUB_OUT = 256
NC_SSD = 64
FF_TILE = 256
GROUP_UNROLL = 8
TABLE_GROUPS = 8
VMEM_LIMIT = 56 * 1024 * 1024


def _log2(n):
    assert n > 0 and n & (n - 1) == 0, n
    return n.bit_length() - 1


def _pdiv(x, n):
    return lax.shift_right_arithmetic(x, jnp.int32(_log2(n)))


def _pmod(x, n):
    return x & (n - 1)


def _rms(x, g):
    return x * lax.rsqrt(jnp.mean(x * x, axis=-1, keepdims=True) + EPS) * g


def _rope(xb, cos, sin_signed, first_half):
    partner = jnp.where(first_half, pltpu.roll(xb, LANES - HEAD_DIM // 2, 1), pltpu.roll(xb, HEAD_DIM // 2, 1))
    return xb * cos + partner * sin_signed


def _group_transpose8(xs, lane_group):
    xs = list(xs)
    for d in (4, 2, 1):
        keep = (lane_group & d) == 0
        nxt = list(xs)
        for i in range(GROUPS_PER_VREG):
            if i & d:
                continue
            a, b = xs[i], xs[i | d]
            nxt[i] = jnp.where(keep, a, pltpu.roll(b, SSM_GROUP * d, 1))
            nxt[i | d] = jnp.where(keep, pltpu.roll(a, LANES - SSM_GROUP * d, 1), b)
        xs = nxt
    return xs


def _row_sel(s, nk, strided, row0=0):
    if strided:
        return pl.ds(row0 + s, nk, stride=SSD_T)
    return pl.ds(row0 + s * nk, nk)


def _to_groups(u_ref, put, nk, strided, row0=0):
    lane_group = _pdiv(lax.broadcasted_iota(jnp.int32, (nk, LANES), 1), SSM_GROUP)
    for j in range(D_SSM // LANES):
        for half in range(SSD_T // GROUPS_PER_VREG):
            xs = [u_ref[j, _row_sel(GROUPS_PER_VREG * half + sl, nk, strided, row0), :]
                  for sl in range(GROUPS_PER_VREG)]
            ws = _group_transpose8(xs, lane_group)
            for gq in range(GROUPS_PER_VREG):
                put(GROUPS_PER_VREG * j + gq, half, ws[gq].astype(BF16))


def _from_groups(get, y_ref, nk, strided):
    lane_group = _pdiv(lax.broadcasted_iota(jnp.int32, (nk, LANES), 1), SSM_GROUP)
    for j in range(D_SSM // LANES):
        for half in range(SSD_T // GROUPS_PER_VREG):
            ws = [get(GROUPS_PER_VREG * j + gq, half).astype(F32) for gq in range(GROUPS_PER_VREG)]
            xs = _group_transpose8(ws, lane_group)
            for sl in range(GROUPS_PER_VREG):
                y_ref[j, _row_sel(GROUPS_PER_VREG * half + sl, nk, strided), :] = xs[sl]


def _cmul(re_full, im_signed, z):
    return re_full * z + im_signed * pltpu.roll(z, SSM_STATE, z.ndim - 1)


def _in_proj(x, g1_ref, win_ref, cos_ref, sin_ref):
    rows = x.shape[0]
    hn = _rms(x, g1_ref[...]).astype(BF16)
    proj = jnp.dot(hn, win_ref[...], preferred_element_type=F32)
    cos = cos_ref[...]
    sin = sin_ref[...]
    lane = lax.broadcasted_iota(jnp.int32, (rows, LANES), 1)
    first_half = _pmod(lane, HEAD_DIM) < (HEAD_DIM // 2)
    scale = HEAD_DIM ** -0.5
    qs = [_rope(proj[:, LANES * j:LANES * (j + 1)], cos, sin, first_half) * scale for j in range(D_ATTN // LANES)]
    k = _rope(proj[:, D_ATTN:D_ATTN + KV_W], cos, sin, first_half)
    v = proj[:, D_ATTN + KV_W:D_ATTN + 2 * KV_W]
    u = proj[:, D_ATTN + 2 * KV_W:]
    return qs, k, v, u


def _kv_variants(k, v):
    lane = lax.broadcasted_iota(jnp.int32, k.shape, 1)
    lo = lane < HEAD_DIM
    out = []
    for t in (k, v):
        tr = pltpu.roll(t, HEAD_DIM, 1)
        zero = jnp.zeros_like(t)
        out += [jnp.where(lo, t, zero), jnp.where(lo, zero, tr), jnp.where(lo, tr, zero), jnp.where(lo, zero, t)]
    return [o.astype(BF16) for o in out]


def _k_variants(k):
    lo = lax.broadcasted_iota(jnp.int32, k.shape, 1) < HEAD_DIM
    kr = pltpu.roll(k, HEAD_DIM, 1)
    zero = jnp.zeros_like(k)
    out = [jnp.where(lo, k, zero), jnp.where(lo, zero, kr), jnp.where(lo, kr, zero), jnp.where(lo, zero, k)]
    return [o.astype(BF16) for o in out]


def _softmax_pv_t(s, vt_win, sink_row):
    m = jnp.maximum(jnp.max(s, axis=0, keepdims=True), sink_row)
    e = jnp.exp(s - m)
    den = jnp.sum(e, axis=0, keepdims=True) + jnp.exp(sink_row - m)
    o = jnp.dot(vt_win, e.astype(BF16), preferred_element_type=F32)
    return o * (1.0 / den)


def _attend(a_bf, kvar, vvar, valid, sink_col):
    s = lax.dot_general(a_bf, kvar, (((1,), (1,)), ((), ())), preferred_element_type=F32)
    for cond in valid:
        s = jnp.where(cond, s, -jnp.inf)
    m = jnp.maximum(jnp.max(s, axis=-1, keepdims=True), sink_col)
    e = jnp.exp(s - m)
    den = jnp.sum(e, axis=-1, keepdims=True) + jnp.exp(sink_col - m)
    o = jnp.dot(e.astype(BF16), vvar, preferred_element_type=F32)
    return o * (1.0 / den)


def _prompt_in_kernel(sinks_ref, x_ref, g1_ref, win_ref, cos_ref, sin_ref,
                      a_ref, ug_ref, kp_ref, vp_ref, q_s, k_s, vb_s, vt_s, u_s, *, sb):
    tb = x_ref.shape[1]
    n_sub = tb // sb
    nks = sb // SSD_T
    blk = pl.program_id(1)
    nbuf = WINDOW + tb
    cps = sb // CHUNK
    n_chunks = tb // CHUNK

    @pl.when(blk == 0)
    def _():
        k_s[:, 0:WINDOW, :] = jnp.zeros((4, WINDOW, LANES), BF16)
        vb_s[0:WINDOW, :] = jnp.zeros((WINDOW, LANES), F32)

    @pl.when(blk > 0)
    def _():
        k_s[:, 0:WINDOW, :] = k_s[:, tb:nbuf, :]
        vb_s[0:WINDOW, :] = vb_s[tb:nbuf, :]

    lane = lax.broadcasted_iota(jnp.int32, (sb, LANES), 1)
    first_half = _pmod(lane, HEAD_DIM) < (HEAD_DIM // 2)
    scale = HEAD_DIM ** -0.5

    def project(s):
        rows = slice(s * sb, (s + 1) * sb)
        brows = slice(WINDOW + s * sb, WINDOW + (s + 1) * sb)
        state = {}

        def piece(c0):
            if 'hn' not in state:
                state['hn'] = _rms(x_ref[0, rows, :], g1_ref[...]).astype(BF16)
                state['cos'], state['sin'] = cos_ref[rows, :], sin_ref[rows, :]
            return jnp.dot(state['hn'], win_ref[:, c0:c0 + 2 * LANES], preferred_element_type=F32)

        def q_piece(j):
            def run():
                p = piece(2 * LANES * j)
                for h in range(2):
                    q_s[rows, LANES * (2 * j + h):LANES * (2 * j + h + 1)] = _rope(
                        p[:, LANES * h:LANES * (h + 1)], state['cos'], state['sin'], first_half) * scale
            return run

        def kv_piece():
            p = piece(D_ATTN)
            k = _rope(p[:, :KV_W], state['cos'], state['sin'], first_half)
            v = p[:, KV_W:]
            for idx, arr in enumerate(_k_variants(k)):
                k_s[idx, brows, :] = arr
            vb_s[brows, :] = v
            if s == n_sub - 1:
                kp_ref[0] = k[sb - WINDOW:, :]
                vp_ref[0] = v[sb - WINDOW:, :]

        def u_piece(j):
            def run():
                p = piece(D_ATTN + 2 * KV_W + 2 * LANES * j)
                for h in range(2):
                    u_s[2 * j + h, rows, :] = p[:, LANES * h:LANES * (h + 1)]
            return run

        return [kv_piece, q_piece(0), q_piece(1), u_piece(0), u_piece(1)]

    vt_tail = {}

    def build_vt(s):
        lo = 0 if s == 0 else WINDOW + s * sb
        hi = WINDOW + (s + 1) * sb
        piece = vb_s[lo:hi, :].T
        src = piece if s == 0 else jnp.concatenate([vt_tail['v'], piece], axis=1)
        lo1 = lo if s == 0 else lo - LANES
        shifted = pltpu.roll(src, src.shape[1] - CHUNK, 1)
        vt_tail['v'] = piece[:, piece.shape[1] - LANES:]
        for sh, arr, b0 in ((0, piece, lo), (1, shifted, lo1)):
            w = arr.shape[1]
            zrows = jnp.zeros((HEAD_DIM, w), F32)
            for g in range(N_KV_HEADS):
                rows = arr[HEAD_DIM * g:HEAD_DIM * (g + 1), :]
                vt_s[sh, g, 0, :, b0:b0 + w] = jnp.concatenate([rows, zrows], axis=0).astype(BF16)
                vt_s[sh, g, 1, :, b0:b0 + w] = jnp.concatenate([zrows, rows], axis=0).astype(BF16)

    def regroup(s):
        def put(g, half, val):
            ug_ref[g, 0, s * nks:(s + 1) * nks, LANES * half:LANES * (half + 1)] = val
        _to_groups(u_s, put, nks, strided=True, row0=s * sb)

    nkeys = WINDOW + CHUNK
    key_row = lax.broadcasted_iota(jnp.int32, (nkeys, 1), 0)
    q_lane = lax.broadcasted_iota(jnp.int32, (1, LANES), 1)

    def scores(i):
        r0 = i * CHUNK
        valid = key_row >= (WINDOW - r0 - blk * tb) if i < N_WIN_CHUNKS else None
        out = []
        for g in range(N_KV_HEADS):
            qa = q_s[r0:r0 + CHUNK, 2 * LANES * g:2 * LANES * g + LANES]
            qb = q_s[r0:r0 + CHUNK, 2 * LANES * g + LANES:2 * LANES * (g + 1)]
            a_bf = jnp.concatenate([qa, qb], axis=0).astype(BF16)
            for par in range(2):
                s = lax.dot_general(k_s[2 * g + par, r0:r0 + nkeys, :], a_bf, (((1,), (1,)), ((), ())),
                                    preferred_element_type=F32)
                out.append(s if valid is None else jnp.where(valid, s, -jnp.inf))
        return out

    for item in project(0):
        item()
    build_vt(0)
    staging = []
    for s in range(1, n_sub):
        staging += [(s, item) for item in project(s)] + [(s, functools.partial(build_vt, s))]
    regroups = [functools.partial(regroup, s) for s in range(n_sub)]
    pending = scores(0)
    for i in range(n_chunks):
        if i + 1 < n_chunks and (i + 1) % cps == 0:
            while staging and staging[0][0] <= (i + 1) // cps:
                staging.pop(0)[1]()
        nxt = scores(i + 1) if i + 1 < n_chunks else None
        if staging:
            staging.pop(0)[1]()
        elif regroups:
            regroups.pop(0)()
        r0 = i * CHUNK
        sh = i % 2
        base = r0 - sh * CHUNK
        for g in range(N_KV_HEADS):
            acc = None
            for par in range(2):
                h0 = 4 * g + par
                sink_row = jnp.where(q_lane < CHUNK, sinks_ref[h0], sinks_ref[h0 + 2])
                o = _softmax_pv_t(pending[2 * g + par], vt_s[sh, g, par, :, base:base + nkeys], sink_row)
                acc = o if acc is None else acc + o
            at = acc.T
            a_ref[0, r0:r0 + CHUNK, 2 * LANES * g:2 * LANES * g + LANES] = at[:CHUNK]
            a_ref[0, r0:r0 + CHUNK, 2 * LANES * g + LANES:2 * LANES * (g + 1)] = at[CHUNK:]
        pending = nxt
    for item in regroups:
        item()


def _prompt_ssd_kernel(ug_ref, m_ref, wd_ref, r_ref, tab_ref, abar_ref,
                       yg_ref, sfin_ref, d_s, sp_s, carry_s):
    nb = ug_ref.shape[1]
    nc = ug_ref.shape[2]
    step = pl.program_id(0)

    @pl.when(step == 0)
    def _():
        carry_s[...] = jnp.zeros(carry_s.shape, F32)

    def state_in(g, c):
        ub = ug_ref[g].reshape(nb * nc, SSD_W)
        d_s[g] = jnp.dot(ub, wd_ref[g], preferred_element_type=F32)
        return c

    lax.fori_loop(0, N_GROUPS, state_in, 0, unroll=GROUP_UNROLL)

    pw = [tab_ref[i] for i in range(8)]
    are, aim = abar_ref[0], abar_ref[1]
    kio = lax.broadcasted_iota(jnp.int32, (N_GROUPS, nc, STATE_W), 1)
    for b in range(nb):
        dg = d_s[:, b * nc:(b + 1) * nc, :]
        s_in = carry_s[b]
        dsh = jnp.where(kio == 0, s_in, pltpu.roll(dg, 1, 1))
        c = jnp.zeros((N_GROUPS, 1, STATE_W), F32)
        for j in range(nc // SUBLANES):
            x = dsh[:, SUBLANES * j:SUBLANES * (j + 1), :]
            for lvl in range(3):
                xs = pltpu.roll(x, 1 << lvl, 1)
                x = x + _cmul(pw[2 * lvl], pw[2 * lvl + 1], xs)
            cb = jnp.broadcast_to(c, x.shape)
            h = x + _cmul(pw[6], pw[7], cb)
            sp_s[:, b * nc + SUBLANES * j:b * nc + SUBLANES * (j + 1), :] = h
            c = h[:, SUBLANES - 1:SUBLANES, :]
        carry_s[b] = _cmul(are, aim, c) + dg[:, nc - 1:nc, :]

    def readout(g, c):
        ub = ug_ref[g].reshape(nb * nc, SSD_W)
        y = jnp.dot(ub, m_ref[g], preferred_element_type=F32)
        y = y + lax.dot_general(sp_s[g].astype(BF16), r_ref[g], (((1,), (1,)), ((), ())),
                                preferred_element_type=F32)
        yg_ref[g] = y.reshape(nb, nc, SSD_W).astype(BF16)
        return c

    lax.fori_loop(0, N_GROUPS, readout, 0, unroll=GROUP_UNROLL)
    sfin_ref[...] = carry_s[...]


def _out_kernel(*refs, strided, cs, has_prev):
    if has_prev:
        x_ref, a_ref, yg_ref, prev_ref = refs[:4]
        refs = refs[4:]
    else:
        x_ref, a_ref, yg_ref = refs[:3]
        prev_ref = None
        refs = refs[3:]
    (wglu_ref, ga_ref, gs_ref, wout_ref, g2_ref, wup_ref, cw_ref, cb_ref, wdown_ref, gf_ref,
     y_ref, cst_ref, ys_s, ext_s, px_s, py_s) = refs
    tb = x_ref.shape[1]
    n_sub, _, sb, _ = ys_s.shape
    nks = sb // SSD_T
    planes = strided
    pr = sb // SUBLANES
    pad = SUBLANES if planes else ext_s.shape[0] - sb
    blk = pl.program_id(1)
    assert n_sub * sb == tb and (not planes or (cs == 1 and not has_prev))

    @pl.when(blk == 0)
    def _():
        if planes:
            ext_s[:, 0:pad, :] = jnp.zeros((2, pad, 2 * D_FF), F32)
        elif has_prev:
            ext_s[0:pad, :] = prev_ref[0]
        else:
            ext_s[0:pad, :] = jnp.zeros((pad, 2 * D_FF), F32)

    def mix(i):
        rows = slice(i * sb, (i + 1) * sb)
        _from_groups(lambda g, half: yg_ref[g, 0, i * nks:(i + 1) * nks, LANES * half:LANES * (half + 1)],
                     ys_s.at[i], nks, strided)
        yv = jnp.concatenate([ys_s[i, j] for j in range(D_SSM // LANES)], axis=-1)
        z = 0.5 * yv * (1.0 + jnp.tanh(math.sqrt(2.0 / math.pi) * (yv + 0.044715 * (yv * yv * yv))))
        gate = jnp.dot(z.astype(BF16), wglu_ref[...], preferred_element_type=F32)
        s_out = z * jax.nn.sigmoid(gate)
        na = _rms(a_ref[0, rows, :], ga_ref[...]).astype(BF16)
        ns = _rms(s_out, gs_ref[...]).astype(BF16)
        x1 = x_ref[0, rows, :] + jnp.dot(na, wout_ref[0:D_ATTN, :], preferred_element_type=F32)
        return x1 + jnp.dot(ns, wout_ref[D_ATTN:, :], preferred_element_type=F32)

    def tile_cols(j, half):
        return slice(half * D_FF + j * FF_TILE, half * D_FF + (j + 1) * FF_TILE)

    def to_planes(x):
        for c in range(D_MODEL // LANES):
            px_s[c] = x[:, LANES * c:LANES * (c + 1)]
        return jnp.concatenate(
            [jnp.concatenate([px_s[c, pl.ds(r, pr, stride=SUBLANES), :] for r in range(SUBLANES)], axis=0)
             for c in range(D_MODEL // LANES)], axis=1)

    def store_rows(i, y):
        rows = slice(i * sb, (i + 1) * sb)
        if not planes:
            y_ref[0, rows, :] = y
            return
        for c in range(D_MODEL // LANES):
            for r in range(SUBLANES):
                py_s[c, pl.ds(r, pr, stride=SUBLANES), :] = y[r * pr:(r + 1) * pr, LANES * c:LANES * (c + 1)]
            y_ref[0, rows, LANES * c:LANES * (c + 1)] = py_s[c]

    def time_shifts(up, cols):
        if not planes:
            ext_s[pad:pad + sb, cols] = up
            sh2 = ext_s[pad - 2 * cs:pad - 2 * cs + sb, cols]
            sh1 = ext_s[pad - cs:pad - cs + sb, cols]
            ext_s[0:pad, cols] = ext_s[sb:sb + pad, cols]
            return sh2, sh1
        for p in range(2):
            ext_s[p, pad:pad + pr, cols] = up[(6 + p) * pr:(7 + p) * pr, :]
        s6 = ext_s[0, pad - 1:pad - 1 + pr, cols]
        s7 = ext_s[1, pad - 1:pad - 1 + pr, cols]
        ext_s[:, 0:pad, cols] = ext_s[:, pr:pr + pad, cols]
        sh1 = jnp.concatenate([s7, up[:7 * pr, :]], axis=0)
        sh2 = jnp.concatenate([s6, s7, up[:6 * pr, :]], axis=0)
        return sh2, sh1

    def conv_ffn(i, x1):
        if planes:
            x1 = to_planes(x1)
        h2 = _rms(x1, g2_ref[...]).astype(BF16)

        def up_proj(j):
            return [jnp.dot(h2, wup_ref[:, tile_cols(j, half)], preferred_element_type=F32) for half in range(2)]

        def down_proj(act, j):
            return jnp.dot(act, wdown_ref[j * FF_TILE:(j + 1) * FF_TILE, :], preferred_element_type=F32)

        n_tiles = D_FF // FF_TILE
        acc = jnp.zeros((sb, D_MODEL), F32)
        ups = up_proj(0)
        act_prev = None
        for j in range(n_tiles):
            nxt = up_proj(j + 1) if j + 1 < n_tiles else None
            if act_prev is not None:
                acc = acc + down_proj(act_prev, j - 1)
            parts = []
            for half in range(2):
                cols = tile_cols(j, half)
                sh2, sh1 = time_shifts(ups[half], cols)
                parts.append(sh2 * cw_ref[0:1, cols] + sh1 * cw_ref[1:2, cols]
                             + ups[half] * cw_ref[2:3, cols] + cb_ref[:, cols])
            cg, cv = parts
            act_prev = (cg * jax.nn.sigmoid(cg) * cv).astype(BF16)
            ups = nxt
        acc = acc + down_proj(act_prev, n_tiles - 1)
        store_rows(i, _rms(x1 + acc, gf_ref[...]))

    mixed = mix(0)
    for i in range(n_sub):
        nxt_mixed = mix(i + 1) if i + 1 < n_sub else None
        conv_ffn(i, mixed)
        mixed = nxt_mixed
    if planes:
        cst_ref[0] = ext_s[:, 0:pad, :]
    else:
        cst_ref[0] = ext_s[0:pad, :]


def _decode_in_kernel(sinks_ref, x_ref, g1_ref, win_ref, cos_ref, sin_ref, kc_ref, vc_ref, h0_ref,
                      m_ref, wd_ref, r_ref, abar_ref,
                      a_ref, kn_ref, vn_ref, yg_ref, sn_ref, u_s, ug_s, *, n_streams, n_steps, n_cache):
    rows = n_streams * n_steps
    qs, k, v, u = _in_proj(x_ref[...], g1_ref, win_ref, cos_ref, sin_ref)
    kn_ref[...] = k
    vn_ref[...] = v
    for j in range(D_SSM // LANES):
        u_s[j] = u[:, LANES * j:LANES * (j + 1)]
    ncache_rows = n_streams * n_cache
    nkeys = ncache_rows + rows
    kall = jnp.concatenate([kc_ref[...], k], axis=0)
    vall = jnp.concatenate([vc_ref[...], v], axis=0)
    variants = _kv_variants(kall, vall)

    colv = lax.broadcasted_iota(jnp.int32, (1, nkeys), 1)
    is_new = colv >= ncache_rows
    cnew = colv - ncache_rows
    k_stream = jnp.where(is_new, _pmod(cnew, n_streams), _pdiv(colv, n_cache))
    k_pos = jnp.where(is_new, PAST_LEN + _pdiv(cnew, n_streams), PAST_LEN - n_cache + _pmod(colv, n_cache))
    k_chunk = _pdiv(k_pos, CHUNK)
    rowv = _pmod(lax.broadcasted_iota(jnp.int32, (2 * rows, 1), 0), rows)
    q_stream = _pmod(rowv, n_streams)
    q_chunk = _pdiv(PAST_LEN + _pdiv(rowv, n_streams), CHUNK)
    ok_col = jnp.where(k_pos >= 0, k_stream, -1)
    d_chunk = q_chunk - k_chunk
    valid = [q_stream == ok_col,
             lax.bitcast_convert_type(d_chunk, jnp.uint32) <= jnp.uint32(N_WIN_CHUNKS)]
    top = lax.broadcasted_iota(jnp.int32, (2 * rows, 1), 0) < rows

    for g in range(N_KV_HEADS):
        a_bf = jnp.concatenate([qs[2 * g], qs[2 * g + 1]], axis=0).astype(BF16)
        acc = jnp.zeros((2 * rows, LANES), F32)
        for par in range(2):
            h0 = 4 * g + par
            sink_col = jnp.where(top, sinks_ref[h0], sinks_ref[h0 + 2])
            acc = acc + _attend(a_bf, variants[2 * g + par], variants[4 + 2 * g + par], valid, sink_col)
        a_ref[:, 2 * LANES * g:2 * LANES * g + LANES] = acc[:rows]
        a_ref[:, 2 * LANES * g + LANES:2 * LANES * (g + 1)] = acc[rows:]

    def put(g, half, val):
        ug_s[g, :, LANES * half:LANES * (half + 1)] = val

    _to_groups(u_s, put, n_streams, strided=False)
    def group_body(g, c):
        ub = ug_s[g]
        h0g = h0_ref[g]
        d = jnp.dot(ub, wd_ref[g], preferred_element_type=F32)
        sn_ref[g] = _cmul(abar_ref[0, g], abar_ref[1, g], h0g) + d
        y = jnp.dot(ub, m_ref[g], preferred_element_type=F32)
        y = y + lax.dot_general(h0g.astype(BF16), r_ref[g], (((1,), (1,)), ((), ())),
                                preferred_element_type=F32)
        yg_ref[g] = y.astype(BF16)
        return c

    lax.fori_loop(0, N_GROUPS, group_body, 0, unroll=GROUP_UNROLL)


def _ssd_tables(a_re, a_im, log_dt, b_re, b_im, c_re, c_im, d_skip):
    dup = lambda x: jnp.concatenate([x, x], axis=-1)
    prm = jnp.stack([dup(a_re), dup(a_im), jnp.broadcast_to(log_dt[:, None], (N_GROUPS, STATE_W))], axis=1)
    bt = jnp.concatenate([b_re.transpose(0, 2, 1), b_im.transpose(0, 2, 1)], axis=-1)
    cc = jnp.concatenate([c_re, c_im], axis=-1)
    dtile = jnp.tile(d_skip, (1, SSD_T))[:, None, :]
    gb = TABLE_GROUPS
    blk3 = lambda r, c: pl.BlockSpec((gb, r, c), lambda i: (i, 0, 0))
    return pl.pallas_call(
        _ssd_tables_kernel,
        grid=(N_GROUPS // gb,),
        in_specs=[blk3(3, STATE_W), blk3(SSM_GROUP, STATE_W), blk3(SSM_GROUP, STATE_W), blk3(1, SSD_W)],
        out_specs=[blk3(SSD_W, SSD_W), blk3(SSD_W, STATE_W), blk3(SSD_W, STATE_W),
                   pl.BlockSpec((8, gb, SUBLANES, STATE_W), lambda i: (0, i, 0, 0)),
                   pl.BlockSpec((2, gb, 1, STATE_W), lambda i: (0, i, 0, 0))],
        out_shape=[jax.ShapeDtypeStruct((N_GROUPS, SSD_W, SSD_W), BF16),
                   jax.ShapeDtypeStruct((N_GROUPS, SSD_W, STATE_W), BF16),
                   jax.ShapeDtypeStruct((N_GROUPS, SSD_W, STATE_W), BF16),
                   jax.ShapeDtypeStruct((8, N_GROUPS, SUBLANES, STATE_W), F32),
                   jax.ShapeDtypeStruct((2, N_GROUPS, 1, STATE_W), F32)],
        compiler_params=pltpu.CompilerParams(dimension_semantics=("arbitrary",)),
        name="ssd_tables",
    )(prm, bt, cc, dtile)


def _ssd_tables_kernel(prm_ref, bt_ref, cc_ref, dt_ref, m_ref, wd_ref, rt_ref, tab_ref, abar_ref):
    t = SSD_T
    lane = lax.broadcasted_iota(jnp.int32, (1, STATE_W), 1)
    sgn = jnp.where(lane >= SSM_STATE, 1.0, -1.0).astype(F32)
    conj = -sgn
    step = lax.broadcasted_iota(jnp.int32, (t, 1), 0).astype(F32)
    sub_i = lax.broadcasted_iota(jnp.int32, (SUBLANES, 1), 0)
    sub = sub_i.astype(F32)
    row = lax.broadcasted_iota(jnp.int32, (SSD_W, SSD_W), 0)
    col = lax.broadcasted_iota(jnp.int32, (SSD_W, SSD_W), 1)
    causal = _pdiv(col, SSM_GROUP) >= _pdiv(row, SSM_GROUP)
    diag = row == col
    for gi in range(prm_ref.shape[0]):
        a_re, a_im = prm_ref[gi, 0:1, :], prm_ref[gi, 1:2, :]
        dt = jnp.exp(prm_ref[gi, 2:3, :])
        lr, li = dt * a_re, dt * a_im

        def power(n):
            mag = jnp.exp(n * lr)
            return mag * jnp.cos(n * li), mag * jnp.sin(n * li) * sgn

        def times_powers(z, zs, pw):
            pr, pis = pw
            return jnp.concatenate([z * pr[n:n + 1, :] + zs * pis[n:n + 1, :] for n in range(t)], axis=0)

        ar, ais = power(jnp.ones((1, 1), F32))
        den = a_re * a_re + a_im * a_im
        nr, ni = ar - 1.0, ais * sgn
        fr, fis = (nr * a_re + ni * a_im) / den, (ni * a_re - nr * a_im) / den * sgn
        bt = bt_ref[gi]
        bb = bt * fr + pltpu.roll(bt, SSM_STATE, 1) * fis
        bbs = pltpu.roll(bb, SSM_STATE, 1)
        cc = cc_ref[gi]
        ccs = pltpu.roll(cc, SSM_STATE, 1)
        x = times_powers(bb, bbs, power(-step)) * conj
        y = times_powers(cc, ccs, power(step))
        m = lax.dot_general(x, y, (((1,), (1,)), ((), ())), precision=lax.Precision.HIGHEST,
                            preferred_element_type=F32)
        m = jnp.where(causal, m, 0.0) + jnp.where(diag, dt_ref[gi], 0.0)
        m_ref[gi] = m.astype(BF16)
        wd_ref[gi] = times_powers(bb, bbs, power(float(t - 1) - step)).astype(BF16)
        rt_ref[gi] = (times_powers(cc, ccs, power(step + 1.0)) * conj).astype(BF16)
        for lvl in range(3):
            d = 1 << lvl
            pr, pis = power(jnp.full((SUBLANES, 1), float(t * d), F32))
            tab_ref[2 * lvl, gi] = jnp.where(sub_i >= d, pr, 0.0)
            tab_ref[2 * lvl + 1, gi] = jnp.where(sub_i >= d, pis, 0.0)
        pr, pis = power(float(t) * (sub + 1.0))
        tab_ref[6, gi] = pr
        tab_ref[7, gi] = pis
        pr, pis = power(jnp.full((1, 1), float(t), F32))
        abar_ref[0, gi] = pr
        abar_ref[1, gi] = pis


def _rope_tables(pos):
    half = HEAD_DIM // 2
    lane = np.arange(LANES)
    inv = ROPE_THETA ** (-jnp.asarray(lane % half, F32) / half)
    sign = jnp.asarray(np.where(lane % HEAD_DIM < half, -1.0, 1.0), F32)
    ang = pos.astype(F32)[:, None] * inv[None, :]
    return jnp.cos(ang), jnp.sin(ang) * sign[None, :]


def _const(shape):
    nd = len(shape)
    return pl.BlockSpec(shape, lambda *_: (0,) * nd, pipeline_mode=pl.Buffered(1))


def _whole(shape):
    nd = len(shape)
    return pl.BlockSpec(shape, lambda *_: (0,) * nd)


def _out_call(x, a, yg, prev, w, *, tb, sb, strided, cs, pad):
    nb, length, _ = x.shape
    nk = tb // SSD_T
    grid = (nb, length // tb)
    in_specs = [
        pl.BlockSpec((1, tb, D_MODEL), lambda b, i: (b, i, 0)),
        pl.BlockSpec((1, tb, D_ATTN), lambda b, i: (b, i, 0)),
        pl.BlockSpec((N_GROUPS, 1, nk, SSD_W), lambda b, i: (0, b, i, 0)),
    ]
    args = [x, a, yg]
    if prev is not None:
        in_specs.append(pl.BlockSpec((1, pad, 2 * D_FF), lambda b, i: (b, 0, 0)))
        args.append(prev)
    weights = [w['w_glu'], w['onorm_a'], w['onorm_s'], w['w_out'], w['norm2'], w['w_up'], w['conv_w'],
               w['conv_b'], w['w_down'], w['final_g']]
    in_specs += [_const(t.shape) for t in weights]
    kern = functools.partial(_out_kernel, strided=strided, cs=cs, has_prev=prev is not None)
    if strided:
        ext_shape = (2, pad + sb // SUBLANES, 2 * D_FF)
        cst_block, cst_map = (1, 2, pad, 2 * D_FF), (lambda b, i: (b, 0, 0, 0))
    else:
        ext_shape = (pad + sb, 2 * D_FF)
        cst_block, cst_map = (1, pad, 2 * D_FF), (lambda b, i: (b, 0, 0))
    return pl.pallas_call(
        kern,
        grid=grid,
        in_specs=in_specs,
        out_specs=[pl.BlockSpec((1, tb, D_MODEL), lambda b, i: (b, i, 0)),
                   pl.BlockSpec(cst_block, cst_map)],
        out_shape=[jax.ShapeDtypeStruct((nb, length, D_MODEL), F32),
                   jax.ShapeDtypeStruct((nb,) + cst_block[1:], F32)],
        scratch_shapes=[pltpu.VMEM((tb // sb, D_SSM // LANES, sb, LANES), F32),
                        pltpu.VMEM(ext_shape, F32),
                        pltpu.VMEM((D_MODEL // LANES, sb, LANES), F32),
                        pltpu.VMEM((D_MODEL // LANES, sb, LANES), F32)],
        compiler_params=pltpu.CompilerParams(dimension_semantics=("arbitrary", "arbitrary"),
                                             vmem_limit_bytes=VMEM_LIMIT),
        name="layer_out",
    )(*args, *weights)


def kernel(x_prompt, x_sample, cache_k, cache_v, state_ssm_re, state_ssm_im, state_conv, norm1_g, w_in, attn_sinks, ssm_A_re, ssm_A_im, ssm_log_dt, ssm_B_re, ssm_B_im, ssm_C_re, ssm_C_im, ssm_D, w_glu, onorm_attn_g, onorm_ssm_g, w_out, norm2_g, w_up, conv_w, conv_b, w_down, final_g):
    assert norm1_g.shape[0] == 1, "one layer"
    nb, length, _ = x_prompt.shape
    ns, nt, _ = x_sample.shape
    n_cache = cache_k.shape[2]
    assert nt == SSD_T and length % TB_IN == 0 and length % TB_OUT == 0 and (length // SSD_T) % NC_SSD == 0

    m_mat, wd_mat, r_mat, tab, abar = _ssd_tables(ssm_A_re[0], ssm_A_im[0], ssm_log_dt[0], ssm_B_re[0],
                                                  ssm_B_im[0], ssm_C_re[0], ssm_C_im[0], ssm_D[0])
    sinks = attn_sinks[0]
    g1 = norm1_g
    win = w_in[0].astype(BF16)
    w = dict(w_glu=w_glu[0].astype(BF16), onorm_a=onorm_attn_g, onorm_s=onorm_ssm_g, w_out=w_out[0].astype(BF16),
             norm2=norm2_g, w_up=w_up[0].astype(BF16), conv_w=conv_w[0], conv_b=conv_b,
             w_down=w_down[0].astype(BF16), final_g=final_g[None, :])
    smem = pl.BlockSpec(memory_space=pltpu.SMEM)

    cos_p, sin_p = _rope_tables(jnp.arange(length))
    nk = TB_IN // SSD_T
    a_p, ug_p, k_p, v_p = pl.pallas_call(
        functools.partial(_prompt_in_kernel, sb=SUB_IN),
        grid=(nb, length // TB_IN),
        in_specs=[smem,
                  pl.BlockSpec((1, TB_IN, D_MODEL), lambda b, i: (b, i, 0)),
                  _const((1, D_MODEL)), _const((D_MODEL, D_IN)),
                  pl.BlockSpec((TB_IN, LANES), lambda b, i: (i, 0)),
                  pl.BlockSpec((TB_IN, LANES), lambda b, i: (i, 0))],
        out_specs=[pl.BlockSpec((1, TB_IN, D_ATTN), lambda b, i: (b, i, 0)),
                   pl.BlockSpec((N_GROUPS, 1, nk, SSD_W), lambda b, i: (0, b, i, 0)),
                   pl.BlockSpec((1, WINDOW, KV_W), lambda b, i: (b, 0, 0)),
                   pl.BlockSpec((1, WINDOW, KV_W), lambda b, i: (b, 0, 0))],
        out_shape=[jax.ShapeDtypeStruct((nb, length, D_ATTN), F32),
                   jax.ShapeDtypeStruct((N_GROUPS, nb, length // SSD_T, SSD_W), BF16),
                   jax.ShapeDtypeStruct((nb, WINDOW, KV_W), F32),
                   jax.ShapeDtypeStruct((nb, WINDOW, KV_W), F32)],
        scratch_shapes=[pltpu.VMEM((TB_IN, D_ATTN), F32),
                        pltpu.VMEM((2 * N_KV_HEADS, WINDOW + TB_IN, LANES), BF16),
                        pltpu.VMEM((WINDOW + TB_IN, LANES), F32),
                        pltpu.VMEM((2, N_KV_HEADS, 2, LANES, WINDOW + TB_IN), BF16),
                        pltpu.VMEM((D_SSM // LANES, TB_IN, LANES), F32)],
        compiler_params=pltpu.CompilerParams(dimension_semantics=("arbitrary", "arbitrary"),
                                             vmem_limit_bytes=VMEM_LIMIT),
        name="prompt_in",
    )(sinks, x_prompt, g1, win, cos_p, sin_p)

    yg_p, sfin_p = pl.pallas_call(
        _prompt_ssd_kernel,
        grid=(length // SSD_T // NC_SSD,),
        in_specs=[pl.BlockSpec((N_GROUPS, nb, NC_SSD, SSD_W), lambda i: (0, 0, i, 0)),
                  _const(m_mat.shape), _const(wd_mat.shape), _const(r_mat.shape), _const(tab.shape),
                  _const(abar.shape)],
        out_specs=[pl.BlockSpec((N_GROUPS, nb, NC_SSD, SSD_W), lambda i: (0, 0, i, 0)),
                   pl.BlockSpec((nb, N_GROUPS, 1, STATE_W), lambda i: (0, 0, 0, 0))],
        out_shape=[jax.ShapeDtypeStruct((N_GROUPS, nb, length // SSD_T, SSD_W), BF16),
                   jax.ShapeDtypeStruct((nb, N_GROUPS, 1, STATE_W), F32)],
        scratch_shapes=[pltpu.VMEM((N_GROUPS, nb * NC_SSD, STATE_W), F32),
                        pltpu.VMEM((N_GROUPS, nb * NC_SSD, STATE_W), F32),
                        pltpu.VMEM((nb, N_GROUPS, 1, STATE_W), F32)],
        compiler_params=pltpu.CompilerParams(dimension_semantics=("arbitrary",), vmem_limit_bytes=VMEM_LIMIT),
        name="prompt_ssd",
    )(ug_p, m_mat, wd_mat, r_mat, tab, abar)

    y_p, cst_p = _out_call(x_prompt, a_p, yg_p, None, w, tb=TB_OUT, sb=SUB_OUT, strided=True, cs=1, pad=SUBLANES)

    rows = ns * nt
    xs = x_sample.transpose(1, 0, 2).reshape(rows, D_MODEL)
    cos_s, sin_s = _rope_tables(PAST_LEN + jnp.arange(rows) // ns)
    kc = cache_k[0].reshape(ns * n_cache, KV_W)
    vc = cache_v[0].reshape(ns * n_cache, KV_W)
    h0 = jnp.concatenate([state_ssm_re[0], state_ssm_im[0]], axis=-1).transpose(1, 0, 2)
    dec = functools.partial(_decode_in_kernel, n_streams=ns, n_steps=nt, n_cache=n_cache)
    dec_in = [xs, g1, win, cos_s, sin_s, kc, vc, h0, m_mat, wd_mat, r_mat, abar]
    a_s, kn, vn, yg_s, sn = pl.pallas_call(
        dec,
        grid=(1,),
        in_specs=[smem] + [_const(t.shape) for t in dec_in],
        out_specs=[_whole((rows, D_ATTN)), _whole((rows, KV_W)), _whole((rows, KV_W)),
                   _whole((N_GROUPS, ns, SSD_W)), _whole((N_GROUPS, ns, STATE_W))],
        out_shape=[jax.ShapeDtypeStruct((rows, D_ATTN), F32), jax.ShapeDtypeStruct((rows, KV_W), F32),
                   jax.ShapeDtypeStruct((rows, KV_W), F32), jax.ShapeDtypeStruct((N_GROUPS, ns, SSD_W), BF16),
                   jax.ShapeDtypeStruct((N_GROUPS, ns, STATE_W), F32)],
        scratch_shapes=[pltpu.VMEM((D_SSM // LANES, rows, LANES), F32), pltpu.VMEM((N_GROUPS, ns, SSD_W), BF16)],
        compiler_params=pltpu.CompilerParams(dimension_semantics=("arbitrary",), vmem_limit_bytes=VMEM_LIMIT),
        name="decode_in",
    )(sinks, *dec_in)

    cs = ns
    prev = state_conv[0].transpose(1, 0, 2).reshape(1, (CONV_W - 1) * cs, 2 * D_FF)
    y_s, cst_s = _out_call(xs[None], a_s[None], yg_s[:, None], prev, w, tb=rows, sb=rows, strided=False, cs=cs,
                           pad=(CONV_W - 1) * cs)

    kv5 = lambda t, n: t.reshape(1, n, -1, N_KV_HEADS, HEAD_DIM)
    y_sample = y_s.reshape(nt, ns, D_MODEL).transpose(1, 0, 2)
    kn_b = kn.reshape(nt, ns, KV_W).transpose(1, 0, 2)
    vn_b = vn.reshape(nt, ns, KV_W).transpose(1, 0, 2)
    k_new = jnp.concatenate([cache_k[0].reshape(ns, n_cache, KV_W), kn_b], axis=1)[:, -n_cache:]
    v_new = jnp.concatenate([cache_v[0].reshape(ns, n_cache, KV_W), vn_b], axis=1)[:, -n_cache:]
    conv_p = cst_p[:, :, SUBLANES - 1, :][None]
    conv_s = cst_s.reshape(CONV_W - 1, ns, 2 * D_FF).transpose(1, 0, 2)[None]
    sn_b = sn.transpose(1, 0, 2)
    return (y_p, y_sample,
            kv5(k_p, nb), kv5(v_p, nb),
            sfin_p[None, :, :, 0, :SSM_STATE], sfin_p[None, :, :, 0, SSM_STATE:], conv_p,
            kv5(k_new, ns), kv5(v_new, ns),
            sn_b[None, :, :, :SSM_STATE], sn_b[None, :, :, SSM_STATE:], conv_s)
```

```python
import functools
import math

import numpy as np
import jax
import jax.numpy as jnp
from jax import lax
from jax.experimental import pallas as pl
from jax.experimental.pallas import tpu as pltpu

F32 = jnp.float32
BF16 = jnp.bfloat16

D_MODEL = 1024
CHUNK = 64
D_ATTN = 512
D_SSM = 512
HEAD_DIM = 64
N_HEADS = 8
N_KV_HEADS = 2
KV_W = N_KV_HEADS * HEAD_DIM
WINDOW = 128
N_WIN_CHUNKS = WINDOW // CHUNK
ROPE_THETA = 10000.0
SSM_GROUP = 16
N_GROUPS = D_SSM // SSM_GROUP
SSM_STATE = 64
D_FF = 2816
CONV_W = 3
EPS = 1e-6
D_IN = D_ATTN + 2 * KV_W + D_SSM
PAST_LEN = 2048

SSD_T = 16
LANES = 128
SUBLANES = 8
GROUPS_PER_VREG = LANES // SSM_GROUP
SSD_W = SSD_T * SSM_GROUP
STATE_W = 2 * SSM_STATE

TB_IN = 2048
SUB_IN = 512
TB_OUT = 512
SUB_OUT = 256
NC_SSD = 64
FF_TILE = 256
GROUP_UNROLL = 8
TABLE_GROUPS = 8
VMEM_LIMIT = 56 * 1024 * 1024


def _log2(n):
    assert n > 0 and n & (n - 1) == 0, n
    return n.bit_length() - 1


def _pdiv(x, n):
    return lax.shift_right_arithmetic(x, jnp.int32(_log2(n)))


def _pmod(x, n):
    return x & (n - 1)


def _rms(x, g):
    return x * lax.rsqrt(jnp.mean(x * x, axis=-1, keepdims=True) + EPS) * g


def _rope(xb, cos, sin_signed, first_half):
    partner = jnp.where(first_half, pltpu.roll(xb, LANES - HEAD_DIM // 2, 1), pltpu.roll(xb, HEAD_DIM // 2, 1))
    return xb * cos + partner * sin_signed


def _group_transpose8(xs, lane_group):
    xs = list(xs)
    for d in (4, 2, 1):
        keep = (lane_group & d) == 0
        nxt = list(xs)
        for i in range(GROUPS_PER_VREG):
            if i & d:
                continue
            a, b = xs[i], xs[i | d]
            nxt[i] = jnp.where(keep, a, pltpu.roll(b, SSM_GROUP * d, 1))
            nxt[i | d] = jnp.where(keep, pltpu.roll(a, LANES - SSM_GROUP * d, 1), b)
        xs = nxt
    return xs


def _row_sel(s, nk, strided, row0=0):
    if strided:
        return pl.ds(row0 + s, nk, stride=SSD_T)
    return pl.ds(row0 + s * nk, nk)


def _to_groups(u_ref, put, nk, strided, row0=0):
    lane_group = _pdiv(lax.broadcasted_iota(jnp.int32, (nk, LANES), 1), SSM_GROUP)
    for j in range(D_SSM // LANES):
        for half in range(SSD_T // GROUPS_PER_VREG):
            xs = [u_ref[j, _row_sel(GROUPS_PER_VREG * half + sl, nk, strided, row0), :]
                  for sl in range(GROUPS_PER_VREG)]
            ws = _group_transpose8(xs, lane_group)
            for gq in range(GROUPS_PER_VREG):
                put(GROUPS_PER_VREG * j + gq, half, ws[gq].astype(BF16))


def _from_groups(get, y_ref, nk, strided):
    lane_group = _pdiv(lax.broadcasted_iota(jnp.int32, (nk, LANES), 1), SSM_GROUP)
    for j in range(D_SSM // LANES):
        for half in range(SSD_T // GROUPS_PER_VREG):
            ws = [get(GROUPS_PER_VREG * j + gq, half).astype(F32) for gq in range(GROUPS_PER_VREG)]
            xs = _group_transpose8(ws, lane_group)
            for sl in range(GROUPS_PER_VREG):
                y_ref[j, _row_sel(GROUPS_PER_VREG * half + sl, nk, strided), :] = xs[sl]


def _cmul(re_full, im_signed, z):
    return re_full * z + im_signed * pltpu.roll(z, SSM_STATE, z.ndim - 1)


def _in_proj(x, g1_ref, win_ref, cos_ref, sin_ref):
    rows = x.shape[0]
    hn = _rms(x, g1_ref[...]).astype(BF16)
    proj = jnp.dot(hn, win_ref[...], preferred_element_type=F32)
    cos = cos_ref[...]
    sin = sin_ref[...]
    lane = lax.broadcasted_iota(jnp.int32, (rows, LANES), 1)
    first_half = _pmod(lane, HEAD_DIM) < (HEAD_DIM // 2)
    scale = HEAD_DIM ** -0.5
    qs = [_rope(proj[:, LANES * j:LANES * (j + 1)], cos, sin, first_half) * scale for j in range(D_ATTN // LANES)]
    k = _rope(proj[:, D_ATTN:D_ATTN + KV_W], cos, sin, first_half)
    v = proj[:, D_ATTN + KV_W:D_ATTN + 2 * KV_W]
    u = proj[:, D_ATTN + 2 * KV_W:]
    return qs, k, v, u


def _kv_variants(k, v):
    lane = lax.broadcasted_iota(jnp.int32, k.shape, 1)
    lo = lane < HEAD_DIM
    out = []
    for t in (k, v):
        tr = pltpu.roll(t, HEAD_DIM, 1)
        zero = jnp.zeros_like(t)
        out += [jnp.where(lo, t, zero), jnp.where(lo, zero, tr), jnp.where(lo, tr, zero), jnp.where(lo, zero, t)]
    return [o.astype(BF16) for o in out]


def _k_variants(k):
    lo = lax.broadcasted_iota(jnp.int32, k.shape, 1) < HEAD_DIM
    kr = pltpu.roll(k, HEAD_DIM, 1)
    zero = jnp.zeros_like(k)
    out = [jnp.where(lo, k, zero), jnp.where(lo, zero, kr), jnp.where(lo, kr, zero), jnp.where(lo, zero, k)]
    return [o.astype(BF16) for o in out]


def _softmax_pv_t(s, vt_win, sink_row):
    m = jnp.maximum(jnp.max(s, axis=0, keepdims=True), sink_row)
    e = jnp.exp(s - m)
    den = jnp.sum(e, axis=0, keepdims=True) + jnp.exp(sink_row - m)
    o = jnp.dot(vt_win, e.astype(BF16), preferred_element_type=F32)
    return o * (1.0 / den)


def _attend(a_bf, kvar, vvar, valid, sink_col):
    s = lax.dot_general(a_bf, kvar, (((1,), (1,)), ((), ())), preferred_element_type=F32)
    for cond in valid:
        s = jnp.where(cond, s, -jnp.inf)
    m = jnp.maximum(jnp.max(s, axis=-1, keepdims=True), sink_col)
    e = jnp.exp(s - m)
    den = jnp.sum(e, axis=-1, keepdims=True) + jnp.exp(sink_col - m)
    o = jnp.dot(e.astype(BF16), vvar, preferred_element_type=F32)
    return o * (1.0 / den)


def _prompt_in_kernel(sinks_ref, x_ref, g1_ref, win_ref, cos_ref, sin_ref,
                      a_ref, ug_ref, kp_ref, vp_ref, q_s, k_s, vb_s, vt_s, u_s, *, sb):
    tb = x_ref.shape[1]
    n_sub = tb // sb
    nks = sb // SSD_T
    blk = pl.program_id(1)
    nbuf = WINDOW + tb
    cps = sb // CHUNK
    n_chunks = tb // CHUNK

    @pl.when(blk == 0)
    def _():
        k_s[:, 0:WINDOW, :] = jnp.zeros((4, WINDOW, LANES), BF16)
        vb_s[0:WINDOW, :] = jnp.zeros((WINDOW, LANES), F32)

    @pl.when(blk > 0)
    def _():
        k_s[:, 0:WINDOW, :] = k_s[:, tb:nbuf, :]
        vb_s[0:WINDOW, :] = vb_s[tb:nbuf, :]

    lane = lax.broadcasted_iota(jnp.int32, (sb, LANES), 1)
    first_half = _pmod(lane, HEAD_DIM) < (HEAD_DIM // 2)
    scale = HEAD_DIM ** -0.5

    def project(s):
        rows = slice(s * sb, (s + 1) * sb)
        brows = slice(WINDOW + s * sb, WINDOW + (s + 1) * sb)
        state = {}

        def piece(c0):
            if 'hn' not in state:
                state['hn'] = _rms(x_ref[0, rows, :], g1_ref[...]).astype(BF16)
                state['cos'], state['sin'] = cos_ref[rows, :], sin_ref[rows, :]
            return jnp.dot(state['hn'], win_ref[:, c0:c0 + 2 * LANES], preferred_element_type=F32)

        def q_piece(j):
            def run():
                p = piece(2 * LANES * j)
                for h in range(2):
                    q_s[rows, LANES * (2 * j + h):LANES * (2 * j + h + 1)] = _rope(
                        p[:, LANES * h:LANES * (h + 1)], state['cos'], state['sin'], first_half) * scale
            return run

        def kv_piece():
            p = piece(D_ATTN)
            k = _rope(p[:, :KV_W], state['cos'], state['sin'], first_half)
            v = p[:, KV_W:]
            for idx, arr in enumerate(_k_variants(k)):
                k_s[idx, brows, :] = arr
            vb_s[brows, :] = v
            if s == n_sub - 1:
                kp_ref[0] = k[sb - WINDOW:, :]
                vp_ref[0] = v[sb - WINDOW:, :]

        def u_piece(j):
            def run():
                p = piece(D_ATTN + 2 * KV_W + 2 * LANES * j)
                for h in range(2):
                    u_s[2 * j + h, rows, :] = p[:, LANES * h:LANES * (h + 1)]
            return run

        return [kv_piece, q_piece(0), q_piece(1), u_piece(0), u_piece(1)]

    vt_tail = {}

    def build_vt(s):
        lo = 0 if s == 0 else WINDOW + s * sb
        hi = WINDOW + (s + 1) * sb
        piece = vb_s[lo:hi, :].T
        src = piece if s == 0 else jnp.concatenate([vt_tail['v'], piece], axis=1)
        lo1 = lo if s == 0 else lo - LANES
        shifted = pltpu.roll(src, src.shape[1] - CHUNK, 1)
        vt_tail['v'] = piece[:, piece.shape[1] - LANES:]
        for sh, arr, b0 in ((0, piece, lo), (1, shifted, lo1)):
            w = arr.shape[1]
            zrows = jnp.zeros((HEAD_DIM, w), F32)
            for g in range(N_KV_HEADS):
                rows = arr[HEAD_DIM * g:HEAD_DIM * (g + 1), :]
                vt_s[sh, g, 0, :, b0:b0 + w] = jnp.concatenate([rows, zrows], axis=0).astype(BF16)
                vt_s[sh, g, 1, :, b0:b0 + w] = jnp.concatenate([zrows, rows], axis=0).astype(BF16)

    def regroup(s):
        def put(g, half, val):
            ug_ref[g, 0, s * nks:(s + 1) * nks, LANES * half:LANES * (half + 1)] = val
        _to_groups(u_s, put, nks, strided=True, row0=s * sb)

    nkeys = WINDOW + CHUNK
    key_row = lax.broadcasted_iota(jnp.int32, (nkeys, 1), 0)
    q_lane = lax.broadcasted_iota(jnp.int32, (1, LANES), 1)

    def scores(i):
        r0 = i * CHUNK
        valid = key_row >= (WINDOW - r0 - blk * tb) if i < N_WIN_CHUNKS else None
        out = []
        for g in range(N_KV_HEADS):
            qa = q_s[r0:r0 + CHUNK, 2 * LANES * g:2 * LANES * g + LANES]
            qb = q_s[r0:r0 + CHUNK, 2 * LANES * g + LANES:2 * LANES * (g + 1)]
            a_bf = jnp.concatenate([qa, qb], axis=0).astype(BF16)
            for par in range(2):
                s = lax.dot_general(k_s[2 * g + par, r0:r0 + nkeys, :], a_bf, (((1,), (1,)), ((), ())),
                                    preferred_element_type=F32)
                out.append(s if valid is None else jnp.where(valid, s, -jnp.inf))
        return out

    for item in project(0):
        item()
    build_vt(0)
    staging = []
    for s in range(1, n_sub):
        staging += [(s, item) for item in project(s)] + [(s, functools.partial(build_vt, s))]
    regroups = [functools.partial(regroup, s) for s in range(n_sub)]
    pending = scores(0)
    for i in range(n_chunks):
        if i + 1 < n_chunks and (i + 1) % cps == 0:
            while staging and staging[0][0] <= (i + 1) // cps:
                staging.pop(0)[1]()
        nxt = scores(i + 1) if i + 1 < n_chunks else None
        if staging:
            staging.pop(0)[1]()
        elif regroups:
            regroups.pop(0)()
        r0 = i * CHUNK
        sh = i % 2
        base = r0 - sh * CHUNK
        for g in range(N_KV_HEADS):
            acc = None
            for par in range(2):
                h0 = 4 * g + par
                sink_row = jnp.where(q_lane < CHUNK, sinks_ref[h0], sinks_ref[h0 + 2])
                o = _softmax_pv_t(pending[2 * g + par], vt_s[sh, g, par, :, base:base + nkeys], sink_row)
                acc = o if acc is None else acc + o
            at = acc.T
            a_ref[0, r0:r0 + CHUNK, 2 * LANES * g:2 * LANES * g + LANES] = at[:CHUNK]
            a_ref[0, r0:r0 + CHUNK, 2 * LANES * g + LANES:2 * LANES * (g + 1)] = at[CHUNK:]
        pending = nxt
    for item in regroups:
        item()


def _prompt_ssd_kernel(ug_ref, m_ref, wd_ref, r_ref, tab_ref, abar_ref,
                       yg_ref, sfin_ref, d_s, sp_s, carry_s):
    nb = ug_ref.shape[1]
    nc = ug_ref.shape[2]
    step = pl.program_id(0)

    @pl.when(step == 0)
    def _():
        carry_s[...] = jnp.zeros(carry_s.shape, F32)

    def state_in(g, c):
        ub = ug_ref[g].reshape(nb * nc, SSD_W)
        d_s[g] = jnp.dot(ub, wd_ref[g], preferred_element_type=F32)
        return c

    lax.fori_loop(0, N_GROUPS, state_in, 0, unroll=GROUP_UNROLL)

    hp = N_GROUPS // 2
    lo_half = lax.broadcasted_iota(jnp.int32, (1, 1, STATE_W), 2) < SSM_STATE

    def split(z):
        a, b = z[:hp], z[hp:]
        return (jnp.where(lo_half, a, pltpu.roll(b, SSM_STATE, 2)),
                jnp.where(lo_half, pltpu.roll(a, SSM_STATE, 2), b))

    def merge(re, im):
        return jnp.concatenate([jnp.where(lo_half, re, pltpu.roll(im, SSM_STATE, 2)),
                                jnp.where(lo_half, pltpu.roll(re, SSM_STATE, 2), im)], axis=0)

    def pair_tables(tre, tim):
        return jnp.where(lo_half, tre[:hp], tre[hp:]), jnp.where(lo_half, -tim[:hp], tim[hp:])

    pw = [pair_tables(tab_ref[2 * i], tab_ref[2 * i + 1]) for i in range(4)]
    ar, ai = pair_tables(abar_ref[0], abar_ref[1])
    kio = lax.broadcasted_iota(jnp.int32, (N_GROUPS, nc, STATE_W), 1)
    for b in range(nb):
        dg = d_s[:, b * nc:(b + 1) * nc, :]
        s_in = carry_s[b]
        xr_all, xi_all = split(jnp.where(kio == 0, s_in, pltpu.roll(dg, 1, 1)))
        cr = ci = jnp.zeros((hp, 1, STATE_W), F32)
        for j in range(nc // SUBLANES):
            xr = xr_all[:, SUBLANES * j:SUBLANES * (j + 1), :]
            xi = xi_all[:, SUBLANES * j:SUBLANES * (j + 1), :]
            for lvl in range(3):
                pr, pi = pw[lvl]
                sr, si = pltpu.roll(xr, 1 << lvl, 1), pltpu.roll(xi, 1 << lvl, 1)
                xr, xi = xr + pr * sr - pi * si, xi + pr * si + pi * sr
            qr, qi = pw[3]
            crb, cib = jnp.broadcast_to(cr, xr.shape), jnp.broadcast_to(ci, xi.shape)
            hr, hi = xr + qr * crb - qi * cib, xi + qr * cib + qi * crb
            sp_s[:, b * nc + SUBLANES * j:b * nc + SUBLANES * (j + 1), :] = merge(hr, hi)
            cr, ci = hr[:, SUBLANES - 1:SUBLANES, :], hi[:, SUBLANES - 1:SUBLANES, :]
        carry_s[b] = merge(ar * cr - ai * ci, ar * ci + ai * cr) + dg[:, nc - 1:nc, :]

    def readout(g, c):
        ub = ug_ref[g].reshape(nb * nc, SSD_W)
        y = jnp.dot(ub, m_ref[g], preferred_element_type=F32)
        y = y + lax.dot_general(sp_s[g].astype(BF16), r_ref[g], (((1,), (1,)), ((), ())),
                                preferred_element_type=F32)
        yg_ref[g] = y.reshape(nb, nc, SSD_W).astype(BF16)
        return c

    lax.fori_loop(0, N_GROUPS, readout, 0, unroll=GROUP_UNROLL)
    sfin_ref[...] = carry_s[...]


def _out_kernel(*refs, strided, cs, has_prev):
    if has_prev:
        x_ref, a_ref, yg_ref, prev_ref = refs[:4]
        refs = refs[4:]
    else:
        x_ref, a_ref, yg_ref = refs[:3]
        prev_ref = None
        refs = refs[3:]
    (wglu_ref, ga_ref, gs_ref, wout_ref, g2_ref, wup_ref, cw_ref, cb_ref, wdown_ref, gf_ref,
     y_ref, cst_ref, ys_s, ext_s, px_s, py_s) = refs
    tb = x_ref.shape[1]
    n_sub, _, sb, _ = ys_s.shape
    nks = sb // SSD_T
    planes = strided
    pr = sb // SUBLANES
    pad = SUBLANES if planes else ext_s.shape[0] - sb
    blk = pl.program_id(1)
    assert n_sub * sb == tb and (not planes or (cs == 1 and not has_prev))

    @pl.when(blk == 0)
    def _():
        if planes:
            ext_s[:, 0:pad, :] = jnp.zeros((2, pad, 2 * D_FF), F32)
        elif has_prev:
            ext_s[0:pad, :] = prev_ref[0]
        else:
            ext_s[0:pad, :] = jnp.zeros((pad, 2 * D_FF), F32)

    def mix(i):
        rows = slice(i * sb, (i + 1) * sb)
        _from_groups(lambda g, half: yg_ref[g, 0, i * nks:(i + 1) * nks, LANES * half:LANES * (half + 1)],
                     ys_s.at[i], nks, strided)
        yv = jnp.concatenate([ys_s[i, j] for j in range(D_SSM // LANES)], axis=-1)
        z = 0.5 * yv * (1.0 + jnp.tanh(math.sqrt(2.0 / math.pi) * (yv + 0.044715 * (yv * yv * yv))))
        gate = jnp.dot(z.astype(BF16), wglu_ref[...], preferred_element_type=F32)
        s_out = z * jax.nn.sigmoid(gate)
        na = _rms(a_ref[0, rows, :], ga_ref[...]).astype(BF16)
        ns = _rms(s_out, gs_ref[...]).astype(BF16)
        x1 = x_ref[0, rows, :] + jnp.dot(na, wout_ref[0:D_ATTN, :], preferred_element_type=F32)
        return x1 + jnp.dot(ns, wout_ref[D_ATTN:, :], preferred_element_type=F32)

    def tile_cols(j, half):
        return slice(half * D_FF + j * FF_TILE, half * D_FF + (j + 1) * FF_TILE)

    def to_planes(x):
        for c in range(D_MODEL // LANES):
            px_s[c] = x[:, LANES * c:LANES * (c + 1)]
        return jnp.concatenate(
            [jnp.concatenate([px_s[c, pl.ds(r, pr, stride=SUBLANES), :] for r in range(SUBLANES)], axis=0)
             for c in range(D_MODEL // LANES)], axis=1)

    def store_rows(i, y):
        rows = slice(i * sb, (i + 1) * sb)
        if not planes:
            y_ref[0, rows, :] = y
            return
        for c in range(D_MODEL // LANES):
            for r in range(SUBLANES):
                py_s[c, pl.ds(r, pr, stride=SUBLANES), :] = y[r * pr:(r + 1) * pr, LANES * c:LANES * (c + 1)]
            y_ref[0, rows, LANES * c:LANES * (c + 1)] = py_s[c]

    def time_shifts(up, cols):
        if not planes:
            ext_s[pad:pad + sb, cols] = up
            sh2 = ext_s[pad - 2 * cs:pad - 2 * cs + sb, cols]
            sh1 = ext_s[pad - cs:pad - cs + sb, cols]
            ext_s[0:pad, cols] = ext_s[sb:sb + pad, cols]
            return sh2, sh1
        for p in range(2):
            ext_s[p, pad:pad + pr, cols] = up[(6 + p) * pr:(7 + p) * pr, :]
        s6 = ext_s[0, pad - 1:pad - 1 + pr, cols]
        s7 = ext_s[1, pad - 1:pad - 1 + pr, cols]
        ext_s[:, 0:pad, cols] = ext_s[:, pr:pr + pad, cols]
        sh1 = jnp.concatenate([s7, up[:7 * pr, :]], axis=0)
        sh2 = jnp.concatenate([s6, s7, up[:6 * pr, :]], axis=0)
        return sh2, sh1

    def conv_ffn(i, x1):
        if planes:
            x1 = to_planes(x1)
        h2 = _rms(x1, g2_ref[...]).astype(BF16)

        def up_proj(j):
            return [jnp.dot(h2, wup_ref[:, tile_cols(j, half)], preferred_element_type=F32) for half in range(2)]

        def down_proj(act, j):
            return jnp.dot(act, wdown_ref[j * FF_TILE:(j + 1) * FF_TILE, :], preferred_element_type=F32)

        n_tiles = D_FF // FF_TILE
        acc = jnp.zeros((sb, D_MODEL), F32)
        ups = up_proj(0)
        act_prev = None
        for j in range(n_tiles):
            nxt = up_proj(j + 1) if j + 1 < n_tiles else None
            if act_prev is not None:
                acc = acc + down_proj(act_prev, j - 1)
            parts = []
            for half in range(2):
                cols = tile_cols(j, half)
                sh2, sh1 = time_shifts(ups[half], cols)
                parts.append(sh2 * cw_ref[0:1, cols] + sh1 * cw_ref[1:2, cols]
                             + ups[half] * cw_ref[2:3, cols] + cb_ref[:, cols])
            cg, cv = parts
            act_prev = (cg * jax.nn.sigmoid(cg) * cv).astype(BF16)
            ups = nxt
        acc = acc + down_proj(act_prev, n_tiles - 1)
        store_rows(i, _rms(x1 + acc, gf_ref[...]))

    mixed = mix(0)
    for i in range(n_sub):
        nxt_mixed = mix(i + 1) if i + 1 < n_sub else None
        conv_ffn(i, mixed)
        mixed = nxt_mixed
    if planes:
        cst_ref[0] = ext_s[:, 0:pad, :]
    else:
        cst_ref[0] = ext_s[0:pad, :]


def _decode_in_kernel(sinks_ref, x_ref, g1_ref, win_ref, cos_ref, sin_ref, kc_ref, vc_ref, h0_ref,
                      m_ref, wd_ref, r_ref, abar_ref,
                      a_ref, kn_ref, vn_ref, yg_ref, sn_ref, u_s, ug_s, *, n_streams, n_steps, n_cache):
    rows = n_streams * n_steps
    qs, k, v, u = _in_proj(x_ref[...], g1_ref, win_ref, cos_ref, sin_ref)
    kn_ref[...] = k
    vn_ref[...] = v
    for j in range(D_SSM // LANES):
        u_s[j] = u[:, LANES * j:LANES * (j + 1)]
    ncache_rows = n_streams * n_cache
    nkeys = ncache_rows + rows
    kall = jnp.concatenate([kc_ref[...], k], axis=0)
    vall = jnp.concatenate([vc_ref[...], v], axis=0)
    variants = _kv_variants(kall, vall)

    colv = lax.broadcasted_iota(jnp.int32, (1, nkeys), 1)
    is_new = colv >= ncache_rows
    cnew = colv - ncache_rows
    k_stream = jnp.where(is_new, _pmod(cnew, n_streams), _pdiv(colv, n_cache))
    k_pos = jnp.where(is_new, PAST_LEN + _pdiv(cnew, n_streams), PAST_LEN - n_cache + _pmod(colv, n_cache))
    k_chunk = _pdiv(k_pos, CHUNK)
    rowv = _pmod(lax.broadcasted_iota(jnp.int32, (2 * rows, 1), 0), rows)
    q_stream = _pmod(rowv, n_streams)
    q_chunk = _pdiv(PAST_LEN + _pdiv(rowv, n_streams), CHUNK)
    ok_col = jnp.where(k_pos >= 0, k_stream, -1)
    d_chunk = q_chunk - k_chunk
    valid = [q_stream == ok_col,
             lax.bitcast_convert_type(d_chunk, jnp.uint32) <= jnp.uint32(N_WIN_CHUNKS)]
    top = lax.broadcasted_iota(jnp.int32, (2 * rows, 1), 0) < rows

    for g in range(N_KV_HEADS):
        a_bf = jnp.concatenate([qs[2 * g], qs[2 * g + 1]], axis=0).astype(BF16)
        acc = jnp.zeros((2 * rows, LANES), F32)
        for par in range(2):
            h0 = 4 * g + par
            sink_col = jnp.where(top, sinks_ref[h0], sinks_ref[h0 + 2])
            acc = acc + _attend(a_bf, variants[2 * g + par], variants[4 + 2 * g + par], valid, sink_col)
        a_ref[:, 2 * LANES * g:2 * LANES * g + LANES] = acc[:rows]
        a_ref[:, 2 * LANES * g + LANES:2 * LANES * (g + 1)] = acc[rows:]

    def put(g, half, val):
        ug_s[g, :, LANES * half:LANES * (half + 1)] = val

    _to_groups(u_s, put, n_streams, strided=False)
    def group_body(g, c):
        ub = ug_s[g]
        h0g = h0_ref[g]
        d = jnp.dot(ub, wd_ref[g], preferred_element_type=F32)
        sn_ref[g] = _cmul(abar_ref[0, g], abar_ref[1, g], h0g) + d
        y = jnp.dot(ub, m_ref[g], preferred_element_type=F32)
        y = y + lax.dot_general(h0g.astype(BF16), r_ref[g], (((1,), (1,)), ((), ())),
                                preferred_element_type=F32)
        yg_ref[g] = y.astype(BF16)
        return c

    lax.fori_loop(0, N_GROUPS, group_body, 0, unroll=GROUP_UNROLL)


def _ssd_tables(a_re, a_im, log_dt, b_re, b_im, c_re, c_im, d_skip):
    dup = lambda x: jnp.concatenate([x, x], axis=-1)
    prm = jnp.stack([dup(a_re), dup(a_im), jnp.broadcast_to(log_dt[:, None], (N_GROUPS, STATE_W))], axis=1)
    bt = jnp.concatenate([b_re.transpose(0, 2, 1), b_im.transpose(0, 2, 1)], axis=-1)
    cc = jnp.concatenate([c_re, c_im], axis=-1)
    dtile = jnp.tile(d_skip, (1, SSD_T))[:, None, :]
    gb = TABLE_GROUPS
    blk3 = lambda r, c: pl.BlockSpec((gb, r, c), lambda i: (i, 0, 0))
    return pl.pallas_call(
        _ssd_tables_kernel,
        grid=(N_GROUPS // gb,),
        in_specs=[blk3(3, STATE_W), blk3(SSM_GROUP, STATE_W), blk3(SSM_GROUP, STATE_W), blk3(1, SSD_W)],
        out_specs=[blk3(SSD_W, SSD_W), blk3(SSD_W, STATE_W), blk3(SSD_W, STATE_W),
                   pl.BlockSpec((8, gb, SUBLANES, STATE_W), lambda i: (0, i, 0, 0)),
                   pl.BlockSpec((2, gb, 1, STATE_W), lambda i: (0, i, 0, 0))],
        out_shape=[jax.ShapeDtypeStruct((N_GROUPS, SSD_W, SSD_W), BF16),
                   jax.ShapeDtypeStruct((N_GROUPS, SSD_W, STATE_W), BF16),
                   jax.ShapeDtypeStruct((N_GROUPS, SSD_W, STATE_W), BF16),
                   jax.ShapeDtypeStruct((8, N_GROUPS, SUBLANES, STATE_W), F32),
                   jax.ShapeDtypeStruct((2, N_GROUPS, 1, STATE_W), F32)],
        compiler_params=pltpu.CompilerParams(dimension_semantics=("arbitrary",)),
        name="ssd_tables",
    )(prm, bt, cc, dtile)


def _ssd_tables_kernel(prm_ref, bt_ref, cc_ref, dt_ref, m_ref, wd_ref, rt_ref, tab_ref, abar_ref):
    t = SSD_T
    lane = lax.broadcasted_iota(jnp.int32, (1, STATE_W), 1)
    sgn = jnp.where(lane >= SSM_STATE, 1.0, -1.0).astype(F32)
    conj = -sgn
    step = lax.broadcasted_iota(jnp.int32, (t, 1), 0).astype(F32)
    sub_i = lax.broadcasted_iota(jnp.int32, (SUBLANES, 1), 0)
    sub = sub_i.astype(F32)
    row = lax.broadcasted_iota(jnp.int32, (SSD_W, SSD_W), 0)
    col = lax.broadcasted_iota(jnp.int32, (SSD_W, SSD_W), 1)
    causal = _pdiv(col, SSM_GROUP) >= _pdiv(row, SSM_GROUP)
    diag = row == col
    for gi in range(prm_ref.shape[0]):
        a_re, a_im = prm_ref[gi, 0:1, :], prm_ref[gi, 1:2, :]
        dt = jnp.exp(prm_ref[gi, 2:3, :])
        lr, li = dt * a_re, dt * a_im

        def power(n):
            mag = jnp.exp(n * lr)
            return mag * jnp.cos(n * li), mag * jnp.sin(n * li) * sgn

        def times_powers(z, zs, pw):
            pr, pis = pw
            return jnp.concatenate([z * pr[n:n + 1, :] + zs * pis[n:n + 1, :] for n in range(t)], axis=0)

        ar, ais = power(jnp.ones((1, 1), F32))
        den = a_re * a_re + a_im * a_im
        nr, ni = ar - 1.0, ais * sgn
        fr, fis = (nr * a_re + ni * a_im) / den, (ni * a_re - nr * a_im) / den * sgn
        bt = bt_ref[gi]
        bb = bt * fr + pltpu.roll(bt, SSM_STATE, 1) * fis
        bbs = pltpu.roll(bb, SSM_STATE, 1)
        cc = cc_ref[gi]
        ccs = pltpu.roll(cc, SSM_STATE, 1)
        x = times_powers(bb, bbs, power(-step)) * conj
        y = times_powers(cc, ccs, power(step))
        m = lax.dot_general(x, y, (((1,), (1,)), ((), ())), precision=lax.Precision.HIGHEST,
                            preferred_element_type=F32)
        m = jnp.where(causal, m, 0.0) + jnp.where(diag, dt_ref[gi], 0.0)
        m_ref[gi] = m.astype(BF16)
        wd_ref[gi] = times_powers(bb, bbs, power(float(t - 1) - step)).astype(BF16)
        rt_ref[gi] = (times_powers(cc, ccs, power(step + 1.0)) * conj).astype(BF16)
        for lvl in range(3):
            d = 1 << lvl
            pr, pis = power(jnp.full((SUBLANES, 1), float(t * d), F32))
            tab_ref[2 * lvl, gi] = jnp.where(sub_i >= d, pr, 0.0)
            tab_ref[2 * lvl + 1, gi] = jnp.where(sub_i >= d, pis, 0.0)
        pr, pis = power(float(t) * (sub + 1.0))
        tab_ref[6, gi] = pr
        tab_ref[7, gi] = pis
        pr, pis = power(jnp.full((1, 1), float(t), F32))
        abar_ref[0, gi] = pr
        abar_ref[1, gi] = pis


def _rope_tables(pos):
    half = HEAD_DIM // 2
    lane = np.arange(LANES)
    inv = ROPE_THETA ** (-jnp.asarray(lane % half, F32) / half)
    sign = jnp.asarray(np.where(lane % HEAD_DIM < half, -1.0, 1.0), F32)
    ang = pos.astype(F32)[:, None] * inv[None, :]
    return jnp.cos(ang), jnp.sin(ang) * sign[None, :]


def _const(shape):
    nd = len(shape)
    return pl.BlockSpec(shape, lambda *_: (0,) * nd, pipeline_mode=pl.Buffered(1))


def _whole(shape):
    nd = len(shape)
    return pl.BlockSpec(shape, lambda *_: (0,) * nd)


def _out_call(x, a, yg, prev, w, *, tb, sb, strided, cs, pad):
    nb, length, _ = x.shape
    nk = tb // SSD_T
    grid = (nb, length // tb)
    in_specs = [
        pl.BlockSpec((1, tb, D_MODEL), lambda b, i: (b, i, 0)),
        pl.BlockSpec((1, tb, D_ATTN), lambda b, i: (b, i, 0)),
        pl.BlockSpec((N_GROUPS, 1, nk, SSD_W), lambda b, i: (0, b, i, 0)),
    ]
    args = [x, a, yg]
    if prev is not None:
        in_specs.append(pl.BlockSpec((1, pad, 2 * D_FF), lambda b, i: (b, 0, 0)))
        args.append(prev)
    weights = [w['w_glu'], w['onorm_a'], w['onorm_s'], w['w_out'], w['norm2'], w['w_up'], w['conv_w'],
               w['conv_b'], w['w_down'], w['final_g']]
    in_specs += [_const(t.shape) for t in weights]
    kern = functools.partial(_out_kernel, strided=strided, cs=cs, has_prev=prev is not None)
    if strided:
        ext_shape = (2, pad + sb // SUBLANES, 2 * D_FF)
        cst_block, cst_map = (1, 2, pad, 2 * D_FF), (lambda b, i: (b, 0, 0, 0))
    else:
        ext_shape = (pad + sb, 2 * D_FF)
        cst_block, cst_map = (1, pad, 2 * D_FF), (lambda b, i: (b, 0, 0))
    return pl.pallas_call(
        kern,
        grid=grid,
        in_specs=in_specs,
        out_specs=[pl.BlockSpec((1, tb, D_MODEL), lambda b, i: (b, i, 0)),
                   pl.BlockSpec(cst_block, cst_map)],
        out_shape=[jax.ShapeDtypeStruct((nb, length, D_MODEL), F32),
                   jax.ShapeDtypeStruct((nb,) + cst_block[1:], F32)],
        scratch_shapes=[pltpu.VMEM((tb // sb, D_SSM // LANES, sb, LANES), F32),
                        pltpu.VMEM(ext_shape, F32),
                        pltpu.VMEM((D_MODEL // LANES, sb, LANES), F32),
                        pltpu.VMEM((D_MODEL // LANES, sb, LANES), F32)],
        compiler_params=pltpu.CompilerParams(dimension_semantics=("arbitrary", "arbitrary"),
                                             vmem_limit_bytes=VMEM_LIMIT),
        name="layer_out",
    )(*args, *weights)


def kernel(x_prompt, x_sample, cache_k, cache_v, state_ssm_re, state_ssm_im, state_conv, norm1_g, w_in, attn_sinks, ssm_A_re, ssm_A_im, ssm_log_dt, ssm_B_re, ssm_B_im, ssm_C_re, ssm_C_im, ssm_D, w_glu, onorm_attn_g, onorm_ssm_g, w_out, norm2_g, w_up, conv_w, conv_b, w_down, final_g):
    assert norm1_g.shape[0] == 1, "one layer"
    nb, length, _ = x_prompt.shape
    ns, nt, _ = x_sample.shape
    n_cache = cache_k.shape[2]
    assert nt == SSD_T and length % TB_IN == 0 and length % TB_OUT == 0 and (length // SSD_T) % NC_SSD == 0

    m_mat, wd_mat, r_mat, tab, abar = _ssd_tables(ssm_A_re[0], ssm_A_im[0], ssm_log_dt[0], ssm_B_re[0],
                                                  ssm_B_im[0], ssm_C_re[0], ssm_C_im[0], ssm_D[0])
    sinks = attn_sinks[0]
    g1 = norm1_g
    win = w_in[0].astype(BF16)
    w = dict(w_glu=w_glu[0].astype(BF16), onorm_a=onorm_attn_g, onorm_s=onorm_ssm_g, w_out=w_out[0].astype(BF16),
             norm2=norm2_g, w_up=w_up[0].astype(BF16), conv_w=conv_w[0], conv_b=conv_b,
             w_down=w_down[0].astype(BF16), final_g=final_g[None, :])
    smem = pl.BlockSpec(memory_space=pltpu.SMEM)

    cos_p, sin_p = _rope_tables(jnp.arange(length))
    nk = TB_IN // SSD_T
    a_p, ug_p, k_p, v_p = pl.pallas_call(
        functools.partial(_prompt_in_kernel, sb=SUB_IN),
        grid=(nb, length // TB_IN),
        in_specs=[smem,
                  pl.BlockSpec((1, TB_IN, D_MODEL), lambda b, i: (b, i, 0)),
                  _const((1, D_MODEL)), _const((D_MODEL, D_IN)),
                  pl.BlockSpec((TB_IN, LANES), lambda b, i: (i, 0)),
                  pl.BlockSpec((TB_IN, LANES), lambda b, i: (i, 0))],
        out_specs=[pl.BlockSpec((1, TB_IN, D_ATTN), lambda b, i: (b, i, 0)),
                   pl.BlockSpec((N_GROUPS, 1, nk, SSD_W), lambda b, i: (0, b, i, 0)),
                   pl.BlockSpec((1, WINDOW, KV_W), lambda b, i: (b, 0, 0)),
                   pl.BlockSpec((1, WINDOW, KV_W), lambda b, i: (b, 0, 0))],
        out_shape=[jax.ShapeDtypeStruct((nb, length, D_ATTN), F32),
                   jax.ShapeDtypeStruct((N_GROUPS, nb, length // SSD_T, SSD_W), BF16),
                   jax.ShapeDtypeStruct((nb, WINDOW, KV_W), F32),
                   jax.ShapeDtypeStruct((nb, WINDOW, KV_W), F32)],
        scratch_shapes=[pltpu.VMEM((TB_IN, D_ATTN), F32),
                        pltpu.VMEM((2 * N_KV_HEADS, WINDOW + TB_IN, LANES), BF16),
                        pltpu.VMEM((WINDOW + TB_IN, LANES), F32),
                        pltpu.VMEM((2, N_KV_HEADS, 2, LANES, WINDOW + TB_IN), BF16),
                        pltpu.VMEM((D_SSM // LANES, TB_IN, LANES), F32)],
        compiler_params=pltpu.CompilerParams(dimension_semantics=("arbitrary", "arbitrary"),
                                             vmem_limit_bytes=VMEM_LIMIT),
        name="prompt_in",
    )(sinks, x_prompt, g1, win, cos_p, sin_p)

    yg_p, sfin_p = pl.pallas_call(
        _prompt_ssd_kernel,
        grid=(length // SSD_T // NC_SSD,),
        in_specs=[pl.BlockSpec((N_GROUPS, nb, NC_SSD, SSD_W), lambda i: (0, 0, i, 0)),
                  _const(m_mat.shape), _const(wd_mat.shape), _const(r_mat.shape), _const(tab.shape),
                  _const(abar.shape)],
        out_specs=[pl.BlockSpec((N_GROUPS, nb, NC_SSD, SSD_W), lambda i: (0, 0, i, 0)),
                   pl.BlockSpec((nb, N_GROUPS, 1, STATE_W), lambda i: (0, 0, 0, 0))],
        out_shape=[jax.ShapeDtypeStruct((N_GROUPS, nb, length // SSD_T, SSD_W), BF16),
                   jax.ShapeDtypeStruct((nb, N_GROUPS, 1, STATE_W), F32)],
        scratch_shapes=[pltpu.VMEM((N_GROUPS, nb * NC_SSD, STATE_W), F32),
                        pltpu.VMEM((N_GROUPS, nb * NC_SSD, STATE_W), F32),
                        pltpu.VMEM((nb, N_GROUPS, 1, STATE_W), F32)],
        compiler_params=pltpu.CompilerParams(dimension_semantics=("arbitrary",), vmem_limit_bytes=VMEM_LIMIT),
        name="prompt_ssd",
    )(ug_p, m_mat, wd_mat, r_mat, tab, abar)

    y_p, cst_p = _out_call(x_prompt, a_p, yg_p, None, w, tb=TB_OUT, sb=SUB_OUT, strided=True, cs=1, pad=SUBLANES)

    rows = ns * nt
    xs = x_sample.transpose(1, 0, 2).reshape(rows, D_MODEL)
    cos_s, sin_s = _rope_tables(PAST_LEN + jnp.arange(rows) // ns)
    kc = cache_k[0].reshape(ns * n_cache, KV_W)
    vc = cache_v[0].reshape(ns * n_cache, KV_W)
    h0 = jnp.concatenate([state_ssm_re[0], state_ssm_im[0]], axis=-1).transpose(1, 0, 2)
    dec = functools.partial(_decode_in_kernel, n_streams=ns, n_steps=nt, n_cache=n_cache)
    dec_in = [xs, g1, win, cos_s, sin_s, kc, vc, h0, m_mat, wd_mat, r_mat, abar]
    a_s, kn, vn, yg_s, sn = pl.pallas_call(
        dec,
        grid=(1,),
        in_specs=[smem] + [_const(t.shape) for t in dec_in],
        out_specs=[_whole((rows, D_ATTN)), _whole((rows, KV_W)), _whole((rows, KV_W)),
                   _whole((N_GROUPS, ns, SSD_W)), _whole((N_GROUPS, ns, STATE_W))],
        out_shape=[jax.ShapeDtypeStruct((rows, D_ATTN), F32), jax.ShapeDtypeStruct((rows, KV_W), F32),
                   jax.ShapeDtypeStruct((rows, KV_W), F32), jax.ShapeDtypeStruct((N_GROUPS, ns, SSD_W), BF16),
                   jax.ShapeDtypeStruct((N_GROUPS, ns, STATE_W), F32)],
        scratch_shapes=[pltpu.VMEM((D_SSM // LANES, rows, LANES), F32), pltpu.VMEM((N_GROUPS, ns, SSD_W), BF16)],
        compiler_params=pltpu.CompilerParams(dimension_semantics=("arbitrary",), vmem_limit_bytes=VMEM_LIMIT),
        name="decode_in",
    )(sinks, *dec_in)

    cs = ns
    prev = state_conv[0].transpose(1, 0, 2).reshape(1, (CONV_W - 1) * cs, 2 * D_FF)
    y_s, cst_s = _out_call(xs[None], a_s[None], yg_s[:, None], prev, w, tb=rows, sb=rows, strided=False, cs=cs,
                           pad=(CONV_W - 1) * cs)

    kv5 = lambda t, n: t.reshape(1, n, -1, N_KV_HEADS, HEAD_DIM)
    y_sample = y_s.reshape(nt, ns, D_MODEL).transpose(1, 0, 2)
    kn_b = kn.reshape(nt, ns, KV_W).transpose(1, 0, 2)
    vn_b = vn.reshape(nt, ns, KV_W).transpose(1, 0, 2)
    k_new = jnp.concatenate([cache_k[0].reshape(ns, n_cache, KV_W), kn_b], axis=1)[:, -n_cache:]
    v_new = jnp.concatenate([cache_v[0].reshape(ns, n_cache, KV_W), vn_b], axis=1)[:, -n_cache:]
    conv_p = cst_p[:, :, SUBLANES - 1, :][None]
    conv_s = cst_s.reshape(CONV_W - 1, ns, 2 * D_FF).transpose(1, 0, 2)[None]
    sn_b = sn.transpose(1, 0, 2)
    return (y_p, y_sample,
            kv5(k_p, nb), kv5(v_p, nb),
            sfin_p[None, :, :, 0, :SSM_STATE], sfin_p[None, :, :, 0, SSM_STATE:], conv_p,
            kv5(k_new, ns), kv5(v_new, ns),
            sn_b[None, :, :, :SSM_STATE], sn_b[None, :, :, SSM_STATE:], conv_s)
```

```python
import functools
import math

import numpy as np
import jax
import jax.numpy as jnp
from jax import lax
from jax.experimental import pallas as pl
from jax.experimental.pallas import tpu as pltpu

F32 = jnp.float32
BF16 = jnp.bfloat16

D_MODEL = 1024
CHUNK = 64
D_ATTN = 512
D_SSM = 512
HEAD_DIM = 64
N_HEADS = 8
N_KV_HEADS = 2
KV_W = N_KV_HEADS * HEAD_DIM
WINDOW = 128
N_WIN_CHUNKS = WINDOW // CHUNK
ROPE_THETA = 10000.0
SSM_GROUP = 16
N_GROUPS = D_SSM // SSM_GROUP
SSM_STATE = 64
D_FF = 2816
CONV_W = 3
EPS = 1e-6
D_IN = D_ATTN + 2 * KV_W + D_SSM
PAST_LEN = 2048

SSD_T = 16
LANES = 128
SUBLANES = 8
GROUPS_PER_VREG = LANES // SSM_GROUP
SSD_W = SSD_T * SSM_GROUP
STATE_W = 2 * SSM_STATE

TB_IN = 2048
SUB_IN = 512
TB_OUT = 512
SUB_OUT = 256
NC_SSD = 64
FF_TILE = 256
GROUP_UNROLL = 8
TABLE_GROUPS = 8
VMEM_LIMIT = 56 * 1024 * 1024


def _log2(n):
    assert n > 0 and n & (n - 1) == 0, n
    return n.bit_length() - 1


def _pdiv(x, n):
    return lax.shift_right_arithmetic(x, jnp.int32(_log2(n)))


def _pmod(x, n):
    return x & (n - 1)


def _rms(x, g):
    return x * lax.rsqrt(jnp.mean(x * x, axis=-1, keepdims=True) + EPS) * g


def _rope(xb, cos, sin_signed, first_half):
    partner = jnp.where(first_half, pltpu.roll(xb, LANES - HEAD_DIM // 2, 1), pltpu.roll(xb, HEAD_DIM // 2, 1))
    return xb * cos + partner * sin_signed


def _group_transpose8(xs, lane_group):
    xs = list(xs)
    for d in (4, 2, 1):
        keep = (lane_group & d) == 0
        nxt = list(xs)
        for i in range(GROUPS_PER_VREG):
            if i & d:
                continue
            a, b = xs[i], xs[i | d]
            nxt[i] = jnp.where(keep, a, pltpu.roll(b, SSM_GROUP * d, 1))
            nxt[i | d] = jnp.where(keep, pltpu.roll(a, LANES - SSM_GROUP * d, 1), b)
        xs = nxt
    return xs


def _row_sel(s, nk, strided, row0=0):
    if strided:
        return pl.ds(row0 + s, nk, stride=SSD_T)
    return pl.ds(row0 + s * nk, nk)


def _to_groups(u_ref, put, nk, strided, row0=0):
    lane_group = _pdiv(lax.broadcasted_iota(jnp.int32, (nk, LANES), 1), SSM_GROUP)
    for j in range(D_SSM // LANES):
        for half in range(SSD_T // GROUPS_PER_VREG):
            xs = [u_ref[j, _row_sel(GROUPS_PER_VREG * half + sl, nk, strided, row0), :]
                  for sl in range(GROUPS_PER_VREG)]
            ws = _group_transpose8(xs, lane_group)
            for gq in range(GROUPS_PER_VREG):
                put(GROUPS_PER_VREG * j + gq, half, ws[gq].astype(BF16))


def _from_groups(get, y_ref, nk, strided):
    lane_group = _pdiv(lax.broadcasted_iota(jnp.int32, (nk, LANES), 1), SSM_GROUP)
    for j in range(D_SSM // LANES):
        for half in range(SSD_T // GROUPS_PER_VREG):
            ws = [get(GROUPS_PER_VREG * j + gq, half).astype(F32) for gq in range(GROUPS_PER_VREG)]
            xs = _group_transpose8(ws, lane_group)
            for sl in range(GROUPS_PER_VREG):
                y_ref[j, _row_sel(GROUPS_PER_VREG * half + sl, nk, strided), :] = xs[sl]


def _cmul(re_full, im_signed, z):
    return re_full * z + im_signed * pltpu.roll(z, SSM_STATE, z.ndim - 1)


def _in_proj(x, g1_ref, win_ref, cos_ref, sin_ref):
    rows = x.shape[0]
    hn = _rms(x, g1_ref[...]).astype(BF16)
    proj = jnp.dot(hn, win_ref[...], preferred_element_type=F32)
    cos = cos_ref[...]
    sin = sin_ref[...]
    lane = lax.broadcasted_iota(jnp.int32, (rows, LANES), 1)
    first_half = _pmod(lane, HEAD_DIM) < (HEAD_DIM // 2)
    scale = HEAD_DIM ** -0.5
    qs = [_rope(proj[:, LANES * j:LANES * (j + 1)], cos, sin, first_half) * scale for j in range(D_ATTN // LANES)]
    k = _rope(proj[:, D_ATTN:D_ATTN + KV_W], cos, sin, first_half)
    v = proj[:, D_ATTN + KV_W:D_ATTN + 2 * KV_W]
    u = proj[:, D_ATTN + 2 * KV_W:]
    return qs, k, v, u


def _kv_variants(k, v):
    lane = lax.broadcasted_iota(jnp.int32, k.shape, 1)
    lo = lane < HEAD_DIM
    out = []
    for t in (k, v):
        tr = pltpu.roll(t, HEAD_DIM, 1)
        zero = jnp.zeros_like(t)
        out += [jnp.where(lo, t, zero), jnp.where(lo, zero, tr), jnp.where(lo, tr, zero), jnp.where(lo, zero, t)]
    return [o.astype(BF16) for o in out]


def _k_variants(k):
    lo = lax.broadcasted_iota(jnp.int32, k.shape, 1) < HEAD_DIM
    kr = pltpu.roll(k, HEAD_DIM, 1)
    zero = jnp.zeros_like(k)
    out = [jnp.where(lo, k, zero), jnp.where(lo, zero, kr), jnp.where(lo, kr, zero), jnp.where(lo, zero, k)]
    return [o.astype(BF16) for o in out]


def _softmax_pv_t(s, vt_win, sink_row):
    m = jnp.maximum(jnp.max(s, axis=0, keepdims=True), sink_row)
    e = jnp.exp(s - m)
    den = jnp.sum(e, axis=0, keepdims=True) + jnp.exp(sink_row - m)
    o = jnp.dot(vt_win, e.astype(BF16), preferred_element_type=F32)
    return o * (1.0 / den)


def _attend(a_bf, kvar, vvar, valid, sink_col):
    s = lax.dot_general(a_bf, kvar, (((1,), (1,)), ((), ())), preferred_element_type=F32)
    for cond in valid:
        s = jnp.where(cond, s, -jnp.inf)
    m = jnp.maximum(jnp.max(s, axis=-1, keepdims=True), sink_col)
    e = jnp.exp(s - m)
    den = jnp.sum(e, axis=-1, keepdims=True) + jnp.exp(sink_col - m)
    o = jnp.dot(e.astype(BF16), vvar, preferred_element_type=F32)
    return o * (1.0 / den)


def _prompt_in_kernel(sinks_ref, x_ref, g1_ref, win_ref, cos_ref, sin_ref, freq_ref,
                      a_ref, ug_ref, kp_ref, vp_ref, q_s, k_s, vb_s, vt_s, u_s, *, sb):
    tb = x_ref.shape[1]
    n_sub = tb // sb
    nks = sb // SSD_T
    blk = pl.program_id(1)
    nbuf = WINDOW + tb
    cps = sb // CHUNK
    n_chunks = tb // CHUNK

    @pl.when(blk == 0)
    def _():
        k_s[:, 0:WINDOW, :] = jnp.zeros((4, WINDOW, LANES), BF16)
        vb_s[0:WINDOW, :] = jnp.zeros((WINDOW, LANES), F32)

    @pl.when(blk > 0)
    def _():
        k_s[:, 0:WINDOW, :] = k_s[:, tb:nbuf, :]
        vb_s[0:WINDOW, :] = vb_s[tb:nbuf, :]

    lane = lax.broadcasted_iota(jnp.int32, (sb, LANES), 1)
    first_half = _pmod(lane, HEAD_DIM) < (HEAD_DIM // 2)
    scale = HEAD_DIM ** -0.5
    ang0 = (blk * tb).astype(F32) * freq_ref[...]
    cos0, sin0 = jnp.cos(ang0), jnp.sin(ang0)
    half_sign = jnp.where(first_half[0:1, :], -1.0, 1.0).astype(F32)

    def project(s):
        rows = slice(s * sb, (s + 1) * sb)
        brows = slice(WINDOW + s * sb, WINDOW + (s + 1) * sb)
        state = {}

        def piece(c0):
            if 'hn' not in state:
                state['hn'] = _rms(x_ref[0, rows, :], g1_ref[...]).astype(BF16)
                cb, sn = cos_ref[rows, :], sin_ref[rows, :]
                state['cos'] = cos0 * cb - sin0 * sn
                state['sin'] = (sin0 * cb + cos0 * sn) * half_sign
            return jnp.dot(state['hn'], win_ref[:, c0:c0 + 2 * LANES], preferred_element_type=F32)

        def q_piece(j):
            def run():
                p = piece(2 * LANES * j)
                for h in range(2):
                    q_s[rows, LANES * (2 * j + h):LANES * (2 * j + h + 1)] = _rope(
                        p[:, LANES * h:LANES * (h + 1)], state['cos'], state['sin'], first_half) * scale
            return run

        def kv_piece():
            p = piece(D_ATTN)
            k = _rope(p[:, :KV_W], state['cos'], state['sin'], first_half)
            v = p[:, KV_W:]
            for idx, arr in enumerate(_k_variants(k)):
                k_s[idx, brows, :] = arr
            vb_s[brows, :] = v
            if s == n_sub - 1:
                kp_ref[0] = k[sb - WINDOW:, :]
                vp_ref[0] = v[sb - WINDOW:, :]

        def u_piece(j):
            def run():
                p = piece(D_ATTN + 2 * KV_W + 2 * LANES * j)
                for h in range(2):
                    u_s[2 * j + h, rows, :] = p[:, LANES * h:LANES * (h + 1)]
            return run

        return [kv_piece, q_piece(0), q_piece(1), u_piece(0), u_piece(1)]

    vt_tail = {}

    def build_vt(s):
        lo = 0 if s == 0 else WINDOW + s * sb
        hi = WINDOW + (s + 1) * sb
        piece = vb_s[lo:hi, :].T
        src = piece if s == 0 else jnp.concatenate([vt_tail['v'], piece], axis=1)
        lo1 = lo if s == 0 else lo - LANES
        shifted = pltpu.roll(src, src.shape[1] - CHUNK, 1)
        vt_tail['v'] = piece[:, piece.shape[1] - LANES:]
        for sh, arr, b0 in ((0, piece, lo), (1, shifted, lo1)):
            w = arr.shape[1]
            zrows = jnp.zeros((HEAD_DIM, w), F32)
            for g in range(N_KV_HEADS):
                rows = arr[HEAD_DIM * g:HEAD_DIM * (g + 1), :]
                vt_s[sh, g, 0, :, b0:b0 + w] = jnp.concatenate([rows, zrows], axis=0).astype(BF16)
                vt_s[sh, g, 1, :, b0:b0 + w] = jnp.concatenate([zrows, rows], axis=0).astype(BF16)

    def regroup(s):
        def put(g, half, val):
            ug_ref[g, 0, s * nks:(s + 1) * nks, LANES * half:LANES * (half + 1)] = val
        _to_groups(u_s, put, nks, strided=True, row0=s * sb)

    nkeys = WINDOW + CHUNK
    key_row = lax.broadcasted_iota(jnp.int32, (nkeys, 1), 0)
    q_lane = lax.broadcasted_iota(jnp.int32, (1, LANES), 1)

    def scores(i):
        r0 = i * CHUNK
        valid = key_row >= (WINDOW - r0 - blk * tb) if i < N_WIN_CHUNKS else None
        out = []
        for g in range(N_KV_HEADS):
            qa = q_s[r0:r0 + CHUNK, 2 * LANES * g:2 * LANES * g + LANES]
            qb = q_s[r0:r0 + CHUNK, 2 * LANES * g + LANES:2 * LANES * (g + 1)]
            a_bf = jnp.concatenate([qa, qb], axis=0).astype(BF16)
            for par in range(2):
                s = lax.dot_general(k_s[2 * g + par, r0:r0 + nkeys, :], a_bf, (((1,), (1,)), ((), ())),
                                    preferred_element_type=F32)
                out.append(s if valid is None else jnp.where(valid, s, -jnp.inf))
        return out

    for item in project(0):
        item()
    build_vt(0)
    staging = []
    for s in range(1, n_sub):
        staging += [(s, item) for item in project(s)] + [(s, functools.partial(build_vt, s))]
    regroups = [functools.partial(regroup, s) for s in range(n_sub)]
    pending = scores(0)
    for i in range(n_chunks):
        if i + 1 < n_chunks and (i + 1) % cps == 0:
            while staging and staging[0][0] <= (i + 1) // cps:
                staging.pop(0)[1]()
        nxt = scores(i + 1) if i + 1 < n_chunks else None
        if staging:
            staging.pop(0)[1]()
        elif regroups:
            regroups.pop(0)()
        r0 = i * CHUNK
        sh = i % 2
        base = r0 - sh * CHUNK
        for g in range(N_KV_HEADS):
            acc = None
            for par in range(2):
                h0 = 4 * g + par
                sink_row = jnp.where(q_lane < CHUNK, sinks_ref[h0], sinks_ref[h0 + 2])
                o = _softmax_pv_t(pending[2 * g + par], vt_s[sh, g, par, :, base:base + nkeys], sink_row)
                acc = o if acc is None else acc + o
            at = acc.T
            a_ref[0, r0:r0 + CHUNK, 2 * LANES * g:2 * LANES * g + LANES] = at[:CHUNK]
            a_ref[0, r0:r0 + CHUNK, 2 * LANES * g + LANES:2 * LANES * (g + 1)] = at[CHUNK:]
        pending = nxt
    for item in regroups:
        item()


def _prompt_ssd_kernel(ug_ref, m_ref, wd_ref, r_ref, tab_ref, abar_ref, wup_f_ref, wdown_f_ref,
                       yg_ref, sfin_ref, wup_b_ref, wdown_b_ref, d_s, sp_s, carry_s):
    nb = ug_ref.shape[1]
    nc = ug_ref.shape[2]
    step = pl.program_id(0)
    wup_b_ref[...] = wup_f_ref[...].astype(BF16)
    wdown_b_ref[...] = wdown_f_ref[...].astype(BF16)

    @pl.when(step == 0)
    def _():
        carry_s[...] = jnp.zeros(carry_s.shape, F32)

    def state_in(g, c):
        ub = ug_ref[g].reshape(nb * nc, SSD_W)
        d_s[g] = jnp.dot(ub, wd_ref[g], preferred_element_type=F32)
        return c

    lax.fori_loop(0, N_GROUPS, state_in, 0, unroll=GROUP_UNROLL)

    hp = N_GROUPS // 2
    lo_half = lax.broadcasted_iota(jnp.int32, (1, 1, STATE_W), 2) < SSM_STATE

    def split(z):
        a, b = z[:hp], z[hp:]
        return (jnp.where(lo_half, a, pltpu.roll(b, SSM_STATE, 2)),
                jnp.where(lo_half, pltpu.roll(a, SSM_STATE, 2), b))

    def merge(re, im):
        return jnp.concatenate([jnp.where(lo_half, re, pltpu.roll(im, SSM_STATE, 2)),
                                jnp.where(lo_half, pltpu.roll(re, SSM_STATE, 2), im)], axis=0)

    def pair_tables(tre, tim):
        return jnp.where(lo_half, tre[:hp], tre[hp:]), jnp.where(lo_half, -tim[:hp], tim[hp:])

    pw = [pair_tables(tab_ref[2 * i], tab_ref[2 * i + 1]) for i in range(4)]
    ar, ai = pair_tables(abar_ref[0], abar_ref[1])
    kio = lax.broadcasted_iota(jnp.int32, (N_GROUPS, nc, STATE_W), 1)
    for b in range(nb):
        dg = d_s[:, b * nc:(b + 1) * nc, :]
        s_in = carry_s[b]
        xr_all, xi_all = split(jnp.where(kio == 0, s_in, pltpu.roll(dg, 1, 1)))
        cr = ci = jnp.zeros((hp, 1, STATE_W), F32)
        for j in range(nc // SUBLANES):
            xr = xr_all[:, SUBLANES * j:SUBLANES * (j + 1), :]
            xi = xi_all[:, SUBLANES * j:SUBLANES * (j + 1), :]
            for lvl in range(3):
                pr, pi = pw[lvl]
                sr, si = pltpu.roll(xr, 1 << lvl, 1), pltpu.roll(xi, 1 << lvl, 1)
                xr, xi = xr + pr * sr - pi * si, xi + pr * si + pi * sr
            qr, qi = pw[3]
            crb, cib = jnp.broadcast_to(cr, xr.shape), jnp.broadcast_to(ci, xi.shape)
            hr, hi = xr + qr * crb - qi * cib, xi + qr * cib + qi * crb
            sp_s[:, b * nc + SUBLANES * j:b * nc + SUBLANES * (j + 1), :] = merge(hr, hi)
            cr, ci = hr[:, SUBLANES - 1:SUBLANES, :], hi[:, SUBLANES - 1:SUBLANES, :]
        carry_s[b] = merge(ar * cr - ai * ci, ar * ci + ai * cr) + dg[:, nc - 1:nc, :]

    def readout(g, c):
        ub = ug_ref[g].reshape(nb * nc, SSD_W)
        y = jnp.dot(ub, m_ref[g], preferred_element_type=F32)
        y = y + lax.dot_general(sp_s[g].astype(BF16), r_ref[g], (((1,), (1,)), ((), ())),
                                preferred_element_type=F32)
        yg_ref[g] = y.reshape(nb, nc, SSD_W).astype(BF16)
        return c

    lax.fori_loop(0, N_GROUPS, readout, 0, unroll=GROUP_UNROLL)
    sfin_ref[...] = carry_s[...]


def _out_kernel(*refs, strided, cs, has_prev):
    if has_prev:
        x_ref, a_ref, yg_ref, prev_ref = refs[:4]
        refs = refs[4:]
    else:
        x_ref, a_ref, yg_ref = refs[:3]
        prev_ref = None
        refs = refs[3:]
    (wglu_ref, ga_ref, gs_ref, wout_ref, g2_ref, wup_ref, cw_ref, cb_ref, wdown_ref, gf_ref,
     y_ref, cst_ref, ys_s, ext_s, px_s, py_s) = refs
    tb = x_ref.shape[1]
    n_sub, _, sb, _ = ys_s.shape
    nks = sb // SSD_T
    planes = strided
    pr = sb // SUBLANES
    pad = SUBLANES if planes else ext_s.shape[0] - sb
    blk = pl.program_id(1)
    assert n_sub * sb == tb and (not planes or (cs == 1 and not has_prev))

    @pl.when(blk == 0)
    def _():
        if planes:
            ext_s[:, 0:pad, :] = jnp.zeros((2, pad, 2 * D_FF), F32)
        elif has_prev:
            ext_s[0:pad, :] = prev_ref[0]
        else:
            ext_s[0:pad, :] = jnp.zeros((pad, 2 * D_FF), F32)

    def mix(i):
        rows = slice(i * sb, (i + 1) * sb)
        _from_groups(lambda g, half: yg_ref[g, 0, i * nks:(i + 1) * nks, LANES * half:LANES * (half + 1)],
                     ys_s.at[i], nks, strided)
        yv = jnp.concatenate([ys_s[i, j] for j in range(D_SSM // LANES)], axis=-1)
        z = 0.5 * yv * (1.0 + jnp.tanh(math.sqrt(2.0 / math.pi) * (yv + 0.044715 * (yv * yv * yv))))
        gate = jnp.dot(z.astype(BF16), wglu_ref[...], preferred_element_type=F32)
        s_out = z * jax.nn.sigmoid(gate)
        na = _rms(a_ref[0, rows, :], ga_ref[...]).astype(BF16)
        ns = _rms(s_out, gs_ref[...]).astype(BF16)
        x1 = x_ref[0, rows, :] + jnp.dot(na, wout_ref[0:D_ATTN, :], preferred_element_type=F32)
        return x1 + jnp.dot(ns, wout_ref[D_ATTN:, :], preferred_element_type=F32)

    def tile_cols(j, half):
        return slice(half * D_FF + j * FF_TILE, half * D_FF + (j + 1) * FF_TILE)

    def to_planes(x):
        for c in range(D_MODEL // LANES):
            px_s[c] = x[:, LANES * c:LANES * (c + 1)]
        return jnp.concatenate(
            [jnp.concatenate([px_s[c, pl.ds(r, pr, stride=SUBLANES), :] for r in range(SUBLANES)], axis=0)
             for c in range(D_MODEL // LANES)], axis=1)

    def store_rows(i, y):
        rows = slice(i * sb, (i + 1) * sb)
        if not planes:
            y_ref[0, rows, :] = y
            return
        for c in range(D_MODEL // LANES):
            for r in range(SUBLANES):
                py_s[c, pl.ds(r, pr, stride=SUBLANES), :] = y[r * pr:(r + 1) * pr, LANES * c:LANES * (c + 1)]
            y_ref[0, rows, LANES * c:LANES * (c + 1)] = py_s[c]

    def time_shifts(up, cols):
        if not planes:
            ext_s[pad:pad + sb, cols] = up
            sh2 = ext_s[pad - 2 * cs:pad - 2 * cs + sb, cols]
            sh1 = ext_s[pad - cs:pad - cs + sb, cols]
            ext_s[0:pad, cols] = ext_s[sb:sb + pad, cols]
            return sh2, sh1
        for p in range(2):
            ext_s[p, pad:pad + pr, cols] = up[(6 + p) * pr:(7 + p) * pr, :]
        s6 = ext_s[0, pad - 1:pad - 1 + pr, cols]
        s7 = ext_s[1, pad - 1:pad - 1 + pr, cols]
        ext_s[:, 0:pad, cols] = ext_s[:, pr:pr + pad, cols]
        sh1 = jnp.concatenate([s7, up[:7 * pr, :]], axis=0)
        sh2 = jnp.concatenate([s6, s7, up[:6 * pr, :]], axis=0)
        return sh2, sh1

    def conv_ffn(i, x1):
        if planes:
            x1 = to_planes(x1)
        h2 = _rms(x1, g2_ref[...]).astype(BF16)

        def up_proj(j):
            return [jnp.dot(h2, wup_ref[:, tile_cols(j, half)], preferred_element_type=F32) for half in range(2)]

        def down_proj(act, j):
            return jnp.dot(act, wdown_ref[j * FF_TILE:(j + 1) * FF_TILE, :], preferred_element_type=F32)

        n_tiles = D_FF // FF_TILE
        acc = jnp.zeros((sb, D_MODEL), F32)
        ups = up_proj(0)
        act_prev = None
        for j in range(n_tiles):
            nxt = up_proj(j + 1) if j + 1 < n_tiles else None
            if act_prev is not None:
                acc = acc + down_proj(act_prev, j - 1)
            parts = []
            for half in range(2):
                cols = tile_cols(j, half)
                sh2, sh1 = time_shifts(ups[half], cols)
                parts.append(sh2 * cw_ref[0:1, cols] + sh1 * cw_ref[1:2, cols]
                             + ups[half] * cw_ref[2:3, cols] + cb_ref[:, cols])
            cg, cv = parts
            act_prev = (cg * jax.nn.sigmoid(cg) * cv).astype(BF16)
            ups = nxt
        acc = acc + down_proj(act_prev, n_tiles - 1)
        store_rows(i, _rms(x1 + acc, gf_ref[...]))

    mixed = mix(0)
    for i in range(n_sub):
        nxt_mixed = mix(i + 1) if i + 1 < n_sub else None
        conv_ffn(i, mixed)
        mixed = nxt_mixed
    if planes:
        cst_ref[0] = ext_s[:, 0:pad, :]
    else:
        cst_ref[0] = ext_s[0:pad, :]


def _decode_in_kernel(sinks_ref, x_ref, g1_ref, win_ref, cos_ref, sin_ref, kc_ref, vc_ref, h0_ref,
                      m_ref, wd_ref, r_ref, abar_ref,
                      a_ref, kn_ref, vn_ref, yg_ref, sn_ref, u_s, ug_s, *, n_streams, n_steps, n_cache):
    rows = n_streams * n_steps
    qs, k, v, u = _in_proj(x_ref[...], g1_ref, win_ref, cos_ref, sin_ref)
    kn_ref[...] = k
    vn_ref[...] = v
    for j in range(D_SSM // LANES):
        u_s[j] = u[:, LANES * j:LANES * (j + 1)]
    ncache_rows = n_streams * n_cache
    nkeys = ncache_rows + rows
    kall = jnp.concatenate([kc_ref[...], k], axis=0)
    vall = jnp.concatenate([vc_ref[...], v], axis=0)
    variants = _kv_variants(kall, vall)

    colv = lax.broadcasted_iota(jnp.int32, (1, nkeys), 1)
    is_new = colv >= ncache_rows
    cnew = colv - ncache_rows
    k_stream = jnp.where(is_new, _pmod(cnew, n_streams), _pdiv(colv, n_cache))
    k_pos = jnp.where(is_new, PAST_LEN + _pdiv(cnew, n_streams), PAST_LEN - n_cache + _pmod(colv, n_cache))
    k_chunk = _pdiv(k_pos, CHUNK)
    rowv = _pmod(lax.broadcasted_iota(jnp.int32, (2 * rows, 1), 0), rows)
    q_stream = _pmod(rowv, n_streams)
    q_chunk = _pdiv(PAST_LEN + _pdiv(rowv, n_streams), CHUNK)
    ok_col = jnp.where(k_pos >= 0, k_stream, -1)
    d_chunk = q_chunk - k_chunk
    valid = [q_stream == ok_col,
             lax.bitcast_convert_type(d_chunk, jnp.uint32) <= jnp.uint32(N_WIN_CHUNKS)]
    top = lax.broadcasted_iota(jnp.int32, (2 * rows, 1), 0) < rows

    for g in range(N_KV_HEADS):
        a_bf = jnp.concatenate([qs[2 * g], qs[2 * g + 1]], axis=0).astype(BF16)
        acc = jnp.zeros((2 * rows, LANES), F32)
        for par in range(2):
            h0 = 4 * g + par
            sink_col = jnp.where(top, sinks_ref[h0], sinks_ref[h0 + 2])
            acc = acc + _attend(a_bf, variants[2 * g + par], variants[4 + 2 * g + par], valid, sink_col)
        a_ref[:, 2 * LANES * g:2 * LANES * g + LANES] = acc[:rows]
        a_ref[:, 2 * LANES * g + LANES:2 * LANES * (g + 1)] = acc[rows:]

    def put(g, half, val):
        ug_s[g, :, LANES * half:LANES * (half + 1)] = val

    _to_groups(u_s, put, n_streams, strided=False)
    def group_body(g, c):
        ub = ug_s[g]
        h0g = h0_ref[g]
        d = jnp.dot(ub, wd_ref[g], preferred_element_type=F32)
        sn_ref[g] = _cmul(abar_ref[0, g], abar_ref[1, g], h0g) + d
        y = jnp.dot(ub, m_ref[g], preferred_element_type=F32)
        y = y + lax.dot_general(h0g.astype(BF16), r_ref[g], (((1,), (1,)), ((), ())),
                                preferred_element_type=F32)
        yg_ref[g] = y.astype(BF16)
        return c

    lax.fori_loop(0, N_GROUPS, group_body, 0, unroll=GROUP_UNROLL)


def _ssd_tables(a_re, a_im, log_dt, b_re, b_im, c_re, c_im, d_skip):
    dup = lambda x: jnp.concatenate([x, x], axis=-1)
    prm = jnp.stack([dup(a_re), dup(a_im), jnp.broadcast_to(log_dt[:, None], (N_GROUPS, STATE_W))], axis=1)
    bt = jnp.concatenate([b_re.transpose(0, 2, 1), b_im.transpose(0, 2, 1)], axis=-1)
    cc = jnp.concatenate([c_re, c_im], axis=-1)
    dtile = jnp.tile(d_skip, (1, SSD_T))[:, None, :]
    gb = TABLE_GROUPS
    blk3 = lambda r, c: pl.BlockSpec((gb, r, c), lambda i: (i, 0, 0))
    return pl.pallas_call(
        _ssd_tables_kernel,
        grid=(N_GROUPS // gb,),
        in_specs=[blk3(3, STATE_W), blk3(SSM_GROUP, STATE_W), blk3(SSM_GROUP, STATE_W), blk3(1, SSD_W)],
        out_specs=[blk3(SSD_W, SSD_W), blk3(SSD_W, STATE_W), blk3(SSD_W, STATE_W),
                   pl.BlockSpec((8, gb, SUBLANES, STATE_W), lambda i: (0, i, 0, 0)),
                   pl.BlockSpec((2, gb, 1, STATE_W), lambda i: (0, i, 0, 0))],
        out_shape=[jax.ShapeDtypeStruct((N_GROUPS, SSD_W, SSD_W), BF16),
                   jax.ShapeDtypeStruct((N_GROUPS, SSD_W, STATE_W), BF16),
                   jax.ShapeDtypeStruct((N_GROUPS, SSD_W, STATE_W), BF16),
                   jax.ShapeDtypeStruct((8, N_GROUPS, SUBLANES, STATE_W), F32),
                   jax.ShapeDtypeStruct((2, N_GROUPS, 1, STATE_W), F32)],
        compiler_params=pltpu.CompilerParams(dimension_semantics=("arbitrary",)),
        name="ssd_tables",
    )(prm, bt, cc, dtile)


def _ssd_tables_kernel(prm_ref, bt_ref, cc_ref, dt_ref, m_ref, wd_ref, rt_ref, tab_ref, abar_ref):
    t = SSD_T
    lane = lax.broadcasted_iota(jnp.int32, (1, STATE_W), 1)
    sgn = jnp.where(lane >= SSM_STATE, 1.0, -1.0).astype(F32)
    conj = -sgn
    step = lax.broadcasted_iota(jnp.int32, (t, 1), 0).astype(F32)
    sub_i = lax.broadcasted_iota(jnp.int32, (SUBLANES, 1), 0)
    sub = sub_i.astype(F32)
    row = lax.broadcasted_iota(jnp.int32, (SSD_W, SSD_W), 0)
    col = lax.broadcasted_iota(jnp.int32, (SSD_W, SSD_W), 1)
    causal = _pdiv(col, SSM_GROUP) >= _pdiv(row, SSM_GROUP)
    diag = row == col
    for gi in range(prm_ref.shape[0]):
        a_re, a_im = prm_ref[gi, 0:1, :], prm_ref[gi, 1:2, :]
        dt = jnp.exp(prm_ref[gi, 2:3, :])
        lr, li = dt * a_re, dt * a_im

        def power(n):
            mag = jnp.exp(n * lr)
            return mag * jnp.cos(n * li), mag * jnp.sin(n * li) * sgn

        def times_powers(z, zs, pw):
            pr, pis = pw
            return jnp.concatenate([z * pr[n:n + 1, :] + zs * pis[n:n + 1, :] for n in range(t)], axis=0)

        ar, ais = power(jnp.ones((1, 1), F32))
        den = a_re * a_re + a_im * a_im
        nr, ni = ar - 1.0, ais * sgn
        fr, fis = (nr * a_re + ni * a_im) / den, (ni * a_re - nr * a_im) / den * sgn
        bt = bt_ref[gi]
        bb = bt * fr + pltpu.roll(bt, SSM_STATE, 1) * fis
        bbs = pltpu.roll(bb, SSM_STATE, 1)
        cc = cc_ref[gi]
        ccs = pltpu.roll(cc, SSM_STATE, 1)
        x = times_powers(bb, bbs, power(-step)) * conj
        y = times_powers(cc, ccs, power(step))
        m = lax.dot_general(x, y, (((1,), (1,)), ((), ())), precision=lax.Precision.HIGHEST,
                            preferred_element_type=F32)
        m = jnp.where(causal, m, 0.0) + jnp.where(diag, dt_ref[gi], 0.0)
        m_ref[gi] = m.astype(BF16)
        wd_ref[gi] = times_powers(bb, bbs, power(float(t - 1) - step)).astype(BF16)
        rt_ref[gi] = (times_powers(cc, ccs, power(step + 1.0)) * conj).astype(BF16)
        for lvl in range(3):
            d = 1 << lvl
            pr, pis = power(jnp.full((SUBLANES, 1), float(t * d), F32))
            tab_ref[2 * lvl, gi] = jnp.where(sub_i >= d, pr, 0.0)
            tab_ref[2 * lvl + 1, gi] = jnp.where(sub_i >= d, pis, 0.0)
        pr, pis = power(float(t) * (sub + 1.0))
        tab_ref[6, gi] = pr
        tab_ref[7, gi] = pis
        pr, pis = power(jnp.full((1, 1), float(t), F32))
        abar_ref[0, gi] = pr
        abar_ref[1, gi] = pis


def _rope_freq():
    half = HEAD_DIM // 2
    return ROPE_THETA ** (-jnp.asarray(np.arange(LANES) % half, F32) / half)


def _rope_tables(pos):
    sign = jnp.asarray(np.where(np.arange(LANES) % HEAD_DIM < HEAD_DIM // 2, -1.0, 1.0), F32)
    ang = pos.astype(F32)[:, None] * _rope_freq()[None, :]
    return jnp.cos(ang), jnp.sin(ang) * sign[None, :]


def _rope_base(rows):
    freq = _rope_freq()[None, :]
    ang = jnp.arange(rows, dtype=F32)[:, None] * freq
    return jnp.cos(ang), jnp.sin(ang), freq


def _const(shape):
    nd = len(shape)
    return pl.BlockSpec(shape, lambda *_: (0,) * nd, pipeline_mode=pl.Buffered(1))


def _whole(shape):
    nd = len(shape)
    return pl.BlockSpec(shape, lambda *_: (0,) * nd)


def _out_call(x, a, yg, prev, w, *, tb, sb, strided, cs, pad):
    nb, length, _ = x.shape
    nk = tb // SSD_T
    grid = (nb, length // tb)
    in_specs = [
        pl.BlockSpec((1, tb, D_MODEL), lambda b, i: (b, i, 0)),
        pl.BlockSpec((1, tb, D_ATTN), lambda b, i: (b, i, 0)),
        pl.BlockSpec((N_GROUPS, 1, nk, SSD_W), lambda b, i: (0, b, i, 0)),
    ]
    args = [x, a, yg]
    if prev is not None:
        in_specs.append(pl.BlockSpec((1, pad, 2 * D_FF), lambda b, i: (b, 0, 0)))
        args.append(prev)
    weights = [w['w_glu'], w['onorm_a'], w['onorm_s'], w['w_out'], w['norm2'], w['w_up'], w['conv_w'],
               w['conv_b'], w['w_down'], w['final_g']]
    in_specs += [_const(t.shape) for t in weights]
    kern = functools.partial(_out_kernel, strided=strided, cs=cs, has_prev=prev is not None)
    if strided:
        ext_shape = (2, pad + sb // SUBLANES, 2 * D_FF)
        cst_block, cst_map = (1, 2, pad, 2 * D_FF), (lambda b, i: (b, 0, 0, 0))
    else:
        ext_shape = (pad + sb, 2 * D_FF)
        cst_block, cst_map = (1, pad, 2 * D_FF), (lambda b, i: (b, 0, 0))
    return pl.pallas_call(
        kern,
        grid=grid,
        in_specs=in_specs,
        out_specs=[pl.BlockSpec((1, tb, D_MODEL), lambda b, i: (b, i, 0)),
                   pl.BlockSpec(cst_block, cst_map)],
        out_shape=[jax.ShapeDtypeStruct((nb, length, D_MODEL), F32),
                   jax.ShapeDtypeStruct((nb,) + cst_block[1:], F32)],
        scratch_shapes=[pltpu.VMEM((tb // sb, D_SSM // LANES, sb, LANES), F32),
                        pltpu.VMEM(ext_shape, F32),
                        pltpu.VMEM((D_MODEL // LANES, sb, LANES), F32),
                        pltpu.VMEM((D_MODEL // LANES, sb, LANES), F32)],
        compiler_params=pltpu.CompilerParams(dimension_semantics=("arbitrary", "arbitrary"),
                                             vmem_limit_bytes=VMEM_LIMIT),
        name="layer_out",
    )(*args, *weights)


def kernel(x_prompt, x_sample, cache_k, cache_v, state_ssm_re, state_ssm_im, state_conv, norm1_g, w_in, attn_sinks, ssm_A_re, ssm_A_im, ssm_log_dt, ssm_B_re, ssm_B_im, ssm_C_re, ssm_C_im, ssm_D, w_glu, onorm_attn_g, onorm_ssm_g, w_out, norm2_g, w_up, conv_w, conv_b, w_down, final_g):
    assert norm1_g.shape[0] == 1, "one layer"
    nb, length, _ = x_prompt.shape
    ns, nt, _ = x_sample.shape
    n_cache = cache_k.shape[2]
    assert nt == SSD_T and length % TB_IN == 0 and length % TB_OUT == 0 and (length // SSD_T) % NC_SSD == 0

    m_mat, wd_mat, r_mat, tab, abar = _ssd_tables(ssm_A_re[0], ssm_A_im[0], ssm_log_dt[0], ssm_B_re[0],
                                                  ssm_B_im[0], ssm_C_re[0], ssm_C_im[0], ssm_D[0])
    sinks = attn_sinks[0]
    g1 = norm1_g
    win = w_in[0].astype(BF16)
    w = dict(w_glu=w_glu[0].astype(BF16), onorm_a=onorm_attn_g, onorm_s=onorm_ssm_g, w_out=w_out[0].astype(BF16),
             norm2=norm2_g, conv_w=conv_w[0], conv_b=conv_b, final_g=final_g[None, :])
    smem = pl.BlockSpec(memory_space=pltpu.SMEM)

    cos_p, sin_p, freq = _rope_base(TB_IN)
    nk = TB_IN // SSD_T
    a_p, ug_p, k_p, v_p = pl.pallas_call(
        functools.partial(_prompt_in_kernel, sb=SUB_IN),
        grid=(nb, length // TB_IN),
        in_specs=[smem,
                  pl.BlockSpec((1, TB_IN, D_MODEL), lambda b, i: (b, i, 0)),
                  _const((1, D_MODEL)), _const((D_MODEL, D_IN)),
                  _const((TB_IN, LANES)), _const((TB_IN, LANES)), _const((1, LANES))],
        out_specs=[pl.BlockSpec((1, TB_IN, D_ATTN), lambda b, i: (b, i, 0)),
                   pl.BlockSpec((N_GROUPS, 1, nk, SSD_W), lambda b, i: (0, b, i, 0)),
                   pl.BlockSpec((1, WINDOW, KV_W), lambda b, i: (b, 0, 0)),
                   pl.BlockSpec((1, WINDOW, KV_W), lambda b, i: (b, 0, 0))],
        out_shape=[jax.ShapeDtypeStruct((nb, length, D_ATTN), F32),
                   jax.ShapeDtypeStruct((N_GROUPS, nb, length // SSD_T, SSD_W), BF16),
                   jax.ShapeDtypeStruct((nb, WINDOW, KV_W), F32),
                   jax.ShapeDtypeStruct((nb, WINDOW, KV_W), F32)],
        scratch_shapes=[pltpu.VMEM((TB_IN, D_ATTN), F32),
                        pltpu.VMEM((2 * N_KV_HEADS, WINDOW + TB_IN, LANES), BF16),
                        pltpu.VMEM((WINDOW + TB_IN, LANES), F32),
                        pltpu.VMEM((2, N_KV_HEADS, 2, LANES, WINDOW + TB_IN), BF16),
                        pltpu.VMEM((D_SSM // LANES, TB_IN, LANES), F32)],
        compiler_params=pltpu.CompilerParams(dimension_semantics=("arbitrary", "arbitrary"),
                                             vmem_limit_bytes=VMEM_LIMIT),
        name="prompt_in",
    )(sinks, x_prompt, g1, win, cos_p, sin_p, freq)

    n_ssd = length // SSD_T // NC_SSD
    up_rows, down_rows = D_MODEL // n_ssd, D_FF // n_ssd
    yg_p, sfin_p, w['w_up'], w['w_down'] = pl.pallas_call(
        _prompt_ssd_kernel,
        grid=(n_ssd,),
        in_specs=[pl.BlockSpec((N_GROUPS, nb, NC_SSD, SSD_W), lambda i: (0, 0, i, 0)),
                  _const(m_mat.shape), _const(wd_mat.shape), _const(r_mat.shape), _const(tab.shape),
                  _const(abar.shape),
                  pl.BlockSpec((up_rows, 2 * D_FF), lambda i: (i, 0)),
                  pl.BlockSpec((down_rows, D_MODEL), lambda i: (i, 0))],
        out_specs=[pl.BlockSpec((N_GROUPS, nb, NC_SSD, SSD_W), lambda i: (0, 0, i, 0)),
                   pl.BlockSpec((nb, N_GROUPS, 1, STATE_W), lambda i: (0, 0, 0, 0)),
                   pl.BlockSpec((up_rows, 2 * D_FF), lambda i: (i, 0)),
                   pl.BlockSpec((down_rows, D_MODEL), lambda i: (i, 0))],
        out_shape=[jax.ShapeDtypeStruct((N_GROUPS, nb, length // SSD_T, SSD_W), BF16),
                   jax.ShapeDtypeStruct((nb, N_GROUPS, 1, STATE_W), F32),
                   jax.ShapeDtypeStruct((D_MODEL, 2 * D_FF), BF16),
                   jax.ShapeDtypeStruct((D_FF, D_MODEL), BF16)],
        scratch_shapes=[pltpu.VMEM((N_GROUPS, nb * NC_SSD, STATE_W), F32),
                        pltpu.VMEM((N_GROUPS, nb * NC_SSD, STATE_W), F32),
                        pltpu.VMEM((nb, N_GROUPS, 1, STATE_W), F32)],
        compiler_params=pltpu.CompilerParams(dimension_semantics=("arbitrary",), vmem_limit_bytes=VMEM_LIMIT),
        name="prompt_ssd",
    )(ug_p, m_mat, wd_mat, r_mat, tab, abar, w_up[0], w_down[0])

    y_p, cst_p = _out_call(x_prompt, a_p, yg_p, None, w, tb=TB_OUT, sb=SUB_OUT, strided=True, cs=1, pad=SUBLANES)

    rows = ns * nt
    xs = x_sample.transpose(1, 0, 2).reshape(rows, D_MODEL)
    cos_s, sin_s = _rope_tables(PAST_LEN + jnp.arange(rows) // ns)
    kc = cache_k[0].reshape(ns * n_cache, KV_W)
    vc = cache_v[0].reshape(ns * n_cache, KV_W)
    h0 = jnp.concatenate([state_ssm_re[0], state_ssm_im[0]], axis=-1).transpose(1, 0, 2)
    dec = functools.partial(_decode_in_kernel, n_streams=ns, n_steps=nt, n_cache=n_cache)
    dec_in = [xs, g1, win, cos_s, sin_s, kc, vc, h0, m_mat, wd_mat, r_mat, abar]
    a_s, kn, vn, yg_s, sn = pl.pallas_call(
        dec,
        grid=(1,),
        in_specs=[smem] + [_const(t.shape) for t in dec_in],
        out_specs=[_whole((rows, D_ATTN)), _whole((rows, KV_W)), _whole((rows, KV_W)),
                   _whole((N_GROUPS, ns, SSD_W)), _whole((N_GROUPS, ns, STATE_W))],
        out_shape=[jax.ShapeDtypeStruct((rows, D_ATTN), F32), jax.ShapeDtypeStruct((rows, KV_W), F32),
                   jax.ShapeDtypeStruct((rows, KV_W), F32), jax.ShapeDtypeStruct((N_GROUPS, ns, SSD_W), BF16),
                   jax.ShapeDtypeStruct((N_GROUPS, ns, STATE_W), F32)],
        scratch_shapes=[pltpu.VMEM((D_SSM // LANES, rows, LANES), F32), pltpu.VMEM((N_GROUPS, ns, SSD_W), BF16)],
        compiler_params=pltpu.CompilerParams(dimension_semantics=("arbitrary",), vmem_limit_bytes=VMEM_LIMIT),
        name="decode_in",
    )(sinks, *dec_in)

    cs = ns
    prev = state_conv[0].transpose(1, 0, 2).reshape(1, (CONV_W - 1) * cs, 2 * D_FF)
    y_s, cst_s = _out_call(xs[None], a_s[None], yg_s[:, None], prev, w, tb=rows, sb=rows, strided=False, cs=cs,
                           pad=(CONV_W - 1) * cs)

    kv5 = lambda t, n: t.reshape(1, n, -1, N_KV_HEADS, HEAD_DIM)
    y_sample = y_s.reshape(nt, ns, D_MODEL).transpose(1, 0, 2)
    kn_b = kn.reshape(nt, ns, KV_W).transpose(1, 0, 2)
    vn_b = vn.reshape(nt, ns, KV_W).transpose(1, 0, 2)
    k_new = jnp.concatenate([cache_k[0].reshape(ns, n_cache, KV_W), kn_b], axis=1)[:, -n_cache:]
    v_new = jnp.concatenate([cache_v[0].reshape(ns, n_cache, KV_W), vn_b], axis=1)[:, -n_cache:]
    conv_p = cst_p[:, :, SUBLANES - 1, :][None]
    conv_s = cst_s.reshape(CONV_W - 1, ns, 2 * D_FF).transpose(1, 0, 2)[None]
    sn_b = sn.transpose(1, 0, 2)
    return (y_p, y_sample,
            kv5(k_p, nb), kv5(v_p, nb),
            sfin_p[None, :, :, 0, :SSM_STATE], sfin_p[None, :, :, 0, SSM_STATE:], conv_p,
            kv5(k_new, ns), kv5(v_new, ns),
            sn_b[None, :, :, :SSM_STATE], sn_b[None, :, :, SSM_STATE:], conv_s)
```

```python
import functools
import math

import numpy as np
import jax
import jax.numpy as jnp
from jax import lax
from jax.experimental import pallas as pl
from jax.experimental.pallas import tpu as pltpu

F32 = jnp.float32
BF16 = jnp.bfloat16

D_MODEL = 1024
CHUNK = 64
D_ATTN = 512
D_SSM = 512
HEAD_DIM = 64
N_HEADS = 8
N_KV_HEADS = 2
KV_W = N_KV_HEADS * HEAD_DIM
WINDOW = 128
N_WIN_CHUNKS = WINDOW // CHUNK
ROPE_THETA = 10000.0
SSM_GROUP = 16
N_GROUPS = D_SSM // SSM_GROUP
SSM_STATE = 64
D_FF = 2816
CONV_W = 3
EPS = 1e-6
D_IN = D_ATTN + 2 * KV_W + D_SSM
PAST_LEN = 2048

SSD_T = 16
LANES = 128
SUBLANES = 8
GROUPS_PER_VREG = LANES // SSM_GROUP
SSD_W = SSD_T * SSM_GROUP
STATE_W = 2 * SSM_STATE

TB_IN = 2048
SUB_IN = 512
TB_OUT = 512
SUB_OUT = 256
NC_SSD = 64
FF_TILE = 256
GROUP_UNROLL = 8
TABLE_GROUPS = 8
VMEM_LIMIT = 56 * 1024 * 1024


def _log2(n):
    assert n > 0 and n & (n - 1) == 0, n
    return n.bit_length() - 1


def _pdiv(x, n):
    return lax.shift_right_arithmetic(x, jnp.int32(_log2(n)))


def _pmod(x, n):
    return x & (n - 1)


def _rms(x, g):
    return x * lax.rsqrt(jnp.mean(x * x, axis=-1, keepdims=True) + EPS) * g


def _rope(xb, cos, sin_signed, first_half):
    partner = jnp.where(first_half, pltpu.roll(xb, LANES - HEAD_DIM // 2, 1), pltpu.roll(xb, HEAD_DIM // 2, 1))
    return xb * cos + partner * sin_signed


def _group_transpose8(xs, lane_group):
    xs = list(xs)
    for d in (4, 2, 1):
        keep = (lane_group & d) == 0
        nxt = list(xs)
        for i in range(GROUPS_PER_VREG):
            if i & d:
                continue
            a, b = xs[i], xs[i | d]
            nxt[i] = jnp.where(keep, a, pltpu.roll(b, SSM_GROUP * d, 1))
            nxt[i | d] = jnp.where(keep, pltpu.roll(a, LANES - SSM_GROUP * d, 1), b)
        xs = nxt
    return xs


def _row_sel(s, nk, strided, row0=0):
    if strided:
        return pl.ds(row0 + s, nk, stride=SSD_T)
    return pl.ds(row0 + s * nk, nk)


def _to_groups(u_ref, put, nk, strided, row0=0):
    lane_group = _pdiv(lax.broadcasted_iota(jnp.int32, (nk, LANES), 1), SSM_GROUP)
    for j in range(D_SSM // LANES):
        for half in range(SSD_T // GROUPS_PER_VREG):
            xs = [u_ref[j, _row_sel(GROUPS_PER_VREG * half + sl, nk, strided, row0), :]
                  for sl in range(GROUPS_PER_VREG)]
            ws = _group_transpose8(xs, lane_group)
            for gq in range(GROUPS_PER_VREG):
                put(GROUPS_PER_VREG * j + gq, half, ws[gq].astype(BF16))


def _from_groups(get, y_ref, nk, strided):
    lane_group = _pdiv(lax.broadcasted_iota(jnp.int32, (nk, LANES), 1), SSM_GROUP)
    for j in range(D_SSM // LANES):
        for half in range(SSD_T // GROUPS_PER_VREG):
            ws = [get(GROUPS_PER_VREG * j + gq, half).astype(F32) for gq in range(GROUPS_PER_VREG)]
            xs = _group_transpose8(ws, lane_group)
            for sl in range(GROUPS_PER_VREG):
                y_ref[j, _row_sel(GROUPS_PER_VREG * half + sl, nk, strided), :] = xs[sl]


def _cmul(re_full, im_signed, z):
    return re_full * z + im_signed * pltpu.roll(z, SSM_STATE, z.ndim - 1)


def _in_proj(x, g1_ref, win_ref, cos_ref, sin_ref):
    rows = x.shape[0]
    hn = _rms(x, g1_ref[...]).astype(BF16)
    proj = jnp.dot(hn, win_ref[...], preferred_element_type=F32)
    cos = cos_ref[...]
    sin = sin_ref[...]
    lane = lax.broadcasted_iota(jnp.int32, (rows, LANES), 1)
    first_half = _pmod(lane, HEAD_DIM) < (HEAD_DIM // 2)
    scale = HEAD_DIM ** -0.5
    qs = [_rope(proj[:, LANES * j:LANES * (j + 1)], cos, sin, first_half) * scale for j in range(D_ATTN // LANES)]
    k = _rope(proj[:, D_ATTN:D_ATTN + KV_W], cos, sin, first_half)
    v = proj[:, D_ATTN + KV_W:D_ATTN + 2 * KV_W]
    u = proj[:, D_ATTN + 2 * KV_W:]
    return qs, k, v, u


def _kv_variants(k, v):
    lane = lax.broadcasted_iota(jnp.int32, k.shape, 1)
    lo = lane < HEAD_DIM
    out = []
    for t in (k, v):
        tr = pltpu.roll(t, HEAD_DIM, 1)
        zero = jnp.zeros_like(t)
        out += [jnp.where(lo, t, zero), jnp.where(lo, zero, tr), jnp.where(lo, tr, zero), jnp.where(lo, zero, t)]
    return [o.astype(BF16) for o in out]


def _k_variants(k):
    lo = lax.broadcasted_iota(jnp.int32, k.shape, 1) < HEAD_DIM
    kr = pltpu.roll(k, HEAD_DIM, 1)
    zero = jnp.zeros_like(k)
    out = [jnp.where(lo, k, zero), jnp.where(lo, zero, kr), jnp.where(lo, kr, zero), jnp.where(lo, zero, k)]
    return [o.astype(BF16) for o in out]


def _softmax_pv_t(s, vt_win, sink_row):
    m = jnp.maximum(jnp.max(s, axis=0, keepdims=True), sink_row)
    e = jnp.exp(s - m)
    den = jnp.sum(e, axis=0, keepdims=True) + jnp.exp(sink_row - m)
    o = jnp.dot(vt_win, e.astype(BF16), preferred_element_type=F32)
    return o * (1.0 / den)


def _attend(a_bf, kvar, vvar, valid, sink_col):
    s = lax.dot_general(a_bf, kvar, (((1,), (1,)), ((), ())), preferred_element_type=F32)
    for cond in valid:
        s = jnp.where(cond, s, -jnp.inf)
    m = jnp.maximum(jnp.max(s, axis=-1, keepdims=True), sink_col)
    e = jnp.exp(s - m)
    den = jnp.sum(e, axis=-1, keepdims=True) + jnp.exp(sink_col - m)
    o = jnp.dot(e.astype(BF16), vvar, preferred_element_type=F32)
    return o * (1.0 / den)


def _prompt_in_kernel(sinks_ref, x_ref, g1_ref, win_ref, cos_ref, sin_ref, freq_ref,
                      a_ref, ug_ref, kp_ref, vp_ref, q_s, k_s, vb_s, vt_s, u_s, *, sb):
    tb = x_ref.shape[1]
    n_sub = tb // sb
    nks = sb // SSD_T
    blk = pl.program_id(1)
    nbuf = WINDOW + tb
    cps = sb // CHUNK
    n_chunks = tb // CHUNK

    @pl.when(blk == 0)
    def _():
        k_s[:, 0:WINDOW, :] = jnp.zeros((4, WINDOW, LANES), BF16)
        vb_s[0:WINDOW, :] = jnp.zeros((WINDOW, LANES), F32)

    @pl.when(blk > 0)
    def _():
        k_s[:, 0:WINDOW, :] = k_s[:, tb:nbuf, :]
        vb_s[0:WINDOW, :] = vb_s[tb:nbuf, :]

    lane = lax.broadcasted_iota(jnp.int32, (sb, LANES), 1)
    first_half = _pmod(lane, HEAD_DIM) < (HEAD_DIM // 2)
    scale = HEAD_DIM ** -0.5
    ang0 = (blk * tb).astype(F32) * freq_ref[...]
    cos0, sin0 = jnp.cos(ang0), jnp.sin(ang0)
    half_sign = jnp.where(first_half[0:1, :], -1.0, 1.0).astype(F32)

    def project(s):
        rows = slice(s * sb, (s + 1) * sb)
        brows = slice(WINDOW + s * sb, WINDOW + (s + 1) * sb)
        state = {}

        def piece(c0):
            if 'hn' not in state:
                state['hn'] = _rms(x_ref[0, rows, :], g1_ref[...]).astype(BF16)
                cb, sn = cos_ref[rows, :], sin_ref[rows, :]
                state['cos'] = cos0 * cb - sin0 * sn
                state['sin'] = (sin0 * cb + cos0 * sn) * half_sign
            return jnp.dot(state['hn'], win_ref[:, c0:c0 + 2 * LANES], preferred_element_type=F32)

        def q_piece(j):
            def run():
                p = piece(2 * LANES * j)
                for h in range(2):
                    q_s[rows, LANES * (2 * j + h):LANES * (2 * j + h + 1)] = _rope(
                        p[:, LANES * h:LANES * (h + 1)], state['cos'], state['sin'], first_half) * scale
            return run

        def kv_piece():
            p = piece(D_ATTN)
            k = _rope(p[:, :KV_W], state['cos'], state['sin'], first_half)
            v = p[:, KV_W:]
            for idx, arr in enumerate(_k_variants(k)):
                k_s[idx, brows, :] = arr
            vb_s[brows, :] = v
            if s == n_sub - 1:
                kp_ref[0] = k[sb - WINDOW:, :]
                vp_ref[0] = v[sb - WINDOW:, :]

        def u_piece(j):
            def run():
                p = piece(D_ATTN + 2 * KV_W + 2 * LANES * j)
                for h in range(2):
                    u_s[2 * j + h, rows, :] = p[:, LANES * h:LANES * (h + 1)]
            return run

        return [kv_piece, q_piece(0), q_piece(1), u_piece(0), u_piece(1)]

    vt_tail = {}

    def build_vt(s):
        lo = 0 if s == 0 else WINDOW + s * sb
        hi = WINDOW + (s + 1) * sb
        piece = vb_s[lo:hi, :].T
        src = piece if s == 0 else jnp.concatenate([vt_tail['v'], piece], axis=1)
        lo1 = lo if s == 0 else lo - LANES
        shifted = pltpu.roll(src, src.shape[1] - CHUNK, 1)
        vt_tail['v'] = piece[:, piece.shape[1] - LANES:]
        for sh, arr, b0 in ((0, piece, lo), (1, shifted, lo1)):
            w = arr.shape[1]
            zrows = jnp.zeros((HEAD_DIM, w), F32)
            for g in range(N_KV_HEADS):
                rows = arr[HEAD_DIM * g:HEAD_DIM * (g + 1), :]
                vt_s[sh, g, 0, :, b0:b0 + w] = jnp.concatenate([rows, zrows], axis=0).astype(BF16)
                vt_s[sh, g, 1, :, b0:b0 + w] = jnp.concatenate([zrows, rows], axis=0).astype(BF16)

    def regroup(s):
        def put(g, half, val):
            ug_ref[g, 0, s * nks:(s + 1) * nks, LANES * half:LANES * (half + 1)] = val
        _to_groups(u_s, put, nks, strided=True, row0=s * sb)

    nkeys = WINDOW + CHUNK
    key_row = lax.broadcasted_iota(jnp.int32, (nkeys, 1), 0)
    q_lane = lax.broadcasted_iota(jnp.int32, (1, LANES), 1)

    def scores(i):
        r0 = i * CHUNK
        valid = key_row >= (WINDOW - r0 - blk * tb) if i < N_WIN_CHUNKS else None
        out = []
        for g in range(N_KV_HEADS):
            qa = q_s[r0:r0 + CHUNK, 2 * LANES * g:2 * LANES * g + LANES]
            qb = q_s[r0:r0 + CHUNK, 2 * LANES * g + LANES:2 * LANES * (g + 1)]
            a_bf = jnp.concatenate([qa, qb], axis=0).astype(BF16)
            for par in range(2):
                s = lax.dot_general(k_s[2 * g + par, r0:r0 + nkeys, :], a_bf, (((1,), (1,)), ((), ())),
                                    preferred_element_type=F32)
                out.append(s if valid is None else jnp.where(valid, s, -jnp.inf))
        return out

    for item in project(0):
        item()
    build_vt(0)
    staging = []
    for s in range(1, n_sub):
        staging += [(s, item) for item in project(s)] + [(s, functools.partial(build_vt, s))]
    regroups = [functools.partial(regroup, s) for s in range(n_sub)]
    pending = scores(0)
    for i in range(n_chunks):
        if i + 1 < n_chunks and (i + 1) % cps == 0:
            while staging and staging[0][0] <= (i + 1) // cps:
                staging.pop(0)[1]()
        nxt = scores(i + 1) if i + 1 < n_chunks else None
        if staging:
            staging.pop(0)[1]()
        elif regroups:
            regroups.pop(0)()
        r0 = i * CHUNK
        sh = i % 2
        base = r0 - sh * CHUNK
        for g in range(N_KV_HEADS):
            acc = None
            for par in range(2):
                h0 = 4 * g + par
                sink_row = jnp.where(q_lane < CHUNK, sinks_ref[h0], sinks_ref[h0 + 2])
                o = _softmax_pv_t(pending[2 * g + par], vt_s[sh, g, par, :, base:base + nkeys], sink_row)
                acc = o if acc is None else acc + o
            at = acc.T
            a_ref[0, r0:r0 + CHUNK, 2 * LANES * g:2 * LANES * g + LANES] = at[:CHUNK]
            a_ref[0, r0:r0 + CHUNK, 2 * LANES * g + LANES:2 * LANES * (g + 1)] = at[CHUNK:]
        pending = nxt
    for item in regroups:
        item()


def _prompt_ssd_kernel(ug_ref, m_ref, wd_ref, r_ref, tab_ref, abar_ref, wup_f_ref, wdown_f_ref,
                       yg_ref, sfin_ref, wup_b_ref, wdown_b_ref, d_s, sp_s, carry_s):
    nb = ug_ref.shape[1]
    nc = ug_ref.shape[2]
    step = pl.program_id(0)
    wup_b_ref[...] = wup_f_ref[...].astype(BF16)
    wdown_b_ref[...] = wdown_f_ref[...].astype(BF16)

    @pl.when(step == 0)
    def _():
        carry_s[...] = jnp.zeros(carry_s.shape, F32)

    def state_in(g, c):
        ub = ug_ref[g].reshape(nb * nc, SSD_W)
        d_s[g] = jnp.dot(ub, wd_ref[g], preferred_element_type=F32)
        return c

    lax.fori_loop(0, N_GROUPS, state_in, 0, unroll=GROUP_UNROLL)

    hp = N_GROUPS // 2
    lo_half = lax.broadcasted_iota(jnp.int32, (1, 1, STATE_W), 2) < SSM_STATE

    def split(z):
        a, b = z[:hp], z[hp:]
        return (jnp.where(lo_half, a, pltpu.roll(b, SSM_STATE, 2)),
                jnp.where(lo_half, pltpu.roll(a, SSM_STATE, 2), b))

    def merge(re, im):
        return jnp.concatenate([jnp.where(lo_half, re, pltpu.roll(im, SSM_STATE, 2)),
                                jnp.where(lo_half, pltpu.roll(re, SSM_STATE, 2), im)], axis=0)

    def pair_tables(tre, tim):
        return jnp.where(lo_half, tre[:hp], tre[hp:]), jnp.where(lo_half, -tim[:hp], tim[hp:])

    pw = [pair_tables(tab_ref[2 * i], tab_ref[2 * i + 1]) for i in range(4)]
    ar, ai = pair_tables(abar_ref[0], abar_ref[1])
    kio = lax.broadcasted_iota(jnp.int32, (N_GROUPS, nc, STATE_W), 1)
    for b in range(nb):
        dg = d_s[:, b * nc:(b + 1) * nc, :]
        s_in = carry_s[b]
        xr_all, xi_all = split(jnp.where(kio == 0, s_in, pltpu.roll(dg, 1, 1)))
        cr = ci = jnp.zeros((hp, 1, STATE_W), F32)
        for j in range(nc // SUBLANES):
            xr = xr_all[:, SUBLANES * j:SUBLANES * (j + 1), :]
            xi = xi_all[:, SUBLANES * j:SUBLANES * (j + 1), :]
            for lvl in range(3):
                pr, pi = pw[lvl]
                sr, si = pltpu.roll(xr, 1 << lvl, 1), pltpu.roll(xi, 1 << lvl, 1)
                xr, xi = xr + pr * sr - pi * si, xi + pr * si + pi * sr
            qr, qi = pw[3]
            crb, cib = jnp.broadcast_to(cr, xr.shape), jnp.broadcast_to(ci, xi.shape)
            hr, hi = xr + qr * crb - qi * cib, xi + qr * cib + qi * crb
            sp_s[:, b * nc + SUBLANES * j:b * nc + SUBLANES * (j + 1), :] = merge(hr, hi)
            cr, ci = hr[:, SUBLANES - 1:SUBLANES, :], hi[:, SUBLANES - 1:SUBLANES, :]
        carry_s[b] = merge(ar * cr - ai * ci, ar * ci + ai * cr) + dg[:, nc - 1:nc, :]

    def readout(g, c):
        ub = ug_ref[g].reshape(nb * nc, SSD_W)
        y = jnp.dot(ub, m_ref[g], preferred_element_type=F32)
        y = y + lax.dot_general(sp_s[g].astype(BF16), r_ref[g], (((1,), (1,)), ((), ())),
                                preferred_element_type=F32)
        yg_ref[g] = y.reshape(nb, nc, SSD_W).astype(BF16)
        return c

    lax.fori_loop(0, N_GROUPS, readout, 0, unroll=GROUP_UNROLL)
    sfin_ref[...] = carry_s[...]


def _out_kernel(*refs, strided, cs, has_prev):
    if has_prev:
        x_ref, a_ref, yg_ref, prev_ref = refs[:4]
        refs = refs[4:]
    else:
        x_ref, a_ref, yg_ref = refs[:3]
        prev_ref = None
        refs = refs[3:]
    (wglu_ref, ga_ref, gs_ref, wout_ref, g2_ref, wup_ref, cw_ref, cb_ref, wdown_ref, gf_ref,
     y_ref, cst_ref, ys_s, ext_s, px_s, py_s) = refs
    tb = x_ref.shape[1]
    n_sub, _, sb, _ = ys_s.shape
    nks = sb // SSD_T
    planes = strided
    pr = sb // SUBLANES
    pad = SUBLANES if planes else ext_s.shape[0] - sb
    blk = pl.program_id(1)
    assert n_sub * sb == tb and (not planes or (cs == 1 and not has_prev))

    @pl.when(blk == 0)
    def _():
        if planes:
            ext_s[:, 0:pad, :] = jnp.zeros((2, pad, 2 * D_FF), F32)
        elif has_prev:
            ext_s[0:pad, :] = prev_ref[0]
        else:
            ext_s[0:pad, :] = jnp.zeros((pad, 2 * D_FF), F32)

    def mix(i):
        rows = slice(i * sb, (i + 1) * sb)
        _from_groups(lambda g, half: yg_ref[g, 0, i * nks:(i + 1) * nks, LANES * half:LANES * (half + 1)],
                     ys_s.at[i], nks, strided)
        yv = jnp.concatenate([ys_s[i, j] for j in range(D_SSM // LANES)], axis=-1)
        z = 0.5 * yv * (1.0 + jnp.tanh(math.sqrt(2.0 / math.pi) * (yv + 0.044715 * (yv * yv * yv))))
        gate = jnp.dot(z.astype(BF16), wglu_ref[...], preferred_element_type=F32)
        s_out = z * jax.nn.sigmoid(gate)
        na = _rms(a_ref[0, rows, :], ga_ref[...]).astype(BF16)
        ns = _rms(s_out, gs_ref[...]).astype(BF16)
        x1 = x_ref[0, rows, :] + jnp.dot(na, wout_ref[0:D_ATTN, :], preferred_element_type=F32)
        return x1 + jnp.dot(ns, wout_ref[D_ATTN:, :], preferred_element_type=F32)

    def tile_cols(j, half):
        return slice(half * D_FF + j * FF_TILE, half * D_FF + (j + 1) * FF_TILE)

    def to_planes(x):
        for c in range(D_MODEL // LANES):
            px_s[c] = x[:, LANES * c:LANES * (c + 1)]
        return jnp.concatenate(
            [jnp.concatenate([px_s[c, pl.ds(r, pr, stride=SUBLANES), :] for r in range(SUBLANES)], axis=0)
             for c in range(D_MODEL // LANES)], axis=1)

    def store_rows(i, y):
        rows = slice(i * sb, (i + 1) * sb)
        if not planes:
            y_ref[0, rows, :] = y
            return
        for c in range(D_MODEL // LANES):
            for r in range(SUBLANES):
                py_s[c, pl.ds(r, pr, stride=SUBLANES), :] = y[r * pr:(r + 1) * pr, LANES * c:LANES * (c + 1)]
            y_ref[0, rows, LANES * c:LANES * (c + 1)] = py_s[c]

    def time_shifts(up, cols):
        if not planes:
            ext_s[pad:pad + sb, cols] = up
            sh2 = ext_s[pad - 2 * cs:pad - 2 * cs + sb, cols]
            sh1 = ext_s[pad - cs:pad - cs + sb, cols]
            ext_s[0:pad, cols] = ext_s[sb:sb + pad, cols]
            return sh2, sh1
        for p in range(2):
            ext_s[p, pad:pad + pr, cols] = up[(6 + p) * pr:(7 + p) * pr, :]
        s6 = ext_s[0, pad - 1:pad - 1 + pr, cols]
        s7 = ext_s[1, pad - 1:pad - 1 + pr, cols]
        ext_s[:, 0:pad, cols] = ext_s[:, pr:pr + pad, cols]
        sh1 = jnp.concatenate([s7, up[:7 * pr, :]], axis=0)
        sh2 = jnp.concatenate([s6, s7, up[:6 * pr, :]], axis=0)
        return sh2, sh1

    def conv_ffn(i, x1):
        if planes:
            x1 = to_planes(x1)
        h2 = _rms(x1, g2_ref[...]).astype(BF16)

        def up_proj(j):
            return [jnp.dot(h2, wup_ref[:, tile_cols(j, half)], preferred_element_type=F32) for half in range(2)]

        def down_proj(act, j):
            return jnp.dot(act, wdown_ref[j * FF_TILE:(j + 1) * FF_TILE, :], preferred_element_type=F32)

        n_tiles = D_FF // FF_TILE
        acc = jnp.zeros((sb, D_MODEL), F32)
        ups = up_proj(0)
        act_prev = None
        for j in range(n_tiles):
            nxt = up_proj(j + 1) if j + 1 < n_tiles else None
            if act_prev is not None:
                acc = acc + down_proj(act_prev, j - 1)
            parts = []
            for half in range(2):
                cols = tile_cols(j, half)
                sh2, sh1 = time_shifts(ups[half], cols)
                parts.append(sh2 * cw_ref[0:1, cols] + sh1 * cw_ref[1:2, cols]
                             + ups[half] * cw_ref[2:3, cols] + cb_ref[:, cols])
            cg, cv = parts
            act_prev = (cg * jax.nn.sigmoid(cg) * cv).astype(BF16)
            ups = nxt
        acc = acc + down_proj(act_prev, n_tiles - 1)
        store_rows(i, _rms(x1 + acc, gf_ref[...]))

    mixed = mix(0)
    for i in range(n_sub):
        nxt_mixed = mix(i + 1) if i + 1 < n_sub else None
        conv_ffn(i, mixed)
        mixed = nxt_mixed
    if planes:
        cst_ref[0] = ext_s[:, 0:pad, :]
    else:
        cst_ref[0] = ext_s[0:pad, :]


def _decode_in_kernel(sinks_ref, x_ref, g1_ref, win_ref, cos_ref, sin_ref, kc_ref, vc_ref, h0_ref,
                      m_ref, wd_ref, r_ref, abar_ref,
                      a_ref, kn_ref, vn_ref, yg_ref, sn_ref, u_s, ug_s, *, n_streams, n_steps, n_cache):
    rows = n_streams * n_steps
    qs, k, v, u = _in_proj(x_ref[...], g1_ref, win_ref, cos_ref, sin_ref)
    kn_ref[...] = k
    vn_ref[...] = v
    for j in range(D_SSM // LANES):
        u_s[j] = u[:, LANES * j:LANES * (j + 1)]
    ncache_rows = n_streams * n_cache
    nkeys = ncache_rows + rows
    kall = jnp.concatenate([kc_ref[...], k], axis=0)
    vall = jnp.concatenate([vc_ref[...], v], axis=0)
    variants = _kv_variants(kall, vall)

    colv = lax.broadcasted_iota(jnp.int32, (1, nkeys), 1)
    is_new = colv >= ncache_rows
    cnew = colv - ncache_rows
    k_stream = jnp.where(is_new, _pmod(cnew, n_streams), _pdiv(colv, n_cache))
    k_pos = jnp.where(is_new, PAST_LEN + _pdiv(cnew, n_streams), PAST_LEN - n_cache + _pmod(colv, n_cache))
    k_chunk = _pdiv(k_pos, CHUNK)
    rowv = _pmod(lax.broadcasted_iota(jnp.int32, (2 * rows, 1), 0), rows)
    q_stream = _pmod(rowv, n_streams)
    q_chunk = _pdiv(PAST_LEN + _pdiv(rowv, n_streams), CHUNK)
    ok_col = jnp.where(k_pos >= 0, k_stream, -1)
    d_chunk = q_chunk - k_chunk
    valid = [q_stream == ok_col,
             lax.bitcast_convert_type(d_chunk, jnp.uint32) <= jnp.uint32(N_WIN_CHUNKS)]
    top = lax.broadcasted_iota(jnp.int32, (2 * rows, 1), 0) < rows

    for g in range(N_KV_HEADS):
        a_bf = jnp.concatenate([qs[2 * g], qs[2 * g + 1]], axis=0).astype(BF16)
        acc = jnp.zeros((2 * rows, LANES), F32)
        for par in range(2):
            h0 = 4 * g + par
            sink_col = jnp.where(top, sinks_ref[h0], sinks_ref[h0 + 2])
            acc = acc + _attend(a_bf, variants[2 * g + par], variants[4 + 2 * g + par], valid, sink_col)
        a_ref[:, 2 * LANES * g:2 * LANES * g + LANES] = acc[:rows]
        a_ref[:, 2 * LANES * g + LANES:2 * LANES * (g + 1)] = acc[rows:]

    def put(g, half, val):
        ug_s[g, :, LANES * half:LANES * (half + 1)] = val

    _to_groups(u_s, put, n_streams, strided=False)
    def group_body(g, c):
        ub = ug_s[g]
        h0g = h0_ref[g]
        d = jnp.dot(ub, wd_ref[g], preferred_element_type=F32)
        sn_ref[g] = _cmul(abar_ref[0, g], abar_ref[1, g], h0g) + d
        y = jnp.dot(ub, m_ref[g], preferred_element_type=F32)
        y = y + lax.dot_general(h0g.astype(BF16), r_ref[g], (((1,), (1,)), ((), ())),
                                preferred_element_type=F32)
        yg_ref[g] = y.astype(BF16)
        return c

    lax.fori_loop(0, N_GROUPS, group_body, 0, unroll=GROUP_UNROLL)


def _ssd_tables(a_re, a_im, log_dt, b_re, b_im, c_re, c_im, d_skip):
    dup = lambda x: jnp.concatenate([x, x], axis=-1)
    prm = jnp.stack([dup(a_re), dup(a_im), jnp.broadcast_to(log_dt[:, None], (N_GROUPS, STATE_W))], axis=1)
    bt = jnp.concatenate([b_re.transpose(0, 2, 1), b_im.transpose(0, 2, 1)], axis=-1)
    cc = jnp.concatenate([c_re, c_im], axis=-1)
    dtile = jnp.tile(d_skip, (1, SSD_T))[:, None, :]
    gb = TABLE_GROUPS
    blk3 = lambda r, c: pl.BlockSpec((gb, r, c), lambda i: (i, 0, 0))
    return pl.pallas_call(
        _ssd_tables_kernel,
        grid=(N_GROUPS // gb,),
        in_specs=[blk3(3, STATE_W), blk3(SSM_GROUP, STATE_W), blk3(SSM_GROUP, STATE_W), blk3(1, SSD_W)],
        out_specs=[blk3(SSD_W, SSD_W), blk3(SSD_W, STATE_W), blk3(SSD_W, STATE_W),
                   pl.BlockSpec((8, gb, SUBLANES, STATE_W), lambda i: (0, i, 0, 0)),
                   pl.BlockSpec((2, gb, 1, STATE_W), lambda i: (0, i, 0, 0))],
        out_shape=[jax.ShapeDtypeStruct((N_GROUPS, SSD_W, SSD_W), BF16),
                   jax.ShapeDtypeStruct((N_GROUPS, SSD_W, STATE_W), BF16),
                   jax.ShapeDtypeStruct((N_GROUPS, SSD_W, STATE_W), BF16),
                   jax.ShapeDtypeStruct((8, N_GROUPS, SUBLANES, STATE_W), F32),
                   jax.ShapeDtypeStruct((2, N_GROUPS, 1, STATE_W), F32)],
        compiler_params=pltpu.CompilerParams(dimension_semantics=("arbitrary",)),
        name="ssd_tables",
    )(prm, bt, cc, dtile)


def _ssd_tables_kernel(prm_ref, bt_ref, cc_ref, dt_ref, m_ref, wd_ref, rt_ref, tab_ref, abar_ref):
    t = SSD_T
    lane = lax.broadcasted_iota(jnp.int32, (1, STATE_W), 1)
    sgn = jnp.where(lane >= SSM_STATE, 1.0, -1.0).astype(F32)
    conj = -sgn
    n_small = (lax.broadcasted_iota(jnp.int32, (2 * t, 1), 0) - (t - 1)).astype(F32)
    sub_i = lax.broadcasted_iota(jnp.int32, (SUBLANES, 1), 0)
    n_big = float(t) * (sub_i + 1).astype(F32)
    row = lax.broadcasted_iota(jnp.int32, (SSD_W, SSD_W), 0)
    col = lax.broadcasted_iota(jnp.int32, (SSD_W, SSD_W), 1)
    causal = _pdiv(col, SSM_GROUP) >= _pdiv(row, SSM_GROUP)
    diag = row == col
    for gi in range(prm_ref.shape[0]):
        a_re, a_im = prm_ref[gi, 0:1, :], prm_ref[gi, 1:2, :]
        dt = jnp.exp(prm_ref[gi, 2:3, :])
        lr, li = dt * a_re, dt * a_im

        def power(n):
            mag = jnp.exp(n * lr)
            return mag * jnp.cos(n * li), mag * jnp.sin(n * li) * sgn

        pr_s, pis_s = power(n_small)
        pr_b, pis_b = power(n_big)

        def times_powers(z, zs, exps):
            rows = [e + t - 1 for e in exps]
            return jnp.concatenate([z * pr_s[r:r + 1, :] + zs * pis_s[r:r + 1, :] for r in rows], axis=0)

        ar, ais = pr_s[t:t + 1, :], pis_s[t:t + 1, :]
        den = a_re * a_re + a_im * a_im
        nr, ni = ar - 1.0, ais * sgn
        fr, fis = (nr * a_re + ni * a_im) / den, (ni * a_re - nr * a_im) / den * sgn
        bt = bt_ref[gi]
        bb = bt * fr + pltpu.roll(bt, SSM_STATE, 1) * fis
        bbs = pltpu.roll(bb, SSM_STATE, 1)
        cc = cc_ref[gi]
        ccs = pltpu.roll(cc, SSM_STATE, 1)
        steps = range(t)
        x = times_powers(bb, bbs, [-s for s in steps]) * conj
        y = times_powers(cc, ccs, list(steps))
        m = lax.dot_general(x, y, (((1,), (1,)), ((), ())), precision=lax.Precision.HIGHEST,
                            preferred_element_type=F32)
        m = jnp.where(causal, m, 0.0) + jnp.where(diag, dt_ref[gi], 0.0)
        m_ref[gi] = m.astype(BF16)
        wd_ref[gi] = times_powers(bb, bbs, [t - 1 - s for s in steps]).astype(BF16)
        rt_ref[gi] = (times_powers(cc, ccs, [s + 1 for s in steps]) * conj).astype(BF16)
        for lvl in range(3):
            d = 1 << lvl
            tab_ref[2 * lvl, gi] = jnp.where(sub_i >= d, pr_b[d - 1:d, :], 0.0)
            tab_ref[2 * lvl + 1, gi] = jnp.where(sub_i >= d, pis_b[d - 1:d, :], 0.0)
        tab_ref[6, gi] = pr_b
        tab_ref[7, gi] = pis_b
        abar_ref[0, gi] = pr_b[0:1, :]
        abar_ref[1, gi] = pis_b[0:1, :]


def _rope_freq():
    half = HEAD_DIM // 2
    return ROPE_THETA ** (-jnp.asarray(np.arange(LANES) % half, F32) / half)


def _rope_tables(pos):
    sign = jnp.asarray(np.where(np.arange(LANES) % HEAD_DIM < HEAD_DIM // 2, -1.0, 1.0), F32)
    ang = pos.astype(F32)[:, None] * _rope_freq()[None, :]
    return jnp.cos(ang), jnp.sin(ang) * sign[None, :]


def _rope_base(rows):
    freq = _rope_freq()[None, :]
    ang = jnp.arange(rows, dtype=F32)[:, None] * freq
    return jnp.cos(ang), jnp.sin(ang), freq


def _const(shape):
    nd = len(shape)
    return pl.BlockSpec(shape, lambda *_: (0,) * nd, pipeline_mode=pl.Buffered(1))


def _whole(shape):
    nd = len(shape)
    return pl.BlockSpec(shape, lambda *_: (0,) * nd)


def _out_call(x, a, yg, prev, w, *, tb, sb, strided, cs, pad):
    nb, length, _ = x.shape
    nk = tb // SSD_T
    grid = (nb, length // tb)
    in_specs = [
        pl.BlockSpec((1, tb, D_MODEL), lambda b, i: (b, i, 0)),
        pl.BlockSpec((1, tb, D_ATTN), lambda b, i: (b, i, 0)),
        pl.BlockSpec((N_GROUPS, 1, nk, SSD_W), lambda b, i: (0, b, i, 0)),
    ]
    args = [x, a, yg]
    if prev is not None:
        in_specs.append(pl.BlockSpec((1, pad, 2 * D_FF), lambda b, i: (b, 0, 0)))
        args.append(prev)
    weights = [w['w_glu'], w['onorm_a'], w['onorm_s'], w['w_out'], w['norm2'], w['w_up'], w['conv_w'],
               w['conv_b'], w['w_down'], w['final_g']]
    in_specs += [_const(t.shape) for t in weights]
    kern = functools.partial(_out_kernel, strided=strided, cs=cs, has_prev=prev is not None)
    if strided:
        ext_shape = (2, pad + sb // SUBLANES, 2 * D_FF)
        cst_block, cst_map = (1, 2, pad, 2 * D_FF), (lambda b, i: (b, 0, 0, 0))
    else:
        ext_shape = (pad + sb, 2 * D_FF)
        cst_block, cst_map = (1, pad, 2 * D_FF), (lambda b, i: (b, 0, 0))
    return pl.pallas_call(
        kern,
        grid=grid,
        in_specs=in_specs,
        out_specs=[pl.BlockSpec((1, tb, D_MODEL), lambda b, i: (b, i, 0)),
                   pl.BlockSpec(cst_block, cst_map)],
        out_shape=[jax.ShapeDtypeStruct((nb, length, D_MODEL), F32),
                   jax.ShapeDtypeStruct((nb,) + cst_block[1:], F32)],
        scratch_shapes=[pltpu.VMEM((tb // sb, D_SSM // LANES, sb, LANES), F32),
                        pltpu.VMEM(ext_shape, F32),
                        pltpu.VMEM((D_MODEL // LANES, sb, LANES), F32),
                        pltpu.VMEM((D_MODEL // LANES, sb, LANES), F32)],
        compiler_params=pltpu.CompilerParams(dimension_semantics=("arbitrary", "arbitrary"),
                                             vmem_limit_bytes=VMEM_LIMIT),
        name="layer_out",
    )(*args, *weights)


def kernel(x_prompt, x_sample, cache_k, cache_v, state_ssm_re, state_ssm_im, state_conv, norm1_g, w_in, attn_sinks, ssm_A_re, ssm_A_im, ssm_log_dt, ssm_B_re, ssm_B_im, ssm_C_re, ssm_C_im, ssm_D, w_glu, onorm_attn_g, onorm_ssm_g, w_out, norm2_g, w_up, conv_w, conv_b, w_down, final_g):
    assert norm1_g.shape[0] == 1, "one layer"
    nb, length, _ = x_prompt.shape
    ns, nt, _ = x_sample.shape
    n_cache = cache_k.shape[2]
    assert nt == SSD_T and length % TB_IN == 0 and length % TB_OUT == 0 and (length // SSD_T) % NC_SSD == 0

    m_mat, wd_mat, r_mat, tab, abar = _ssd_tables(ssm_A_re[0], ssm_A_im[0], ssm_log_dt[0], ssm_B_re[0],
                                                  ssm_B_im[0], ssm_C_re[0], ssm_C_im[0], ssm_D[0])
    sinks = attn_sinks[0]
    g1 = norm1_g
    win = w_in[0].astype(BF16)
    w = dict(w_glu=w_glu[0].astype(BF16), onorm_a=onorm_attn_g, onorm_s=onorm_ssm_g, w_out=w_out[0].astype(BF16),
             norm2=norm2_g, conv_w=conv_w[0], conv_b=conv_b, final_g=final_g[None, :])
    smem = pl.BlockSpec(memory_space=pltpu.SMEM)

    cos_p, sin_p, freq = _rope_base(TB_IN)
    nk = TB_IN // SSD_T
    a_p, ug_p, k_p, v_p = pl.pallas_call(
        functools.partial(_prompt_in_kernel, sb=SUB_IN),
        grid=(nb, length // TB_IN),
        in_specs=[smem,
                  pl.BlockSpec((1, TB_IN, D_MODEL), lambda b, i: (b, i, 0)),
                  _const((1, D_MODEL)), _const((D_MODEL, D_IN)),
                  _const((TB_IN, LANES)), _const((TB_IN, LANES)), _const((1, LANES))],
        out_specs=[pl.BlockSpec((1, TB_IN, D_ATTN), lambda b, i: (b, i, 0)),
                   pl.BlockSpec((N_GROUPS, 1, nk, SSD_W), lambda b, i: (0, b, i, 0)),
                   pl.BlockSpec((1, WINDOW, KV_W), lambda b, i: (b, 0, 0)),
                   pl.BlockSpec((1, WINDOW, KV_W), lambda b, i: (b, 0, 0))],
        out_shape=[jax.ShapeDtypeStruct((nb, length, D_ATTN), F32),
                   jax.ShapeDtypeStruct((N_GROUPS, nb, length // SSD_T, SSD_W), BF16),
                   jax.ShapeDtypeStruct((nb, WINDOW, KV_W), F32),
                   jax.ShapeDtypeStruct((nb, WINDOW, KV_W), F32)],
        scratch_shapes=[pltpu.VMEM((TB_IN, D_ATTN), F32),
                        pltpu.VMEM((2 * N_KV_HEADS, WINDOW + TB_IN, LANES), BF16),
                        pltpu.VMEM((WINDOW + TB_IN, LANES), F32),
                        pltpu.VMEM((2, N_KV_HEADS, 2, LANES, WINDOW + TB_IN), BF16),
                        pltpu.VMEM((D_SSM // LANES, TB_IN, LANES), F32)],
        compiler_params=pltpu.CompilerParams(dimension_semantics=("arbitrary", "arbitrary"),
                                             vmem_limit_bytes=VMEM_LIMIT),
        name="prompt_in",
    )(sinks, x_prompt, g1, win, cos_p, sin_p, freq)

    n_ssd = length // SSD_T // NC_SSD
    up_rows, down_rows = D_MODEL // n_ssd, D_FF // n_ssd
    yg_p, sfin_p, w['w_up'], w['w_down'] = pl.pallas_call(
        _prompt_ssd_kernel,
        grid=(n_ssd,),
        in_specs=[pl.BlockSpec((N_GROUPS, nb, NC_SSD, SSD_W), lambda i: (0, 0, i, 0)),
                  _const(m_mat.shape), _const(wd_mat.shape), _const(r_mat.shape), _const(tab.shape),
                  _const(abar.shape),
                  pl.BlockSpec((up_rows, 2 * D_FF), lambda i: (i, 0)),
                  pl.BlockSpec((down_rows, D_MODEL), lambda i: (i, 0))],
        out_specs=[pl.BlockSpec((N_GROUPS, nb, NC_SSD, SSD_W), lambda i: (0, 0, i, 0)),
                   pl.BlockSpec((nb, N_GROUPS, 1, STATE_W), lambda i: (0, 0, 0, 0)),
                   pl.BlockSpec((up_rows, 2 * D_FF), lambda i: (i, 0)),
                   pl.BlockSpec((down_rows, D_MODEL), lambda i: (i, 0))],
        out_shape=[jax.ShapeDtypeStruct((N_GROUPS, nb, length // SSD_T, SSD_W), BF16),
                   jax.ShapeDtypeStruct((nb, N_GROUPS, 1, STATE_W), F32),
                   jax.ShapeDtypeStruct((D_MODEL, 2 * D_FF), BF16),
                   jax.ShapeDtypeStruct((D_FF, D_MODEL), BF16)],
        scratch_shapes=[pltpu.VMEM((N_GROUPS, nb * NC_SSD, STATE_W), F32),
                        pltpu.VMEM((N_GROUPS, nb * NC_SSD, STATE_W), F32),
                        pltpu.VMEM((nb, N_GROUPS, 1, STATE_W), F32)],
        compiler_params=pltpu.CompilerParams(dimension_semantics=("arbitrary",), vmem_limit_bytes=VMEM_LIMIT),
        name="prompt_ssd",
    )(ug_p, m_mat, wd_mat, r_mat, tab, abar, w_up[0], w_down[0])

    y_p, cst_p = _out_call(x_prompt, a_p, yg_p, None, w, tb=TB_OUT, sb=SUB_OUT, strided=True, cs=1, pad=SUBLANES)

    rows = ns * nt
    xs = x_sample.transpose(1, 0, 2).reshape(rows, D_MODEL)
    cos_s, sin_s = _rope_tables(PAST_LEN + jnp.arange(rows) // ns)
    kc = cache_k[0].reshape(ns * n_cache, KV_W)
    vc = cache_v[0].reshape(ns * n_cache, KV_W)
    h0 = jnp.concatenate([state_ssm_re[0], state_ssm_im[0]], axis=-1).transpose(1, 0, 2)
    dec = functools.partial(_decode_in_kernel, n_streams=ns, n_steps=nt, n_cache=n_cache)
    dec_in = [xs, g1, win, cos_s, sin_s, kc, vc, h0, m_mat, wd_mat, r_mat, abar]
    a_s, kn, vn, yg_s, sn = pl.pallas_call(
        dec,
        grid=(1,),
        in_specs=[smem] + [_const(t.shape) for t in dec_in],
        out_specs=[_whole((rows, D_ATTN)), _whole((rows, KV_W)), _whole((rows, KV_W)),
                   _whole((N_GROUPS, ns, SSD_W)), _whole((N_GROUPS, ns, STATE_W))],
        out_shape=[jax.ShapeDtypeStruct((rows, D_ATTN), F32), jax.ShapeDtypeStruct((rows, KV_W), F32),
                   jax.ShapeDtypeStruct((rows, KV_W), F32), jax.ShapeDtypeStruct((N_GROUPS, ns, SSD_W), BF16),
                   jax.ShapeDtypeStruct((N_GROUPS, ns, STATE_W), F32)],
        scratch_shapes=[pltpu.VMEM((D_SSM // LANES, rows, LANES), F32), pltpu.VMEM((N_GROUPS, ns, SSD_W), BF16)],
        compiler_params=pltpu.CompilerParams(dimension_semantics=("arbitrary",), vmem_limit_bytes=VMEM_LIMIT),
        name="decode_in",
    )(sinks, *dec_in)

    cs = ns
    prev = state_conv[0].transpose(1, 0, 2).reshape(1, (CONV_W - 1) * cs, 2 * D_FF)
    y_s, cst_s = _out_call(xs[None], a_s[None], yg_s[:, None], prev, w, tb=rows, sb=rows, strided=False, cs=cs,
                           pad=(CONV_W - 1) * cs)

    kv5 = lambda t, n: t.reshape(1, n, -1, N_KV_HEADS, HEAD_DIM)
    y_sample = y_s.reshape(nt, ns, D_MODEL).transpose(1, 0, 2)
    kn_b = kn.reshape(nt, ns, KV_W).transpose(1, 0, 2)
    vn_b = vn.reshape(nt, ns, KV_W).transpose(1, 0, 2)
    k_new = jnp.concatenate([cache_k[0].reshape(ns, n_cache, KV_W), kn_b], axis=1)[:, -n_cache:]
    v_new = jnp.concatenate([cache_v[0].reshape(ns, n_cache, KV_W), vn_b], axis=1)[:, -n_cache:]
    conv_p = cst_p[:, :, SUBLANES - 1, :][None]
    conv_s = cst_s.reshape(CONV_W - 1, ns, 2 * D_FF).transpose(1, 0, 2)[None]
    sn_b = sn.transpose(1, 0, 2)
    return (y_p, y_sample,
            kv5(k_p, nb), kv5(v_p, nb),
            sfin_p[None, :, :, 0, :SSM_STATE], sfin_p[None, :, :, 0, SSM_STATE:], conv_p,
            kv5(k_new, ns), kv5(v_new, ns),
            sn_b[None, :, :, :SSM_STATE], sn_b[None, :, :, SSM_STATE:], conv_s)
```

```python
import functools
import math

import numpy as np
import jax
import jax.numpy as jnp
from jax import lax
from jax.experimental import pallas as pl
from jax.experimental.pallas import tpu as pltpu

F32 = jnp.float32
BF16 = jnp.bfloat16

D_MODEL = 1024
CHUNK = 64
D_ATTN = 512
D_SSM = 512
HEAD_DIM = 64
N_HEADS = 8
N_KV_HEADS = 2
KV_W = N_KV_HEADS * HEAD_DIM
WINDOW = 128
N_WIN_CHUNKS = WINDOW // CHUNK
ROPE_THETA = 10000.0
SSM_GROUP = 16
N_GROUPS = D_SSM // SSM_GROUP
SSM_STATE = 64
D_FF = 2816
CONV_W = 3
EPS = 1e-6
D_IN = D_ATTN + 2 * KV_W + D_SSM
PAST_LEN = 2048

SSD_T = 16
LANES = 128
SUBLANES = 8
GROUPS_PER_VREG = LANES // SSM_GROUP
SSD_W = SSD_T * SSM_GROUP
STATE_W = 2 * SSM_STATE

TB_IN = 2048
SUB_IN = 512
TB_OUT = 512
SUB_OUT = 256
NC_SSD = 64
FF_TILE = 256
GROUP_UNROLL = 8
TABLE_GROUPS = 8
VMEM_LIMIT = 56 * 1024 * 1024


def _log2(n):
    assert n > 0 and n & (n - 1) == 0, n
    return n.bit_length() - 1


def _pdiv(x, n):
    return lax.shift_right_arithmetic(x, jnp.int32(_log2(n)))


def _pmod(x, n):
    return x & (n - 1)


def _rms(x, g):
    return x * lax.rsqrt(jnp.mean(x * x, axis=-1, keepdims=True) + EPS) * g


def _rope(xb, cos, sin_signed, first_half):
    partner = jnp.where(first_half, pltpu.roll(xb, LANES - HEAD_DIM // 2, 1), pltpu.roll(xb, HEAD_DIM // 2, 1))
    return xb * cos + partner * sin_signed


def _group_transpose8(xs, lane_group):
    xs = list(xs)
    for d in (4, 2, 1):
        keep = (lane_group & d) == 0
        nxt = list(xs)
        for i in range(GROUPS_PER_VREG):
            if i & d:
                continue
            a, b = xs[i], xs[i | d]
            nxt[i] = jnp.where(keep, a, pltpu.roll(b, SSM_GROUP * d, 1))
            nxt[i | d] = jnp.where(keep, pltpu.roll(a, LANES - SSM_GROUP * d, 1), b)
        xs = nxt
    return xs


def _row_sel(s, nk, strided, row0=0):
    if strided:
        return pl.ds(row0 + s, nk, stride=SSD_T)
    return pl.ds(row0 + s * nk, nk)


def _to_groups(u_ref, put, nk, strided, row0=0):
    lane_group = _pdiv(lax.broadcasted_iota(jnp.int32, (nk, LANES), 1), SSM_GROUP)
    for j in range(D_SSM // LANES):
        for half in range(SSD_T // GROUPS_PER_VREG):
            xs = [u_ref[j, _row_sel(GROUPS_PER_VREG * half + sl, nk, strided, row0), :]
                  for sl in range(GROUPS_PER_VREG)]
            ws = _group_transpose8(xs, lane_group)
            for gq in range(GROUPS_PER_VREG):
                put(GROUPS_PER_VREG * j + gq, half, ws[gq].astype(BF16))


def _from_groups(get, y_ref, nk, strided):
    lane_group = _pdiv(lax.broadcasted_iota(jnp.int32, (nk, LANES), 1), SSM_GROUP)
    for j in range(D_SSM // LANES):
        for half in range(SSD_T // GROUPS_PER_VREG):
            ws = [get(GROUPS_PER_VREG * j + gq, half).astype(F32) for gq in range(GROUPS_PER_VREG)]
            xs = _group_transpose8(ws, lane_group)
            for sl in range(GROUPS_PER_VREG):
                y_ref[j, _row_sel(GROUPS_PER_VREG * half + sl, nk, strided), :] = xs[sl]


def _cmul(re_full, im_signed, z):
    return re_full * z + im_signed * pltpu.roll(z, SSM_STATE, z.ndim - 1)


def _in_proj(x, g1_ref, win_ref, cos_ref, sin_ref):
    rows = x.shape[0]
    hn = _rms(x, g1_ref[...]).astype(BF16)
    proj = jnp.dot(hn, win_ref[...], preferred_element_type=F32)
    cos = cos_ref[...]
    sin = sin_ref[...]
    lane = lax.broadcasted_iota(jnp.int32, (rows, LANES), 1)
    first_half = _pmod(lane, HEAD_DIM) < (HEAD_DIM // 2)
    scale = HEAD_DIM ** -0.5
    qs = [_rope(proj[:, LANES * j:LANES * (j + 1)], cos, sin, first_half) * scale for j in range(D_ATTN // LANES)]
    k = _rope(proj[:, D_ATTN:D_ATTN + KV_W], cos, sin, first_half)
    v = proj[:, D_ATTN + KV_W:D_ATTN + 2 * KV_W]
    u = proj[:, D_ATTN + 2 * KV_W:]
    return qs, k, v, u


def _kv_variants(k, v):
    lane = lax.broadcasted_iota(jnp.int32, k.shape, 1)
    lo = lane < HEAD_DIM
    out = []
    for t in (k, v):
        tr = pltpu.roll(t, HEAD_DIM, 1)
        zero = jnp.zeros_like(t)
        out += [jnp.where(lo, t, zero), jnp.where(lo, zero, tr), jnp.where(lo, tr, zero), jnp.where(lo, zero, t)]
    return [o.astype(BF16) for o in out]


def _k_variants(k):
    lo = lax.broadcasted_iota(jnp.int32, k.shape, 1) < HEAD_DIM
    kr = pltpu.roll(k, HEAD_DIM, 1)
    zero = jnp.zeros_like(k)
    out = [jnp.where(lo, k, zero), jnp.where(lo, zero, kr), jnp.where(lo, kr, zero), jnp.where(lo, zero, k)]
    return [o.astype(BF16) for o in out]


def _softmax_pv_t(s, vt_win, sink_row):
    m = jnp.maximum(jnp.max(s, axis=0, keepdims=True), sink_row)
    e = jnp.exp(s - m)
    den = jnp.sum(e, axis=0, keepdims=True) + jnp.exp(sink_row - m)
    o = jnp.dot(vt_win, e.astype(BF16), preferred_element_type=F32)
    return o * (1.0 / den)


def _attend(a_bf, kvar, vvar, valid, sink_col):
    s = lax.dot_general(a_bf, kvar, (((1,), (1,)), ((), ())), preferred_element_type=F32)
    for cond in valid:
        s = jnp.where(cond, s, -jnp.inf)
    m = jnp.maximum(jnp.max(s, axis=-1, keepdims=True), sink_col)
    e = jnp.exp(s - m)
    den = jnp.sum(e, axis=-1, keepdims=True) + jnp.exp(sink_col - m)
    o = jnp.dot(e.astype(BF16), vvar, preferred_element_type=F32)
    return o * (1.0 / den)


def _prompt_in_kernel(sinks_ref, x_ref, g1_ref, win_ref, cos_ref, sin_ref, freq_ref,
                      a_ref, ug_ref, kp_ref, vp_ref, q_s, k_s, vb_s, vt_s, u_s, *, sb):
    tb = x_ref.shape[1]
    n_sub = tb // sb
    nks = sb // SSD_T
    blk = pl.program_id(1)
    nbuf = WINDOW + tb
    cps = sb // CHUNK
    n_chunks = tb // CHUNK

    @pl.when(blk == 0)
    def _():
        k_s[:, 0:WINDOW, :] = jnp.zeros((4, WINDOW, LANES), BF16)
        vb_s[0:WINDOW, :] = jnp.zeros((WINDOW, LANES), F32)

    @pl.when(blk > 0)
    def _():
        k_s[:, 0:WINDOW, :] = k_s[:, tb:nbuf, :]
        vb_s[0:WINDOW, :] = vb_s[tb:nbuf, :]

    lane = lax.broadcasted_iota(jnp.int32, (sb, LANES), 1)
    first_half = _pmod(lane, HEAD_DIM) < (HEAD_DIM // 2)
    scale = HEAD_DIM ** -0.5
    ang0 = (blk * tb).astype(F32) * freq_ref[...]
    cos0, sin0 = jnp.cos(ang0), jnp.sin(ang0)
    half_sign = jnp.where(first_half[0:1, :], -1.0, 1.0).astype(F32)

    def project(s):
        rows = slice(s * sb, (s + 1) * sb)
        brows = slice(WINDOW + s * sb, WINDOW + (s + 1) * sb)
        state = {}

        def piece(c0):
            if 'hn' not in state:
                state['hn'] = _rms(x_ref[0, rows, :], g1_ref[...]).astype(BF16)
                cb, sn = cos_ref[rows, :], sin_ref[rows, :]
                state['cos'] = cos0 * cb - sin0 * sn
                state['sin'] = (sin0 * cb + cos0 * sn) * half_sign
            return jnp.dot(state['hn'], win_ref[:, c0:c0 + 2 * LANES], preferred_element_type=F32)

        def q_piece(j):
            def run():
                p = piece(2 * LANES * j)
                for h in range(2):
                    q_s[rows, LANES * (2 * j + h):LANES * (2 * j + h + 1)] = _rope(
                        p[:, LANES * h:LANES * (h + 1)], state['cos'], state['sin'], first_half) * scale
            return run

        def kv_piece():
            p = piece(D_ATTN)
            k = _rope(p[:, :KV_W], state['cos'], state['sin'], first_half)
            v = p[:, KV_W:]
            for idx, arr in enumerate(_k_variants(k)):
                k_s[idx, brows, :] = arr
            vb_s[brows, :] = v
            if s == n_sub - 1:
                kp_ref[0] = k[sb - WINDOW:, :]
                vp_ref[0] = v[sb - WINDOW:, :]

        def u_piece(j):
            def run():
                p = piece(D_ATTN + 2 * KV_W + 2 * LANES * j)
                for h in range(2):
                    u_s[2 * j + h, rows, :] = p[:, LANES * h:LANES * (h + 1)]
            return run

        return [kv_piece, q_piece(0), q_piece(1), u_piece(0), u_piece(1)]

    vt_tail = {}

    def build_vt(s):
        lo = 0 if s == 0 else WINDOW + s * sb
        hi = WINDOW + (s + 1) * sb
        piece = vb_s[lo:hi, :].T
        src = piece if s == 0 else jnp.concatenate([vt_tail['v'], piece], axis=1)
        lo1 = lo if s == 0 else lo - LANES
        shifted = pltpu.roll(src, src.shape[1] - CHUNK, 1)
        vt_tail['v'] = piece[:, piece.shape[1] - LANES:]
        for sh, arr, b0 in ((0, piece, lo), (1, shifted, lo1)):
            w = arr.shape[1]
            zrows = jnp.zeros((HEAD_DIM, w), F32)
            for g in range(N_KV_HEADS):
                rows = arr[HEAD_DIM * g:HEAD_DIM * (g + 1), :]
                vt_s[sh, g, 0, :, b0:b0 + w] = jnp.concatenate([rows, zrows], axis=0).astype(BF16)
                vt_s[sh, g, 1, :, b0:b0 + w] = jnp.concatenate([zrows, rows], axis=0).astype(BF16)

    def regroup(s):
        def put(g, half, val):
            ug_ref[g, 0, s * nks:(s + 1) * nks, LANES * half:LANES * (half + 1)] = val
        _to_groups(u_s, put, nks, strided=True, row0=s * sb)

    nkeys = WINDOW + CHUNK
    key_row = lax.broadcasted_iota(jnp.int32, (nkeys, 1), 0)
    q_lane = lax.broadcasted_iota(jnp.int32, (1, LANES), 1)

    def scores(i):
        r0 = i * CHUNK
        valid = key_row >= (WINDOW - r0 - blk * tb) if i < N_WIN_CHUNKS else None
        out = []
        for g in range(N_KV_HEADS):
            qa = q_s[r0:r0 + CHUNK, 2 * LANES * g:2 * LANES * g + LANES]
            qb = q_s[r0:r0 + CHUNK, 2 * LANES * g + LANES:2 * LANES * (g + 1)]
            a_bf = jnp.concatenate([qa, qb], axis=0).astype(BF16)
            for par in range(2):
                s = lax.dot_general(k_s[2 * g + par, r0:r0 + nkeys, :], a_bf, (((1,), (1,)), ((), ())),
                                    preferred_element_type=F32)
                out.append(s if valid is None else jnp.where(valid, s, -jnp.inf))
        return out

    for item in project(0):
        item()
    build_vt(0)
    staging = []
    for s in range(1, n_sub):
        staging += [(s, item) for item in project(s)] + [(s, functools.partial(build_vt, s))]
    regroups = [functools.partial(regroup, s) for s in range(n_sub)]
    pending = scores(0)
    for i in range(n_chunks):
        if i + 1 < n_chunks and (i + 1) % cps == 0:
            while staging and staging[0][0] <= (i + 1) // cps:
                staging.pop(0)[1]()
        nxt = scores(i + 1) if i + 1 < n_chunks else None
        if staging:
            staging.pop(0)[1]()
        elif regroups:
            regroups.pop(0)()
        r0 = i * CHUNK
        sh = i % 2
        base = r0 - sh * CHUNK
        for g in range(N_KV_HEADS):
            acc = None
            for par in range(2):
                h0 = 4 * g + par
                sink_row = jnp.where(q_lane < CHUNK, sinks_ref[h0], sinks_ref[h0 + 2])
                o = _softmax_pv_t(pending[2 * g + par], vt_s[sh, g, par, :, base:base + nkeys], sink_row)
                acc = o if acc is None else acc + o
            at = acc.T
            a_ref[0, r0:r0 + CHUNK, 2 * LANES * g:2 * LANES * g + LANES] = at[:CHUNK]
            a_ref[0, r0:r0 + CHUNK, 2 * LANES * g + LANES:2 * LANES * (g + 1)] = at[CHUNK:]
        pending = nxt
    for item in regroups:
        item()


def _prompt_ssd_kernel(ug_ref, m_ref, wd_ref, r_ref, tab_ref, abar_ref, wup_f_ref, wdown_f_ref,
                       yg_ref, sfin_ref, wup_b_ref, wdown_b_ref, d_s, sp_s, carry_s):
    nb = ug_ref.shape[1]
    nc = ug_ref.shape[2]
    step = pl.program_id(0)
    wup_b_ref[...] = wup_f_ref[...].astype(BF16)
    wdown_b_ref[...] = wdown_f_ref[...].astype(BF16)

    @pl.when(step == 0)
    def _():
        carry_s[...] = jnp.zeros(carry_s.shape, F32)

    def state_in(g, c):
        ub = ug_ref[g].reshape(nb * nc, SSD_W)
        d_s[g] = jnp.dot(ub, wd_ref[g], preferred_element_type=F32)
        return c

    lax.fori_loop(0, N_GROUPS, state_in, 0, unroll=GROUP_UNROLL)

    hp = N_GROUPS // 2
    lo_half = lax.broadcasted_iota(jnp.int32, (1, 1, STATE_W), 2) < SSM_STATE

    def split(z):
        a, b = z[:hp], z[hp:]
        return (jnp.where(lo_half, a, pltpu.roll(b, SSM_STATE, 2)),
                jnp.where(lo_half, pltpu.roll(a, SSM_STATE, 2), b))

    def merge(re, im):
        return jnp.concatenate([jnp.where(lo_half, re, pltpu.roll(im, SSM_STATE, 2)),
                                jnp.where(lo_half, pltpu.roll(re, SSM_STATE, 2), im)], axis=0)

    def pair_tables(tre, tim):
        return jnp.where(lo_half, tre[:hp], tre[hp:]), jnp.where(lo_half, -tim[:hp], tim[hp:])

    pw = [pair_tables(tab_ref[2 * i], tab_ref[2 * i + 1]) for i in range(4)]
    ar, ai = pair_tables(abar_ref[0], abar_ref[1])
    kio = lax.broadcasted_iota(jnp.int32, (N_GROUPS, nc, STATE_W), 1)
    for b in range(nb):
        dg = d_s[:, b * nc:(b + 1) * nc, :]
        s_in = carry_s[b]
        xr_all, xi_all = split(jnp.where(kio == 0, s_in, pltpu.roll(dg, 1, 1)))
        cr = ci = jnp.zeros((hp, 1, STATE_W), F32)
        for j in range(nc // SUBLANES):
            xr = xr_all[:, SUBLANES * j:SUBLANES * (j + 1), :]
            xi = xi_all[:, SUBLANES * j:SUBLANES * (j + 1), :]
            for lvl in range(3):
                pr, pi = pw[lvl]
                sr, si = pltpu.roll(xr, 1 << lvl, 1), pltpu.roll(xi, 1 << lvl, 1)
                xr, xi = xr + pr * sr - pi * si, xi + pr * si + pi * sr
            qr, qi = pw[3]
            crb, cib = jnp.broadcast_to(cr, xr.shape), jnp.broadcast_to(ci, xi.shape)
            hr, hi = xr + qr * crb - qi * cib, xi + qr * cib + qi * crb
            sp_s[:, b * nc + SUBLANES * j:b * nc + SUBLANES * (j + 1), :] = merge(hr, hi)
            cr, ci = hr[:, SUBLANES - 1:SUBLANES, :], hi[:, SUBLANES - 1:SUBLANES, :]
        carry_s[b] = merge(ar * cr - ai * ci, ar * ci + ai * cr) + dg[:, nc - 1:nc, :]

    def readout(g, c):
        ub = ug_ref[g].reshape(nb * nc, SSD_W)
        y = jnp.dot(ub, m_ref[g], preferred_element_type=F32)
        y = y + lax.dot_general(sp_s[g].astype(BF16), r_ref[g], (((1,), (1,)), ((), ())),
                                preferred_element_type=F32)
        yg_ref[g] = y.reshape(nb, nc, SSD_W).astype(BF16)
        return c

    lax.fori_loop(0, N_GROUPS, readout, 0, unroll=GROUP_UNROLL)
    sfin_ref[...] = carry_s[...]


def _out_kernel(*refs, strided, cs, has_prev):
    if has_prev:
        x_ref, a_ref, yg_ref, prev_ref = refs[:4]
        refs = refs[4:]
    else:
        x_ref, a_ref, yg_ref = refs[:3]
        prev_ref = None
        refs = refs[3:]
    (wglu_ref, ga_ref, gs_ref, wout_ref, g2_ref, wup_ref, cw_ref, cb_ref, wdown_ref, gf_ref,
     y_ref, cst_ref, ys_s, ext_s, px_s, py_s) = refs
    tb = x_ref.shape[1]
    n_sub, _, sb, _ = ys_s.shape
    nks = sb // SSD_T
    planes = strided
    pr = sb // SUBLANES
    pad = SUBLANES if planes else ext_s.shape[0] - sb
    blk = pl.program_id(1)
    assert n_sub * sb == tb and (not planes or (cs == 1 and not has_prev))

    @pl.when(blk == 0)
    def _():
        if planes:
            ext_s[:, 0:pad, :] = jnp.zeros((2, pad, 2 * D_FF), F32)
        elif has_prev:
            ext_s[0:pad, :] = prev_ref[0]
        else:
            ext_s[0:pad, :] = jnp.zeros((pad, 2 * D_FF), F32)

    def mix(i):
        rows = slice(i * sb, (i + 1) * sb)
        _from_groups(lambda g, half: yg_ref[g, 0, i * nks:(i + 1) * nks, LANES * half:LANES * (half + 1)],
                     ys_s.at[i], nks, strided)
        yv = jnp.concatenate([ys_s[i, j] for j in range(D_SSM // LANES)], axis=-1)
        z = 0.5 * yv * (1.0 + jnp.tanh(math.sqrt(2.0 / math.pi) * (yv + 0.044715 * (yv * yv * yv))))
        gate = jnp.dot(z.astype(BF16), wglu_ref[...], preferred_element_type=F32)
        s_out = z * jax.nn.sigmoid(gate)
        na = _rms(a_ref[0, rows, :], ga_ref[...]).astype(BF16)
        ns = _rms(s_out, gs_ref[...]).astype(BF16)
        x1 = x_ref[0, rows, :] + jnp.dot(na, wout_ref[0:D_ATTN, :], preferred_element_type=F32)
        return x1 + jnp.dot(ns, wout_ref[D_ATTN:, :], preferred_element_type=F32)

    def tile_cols(j, half):
        return slice(half * D_FF + j * FF_TILE, half * D_FF + (j + 1) * FF_TILE)

    def to_planes(x):
        for c in range(D_MODEL // LANES):
            px_s[c] = x[:, LANES * c:LANES * (c + 1)]
        return jnp.concatenate(
            [jnp.concatenate([px_s[c, pl.ds(r, pr, stride=SUBLANES), :] for r in range(SUBLANES)], axis=0)
             for c in range(D_MODEL // LANES)], axis=1)

    def store_rows(i, y):
        rows = slice(i * sb, (i + 1) * sb)
        if not planes:
            y_ref[0, rows, :] = y
            return
        for c in range(D_MODEL // LANES):
            for r in range(SUBLANES):
                py_s[c, pl.ds(r, pr, stride=SUBLANES), :] = y[r * pr:(r + 1) * pr, LANES * c:LANES * (c + 1)]
            y_ref[0, rows, LANES * c:LANES * (c + 1)] = py_s[c]

    def time_shifts(up, cols):
        if not planes:
            ext_s[pad:pad + sb, cols] = up
            sh2 = ext_s[pad - 2 * cs:pad - 2 * cs + sb, cols]
            sh1 = ext_s[pad - cs:pad - cs + sb, cols]
            ext_s[0:pad, cols] = ext_s[sb:sb + pad, cols]
            return sh2, sh1
        for p in range(2):
            ext_s[p, pad:pad + pr, cols] = up[(6 + p) * pr:(7 + p) * pr, :]
        s6 = ext_s[0, pad - 1:pad - 1 + pr, cols]
        s7 = ext_s[1, pad - 1:pad - 1 + pr, cols]
        ext_s[:, 0:pad, cols] = ext_s[:, pr:pr + pad, cols]
        sh1 = jnp.concatenate([s7, up[:7 * pr, :]], axis=0)
        sh2 = jnp.concatenate([s6, s7, up[:6 * pr, :]], axis=0)
        return sh2, sh1

    def conv_ffn(i, x1):
        if planes:
            x1 = to_planes(x1)
        h2 = _rms(x1, g2_ref[...]).astype(BF16)

        def up_proj(j):
            return [jnp.dot(h2, wup_ref[:, tile_cols(j, half)], preferred_element_type=F32) for half in range(2)]

        def down_proj(act, j):
            return jnp.dot(act, wdown_ref[j * FF_TILE:(j + 1) * FF_TILE, :], preferred_element_type=F32)

        n_tiles = D_FF // FF_TILE
        acc = jnp.zeros((sb, D_MODEL), F32)
        ups = up_proj(0)
        act_prev = None
        for j in range(n_tiles):
            nxt = up_proj(j + 1) if j + 1 < n_tiles else None
            if act_prev is not None:
                acc = acc + down_proj(act_prev, j - 1)
            parts = []
            for half in range(2):
                cols = tile_cols(j, half)
                sh2, sh1 = time_shifts(ups[half], cols)
                parts.append(sh2 * cw_ref[0:1, cols] + sh1 * cw_ref[1:2, cols]
                             + ups[half] * cw_ref[2:3, cols] + cb_ref[:, cols])
            cg, cv = parts
            act_prev = (cg * jax.nn.sigmoid(cg) * cv).astype(BF16)
            ups = nxt
        acc = acc + down_proj(act_prev, n_tiles - 1)
        store_rows(i, _rms(x1 + acc, gf_ref[...]))

    mixed = mix(0)
    for i in range(n_sub):
        nxt_mixed = mix(i + 1) if i + 1 < n_sub else None
        conv_ffn(i, mixed)
        mixed = nxt_mixed
    if planes:
        cst_ref[0] = ext_s[:, 0:pad, :]
    else:
        cst_ref[0] = ext_s[0:pad, :]


def _decode_in_kernel(sinks_ref, x_ref, g1_ref, win_ref, cos_ref, sin_ref, kc_ref, vc_ref, h0_ref,
                      m_ref, wd_ref, r_ref, abar_ref,
                      a_ref, kn_ref, vn_ref, yg_ref, sn_ref, u_s, ug_s, *, n_streams, n_steps, n_cache):
    rows = n_streams * n_steps
    qs, k, v, u = _in_proj(x_ref[...], g1_ref, win_ref, cos_ref, sin_ref)
    kn_ref[...] = k
    vn_ref[...] = v
    for j in range(D_SSM // LANES):
        u_s[j] = u[:, LANES * j:LANES * (j + 1)]
    ncache_rows = n_streams * n_cache
    nkeys = ncache_rows + rows
    kall = jnp.concatenate([kc_ref[...], k], axis=0)
    vall = jnp.concatenate([vc_ref[...], v], axis=0)
    variants = _kv_variants(kall, vall)

    colv = lax.broadcasted_iota(jnp.int32, (1, nkeys), 1)
    is_new = colv >= ncache_rows
    cnew = colv - ncache_rows
    k_stream = jnp.where(is_new, _pmod(cnew, n_streams), _pdiv(colv, n_cache))
    k_pos = jnp.where(is_new, PAST_LEN + _pdiv(cnew, n_streams), PAST_LEN - n_cache + _pmod(colv, n_cache))
    k_chunk = _pdiv(k_pos, CHUNK)
    rowv = _pmod(lax.broadcasted_iota(jnp.int32, (2 * rows, 1), 0), rows)
    q_stream = _pmod(rowv, n_streams)
    q_chunk = _pdiv(PAST_LEN + _pdiv(rowv, n_streams), CHUNK)
    ok_col = jnp.where(k_pos >= 0, k_stream, -1)
    d_chunk = q_chunk - k_chunk
    valid = [q_stream == ok_col,
             lax.bitcast_convert_type(d_chunk, jnp.uint32) <= jnp.uint32(N_WIN_CHUNKS)]
    top = lax.broadcasted_iota(jnp.int32, (2 * rows, 1), 0) < rows

    for g in range(N_KV_HEADS):
        a_bf = jnp.concatenate([qs[2 * g], qs[2 * g + 1]], axis=0).astype(BF16)
        acc = jnp.zeros((2 * rows, LANES), F32)
        for par in range(2):
            h0 = 4 * g + par
            sink_col = jnp.where(top, sinks_ref[h0], sinks_ref[h0 + 2])
            acc = acc + _attend(a_bf, variants[2 * g + par], variants[4 + 2 * g + par], valid, sink_col)
        a_ref[:, 2 * LANES * g:2 * LANES * g + LANES] = acc[:rows]
        a_ref[:, 2 * LANES * g + LANES:2 * LANES * (g + 1)] = acc[rows:]

    def put(g, half, val):
        ug_s[g, :, LANES * half:LANES * (half + 1)] = val

    _to_groups(u_s, put, n_streams, strided=False)
    def group_body(g, c):
        ub = ug_s[g]
        h0g = h0_ref[g]
        d = jnp.dot(ub, wd_ref[g], preferred_element_type=F32)
        sn_ref[g] = _cmul(abar_ref[0, g], abar_ref[1, g], h0g) + d
        y = jnp.dot(ub, m_ref[g], preferred_element_type=F32)
        y = y + lax.dot_general(h0g.astype(BF16), r_ref[g], (((1,), (1,)), ((), ())),
                                preferred_element_type=F32)
        yg_ref[g] = y.astype(BF16)
        return c

    lax.fori_loop(0, N_GROUPS, group_body, 0, unroll=GROUP_UNROLL)


def _ssd_tables(a_re, a_im, log_dt, b_re, b_im, c_re, c_im, d_skip):
    dup = lambda x: jnp.concatenate([x, x], axis=-1)
    prm = jnp.stack([dup(a_re), dup(a_im), jnp.broadcast_to(log_dt[:, None], (N_GROUPS, STATE_W))], axis=1)
    bt = jnp.concatenate([b_re.transpose(0, 2, 1), b_im.transpose(0, 2, 1)], axis=-1)
    cc = jnp.concatenate([c_re, c_im], axis=-1)
    dtile = jnp.tile(d_skip, (1, SSD_T))[:, None, :]
    gb = TABLE_GROUPS
    blk3 = lambda r, c: pl.BlockSpec((gb, r, c), lambda i: (i, 0, 0))
    return pl.pallas_call(
        _ssd_tables_kernel,
        grid=(N_GROUPS // gb,),
        in_specs=[blk3(3, STATE_W), blk3(SSM_GROUP, STATE_W), blk3(SSM_GROUP, STATE_W), blk3(1, SSD_W)],
        out_specs=[blk3(SSD_W, SSD_W), blk3(SSD_W, STATE_W), blk3(SSD_W, STATE_W),
                   pl.BlockSpec((8, gb, SUBLANES, STATE_W), lambda i: (0, i, 0, 0)),
                   pl.BlockSpec((2, gb, 1, STATE_W), lambda i: (0, i, 0, 0))],
        out_shape=[jax.ShapeDtypeStruct((N_GROUPS, SSD_W, SSD_W), BF16),
                   jax.ShapeDtypeStruct((N_GROUPS, SSD_W, STATE_W), BF16),
                   jax.ShapeDtypeStruct((N_GROUPS, SSD_W, STATE_W), BF16),
                   jax.ShapeDtypeStruct((8, N_GROUPS, SUBLANES, STATE_W), F32),
                   jax.ShapeDtypeStruct((2, N_GROUPS, 1, STATE_W), F32)],
        compiler_params=pltpu.CompilerParams(dimension_semantics=("arbitrary",)),
        name="ssd_tables",
    )(prm, bt, cc, dtile)


def _ssd_tables_kernel(prm_ref, bt_ref, cc_ref, dt_ref, m_ref, wd_ref, rt_ref, tab_ref, abar_ref):
    t = SSD_T
    lane = lax.broadcasted_iota(jnp.int32, (1, STATE_W), 1)
    sgn = jnp.where(lane >= SSM_STATE, 1.0, -1.0).astype(F32)
    conj = -sgn
    n_small = lax.broadcasted_iota(jnp.int32, (3 * SUBLANES, 1), 0).astype(F32)
    sub_i = lax.broadcasted_iota(jnp.int32, (SUBLANES, 1), 0)
    n_big = float(t) * (sub_i + 1).astype(F32)
    row = lax.broadcasted_iota(jnp.int32, (SSD_W, SSD_W), 0)
    col = lax.broadcasted_iota(jnp.int32, (SSD_W, SSD_W), 1)
    diag = row == col
    kcol = lax.broadcasted_iota(jnp.int32, (SSM_GROUP, SSD_W), 1)
    for gi in range(prm_ref.shape[0]):
        a_re, a_im = prm_ref[gi, 0:1, :], prm_ref[gi, 1:2, :]
        dt = jnp.exp(prm_ref[gi, 2:3, :])
        lr, li = dt * a_re, dt * a_im

        def power(n):
            mag = jnp.exp(n * lr)
            return mag * jnp.cos(n * li), mag * jnp.sin(n * li) * sgn

        pr_s, pis_s = power(n_small)
        pr_b, pis_b = power(n_big)

        def times_powers(z, zs, exps):
            return jnp.concatenate([z * pr_s[e:e + 1, :] + zs * pis_s[e:e + 1, :] for e in exps], axis=0)

        ar, ais = pr_s[1:2, :], pis_s[1:2, :]
        den = a_re * a_re + a_im * a_im
        nr, ni = ar - 1.0, ais * sgn
        fr, fis = (nr * a_re + ni * a_im) / den, (ni * a_re - nr * a_im) / den * sgn
        bt = bt_ref[gi]
        bb = bt * fr + pltpu.roll(bt, SSM_STATE, 1) * fis
        bbs = pltpu.roll(bb, SSM_STATE, 1)
        cc = cc_ref[gi]
        ccs = pltpu.roll(cc, SSM_STATE, 1)
        steps = range(t)
        y = times_powers(cc, ccs, list(steps))
        kt = lax.dot_general(bb * conj, y, (((1,), (1,)), ((), ())), precision=lax.Precision.HIGHEST,
                             preferred_element_type=F32)
        m = jnp.concatenate(
            [kt if s == 0 else jnp.where(kcol >= SSM_GROUP * s, pltpu.roll(kt, SSM_GROUP * s, 1), 0.0)
             for s in steps], axis=0)
        m_ref[gi] = (m + jnp.where(diag, dt_ref[gi], 0.0)).astype(BF16)
        wd_ref[gi] = times_powers(bb, bbs, [t - 1 - s for s in steps]).astype(BF16)
        rt_ref[gi] = (times_powers(cc, ccs, [s + 1 for s in steps]) * conj).astype(BF16)
        for lvl in range(3):
            d = 1 << lvl
            tab_ref[2 * lvl, gi] = jnp.where(sub_i >= d, pr_b[d - 1:d, :], 0.0)
            tab_ref[2 * lvl + 1, gi] = jnp.where(sub_i >= d, pis_b[d - 1:d, :], 0.0)
        tab_ref[6, gi] = pr_b
        tab_ref[7, gi] = pis_b
        abar_ref[0, gi] = pr_b[0:1, :]
        abar_ref[1, gi] = pis_b[0:1, :]


def _rope_freq():
    half = HEAD_DIM // 2
    return ROPE_THETA ** (-jnp.asarray(np.arange(LANES) % half, F32) / half)


def _rope_tables(pos):
    sign = jnp.asarray(np.where(np.arange(LANES) % HEAD_DIM < HEAD_DIM // 2, -1.0, 1.0), F32)
    ang = pos.astype(F32)[:, None] * _rope_freq()[None, :]
    return jnp.cos(ang), jnp.sin(ang) * sign[None, :]


def _rope_base(rows):
    freq = _rope_freq()[None, :]
    ang = jnp.arange(rows, dtype=F32)[:, None] * freq
    return jnp.cos(ang), jnp.sin(ang), freq


def _const(shape):
    nd = len(shape)
    return pl.BlockSpec(shape, lambda *_: (0,) * nd, pipeline_mode=pl.Buffered(1))


def _whole(shape):
    nd = len(shape)
    return pl.BlockSpec(shape, lambda *_: (0,) * nd)


def _out_call(x, a, yg, prev, w, *, tb, sb, strided, cs, pad):
    nb, length, _ = x.shape
    nk = tb // SSD_T
    grid = (nb, length // tb)
    in_specs = [
        pl.BlockSpec((1, tb, D_MODEL), lambda b, i: (b, i, 0)),
        pl.BlockSpec((1, tb, D_ATTN), lambda b, i: (b, i, 0)),
        pl.BlockSpec((N_GROUPS, 1, nk, SSD_W), lambda b, i: (0, b, i, 0)),
    ]
    args = [x, a, yg]
    if prev is not None:
        in_specs.append(pl.BlockSpec((1, pad, 2 * D_FF), lambda b, i: (b, 0, 0)))
        args.append(prev)
    weights = [w['w_glu'], w['onorm_a'], w['onorm_s'], w['w_out'], w['norm2'], w['w_up'], w['conv_w'],
               w['conv_b'], w['w_down'], w['final_g']]
    in_specs += [_const(t.shape) for t in weights]
    kern = functools.partial(_out_kernel, strided=strided, cs=cs, has_prev=prev is not None)
    if strided:
        ext_shape = (2, pad + sb // SUBLANES, 2 * D_FF)
        cst_block, cst_map = (1, 2, pad, 2 * D_FF), (lambda b, i: (b, 0, 0, 0))
    else:
        ext_shape = (pad + sb, 2 * D_FF)
        cst_block, cst_map = (1, pad, 2 * D_FF), (lambda b, i: (b, 0, 0))
    return pl.pallas_call(
        kern,
        grid=grid,
        in_specs=in_specs,
        out_specs=[pl.BlockSpec((1, tb, D_MODEL), lambda b, i: (b, i, 0)),
                   pl.BlockSpec(cst_block, cst_map)],
        out_shape=[jax.ShapeDtypeStruct((nb, length, D_MODEL), F32),
                   jax.ShapeDtypeStruct((nb,) + cst_block[1:], F32)],
        scratch_shapes=[pltpu.VMEM((tb // sb, D_SSM // LANES, sb, LANES), F32),
                        pltpu.VMEM(ext_shape, F32),
                        pltpu.VMEM((D_MODEL // LANES, sb, LANES), F32),
                        pltpu.VMEM((D_MODEL // LANES, sb, LANES), F32)],
        compiler_params=pltpu.CompilerParams(dimension_semantics=("arbitrary", "arbitrary"),
                                             vmem_limit_bytes=VMEM_LIMIT),
        name="layer_out",
    )(*args, *weights)


def kernel(x_prompt, x_sample, cache_k, cache_v, state_ssm_re, state_ssm_im, state_conv, norm1_g, w_in, attn_sinks, ssm_A_re, ssm_A_im, ssm_log_dt, ssm_B_re, ssm_B_im, ssm_C_re, ssm_C_im, ssm_D, w_glu, onorm_attn_g, onorm_ssm_g, w_out, norm2_g, w_up, conv_w, conv_b, w_down, final_g):
    assert norm1_g.shape[0] == 1, "one layer"
    nb, length, _ = x_prompt.shape
    ns, nt, _ = x_sample.shape
    n_cache = cache_k.shape[2]
    assert nt == SSD_T and length % TB_IN == 0 and length % TB_OUT == 0 and (length // SSD_T) % NC_SSD == 0

    m_mat, wd_mat, r_mat, tab, abar = _ssd_tables(ssm_A_re[0], ssm_A_im[0], ssm_log_dt[0], ssm_B_re[0],
                                                  ssm_B_im[0], ssm_C_re[0], ssm_C_im[0], ssm_D[0])
    sinks = attn_sinks[0]
    g1 = norm1_g
    win = w_in[0].astype(BF16)
    w = dict(w_glu=w_glu[0].astype(BF16), onorm_a=onorm_attn_g, onorm_s=onorm_ssm_g, w_out=w_out[0].astype(BF16),
             norm2=norm2_g, conv_w=conv_w[0], conv_b=conv_b, final_g=final_g[None, :])
    smem = pl.BlockSpec(memory_space=pltpu.SMEM)

    cos_p, sin_p, freq = _rope_base(TB_IN)
    nk = TB_IN // SSD_T
    a_p, ug_p, k_p, v_p = pl.pallas_call(
        functools.partial(_prompt_in_kernel, sb=SUB_IN),
        grid=(nb, length // TB_IN),
        in_specs=[smem,
                  pl.BlockSpec((1, TB_IN, D_MODEL), lambda b, i: (b, i, 0)),
                  _const((1, D_MODEL)), _const((D_MODEL, D_IN)),
                  _const((TB_IN, LANES)), _const((TB_IN, LANES)), _const((1, LANES))],
        out_specs=[pl.BlockSpec((1, TB_IN, D_ATTN), lambda b, i: (b, i, 0)),
                   pl.BlockSpec((N_GROUPS, 1, nk, SSD_W), lambda b, i: (0, b, i, 0)),
                   pl.BlockSpec((1, WINDOW, KV_W), lambda b, i: (b, 0, 0)),
                   pl.BlockSpec((1, WINDOW, KV_W), lambda b, i: (b, 0, 0))],
        out_shape=[jax.ShapeDtypeStruct((nb, length, D_ATTN), F32),
                   jax.ShapeDtypeStruct((N_GROUPS, nb, length // SSD_T, SSD_W), BF16),
                   jax.ShapeDtypeStruct((nb, WINDOW, KV_W), F32),
                   jax.ShapeDtypeStruct((nb, WINDOW, KV_W), F32)],
        scratch_shapes=[pltpu.VMEM((TB_IN, D_ATTN), F32),
                        pltpu.VMEM((2 * N_KV_HEADS, WINDOW + TB_IN, LANES), BF16),
                        pltpu.VMEM((WINDOW + TB_IN, LANES), F32),
                        pltpu.VMEM((2, N_KV_HEADS, 2, LANES, WINDOW + TB_IN), BF16),
                        pltpu.VMEM((D_SSM // LANES, TB_IN, LANES), F32)],
        compiler_params=pltpu.CompilerParams(dimension_semantics=("arbitrary", "arbitrary"),
                                             vmem_limit_bytes=VMEM_LIMIT),
        name="prompt_in",
    )(sinks, x_prompt, g1, win, cos_p, sin_p, freq)

    n_ssd = length // SSD_T // NC_SSD
    up_rows, down_rows = D_MODEL // n_ssd, D_FF // n_ssd
    yg_p, sfin_p, w['w_up'], w['w_down'] = pl.pallas_call(
        _prompt_ssd_kernel,
        grid=(n_ssd,),
        in_specs=[pl.BlockSpec((N_GROUPS, nb, NC_SSD, SSD_W), lambda i: (0, 0, i, 0)),
                  _const(m_mat.shape), _const(wd_mat.shape), _const(r_mat.shape), _const(tab.shape),
                  _const(abar.shape),
                  pl.BlockSpec((up_rows, 2 * D_FF), lambda i: (i, 0)),
                  pl.BlockSpec((down_rows, D_MODEL), lambda i: (i, 0))],
        out_specs=[pl.BlockSpec((N_GROUPS, nb, NC_SSD, SSD_W), lambda i: (0, 0, i, 0)),
                   pl.BlockSpec((nb, N_GROUPS, 1, STATE_W), lambda i: (0, 0, 0, 0)),
                   pl.BlockSpec((up_rows, 2 * D_FF), lambda i: (i, 0)),
                   pl.BlockSpec((down_rows, D_MODEL), lambda i: (i, 0))],
        out_shape=[jax.ShapeDtypeStruct((N_GROUPS, nb, length // SSD_T, SSD_W), BF16),
                   jax.ShapeDtypeStruct((nb, N_GROUPS, 1, STATE_W), F32),
                   jax.ShapeDtypeStruct((D_MODEL, 2 * D_FF), BF16),
                   jax.ShapeDtypeStruct((D_FF, D_MODEL), BF16)],
        scratch_shapes=[pltpu.VMEM((N_GROUPS, nb * NC_SSD, STATE_W), F32),
                        pltpu.VMEM((N_GROUPS, nb * NC_SSD, STATE_W), F32),
                        pltpu.VMEM((nb, N_GROUPS, 1, STATE_W), F32)],
        compiler_params=pltpu.CompilerParams(dimension_semantics=("arbitrary",), vmem_limit_bytes=VMEM_LIMIT),
        name="prompt_ssd",
    )(ug_p, m_mat, wd_mat, r_mat, tab, abar, w_up[0], w_down[0])

    y_p, cst_p = _out_call(x_prompt, a_p, yg_p, None, w, tb=TB_OUT, sb=SUB_OUT, strided=True, cs=1, pad=SUBLANES)

    rows = ns * nt
    xs = x_sample.transpose(1, 0, 2).reshape(rows, D_MODEL)
    cos_s, sin_s = _rope_tables(PAST_LEN + jnp.arange(rows) // ns)
    kc = cache_k[0].reshape(ns * n_cache, KV_W)
    vc = cache_v[0].reshape(ns * n_cache, KV_W)
    h0 = jnp.concatenate([state_ssm_re[0], state_ssm_im[0]], axis=-1).transpose(1, 0, 2)
    dec = functools.partial(_decode_in_kernel, n_streams=ns, n_steps=nt, n_cache=n_cache)
    dec_in = [xs, g1, win, cos_s, sin_s, kc, vc, h0, m_mat, wd_mat, r_mat, abar]
    a_s, kn, vn, yg_s, sn = pl.pallas_call(
        dec,
        grid=(1,),
        in_specs=[smem] + [_const(t.shape) for t in dec_in],
        out_specs=[_whole((rows, D_ATTN)), _whole((rows, KV_W)), _whole((rows, KV_W)),
                   _whole((N_GROUPS, ns, SSD_W)), _whole((N_GROUPS, ns, STATE_W))],
        out_shape=[jax.ShapeDtypeStruct((rows, D_ATTN), F32), jax.ShapeDtypeStruct((rows, KV_W), F32),
                   jax.ShapeDtypeStruct((rows, KV_W), F32), jax.ShapeDtypeStruct((N_GROUPS, ns, SSD_W), BF16),
                   jax.ShapeDtypeStruct((N_GROUPS, ns, STATE_W), F32)],
        scratch_shapes=[pltpu.VMEM((D_SSM // LANES, rows, LANES), F32), pltpu.VMEM((N_GROUPS, ns, SSD_W), BF16)],
        compiler_params=pltpu.CompilerParams(dimension_semantics=("arbitrary",), vmem_limit_bytes=VMEM_LIMIT),
        name="decode_in",
    )(sinks, *dec_in)

    cs = ns
    prev = state_conv[0].transpose(1, 0, 2).reshape(1, (CONV_W - 1) * cs, 2 * D_FF)
    y_s, cst_s = _out_call(xs[None], a_s[None], yg_s[:, None], prev, w, tb=rows, sb=rows, strided=False, cs=cs,
                           pad=(CONV_W - 1) * cs)

    kv5 = lambda t, n: t.reshape(1, n, -1, N_KV_HEADS, HEAD_DIM)
    y_sample = y_s.reshape(nt, ns, D_MODEL).transpose(1, 0, 2)
    kn_b = kn.reshape(nt, ns, KV_W).transpose(1, 0, 2)
    vn_b = vn.reshape(nt, ns, KV_W).transpose(1, 0, 2)
    k_new = jnp.concatenate([cache_k[0].reshape(ns, n_cache, KV_W), kn_b], axis=1)[:, -n_cache:]
    v_new = jnp.concatenate([cache_v[0].reshape(ns, n_cache, KV_W), vn_b], axis=1)[:, -n_cache:]
    conv_p = cst_p[:, :, SUBLANES - 1, :][None]
    conv_s = cst_s.reshape(CONV_W - 1, ns, 2 * D_FF).transpose(1, 0, 2)[None]
    sn_b = sn.transpose(1, 0, 2)
    return (y_p, y_sample,
            kv5(k_p, nb), kv5(v_p, nb),
            sfin_p[None, :, :, 0, :SSM_STATE], sfin_p[None, :, :, 0, SSM_STATE:], conv_p,
            kv5(k_new, ns), kv5(v_new, ns),
            sn_b[None, :, :, :SSM_STATE], sn_b[None, :, :, SSM_STATE:], conv_s)
```

```python
import functools
import math

import numpy as np
import jax
import jax.numpy as jnp
from jax import lax
from jax.experimental import pallas as pl
from jax.experimental.pallas import tpu as pltpu

F32 = jnp.float32
BF16 = jnp.bfloat16

D_MODEL = 1024
CHUNK = 64
D_ATTN = 512
D_SSM = 512
HEAD_DIM = 64
N_HEADS = 8
N_KV_HEADS = 2
KV_W = N_KV_HEADS * HEAD_DIM
WINDOW = 128
N_WIN_CHUNKS = WINDOW // CHUNK
ROPE_THETA = 10000.0
SSM_GROUP = 16
N_GROUPS = D_SSM // SSM_GROUP
SSM_STATE = 64
D_FF = 2816
CONV_W = 3
EPS = 1e-6
D_IN = D_ATTN + 2 * KV_W + D_SSM
PAST_LEN = 2048

SSD_T = 16
LANES = 128
SUBLANES = 8
GROUPS_PER_VREG = LANES // SSM_GROUP
SSD_W = SSD_T * SSM_GROUP
STATE_W = 2 * SSM_STATE

TB_IN = 2048
SUB_IN = 512
TB_OUT = 512
SUB_OUT = 256
NC_SSD = 64
FF_TILE = 256
GROUP_UNROLL = 16
DECODE_UNROLL = 8
TABLE_GROUPS = 8
VMEM_LIMIT = 56 * 1024 * 1024


def _log2(n):
    assert n > 0 and n & (n - 1) == 0, n
    return n.bit_length() - 1


def _pdiv(x, n):
    return lax.shift_right_arithmetic(x, jnp.int32(_log2(n)))


def _pmod(x, n):
    return x & (n - 1)


def _rms(x, g):
    return x * lax.rsqrt(jnp.mean(x * x, axis=-1, keepdims=True) + EPS) * g


def _rope(xb, cos, sin_signed, first_half):
    partner = jnp.where(first_half, pltpu.roll(xb, LANES - HEAD_DIM // 2, 1), pltpu.roll(xb, HEAD_DIM // 2, 1))
    return xb * cos + partner * sin_signed


def _group_transpose8(xs, lane_group):
    xs = list(xs)
    for d in (4, 2, 1):
        keep = (lane_group & d) == 0
        nxt = list(xs)
        for i in range(GROUPS_PER_VREG):
            if i & d:
                continue
            a, b = xs[i], xs[i | d]
            nxt[i] = jnp.where(keep, a, pltpu.roll(b, SSM_GROUP * d, 1))
            nxt[i | d] = jnp.where(keep, pltpu.roll(a, LANES - SSM_GROUP * d, 1), b)
        xs = nxt
    return xs


def _row_sel(s, nk, strided, row0=0):
    if strided:
        return pl.ds(row0 + s, nk, stride=SSD_T)
    return pl.ds(row0 + s * nk, nk)


def _to_groups(u_ref, put, nk, strided, row0=0):
    lane_group = _pdiv(lax.broadcasted_iota(jnp.int32, (nk, LANES), 1), SSM_GROUP)
    for j in range(D_SSM // LANES):
        for half in range(SSD_T // GROUPS_PER_VREG):
            xs = [u_ref[j, _row_sel(GROUPS_PER_VREG * half + sl, nk, strided, row0), :]
                  for sl in range(GROUPS_PER_VREG)]
            ws = _group_transpose8(xs, lane_group)
            for gq in range(GROUPS_PER_VREG):
                put(GROUPS_PER_VREG * j + gq, half, ws[gq].astype(BF16))


def _from_groups(get, y_ref, nk, strided):
    lane_group = _pdiv(lax.broadcasted_iota(jnp.int32, (nk, LANES), 1), SSM_GROUP)
    for j in range(D_SSM // LANES):
        for half in range(SSD_T // GROUPS_PER_VREG):
            ws = [get(GROUPS_PER_VREG * j + gq, half).astype(F32) for gq in range(GROUPS_PER_VREG)]
            xs = _group_transpose8(ws, lane_group)
            for sl in range(GROUPS_PER_VREG):
                y_ref[j, _row_sel(GROUPS_PER_VREG * half + sl, nk, strided), :] = xs[sl]


def _cmul(re_full, im_signed, z):
    return re_full * z + im_signed * pltpu.roll(z, SSM_STATE, z.ndim - 1)


def _in_proj(x, g1_ref, win_ref, cos_ref, sin_ref):
    rows = x.shape[0]
    hn = _rms(x, g1_ref[...]).astype(BF16)
    proj = jnp.dot(hn, win_ref[...], preferred_element_type=F32)
    cos = cos_ref[...]
    sin = sin_ref[...]
    lane = lax.broadcasted_iota(jnp.int32, (rows, LANES), 1)
    first_half = _pmod(lane, HEAD_DIM) < (HEAD_DIM // 2)
    scale = HEAD_DIM ** -0.5
    qs = [_rope(proj[:, LANES * j:LANES * (j + 1)], cos, sin, first_half) * scale for j in range(D_ATTN // LANES)]
    k = _rope(proj[:, D_ATTN:D_ATTN + KV_W], cos, sin, first_half)
    v = proj[:, D_ATTN + KV_W:D_ATTN + 2 * KV_W]
    u = proj[:, D_ATTN + 2 * KV_W:]
    return qs, k, v, u


def _kv_variants(k, v):
    lane = lax.broadcasted_iota(jnp.int32, k.shape, 1)
    lo = lane < HEAD_DIM
    out = []
    for t in (k, v):
        tr = pltpu.roll(t, HEAD_DIM, 1)
        zero = jnp.zeros_like(t)
        out += [jnp.where(lo, t, zero), jnp.where(lo, zero, tr), jnp.where(lo, tr, zero), jnp.where(lo, zero, t)]
    return [o.astype(BF16) for o in out]


def _k_variants(k):
    lo = lax.broadcasted_iota(jnp.int32, k.shape, 1) < HEAD_DIM
    kr = pltpu.roll(k, HEAD_DIM, 1)
    zero = jnp.zeros_like(k)
    out = [jnp.where(lo, k, zero), jnp.where(lo, zero, kr), jnp.where(lo, kr, zero), jnp.where(lo, zero, k)]
    return [o.astype(BF16) for o in out]


def _softmax_pv_t(s, vt_win, sink_row):
    m = jnp.maximum(jnp.max(s, axis=0, keepdims=True), sink_row)
    e = jnp.exp(s - m)
    den = jnp.sum(e, axis=0, keepdims=True) + jnp.exp(sink_row - m)
    o = jnp.dot(vt_win, e.astype(BF16), preferred_element_type=F32)
    return o * (1.0 / den)


def _attend(a_bf, kvar, vvar, valid, sink_col):
    s = lax.dot_general(a_bf, kvar, (((1,), (1,)), ((), ())), preferred_element_type=F32)
    for cond in valid:
        s = jnp.where(cond, s, -jnp.inf)
    m = jnp.maximum(jnp.max(s, axis=-1, keepdims=True), sink_col)
    e = jnp.exp(s - m)
    den = jnp.sum(e, axis=-1, keepdims=True) + jnp.exp(sink_col - m)
    o = jnp.dot(e.astype(BF16), vvar, preferred_element_type=F32)
    return o * (1.0 / den)


def _prompt_in_kernel(sinks_ref, x_ref, g1_ref, win_ref, cos_ref, sin_ref, freq_ref,
                      a_ref, ug_ref, kp_ref, vp_ref, q_s, k_s, vb_s, vt_s, u_s, *, sb):
    tb = x_ref.shape[1]
    n_sub = tb // sb
    nks = sb // SSD_T
    blk = pl.program_id(1)
    nbuf = WINDOW + tb
    cps = sb // CHUNK
    n_chunks = tb // CHUNK

    @pl.when(blk == 0)
    def _():
        k_s[:, 0:WINDOW, :] = jnp.zeros((4, WINDOW, LANES), BF16)
        vb_s[0:WINDOW, :] = jnp.zeros((WINDOW, LANES), F32)

    @pl.when(blk > 0)
    def _():
        k_s[:, 0:WINDOW, :] = k_s[:, tb:nbuf, :]
        vb_s[0:WINDOW, :] = vb_s[tb:nbuf, :]

    lane = lax.broadcasted_iota(jnp.int32, (sb, LANES), 1)
    first_half = _pmod(lane, HEAD_DIM) < (HEAD_DIM // 2)
    scale = HEAD_DIM ** -0.5
    ang0 = (blk * tb).astype(F32) * freq_ref[...]
    cos0, sin0 = jnp.cos(ang0), jnp.sin(ang0)
    half_sign = jnp.where(first_half[0:1, :], -1.0, 1.0).astype(F32)

    def project(s):
        rows = slice(s * sb, (s + 1) * sb)
        brows = slice(WINDOW + s * sb, WINDOW + (s + 1) * sb)
        state = {}

        def piece(c0):
            if 'hn' not in state:
                state['hn'] = _rms(x_ref[0, rows, :], g1_ref[...]).astype(BF16)
                cb, sn = cos_ref[rows, :], sin_ref[rows, :]
                state['cos'] = cos0 * cb - sin0 * sn
                state['sin'] = (sin0 * cb + cos0 * sn) * half_sign
            return jnp.dot(state['hn'], win_ref[:, c0:c0 + 2 * LANES], preferred_element_type=F32)

        def q_piece(j):
            def run():
                p = piece(2 * LANES * j)
                for h in range(2):
                    q_s[rows, LANES * (2 * j + h):LANES * (2 * j + h + 1)] = _rope(
                        p[:, LANES * h:LANES * (h + 1)], state['cos'], state['sin'], first_half) * scale
            return run

        def kv_piece():
            p = piece(D_ATTN)
            k = _rope(p[:, :KV_W], state['cos'], state['sin'], first_half)
            v = p[:, KV_W:]
            for idx, arr in enumerate(_k_variants(k)):
                k_s[idx, brows, :] = arr
            vb_s[brows, :] = v
            if s == n_sub - 1:
                kp_ref[0] = k[sb - WINDOW:, :]
                vp_ref[0] = v[sb - WINDOW:, :]

        def u_piece(j):
            def run():
                p = piece(D_ATTN + 2 * KV_W + 2 * LANES * j)
                for h in range(2):
                    u_s[2 * j + h, rows, :] = p[:, LANES * h:LANES * (h + 1)]
            return run

        return [kv_piece, q_piece(0), q_piece(1), u_piece(0), u_piece(1)]

    vt_tail = {}

    def build_vt(s):
        lo = 0 if s == 0 else WINDOW + s * sb
        hi = WINDOW + (s + 1) * sb
        piece = vb_s[lo:hi, :].T
        src = piece if s == 0 else jnp.concatenate([vt_tail['v'], piece], axis=1)
        lo1 = lo if s == 0 else lo - LANES
        shifted = pltpu.roll(src, src.shape[1] - CHUNK, 1)
        vt_tail['v'] = piece[:, piece.shape[1] - LANES:]
        for sh, arr, b0 in ((0, piece, lo), (1, shifted, lo1)):
            w = arr.shape[1]
            zrows = jnp.zeros((HEAD_DIM, w), F32)
            for g in range(N_KV_HEADS):
                rows = arr[HEAD_DIM * g:HEAD_DIM * (g + 1), :]
                vt_s[sh, g, 0, :, b0:b0 + w] = jnp.concatenate([rows, zrows], axis=0).astype(BF16)
                vt_s[sh, g, 1, :, b0:b0 + w] = jnp.concatenate([zrows, rows], axis=0).astype(BF16)

    def regroup(s):
        def put(g, half, val):
            ug_ref[g, 0, s * nks:(s + 1) * nks, LANES * half:LANES * (half + 1)] = val
        _to_groups(u_s, put, nks, strided=True, row0=s * sb)

    nkeys = WINDOW + CHUNK
    key_row = lax.broadcasted_iota(jnp.int32, (nkeys, 1), 0)
    q_lane = lax.broadcasted_iota(jnp.int32, (1, LANES), 1)

    def scores(i):
        r0 = i * CHUNK
        valid = key_row >= (WINDOW - r0 - blk * tb) if i < N_WIN_CHUNKS else None
        out = []
        for g in range(N_KV_HEADS):
            qa = q_s[r0:r0 + CHUNK, 2 * LANES * g:2 * LANES * g + LANES]
            qb = q_s[r0:r0 + CHUNK, 2 * LANES * g + LANES:2 * LANES * (g + 1)]
            a_bf = jnp.concatenate([qa, qb], axis=0).astype(BF16)
            for par in range(2):
                s = lax.dot_general(k_s[2 * g + par, r0:r0 + nkeys, :], a_bf, (((1,), (1,)), ((), ())),
                                    preferred_element_type=F32)
                out.append(s if valid is None else jnp.where(valid, s, -jnp.inf))
        return out

    for item in project(0):
        item()
    build_vt(0)
    staging = []
    for s in range(1, n_sub):
        staging += [(s, item) for item in project(s)] + [(s, functools.partial(build_vt, s))]
    regroups = [functools.partial(regroup, s) for s in range(n_sub)]
    pending = scores(0)
    for i in range(n_chunks):
        if i + 1 < n_chunks and (i + 1) % cps == 0:
            while staging and staging[0][0] <= (i + 1) // cps:
                staging.pop(0)[1]()
        nxt = scores(i + 1) if i + 1 < n_chunks else None
        if staging:
            staging.pop(0)[1]()
        elif regroups:
            regroups.pop(0)()
        r0 = i * CHUNK
        sh = i % 2
        base = r0 - sh * CHUNK
        for g in range(N_KV_HEADS):
            acc = None
            for par in range(2):
                h0 = 4 * g + par
                sink_row = jnp.where(q_lane < CHUNK, sinks_ref[h0], sinks_ref[h0 + 2])
                o = _softmax_pv_t(pending[2 * g + par], vt_s[sh, g, par, :, base:base + nkeys], sink_row)
                acc = o if acc is None else acc + o
            at = acc.T
            a_ref[0, r0:r0 + CHUNK, 2 * LANES * g:2 * LANES * g + LANES] = at[:CHUNK]
            a_ref[0, r0:r0 + CHUNK, 2 * LANES * g + LANES:2 * LANES * (g + 1)] = at[CHUNK:]
        pending = nxt
    for item in regroups:
        item()


def _prompt_ssd_kernel(ug_ref, m_ref, wd_ref, r_ref, tab_ref, abar_ref, wup_f_ref, wdown_f_ref,
                       yg_ref, sfin_ref, wup_b_ref, wdown_b_ref, d_s, sp_s, carry_s):
    nb = ug_ref.shape[1]
    nc = ug_ref.shape[2]
    step = pl.program_id(0)
    wup_b_ref[...] = wup_f_ref[...].astype(BF16)
    wdown_b_ref[...] = wdown_f_ref[...].astype(BF16)

    @pl.when(step == 0)
    def _():
        carry_s[...] = jnp.zeros(carry_s.shape, F32)

    def state_in(g, c):
        ub = ug_ref[g].reshape(nb * nc, SSD_W)
        d_s[g] = jnp.dot(ub, wd_ref[g], preferred_element_type=F32)
        return c

    lax.fori_loop(0, N_GROUPS, state_in, 0, unroll=GROUP_UNROLL)

    hp = N_GROUPS // 2
    lo_half = lax.broadcasted_iota(jnp.int32, (1, 1, STATE_W), 2) < SSM_STATE

    def split(z):
        a, b = z[:hp], z[hp:]
        return (jnp.where(lo_half, a, pltpu.roll(b, SSM_STATE, 2)),
                jnp.where(lo_half, pltpu.roll(a, SSM_STATE, 2), b))

    def merge(re, im):
        return jnp.concatenate([jnp.where(lo_half, re, pltpu.roll(im, SSM_STATE, 2)),
                                jnp.where(lo_half, pltpu.roll(re, SSM_STATE, 2), im)], axis=0)

    def pair_tables(tre, tim):
        return jnp.where(lo_half, tre[:hp], tre[hp:]), jnp.where(lo_half, -tim[:hp], tim[hp:])

    pw = [pair_tables(tab_ref[2 * i], tab_ref[2 * i + 1]) for i in range(4)]
    ar, ai = pair_tables(abar_ref[0], abar_ref[1])
    kio = lax.broadcasted_iota(jnp.int32, (N_GROUPS, nc, STATE_W), 1)
    for b in range(nb):
        dg = d_s[:, b * nc:(b + 1) * nc, :]
        s_in = carry_s[b]
        xr_all, xi_all = split(jnp.where(kio == 0, s_in, pltpu.roll(dg, 1, 1)))
        cr = ci = jnp.zeros((hp, 1, STATE_W), F32)
        for j in range(nc // SUBLANES):
            xr = xr_all[:, SUBLANES * j:SUBLANES * (j + 1), :]
            xi = xi_all[:, SUBLANES * j:SUBLANES * (j + 1), :]
            for lvl in range(3):
                pr, pi = pw[lvl]
                sr, si = pltpu.roll(xr, 1 << lvl, 1), pltpu.roll(xi, 1 << lvl, 1)
                xr, xi = xr + pr * sr - pi * si, xi + pr * si + pi * sr
            qr, qi = pw[3]
            crb, cib = jnp.broadcast_to(cr, xr.shape), jnp.broadcast_to(ci, xi.shape)
            hr, hi = xr + qr * crb - qi * cib, xi + qr * cib + qi * crb
            sp_s[:, b * nc + SUBLANES * j:b * nc + SUBLANES * (j + 1), :] = merge(hr, hi)
            cr, ci = hr[:, SUBLANES - 1:SUBLANES, :], hi[:, SUBLANES - 1:SUBLANES, :]
        carry_s[b] = merge(ar * cr - ai * ci, ar * ci + ai * cr) + dg[:, nc - 1:nc, :]

    def readout(g, c):
        ub = ug_ref[g].reshape(nb * nc, SSD_W)
        y = jnp.dot(ub, m_ref[g], preferred_element_type=F32)
        y = y + lax.dot_general(sp_s[g].astype(BF16), r_ref[g], (((1,), (1,)), ((), ())),
                                preferred_element_type=F32)
        yg_ref[g] = y.reshape(nb, nc, SSD_W).astype(BF16)
        return c

    lax.fori_loop(0, N_GROUPS, readout, 0, unroll=GROUP_UNROLL)
    sfin_ref[...] = carry_s[...]


def _out_kernel(*refs, strided, cs, has_prev):
    if has_prev:
        x_ref, a_ref, yg_ref, prev_ref = refs[:4]
        refs = refs[4:]
    else:
        x_ref, a_ref, yg_ref = refs[:3]
        prev_ref = None
        refs = refs[3:]
    (wglu_ref, ga_ref, gs_ref, wout_ref, g2_ref, wup_ref, cw_ref, cb_ref, wdown_ref, gf_ref,
     y_ref, cst_ref, ys_s, ext_s, px_s, py_s) = refs
    tb = x_ref.shape[1]
    n_sub, _, sb, _ = ys_s.shape
    nks = sb // SSD_T
    planes = strided
    pr = sb // SUBLANES
    pad = SUBLANES if planes else ext_s.shape[0] - sb
    blk = pl.program_id(1)
    assert n_sub * sb == tb and (not planes or (cs == 1 and not has_prev))

    @pl.when(blk == 0)
    def _():
        if planes:
            ext_s[:, 0:pad, :] = jnp.zeros((2, pad, 2 * D_FF), F32)
        elif has_prev:
            ext_s[0:pad, :] = prev_ref[0]
        else:
            ext_s[0:pad, :] = jnp.zeros((pad, 2 * D_FF), F32)

    def mix(i):
        rows = slice(i * sb, (i + 1) * sb)
        _from_groups(lambda g, half: yg_ref[g, 0, i * nks:(i + 1) * nks, LANES * half:LANES * (half + 1)],
                     ys_s.at[i], nks, strided)
        yv = jnp.concatenate([ys_s[i, j] for j in range(D_SSM // LANES)], axis=-1)
        z = 0.5 * yv * (1.0 + jnp.tanh(math.sqrt(2.0 / math.pi) * (yv + 0.044715 * (yv * yv * yv))))
        gate = jnp.dot(z.astype(BF16), wglu_ref[...], preferred_element_type=F32)
        s_out = z * jax.nn.sigmoid(gate)
        na = _rms(a_ref[0, rows, :], ga_ref[...]).astype(BF16)
        ns = _rms(s_out, gs_ref[...]).astype(BF16)
        x1 = x_ref[0, rows, :] + jnp.dot(na, wout_ref[0:D_ATTN, :], preferred_element_type=F32)
        return x1 + jnp.dot(ns, wout_ref[D_ATTN:, :], preferred_element_type=F32)

    def tile_cols(j, half):
        return slice(half * D_FF + j * FF_TILE, half * D_FF + (j + 1) * FF_TILE)

    def to_planes(x):
        for c in range(D_MODEL // LANES):
            px_s[c] = x[:, LANES * c:LANES * (c + 1)]
        return jnp.concatenate(
            [jnp.concatenate([px_s[c, pl.ds(r, pr, stride=SUBLANES), :] for r in range(SUBLANES)], axis=0)
             for c in range(D_MODEL // LANES)], axis=1)

    def store_rows(i, y):
        rows = slice(i * sb, (i + 1) * sb)
        if not planes:
            y_ref[0, rows, :] = y
            return
        for c in range(D_MODEL // LANES):
            for r in range(SUBLANES):
                py_s[c, pl.ds(r, pr, stride=SUBLANES), :] = y[r * pr:(r + 1) * pr, LANES * c:LANES * (c + 1)]
            y_ref[0, rows, LANES * c:LANES * (c + 1)] = py_s[c]

    def time_shifts(up, cols):
        if not planes:
            ext_s[pad:pad + sb, cols] = up
            sh2 = ext_s[pad - 2 * cs:pad - 2 * cs + sb, cols]
            sh1 = ext_s[pad - cs:pad - cs + sb, cols]
            ext_s[0:pad, cols] = ext_s[sb:sb + pad, cols]
            return sh2, sh1
        for p in range(2):
            ext_s[p, pad:pad + pr, cols] = up[(6 + p) * pr:(7 + p) * pr, :]
        s6 = ext_s[0, pad - 1:pad - 1 + pr, cols]
        s7 = ext_s[1, pad - 1:pad - 1 + pr, cols]
        ext_s[:, 0:pad, cols] = ext_s[:, pr:pr + pad, cols]
        sh1 = jnp.concatenate([s7, up[:7 * pr, :]], axis=0)
        sh2 = jnp.concatenate([s6, s7, up[:6 * pr, :]], axis=0)
        return sh2, sh1

    def conv_ffn(i, x1):
        if planes:
            x1 = to_planes(x1)
        h2 = _rms(x1, g2_ref[...]).astype(BF16)

        def up_proj(j):
            return [jnp.dot(h2, wup_ref[:, tile_cols(j, half)], preferred_element_type=F32) for half in range(2)]

        def down_proj(act, j):
            return jnp.dot(act, wdown_ref[j * FF_TILE:(j + 1) * FF_TILE, :], preferred_element_type=F32)

        n_tiles = D_FF // FF_TILE
        acc = jnp.zeros((sb, D_MODEL), F32)
        ups = up_proj(0)
        act_prev = None
        for j in range(n_tiles):
            nxt = up_proj(j + 1) if j + 1 < n_tiles else None
            if act_prev is not None:
                acc = acc + down_proj(act_prev, j - 1)
            parts = []
            for half in range(2):
                cols = tile_cols(j, half)
                sh2, sh1 = time_shifts(ups[half], cols)
                parts.append(sh2 * cw_ref[0:1, cols] + sh1 * cw_ref[1:2, cols]
                             + ups[half] * cw_ref[2:3, cols] + cb_ref[:, cols])
            cg, cv = parts
            act_prev = (cg * jax.nn.sigmoid(cg) * cv).astype(BF16)
            ups = nxt
        acc = acc + down_proj(act_prev, n_tiles - 1)
        store_rows(i, _rms(x1 + acc, gf_ref[...]))

    mixed = mix(0)
    for i in range(n_sub):
        nxt_mixed = mix(i + 1) if i + 1 < n_sub else None
        conv_ffn(i, mixed)
        mixed = nxt_mixed
    if planes:
        cst_ref[0] = ext_s[:, 0:pad, :]
    else:
        cst_ref[0] = ext_s[0:pad, :]


def _decode_in_kernel(sinks_ref, x_ref, g1_ref, win_ref, cos_ref, sin_ref, kc_ref, vc_ref, h0_ref,
                      m_ref, wd_ref, r_ref, abar_ref,
                      a_ref, kn_ref, vn_ref, yg_ref, sn_ref, u_s, ug_s, *, n_streams, n_steps, n_cache):
    rows = n_streams * n_steps
    qs, k, v, u = _in_proj(x_ref[...], g1_ref, win_ref, cos_ref, sin_ref)
    kn_ref[...] = k
    vn_ref[...] = v
    for j in range(D_SSM // LANES):
        u_s[j] = u[:, LANES * j:LANES * (j + 1)]
    ncache_rows = n_streams * n_cache
    nkeys = ncache_rows + rows
    kall = jnp.concatenate([kc_ref[...], k], axis=0)
    vall = jnp.concatenate([vc_ref[...], v], axis=0)
    variants = _kv_variants(kall, vall)

    colv = lax.broadcasted_iota(jnp.int32, (1, nkeys), 1)
    is_new = colv >= ncache_rows
    cnew = colv - ncache_rows
    k_stream = jnp.where(is_new, _pmod(cnew, n_streams), _pdiv(colv, n_cache))
    k_pos = jnp.where(is_new, PAST_LEN + _pdiv(cnew, n_streams), PAST_LEN - n_cache + _pmod(colv, n_cache))
    k_chunk = _pdiv(k_pos, CHUNK)
    rowv = _pmod(lax.broadcasted_iota(jnp.int32, (2 * rows, 1), 0), rows)
    q_stream = _pmod(rowv, n_streams)
    q_chunk = _pdiv(PAST_LEN + _pdiv(rowv, n_streams), CHUNK)
    ok_col = jnp.where(k_pos >= 0, k_stream, -1)
    d_chunk = q_chunk - k_chunk
    valid = [q_stream == ok_col,
             lax.bitcast_convert_type(d_chunk, jnp.uint32) <= jnp.uint32(N_WIN_CHUNKS)]
    top = lax.broadcasted_iota(jnp.int32, (2 * rows, 1), 0) < rows

    for g in range(N_KV_HEADS):
        a_bf = jnp.concatenate([qs[2 * g], qs[2 * g + 1]], axis=0).astype(BF16)
        acc = jnp.zeros((2 * rows, LANES), F32)
        for par in range(2):
            h0 = 4 * g + par
            sink_col = jnp.where(top, sinks_ref[h0], sinks_ref[h0 + 2])
            acc = acc + _attend(a_bf, variants[2 * g + par], variants[4 + 2 * g + par], valid, sink_col)
        a_ref[:, 2 * LANES * g:2 * LANES * g + LANES] = acc[:rows]
        a_ref[:, 2 * LANES * g + LANES:2 * LANES * (g + 1)] = acc[rows:]

    def put(g, half, val):
        ug_s[g, :, LANES * half:LANES * (half + 1)] = val

    _to_groups(u_s, put, n_streams, strided=False)
    def group_body(g, c):
        ub = ug_s[g]
        h0g = h0_ref[g]
        d = jnp.dot(ub, wd_ref[g], preferred_element_type=F32)
        sn_ref[g] = _cmul(abar_ref[0, g], abar_ref[1, g], h0g) + d
        y = jnp.dot(ub, m_ref[g], preferred_element_type=F32)
        y = y + lax.dot_general(h0g.astype(BF16), r_ref[g], (((1,), (1,)), ((), ())),
                                preferred_element_type=F32)
        yg_ref[g] = y.astype(BF16)
        return c

    lax.fori_loop(0, N_GROUPS, group_body, 0, unroll=DECODE_UNROLL)


def _ssd_tables(a_re, a_im, log_dt, b_re, b_im, c_re, c_im, d_skip):
    dup = lambda x: jnp.concatenate([x, x], axis=-1)
    prm = jnp.stack([dup(a_re), dup(a_im), jnp.broadcast_to(log_dt[:, None], (N_GROUPS, STATE_W))], axis=1)
    bt = jnp.concatenate([b_re.transpose(0, 2, 1), b_im.transpose(0, 2, 1)], axis=-1)
    cc = jnp.concatenate([c_re, c_im], axis=-1)
    dtile = jnp.tile(d_skip, (1, SSD_T))[:, None, :]
    gb = TABLE_GROUPS
    blk3 = lambda r, c: pl.BlockSpec((gb, r, c), lambda i: (i, 0, 0))
    return pl.pallas_call(
        _ssd_tables_kernel,
        grid=(N_GROUPS // gb,),
        in_specs=[blk3(3, STATE_W), blk3(SSM_GROUP, STATE_W), blk3(SSM_GROUP, STATE_W), blk3(1, SSD_W)],
        out_specs=[blk3(SSD_W, SSD_W), blk3(SSD_W, STATE_W), blk3(SSD_W, STATE_W),
                   pl.BlockSpec((8, gb, SUBLANES, STATE_W), lambda i: (0, i, 0, 0)),
                   pl.BlockSpec((2, gb, 1, STATE_W), lambda i: (0, i, 0, 0))],
        out_shape=[jax.ShapeDtypeStruct((N_GROUPS, SSD_W, SSD_W), BF16),
                   jax.ShapeDtypeStruct((N_GROUPS, SSD_W, STATE_W), BF16),
                   jax.ShapeDtypeStruct((N_GROUPS, SSD_W, STATE_W), BF16),
                   jax.ShapeDtypeStruct((8, N_GROUPS, SUBLANES, STATE_W), F32),
                   jax.ShapeDtypeStruct((2, N_GROUPS, 1, STATE_W), F32)],
        compiler_params=pltpu.CompilerParams(dimension_semantics=("arbitrary",)),
        name="ssd_tables",
    )(prm, bt, cc, dtile)


def _ssd_tables_kernel(prm_ref, bt_ref, cc_ref, dt_ref, m_ref, wd_ref, rt_ref, tab_ref, abar_ref):
    t = SSD_T
    lane = lax.broadcasted_iota(jnp.int32, (1, STATE_W), 1)
    sgn = jnp.where(lane >= SSM_STATE, 1.0, -1.0).astype(F32)
    conj = -sgn
    n_small = lax.broadcasted_iota(jnp.int32, (3 * SUBLANES, 1), 0).astype(F32)
    sub_i = lax.broadcasted_iota(jnp.int32, (SUBLANES, 1), 0)
    n_big = float(t) * (sub_i + 1).astype(F32)
    row = lax.broadcasted_iota(jnp.int32, (SSD_W, SSD_W), 0)
    col = lax.broadcasted_iota(jnp.int32, (SSD_W, SSD_W), 1)
    diag = row == col
    kcol = lax.broadcasted_iota(jnp.int32, (SSM_GROUP, SSD_W), 1)
    for gi in range(prm_ref.shape[0]):
        a_re, a_im = prm_ref[gi, 0:1, :], prm_ref[gi, 1:2, :]
        dt = jnp.exp(prm_ref[gi, 2:3, :])
        lr, li = dt * a_re, dt * a_im

        def power(n):
            mag = jnp.exp(n * lr)
            return mag * jnp.cos(n * li), mag * jnp.sin(n * li) * sgn

        pr_s, pis_s = power(n_small)
        pr_b, pis_b = power(n_big)

        def times_powers(z, zs, exps):
            return jnp.concatenate([z * pr_s[e:e + 1, :] + zs * pis_s[e:e + 1, :] for e in exps], axis=0)

        ar, ais = pr_s[1:2, :], pis_s[1:2, :]
        den = a_re * a_re + a_im * a_im
        nr, ni = ar - 1.0, ais * sgn
        fr, fis = (nr * a_re + ni * a_im) / den, (ni * a_re - nr * a_im) / den * sgn
        bt = bt_ref[gi]
        bb = bt * fr + pltpu.roll(bt, SSM_STATE, 1) * fis
        bbs = pltpu.roll(bb, SSM_STATE, 1)
        cc = cc_ref[gi]
        ccs = pltpu.roll(cc, SSM_STATE, 1)
        steps = range(t)
        y = times_powers(cc, ccs, list(steps))
        kt = lax.dot_general(bb * conj, y, (((1,), (1,)), ((), ())), precision=lax.Precision.HIGHEST,
                             preferred_element_type=F32)
        m = jnp.concatenate(
            [kt if s == 0 else jnp.where(kcol >= SSM_GROUP * s, pltpu.roll(kt, SSM_GROUP * s, 1), 0.0)
             for s in steps], axis=0)
        m_ref[gi] = (m + jnp.where(diag, dt_ref[gi], 0.0)).astype(BF16)
        wd_ref[gi] = times_powers(bb, bbs, [t - 1 - s for s in steps]).astype(BF16)
        rt_ref[gi] = (times_powers(cc, ccs, [s + 1 for s in steps]) * conj).astype(BF16)
        for lvl in range(3):
            d = 1 << lvl
            tab_ref[2 * lvl, gi] = jnp.where(sub_i >= d, pr_b[d - 1:d, :], 0.0)
            tab_ref[2 * lvl + 1, gi] = jnp.where(sub_i >= d, pis_b[d - 1:d, :], 0.0)
        tab_ref[6, gi] = pr_b
        tab_ref[7, gi] = pis_b
        abar_ref[0, gi] = pr_b[0:1, :]
        abar_ref[1, gi] = pis_b[0:1, :]


def _rope_freq():
    half = HEAD_DIM // 2
    return ROPE_THETA ** (-jnp.asarray(np.arange(LANES) % half, F32) / half)


def _rope_tables(pos):
    sign = jnp.asarray(np.where(np.arange(LANES) % HEAD_DIM < HEAD_DIM // 2, -1.0, 1.0), F32)
    ang = pos.astype(F32)[:, None] * _rope_freq()[None, :]
    return jnp.cos(ang), jnp.sin(ang) * sign[None, :]


def _rope_base(rows):
    freq = _rope_freq()[None, :]
    ang = jnp.arange(rows, dtype=F32)[:, None] * freq
    return jnp.cos(ang), jnp.sin(ang), freq


def _const(shape):
    nd = len(shape)
    return pl.BlockSpec(shape, lambda *_: (0,) * nd, pipeline_mode=pl.Buffered(1))


def _whole(shape):
    nd = len(shape)
    return pl.BlockSpec(shape, lambda *_: (0,) * nd)


def _out_call(x, a, yg, prev, w, *, tb, sb, strided, cs, pad):
    nb, length, _ = x.shape
    nk = tb // SSD_T
    grid = (nb, length // tb)
    in_specs = [
        pl.BlockSpec((1, tb, D_MODEL), lambda b, i: (b, i, 0)),
        pl.BlockSpec((1, tb, D_ATTN), lambda b, i: (b, i, 0)),
        pl.BlockSpec((N_GROUPS, 1, nk, SSD_W), lambda b, i: (0, b, i, 0)),
    ]
    args = [x, a, yg]
    if prev is not None:
        in_specs.append(pl.BlockSpec((1, pad, 2 * D_FF), lambda b, i: (b, 0, 0)))
        args.append(prev)
    weights = [w['w_glu'], w['onorm_a'], w['onorm_s'], w['w_out'], w['norm2'], w['w_up'], w['conv_w'],
               w['conv_b'], w['w_down'], w['final_g']]
    in_specs += [_const(t.shape) for t in weights]
    kern = functools.partial(_out_kernel, strided=strided, cs=cs, has_prev=prev is not None)
    if strided:
        ext_shape = (2, pad + sb // SUBLANES, 2 * D_FF)
        cst_block, cst_map = (1, 2, pad, 2 * D_FF), (lambda b, i: (b, 0, 0, 0))
    else:
        ext_shape = (pad + sb, 2 * D_FF)
        cst_block, cst_map = (1, pad, 2 * D_FF), (lambda b, i: (b, 0, 0))
    return pl.pallas_call(
        kern,
        grid=grid,
        in_specs=in_specs,
        out_specs=[pl.BlockSpec((1, tb, D_MODEL), lambda b, i: (b, i, 0)),
                   pl.BlockSpec(cst_block, cst_map)],
        out_shape=[jax.ShapeDtypeStruct((nb, length, D_MODEL), F32),
                   jax.ShapeDtypeStruct((nb,) + cst_block[1:], F32)],
        scratch_shapes=[pltpu.VMEM((tb // sb, D_SSM // LANES, sb, LANES), F32),
                        pltpu.VMEM(ext_shape, F32),
                        pltpu.VMEM((D_MODEL // LANES, sb, LANES), F32),
                        pltpu.VMEM((D_MODEL // LANES, sb, LANES), F32)],
        compiler_params=pltpu.CompilerParams(dimension_semantics=("arbitrary", "arbitrary"),
                                             vmem_limit_bytes=VMEM_LIMIT),
        name="layer_out",
    )(*args, *weights)


def kernel(x_prompt, x_sample, cache_k, cache_v, state_ssm_re, state_ssm_im, state_conv, norm1_g, w_in, attn_sinks, ssm_A_re, ssm_A_im, ssm_log_dt, ssm_B_re, ssm_B_im, ssm_C_re, ssm_C_im, ssm_D, w_glu, onorm_attn_g, onorm_ssm_g, w_out, norm2_g, w_up, conv_w, conv_b, w_down, final_g):
    assert norm1_g.shape[0] == 1, "one layer"
    nb, length, _ = x_prompt.shape
    ns, nt, _ = x_sample.shape
    n_cache = cache_k.shape[2]
    assert nt == SSD_T and length % TB_IN == 0 and length % TB_OUT == 0 and (length // SSD_T) % NC_SSD == 0

    m_mat, wd_mat, r_mat, tab, abar = _ssd_tables(ssm_A_re[0], ssm_A_im[0], ssm_log_dt[0], ssm_B_re[0],
                                                  ssm_B_im[0], ssm_C_re[0], ssm_C_im[0], ssm_D[0])
    sinks = attn_sinks[0]
    g1 = norm1_g
    win = w_in[0].astype(BF16)
    w = dict(w_glu=w_glu[0].astype(BF16), onorm_a=onorm_attn_g, onorm_s=onorm_ssm_g, w_out=w_out[0].astype(BF16),
             norm2=norm2_g, conv_w=conv_w[0], conv_b=conv_b, final_g=final_g[None, :])
    smem = pl.BlockSpec(memory_space=pltpu.SMEM)

    cos_p, sin_p, freq = _rope_base(TB_IN)
    nk = TB_IN // SSD_T
    a_p, ug_p, k_p, v_p = pl.pallas_call(
        functools.partial(_prompt_in_kernel, sb=SUB_IN),
        grid=(nb, length // TB_IN),
        in_specs=[smem,
                  pl.BlockSpec((1, TB_IN, D_MODEL), lambda b, i: (b, i, 0)),
                  _const((1, D_MODEL)), _const((D_MODEL, D_IN)),
                  _const((TB_IN, LANES)), _const((TB_IN, LANES)), _const((1, LANES))],
        out_specs=[pl.BlockSpec((1, TB_IN, D_ATTN), lambda b, i: (b, i, 0)),
                   pl.BlockSpec((N_GROUPS, 1, nk, SSD_W), lambda b, i: (0, b, i, 0)),
                   pl.BlockSpec((1, WINDOW, KV_W), lambda b, i: (b, 0, 0)),
                   pl.BlockSpec((1, WINDOW, KV_W), lambda b, i: (b, 0, 0))],
        out_shape=[jax.ShapeDtypeStruct((nb, length, D_ATTN), F32),
                   jax.ShapeDtypeStruct((N_GROUPS, nb, length // SSD_T, SSD_W), BF16),
                   jax.ShapeDtypeStruct((nb, WINDOW, KV_W), F32),
                   jax.ShapeDtypeStruct((nb, WINDOW, KV_W), F32)],
        scratch_shapes=[pltpu.VMEM((TB_IN, D_ATTN), F32),
                        pltpu.VMEM((2 * N_KV_HEADS, WINDOW + TB_IN, LANES), BF16),
                        pltpu.VMEM((WINDOW + TB_IN, LANES), F32),
                        pltpu.VMEM((2, N_KV_HEADS, 2, LANES, WINDOW + TB_IN), BF16),
                        pltpu.VMEM((D_SSM // LANES, TB_IN, LANES), F32)],
        compiler_params=pltpu.CompilerParams(dimension_semantics=("arbitrary", "arbitrary"),
                                             vmem_limit_bytes=VMEM_LIMIT),
        name="prompt_in",
    )(sinks, x_prompt, g1, win, cos_p, sin_p, freq)

    n_ssd = length // SSD_T // NC_SSD
    up_rows, down_rows = D_MODEL // n_ssd, D_FF // n_ssd
    yg_p, sfin_p, w['w_up'], w['w_down'] = pl.pallas_call(
        _prompt_ssd_kernel,
        grid=(n_ssd,),
        in_specs=[pl.BlockSpec((N_GROUPS, nb, NC_SSD, SSD_W), lambda i: (0, 0, i, 0)),
                  _const(m_mat.shape), _const(wd_mat.shape), _const(r_mat.shape), _const(tab.shape),
                  _const(abar.shape),
                  pl.BlockSpec((up_rows, 2 * D_FF), lambda i: (i, 0)),
                  pl.BlockSpec((down_rows, D_MODEL), lambda i: (i, 0))],
        out_specs=[pl.BlockSpec((N_GROUPS, nb, NC_SSD, SSD_W), lambda i: (0, 0, i, 0)),
                   pl.BlockSpec((nb, N_GROUPS, 1, STATE_W), lambda i: (0, 0, 0, 0)),
                   pl.BlockSpec((up_rows, 2 * D_FF), lambda i: (i, 0)),
                   pl.BlockSpec((down_rows, D_MODEL), lambda i: (i, 0))],
        out_shape=[jax.ShapeDtypeStruct((N_GROUPS, nb, length // SSD_T, SSD_W), BF16),
                   jax.ShapeDtypeStruct((nb, N_GROUPS, 1, STATE_W), F32),
                   jax.ShapeDtypeStruct((D_MODEL, 2 * D_FF), BF16),
                   jax.ShapeDtypeStruct((D_FF, D_MODEL), BF16)],
        scratch_shapes=[pltpu.VMEM((N_GROUPS, nb * NC_SSD, STATE_W), F32),
                        pltpu.VMEM((N_GROUPS, nb * NC_SSD, STATE_W), F32),
                        pltpu.VMEM((nb, N_GROUPS, 1, STATE_W), F32)],
        compiler_params=pltpu.CompilerParams(dimension_semantics=("arbitrary",), vmem_limit_bytes=VMEM_LIMIT),
        name="prompt_ssd",
    )(ug_p, m_mat, wd_mat, r_mat, tab, abar, w_up[0], w_down[0])

    y_p, cst_p = _out_call(x_prompt, a_p, yg_p, None, w, tb=TB_OUT, sb=SUB_OUT, strided=True, cs=1, pad=SUBLANES)

    rows = ns * nt
    xs = x_sample.transpose(1, 0, 2).reshape(rows, D_MODEL)
    cos_s, sin_s = _rope_tables(PAST_LEN + jnp.arange(rows) // ns)
    kc = cache_k[0].reshape(ns * n_cache, KV_W)
    vc = cache_v[0].reshape(ns * n_cache, KV_W)
    h0 = jnp.concatenate([state_ssm_re[0], state_ssm_im[0]], axis=-1).transpose(1, 0, 2)
    dec = functools.partial(_decode_in_kernel, n_streams=ns, n_steps=nt, n_cache=n_cache)
    dec_in = [xs, g1, win, cos_s, sin_s, kc, vc, h0, m_mat, wd_mat, r_mat, abar]
    a_s, kn, vn, yg_s, sn = pl.pallas_call(
        dec,
        grid=(1,),
        in_specs=[smem] + [_const(t.shape) for t in dec_in],
        out_specs=[_whole((rows, D_ATTN)), _whole((rows, KV_W)), _whole((rows, KV_W)),
                   _whole((N_GROUPS, ns, SSD_W)), _whole((N_GROUPS, ns, STATE_W))],
        out_shape=[jax.ShapeDtypeStruct((rows, D_ATTN), F32), jax.ShapeDtypeStruct((rows, KV_W), F32),
                   jax.ShapeDtypeStruct((rows, KV_W), F32), jax.ShapeDtypeStruct((N_GROUPS, ns, SSD_W), BF16),
                   jax.ShapeDtypeStruct((N_GROUPS, ns, STATE_W), F32)],
        scratch_shapes=[pltpu.VMEM((D_SSM // LANES, rows, LANES), F32), pltpu.VMEM((N_GROUPS, ns, SSD_W), BF16)],
        compiler_params=pltpu.CompilerParams(dimension_semantics=("arbitrary",), vmem_limit_bytes=VMEM_LIMIT),
        name="decode_in",
    )(sinks, *dec_in)

    cs = ns
    prev = state_conv[0].transpose(1, 0, 2).reshape(1, (CONV_W - 1) * cs, 2 * D_FF)
    y_s, cst_s = _out_call(xs[None], a_s[None], yg_s[:, None], prev, w, tb=rows, sb=rows, strided=False, cs=cs,
                           pad=(CONV_W - 1) * cs)

    kv5 = lambda t, n: t.reshape(1, n, -1, N_KV_HEADS, HEAD_DIM)
    y_sample = y_s.reshape(nt, ns, D_MODEL).transpose(1, 0, 2)
    kn_b = kn.reshape(nt, ns, KV_W).transpose(1, 0, 2)
    vn_b = vn.reshape(nt, ns, KV_W).transpose(1, 0, 2)
    k_new = jnp.concatenate([cache_k[0].reshape(ns, n_cache, KV_W), kn_b], axis=1)[:, -n_cache:]
    v_new = jnp.concatenate([cache_v[0].reshape(ns, n_cache, KV_W), vn_b], axis=1)[:, -n_cache:]
    conv_p = cst_p[:, :, SUBLANES - 1, :][None]
    conv_s = cst_s.reshape(CONV_W - 1, ns, 2 * D_FF).transpose(1, 0, 2)[None]
    sn_b = sn.transpose(1, 0, 2)
    return (y_p, y_sample,
            kv5(k_p, nb), kv5(v_p, nb),
            sfin_p[None, :, :, 0, :SSM_STATE], sfin_p[None, :, :, 0, SSM_STATE:], conv_p,
            kv5(k_new, ns), kv5(v_new, ns),
            sn_b[None, :, :, :SSM_STATE], sn_b[None, :, :, SSM_STATE:], conv_s)
```

```python
import functools
import math

import numpy as np
import jax
import jax.numpy as jnp
from jax import lax
from jax.experimental import pallas as pl
from jax.experimental.pallas import tpu as pltpu

F32 = jnp.float32
BF16 = jnp.bfloat16

D_MODEL = 1024
CHUNK = 64
D_ATTN = 512
D_SSM = 512
HEAD_DIM = 64
N_HEADS = 8
N_KV_HEADS = 2
KV_W = N_KV_HEADS * HEAD_DIM
WINDOW = 128
N_WIN_CHUNKS = WINDOW // CHUNK
ROPE_THETA = 10000.0
SSM_GROUP = 16
N_GROUPS = D_SSM // SSM_GROUP
SSM_STATE = 64
D_FF = 2816
CONV_W = 3
EPS = 1e-6
D_IN = D_ATTN + 2 * KV_W + D_SSM
PAST_LEN = 2048

SSD_T = 16
LANES = 128
SUBLANES = 8
GROUPS_PER_VREG = LANES // SSM_GROUP
SSD_W = SSD_T * SSM_GROUP
STATE_W = 2 * SSM_STATE

TB_IN = 2048
SUB_IN = 512
TB_OUT = 512
SUB_OUT = 256
NC_SSD = 64
FF_TILE = 256
GROUP_UNROLL = 32
DECODE_UNROLL = 8
TABLE_GROUPS = 8
VMEM_LIMIT = 56 * 1024 * 1024


def _log2(n):
    assert n > 0 and n & (n - 1) == 0, n
    return n.bit_length() - 1


def _pdiv(x, n):
    return lax.shift_right_arithmetic(x, jnp.int32(_log2(n)))


def _pmod(x, n):
    return x & (n - 1)


def _rms(x, g):
    return x * lax.rsqrt(jnp.mean(x * x, axis=-1, keepdims=True) + EPS) * g


def _rope(xb, cos, sin_signed, first_half):
    partner = jnp.where(first_half, pltpu.roll(xb, LANES - HEAD_DIM // 2, 1), pltpu.roll(xb, HEAD_DIM // 2, 1))
    return xb * cos + partner * sin_signed


def _group_transpose8(xs, lane_group):
    xs = list(xs)
    for d in (4, 2, 1):
        keep = (lane_group & d) == 0
        nxt = list(xs)
        for i in range(GROUPS_PER_VREG):
            if i & d:
                continue
            a, b = xs[i], xs[i | d]
            nxt[i] = jnp.where(keep, a, pltpu.roll(b, SSM_GROUP * d, 1))
            nxt[i | d] = jnp.where(keep, pltpu.roll(a, LANES - SSM_GROUP * d, 1), b)
        xs = nxt
    return xs


def _row_sel(s, nk, strided, row0=0):
    if strided:
        return pl.ds(row0 + s, nk, stride=SSD_T)
    return pl.ds(row0 + s * nk, nk)


def _to_groups(u_ref, put, nk, strided, row0=0):
    lane_group = _pdiv(lax.broadcasted_iota(jnp.int32, (nk, LANES), 1), SSM_GROUP)
    for j in range(D_SSM // LANES):
        for half in range(SSD_T // GROUPS_PER_VREG):
            xs = [u_ref[j, _row_sel(GROUPS_PER_VREG * half + sl, nk, strided, row0), :]
                  for sl in range(GROUPS_PER_VREG)]
            ws = _group_transpose8(xs, lane_group)
            for gq in range(GROUPS_PER_VREG):
                put(GROUPS_PER_VREG * j + gq, half, ws[gq].astype(BF16))


def _from_groups(get, y_ref, nk, strided):
    lane_group = _pdiv(lax.broadcasted_iota(jnp.int32, (nk, LANES), 1), SSM_GROUP)
    for j in range(D_SSM // LANES):
        for half in range(SSD_T // GROUPS_PER_VREG):
            ws = [get(GROUPS_PER_VREG * j + gq, half).astype(F32) for gq in range(GROUPS_PER_VREG)]
            xs = _group_transpose8(ws, lane_group)
            for sl in range(GROUPS_PER_VREG):
                y_ref[j, _row_sel(GROUPS_PER_VREG * half + sl, nk, strided), :] = xs[sl]


def _cmul(re_full, im_signed, z):
    return re_full * z + im_signed * pltpu.roll(z, SSM_STATE, z.ndim - 1)


def _in_proj(x, g1_ref, win_ref, cos_ref, sin_ref):
    rows = x.shape[0]
    hn = _rms(x, g1_ref[...]).astype(BF16)
    proj = jnp.dot(hn, win_ref[...], preferred_element_type=F32)
    cos = cos_ref[...]
    sin = sin_ref[...]
    lane = lax.broadcasted_iota(jnp.int32, (rows, LANES), 1)
    first_half = _pmod(lane, HEAD_DIM) < (HEAD_DIM // 2)
    scale = HEAD_DIM ** -0.5
    qs = [_rope(proj[:, LANES * j:LANES * (j + 1)], cos, sin, first_half) * scale for j in range(D_ATTN // LANES)]
    k = _rope(proj[:, D_ATTN:D_ATTN + KV_W], cos, sin, first_half)
    v = proj[:, D_ATTN + KV_W:D_ATTN + 2 * KV_W]
    u = proj[:, D_ATTN + 2 * KV_W:]
    return qs, k, v, u


def _kv_variants(k, v):
    lane = lax.broadcasted_iota(jnp.int32, k.shape, 1)
    lo = lane < HEAD_DIM
    out = []
    for t in (k, v):
        tr = pltpu.roll(t, HEAD_DIM, 1)
        zero = jnp.zeros_like(t)
        out += [jnp.where(lo, t, zero), jnp.where(lo, zero, tr), jnp.where(lo, tr, zero), jnp.where(lo, zero, t)]
    return [o.astype(BF16) for o in out]


def _k_variants(k):
    lo = lax.broadcasted_iota(jnp.int32, k.shape, 1) < HEAD_DIM
    kr = pltpu.roll(k, HEAD_DIM, 1)
    zero = jnp.zeros_like(k)
    out = [jnp.where(lo, k, zero), jnp.where(lo, zero, kr), jnp.where(lo, kr, zero), jnp.where(lo, zero, k)]
    return [o.astype(BF16) for o in out]


def _softmax_pv_t(s, vt_win, sink_row):
    m = jnp.maximum(jnp.max(s, axis=0, keepdims=True), sink_row)
    e = jnp.exp(s - m)
    den = jnp.sum(e, axis=0, keepdims=True) + jnp.exp(sink_row - m)
    o = jnp.dot(vt_win, e.astype(BF16), preferred_element_type=F32)
    return o * (1.0 / den)


def _attend(a_bf, kvar, vvar, valid, sink_col):
    s = lax.dot_general(a_bf, kvar, (((1,), (1,)), ((), ())), preferred_element_type=F32)
    for cond in valid:
        s = jnp.where(cond, s, -jnp.inf)
    m = jnp.maximum(jnp.max(s, axis=-1, keepdims=True), sink_col)
    e = jnp.exp(s - m)
    den = jnp.sum(e, axis=-1, keepdims=True) + jnp.exp(sink_col - m)
    o = jnp.dot(e.astype(BF16), vvar, preferred_element_type=F32)
    return o * (1.0 / den)


def _prompt_in_kernel(sinks_ref, x_ref, g1_ref, win_ref, cos_ref, sin_ref, freq_ref,
                      a_ref, ug_ref, kp_ref, vp_ref, q_s, k_s, vb_s, vt_s, u_s, *, sb):
    tb = x_ref.shape[1]
    n_sub = tb // sb
    nks = sb // SSD_T
    blk = pl.program_id(1)
    nbuf = WINDOW + tb
    cps = sb // CHUNK
    n_chunks = tb // CHUNK

    @pl.when(blk == 0)
    def _():
        k_s[:, 0:WINDOW, :] = jnp.zeros((4, WINDOW, LANES), BF16)
        vb_s[0:WINDOW, :] = jnp.zeros((WINDOW, LANES), F32)

    @pl.when(blk > 0)
    def _():
        k_s[:, 0:WINDOW, :] = k_s[:, tb:nbuf, :]
        vb_s[0:WINDOW, :] = vb_s[tb:nbuf, :]

    lane = lax.broadcasted_iota(jnp.int32, (sb, LANES), 1)
    first_half = _pmod(lane, HEAD_DIM) < (HEAD_DIM // 2)
    scale = HEAD_DIM ** -0.5
    ang0 = (blk * tb).astype(F32) * freq_ref[...]
    cos0, sin0 = jnp.cos(ang0), jnp.sin(ang0)
    half_sign = jnp.where(first_half[0:1, :], -1.0, 1.0).astype(F32)

    def project(s):
        rows = slice(s * sb, (s + 1) * sb)
        brows = slice(WINDOW + s * sb, WINDOW + (s + 1) * sb)
        state = {}

        def piece(c0):
            if 'hn' not in state:
                state['hn'] = _rms(x_ref[0, rows, :], g1_ref[...]).astype(BF16)
                cb, sn = cos_ref[rows, :], sin_ref[rows, :]
                state['cos'] = cos0 * cb - sin0 * sn
                state['sin'] = (sin0 * cb + cos0 * sn) * half_sign
            return jnp.dot(state['hn'], win_ref[:, c0:c0 + 2 * LANES], preferred_element_type=F32)

        def q_piece(j):
            def run():
                p = piece(2 * LANES * j)
                for h in range(2):
                    q_s[rows, LANES * (2 * j + h):LANES * (2 * j + h + 1)] = _rope(
                        p[:, LANES * h:LANES * (h + 1)], state['cos'], state['sin'], first_half) * scale
            return run

        def kv_piece():
            p = piece(D_ATTN)
            k = _rope(p[:, :KV_W], state['cos'], state['sin'], first_half)
            v = p[:, KV_W:]
            for idx, arr in enumerate(_k_variants(k)):
                k_s[idx, brows, :] = arr
            vb_s[brows, :] = v
            if s == n_sub - 1:
                kp_ref[0] = k[sb - WINDOW:, :]
                vp_ref[0] = v[sb - WINDOW:, :]

        def u_piece(j):
            def run():
                p = piece(D_ATTN + 2 * KV_W + 2 * LANES * j)
                for h in range(2):
                    u_s[2 * j + h, rows, :] = p[:, LANES * h:LANES * (h + 1)]
            return run

        return [kv_piece, q_piece(0), q_piece(1), u_piece(0), u_piece(1)]

    vt_tail = {}

    def build_vt(s):
        lo = 0 if s == 0 else WINDOW + s * sb
        hi = WINDOW + (s + 1) * sb
        piece = vb_s[lo:hi, :].T
        src = piece if s == 0 else jnp.concatenate([vt_tail['v'], piece], axis=1)
        lo1 = lo if s == 0 else lo - LANES
        shifted = pltpu.roll(src, src.shape[1] - CHUNK, 1)
        vt_tail['v'] = piece[:, piece.shape[1] - LANES:]
        for sh, arr, b0 in ((0, piece, lo), (1, shifted, lo1)):
            w = arr.shape[1]
            zrows = jnp.zeros((HEAD_DIM, w), F32)
            for g in range(N_KV_HEADS):
                rows = arr[HEAD_DIM * g:HEAD_DIM * (g + 1), :]
                vt_s[sh, g, 0, :, b0:b0 + w] = jnp.concatenate([rows, zrows], axis=0).astype(BF16)
                vt_s[sh, g, 1, :, b0:b0 + w] = jnp.concatenate([zrows, rows], axis=0).astype(BF16)

    def regroup(s):
        def put(g, half, val):
            ug_ref[g, 0, s * nks:(s + 1) * nks, LANES * half:LANES * (half + 1)] = val
        _to_groups(u_s, put, nks, strided=True, row0=s * sb)

    nkeys = WINDOW + CHUNK
    key_row = lax.broadcasted_iota(jnp.int32, (nkeys, 1), 0)
    q_lane = lax.broadcasted_iota(jnp.int32, (1, LANES), 1)

    def scores(i):
        r0 = i * CHUNK
        valid = key_row >= (WINDOW - r0 - blk * tb) if i < N_WIN_CHUNKS else None
        out = []
        for g in range(N_KV_HEADS):
            qa = q_s[r0:r0 + CHUNK, 2 * LANES * g:2 * LANES * g + LANES]
            qb = q_s[r0:r0 + CHUNK, 2 * LANES * g + LANES:2 * LANES * (g + 1)]
            a_bf = jnp.concatenate([qa, qb], axis=0).astype(BF16)
            for par in range(2):
                s = lax.dot_general(k_s[2 * g + par, r0:r0 + nkeys, :], a_bf, (((1,), (1,)), ((), ())),
                                    preferred_element_type=F32)
                out.append(s if valid is None else jnp.where(valid, s, -jnp.inf))
        return out

    for item in project(0):
        item()
    build_vt(0)
    staging = []
    for s in range(1, n_sub):
        staging += [(s, item) for item in project(s)] + [(s, functools.partial(build_vt, s))]
    regroups = [functools.partial(regroup, s) for s in range(n_sub)]
    pending = scores(0)
    for i in range(n_chunks):
        if i + 1 < n_chunks and (i + 1) % cps == 0:
            while staging and staging[0][0] <= (i + 1) // cps:
                staging.pop(0)[1]()
        nxt = scores(i + 1) if i + 1 < n_chunks else None
        if staging:
            staging.pop(0)[1]()
        elif regroups:
            regroups.pop(0)()
        r0 = i * CHUNK
        sh = i % 2
        base = r0 - sh * CHUNK
        for g in range(N_KV_HEADS):
            acc = None
            for par in range(2):
                h0 = 4 * g + par
                sink_row = jnp.where(q_lane < CHUNK, sinks_ref[h0], sinks_ref[h0 + 2])
                o = _softmax_pv_t(pending[2 * g + par], vt_s[sh, g, par, :, base:base + nkeys], sink_row)
                acc = o if acc is None else acc + o
            at = acc.T
            a_ref[0, r0:r0 + CHUNK, 2 * LANES * g:2 * LANES * g + LANES] = at[:CHUNK]
            a_ref[0, r0:r0 + CHUNK, 2 * LANES * g + LANES:2 * LANES * (g + 1)] = at[CHUNK:]
        pending = nxt
    for item in regroups:
        item()


def _prompt_ssd_kernel(ug_ref, m_ref, wd_ref, r_ref, tab_ref, abar_ref, wup_f_ref, wdown_f_ref,
                       yg_ref, sfin_ref, wup_b_ref, wdown_b_ref, d_s, sp_s, carry_s):
    nb = ug_ref.shape[1]
    nc = ug_ref.shape[2]
    step = pl.program_id(0)
    wup_b_ref[...] = wup_f_ref[...].astype(BF16)
    wdown_b_ref[...] = wdown_f_ref[...].astype(BF16)

    @pl.when(step == 0)
    def _():
        carry_s[...] = jnp.zeros(carry_s.shape, F32)

    def state_in(g, c):
        ub = ug_ref[g].reshape(nb * nc, SSD_W)
        d_s[g] = jnp.dot(ub, wd_ref[g], preferred_element_type=F32)
        return c

    lax.fori_loop(0, N_GROUPS, state_in, 0, unroll=GROUP_UNROLL)

    hp = N_GROUPS // 2
    lo_half = lax.broadcasted_iota(jnp.int32, (1, 1, STATE_W), 2) < SSM_STATE

    def split(z):
        a, b = z[:hp], z[hp:]
        return (jnp.where(lo_half, a, pltpu.roll(b, SSM_STATE, 2)),
                jnp.where(lo_half, pltpu.roll(a, SSM_STATE, 2), b))

    def merge(re, im):
        return jnp.concatenate([jnp.where(lo_half, re, pltpu.roll(im, SSM_STATE, 2)),
                                jnp.where(lo_half, pltpu.roll(re, SSM_STATE, 2), im)], axis=0)

    def pair_tables(tre, tim):
        return jnp.where(lo_half, tre[:hp], tre[hp:]), jnp.where(lo_half, -tim[:hp], tim[hp:])

    pw = [pair_tables(tab_ref[2 * i], tab_ref[2 * i + 1]) for i in range(4)]
    ar, ai = pair_tables(abar_ref[0], abar_ref[1])
    kio = lax.broadcasted_iota(jnp.int32, (N_GROUPS, nc, STATE_W), 1)
    for b in range(nb):
        dg = d_s[:, b * nc:(b + 1) * nc, :]
        s_in = carry_s[b]
        xr_all, xi_all = split(jnp.where(kio == 0, s_in, pltpu.roll(dg, 1, 1)))
        cr = ci = jnp.zeros((hp, 1, STATE_W), F32)
        for j in range(nc // SUBLANES):
            xr = xr_all[:, SUBLANES * j:SUBLANES * (j + 1), :]
            xi = xi_all[:, SUBLANES * j:SUBLANES * (j + 1), :]
            for lvl in range(3):
                pr, pi = pw[lvl]
                sr, si = pltpu.roll(xr, 1 << lvl, 1), pltpu.roll(xi, 1 << lvl, 1)
                xr, xi = xr + pr * sr - pi * si, xi + pr * si + pi * sr
            qr, qi = pw[3]
            crb, cib = jnp.broadcast_to(cr, xr.shape), jnp.broadcast_to(ci, xi.shape)
            hr, hi = xr + qr * crb - qi * cib, xi + qr * cib + qi * crb
            sp_s[:, b * nc + SUBLANES * j:b * nc + SUBLANES * (j + 1), :] = merge(hr, hi)
            cr, ci = hr[:, SUBLANES - 1:SUBLANES, :], hi[:, SUBLANES - 1:SUBLANES, :]
        carry_s[b] = merge(ar * cr - ai * ci, ar * ci + ai * cr) + dg[:, nc - 1:nc, :]

    def readout(g, c):
        ub = ug_ref[g].reshape(nb * nc, SSD_W)
        y = jnp.dot(ub, m_ref[g], preferred_element_type=F32)
        y = y + lax.dot_general(sp_s[g].astype(BF16), r_ref[g], (((1,), (1,)), ((), ())),
                                preferred_element_type=F32)
        yg_ref[g] = y.reshape(nb, nc, SSD_W).astype(BF16)
        return c

    lax.fori_loop(0, N_GROUPS, readout, 0, unroll=GROUP_UNROLL)
    sfin_ref[...] = carry_s[...]


def _out_kernel(*refs, strided, cs, has_prev):
    if has_prev:
        x_ref, a_ref, yg_ref, prev_ref = refs[:4]
        refs = refs[4:]
    else:
        x_ref, a_ref, yg_ref = refs[:3]
        prev_ref = None
        refs = refs[3:]
    (wglu_ref, ga_ref, gs_ref, wout_ref, g2_ref, wup_ref, cw_ref, cb_ref, wdown_ref, gf_ref,
     y_ref, cst_ref, ys_s, ext_s, px_s, py_s) = refs
    tb = x_ref.shape[1]
    n_sub, _, sb, _ = ys_s.shape
    nks = sb // SSD_T
    planes = strided
    pr = sb // SUBLANES
    pad = SUBLANES if planes else ext_s.shape[0] - sb
    blk = pl.program_id(1)
    assert n_sub * sb == tb and (not planes or (cs == 1 and not has_prev))

    @pl.when(blk == 0)
    def _():
        if planes:
            ext_s[:, 0:pad, :] = jnp.zeros((2, pad, 2 * D_FF), F32)
        elif has_prev:
            ext_s[0:pad, :] = prev_ref[0]
        else:
            ext_s[0:pad, :] = jnp.zeros((pad, 2 * D_FF), F32)

    def mix(i):
        rows = slice(i * sb, (i + 1) * sb)
        _from_groups(lambda g, half: yg_ref[g, 0, i * nks:(i + 1) * nks, LANES * half:LANES * (half + 1)],
                     ys_s.at[i], nks, strided)
        yv = jnp.concatenate([ys_s[i, j] for j in range(D_SSM // LANES)], axis=-1)
        z = 0.5 * yv * (1.0 + jnp.tanh(math.sqrt(2.0 / math.pi) * (yv + 0.044715 * (yv * yv * yv))))
        gate = jnp.dot(z.astype(BF16), wglu_ref[...], preferred_element_type=F32)
        s_out = z * jax.nn.sigmoid(gate)
        na = _rms(a_ref[0, rows, :], ga_ref[...]).astype(BF16)
        ns = _rms(s_out, gs_ref[...]).astype(BF16)
        x1 = x_ref[0, rows, :] + jnp.dot(na, wout_ref[0:D_ATTN, :], preferred_element_type=F32)
        return x1 + jnp.dot(ns, wout_ref[D_ATTN:, :], preferred_element_type=F32)

    def tile_cols(j, half):
        return slice(half * D_FF + j * FF_TILE, half * D_FF + (j + 1) * FF_TILE)

    def to_planes(x):
        for c in range(D_MODEL // LANES):
            px_s[c] = x[:, LANES * c:LANES * (c + 1)]
        return jnp.concatenate(
            [jnp.concatenate([px_s[c, pl.ds(r, pr, stride=SUBLANES), :] for r in range(SUBLANES)], axis=0)
             for c in range(D_MODEL // LANES)], axis=1)

    def store_rows(i, y):
        rows = slice(i * sb, (i + 1) * sb)
        if not planes:
            y_ref[0, rows, :] = y
            return
        for c in range(D_MODEL // LANES):
            for r in range(SUBLANES):
                py_s[c, pl.ds(r, pr, stride=SUBLANES), :] = y[r * pr:(r + 1) * pr, LANES * c:LANES * (c + 1)]
            y_ref[0, rows, LANES * c:LANES * (c + 1)] = py_s[c]

    def time_shifts(up, cols):
        if not planes:
            ext_s[pad:pad + sb, cols] = up
            sh2 = ext_s[pad - 2 * cs:pad - 2 * cs + sb, cols]
            sh1 = ext_s[pad - cs:pad - cs + sb, cols]
            ext_s[0:pad, cols] = ext_s[sb:sb + pad, cols]
            return sh2, sh1
        for p in range(2):
            ext_s[p, pad:pad + pr, cols] = up[(6 + p) * pr:(7 + p) * pr, :]
        s6 = ext_s[0, pad - 1:pad - 1 + pr, cols]
        s7 = ext_s[1, pad - 1:pad - 1 + pr, cols]
        ext_s[:, 0:pad, cols] = ext_s[:, pr:pr + pad, cols]
        sh1 = jnp.concatenate([s7, up[:7 * pr, :]], axis=0)
        sh2 = jnp.concatenate([s6, s7, up[:6 * pr, :]], axis=0)
        return sh2, sh1

    def conv_ffn(i, x1):
        if planes:
            x1 = to_planes(x1)
        h2 = _rms(x1, g2_ref[...]).astype(BF16)

        def up_proj(j):
            return [jnp.dot(h2, wup_ref[:, tile_cols(j, half)], preferred_element_type=F32) for half in range(2)]

        def down_proj(act, j):
            return jnp.dot(act, wdown_ref[j * FF_TILE:(j + 1) * FF_TILE, :], preferred_element_type=F32)

        n_tiles = D_FF // FF_TILE
        acc = jnp.zeros((sb, D_MODEL), F32)
        ups = up_proj(0)
        act_prev = None
        for j in range(n_tiles):
            nxt = up_proj(j + 1) if j + 1 < n_tiles else None
            if act_prev is not None:
                acc = acc + down_proj(act_prev, j - 1)
            parts = []
            for half in range(2):
                cols = tile_cols(j, half)
                sh2, sh1 = time_shifts(ups[half], cols)
                parts.append(sh2 * cw_ref[0:1, cols] + sh1 * cw_ref[1:2, cols]
                             + ups[half] * cw_ref[2:3, cols] + cb_ref[:, cols])
            cg, cv = parts
            act_prev = (cg * jax.nn.sigmoid(cg) * cv).astype(BF16)
            ups = nxt
        acc = acc + down_proj(act_prev, n_tiles - 1)
        store_rows(i, _rms(x1 + acc, gf_ref[...]))

    mixed = mix(0)
    for i in range(n_sub):
        nxt_mixed = mix(i + 1) if i + 1 < n_sub else None
        conv_ffn(i, mixed)
        mixed = nxt_mixed
    if planes:
        cst_ref[0] = ext_s[:, 0:pad, :]
    else:
        cst_ref[0] = ext_s[0:pad, :]


def _decode_in_kernel(sinks_ref, x_ref, g1_ref, win_ref, cos_ref, sin_ref, kc_ref, vc_ref, h0_ref,
                      m_ref, wd_ref, r_ref, abar_ref,
                      a_ref, kn_ref, vn_ref, yg_ref, sn_ref, u_s, ug_s, *, n_streams, n_steps, n_cache):
    rows = n_streams * n_steps
    qs, k, v, u = _in_proj(x_ref[...], g1_ref, win_ref, cos_ref, sin_ref)
    kn_ref[...] = k
    vn_ref[...] = v
    for j in range(D_SSM // LANES):
        u_s[j] = u[:, LANES * j:LANES * (j + 1)]
    ncache_rows = n_streams * n_cache
    nkeys = ncache_rows + rows
    kall = jnp.concatenate([kc_ref[...], k], axis=0)
    vall = jnp.concatenate([vc_ref[...], v], axis=0)
    variants = _kv_variants(kall, vall)

    colv = lax.broadcasted_iota(jnp.int32, (1, nkeys), 1)
    is_new = colv >= ncache_rows
    cnew = colv - ncache_rows
    k_stream = jnp.where(is_new, _pmod(cnew, n_streams), _pdiv(colv, n_cache))
    k_pos = jnp.where(is_new, PAST_LEN + _pdiv(cnew, n_streams), PAST_LEN - n_cache + _pmod(colv, n_cache))
    k_chunk = _pdiv(k_pos, CHUNK)
    rowv = _pmod(lax.broadcasted_iota(jnp.int32, (2 * rows, 1), 0), rows)
    q_stream = _pmod(rowv, n_streams)
    q_chunk = _pdiv(PAST_LEN + _pdiv(rowv, n_streams), CHUNK)
    ok_col = jnp.where(k_pos >= 0, k_stream, -1)
    d_chunk = q_chunk - k_chunk
    valid = [q_stream == ok_col,
             lax.bitcast_convert_type(d_chunk, jnp.uint32) <= jnp.uint32(N_WIN_CHUNKS)]
    top = lax.broadcasted_iota(jnp.int32, (2 * rows, 1), 0) < rows

    for g in range(N_KV_HEADS):
        a_bf = jnp.concatenate([qs[2 * g], qs[2 * g + 1]], axis=0).astype(BF16)
        acc = jnp.zeros((2 * rows, LANES), F32)
        for par in range(2):
            h0 = 4 * g + par
            sink_col = jnp.where(top, sinks_ref[h0], sinks_ref[h0 + 2])
            acc = acc + _attend(a_bf, variants[2 * g + par], variants[4 + 2 * g + par], valid, sink_col)
        a_ref[:, 2 * LANES * g:2 * LANES * g + LANES] = acc[:rows]
        a_ref[:, 2 * LANES * g + LANES:2 * LANES * (g + 1)] = acc[rows:]

    def put(g, half, val):
        ug_s[g, :, LANES * half:LANES * (half + 1)] = val

    _to_groups(u_s, put, n_streams, strided=False)
    def group_body(g, c):
        ub = ug_s[g]
        h0g = h0_ref[g]
        d = jnp.dot(ub, wd_ref[g], preferred_element_type=F32)
        sn_ref[g] = _cmul(abar_ref[0, g], abar_ref[1, g], h0g) + d
        y = jnp.dot(ub, m_ref[g], preferred_element_type=F32)
        y = y + lax.dot_general(h0g.astype(BF16), r_ref[g], (((1,), (1,)), ((), ())),
                                preferred_element_type=F32)
        yg_ref[g] = y.astype(BF16)
        return c

    lax.fori_loop(0, N_GROUPS, group_body, 0, unroll=DECODE_UNROLL)


def _ssd_tables(a_re, a_im, log_dt, b_re, b_im, c_re, c_im, d_skip):
    dup = lambda x: jnp.concatenate([x, x], axis=-1)
    prm = jnp.stack([dup(a_re), dup(a_im), jnp.broadcast_to(log_dt[:, None], (N_GROUPS, STATE_W))], axis=1)
    bt = jnp.concatenate([b_re.transpose(0, 2, 1), b_im.transpose(0, 2, 1)], axis=-1)
    cc = jnp.concatenate([c_re, c_im], axis=-1)
    dtile = jnp.tile(d_skip, (1, SSD_T))[:, None, :]
    gb = TABLE_GROUPS
    blk3 = lambda r, c: pl.BlockSpec((gb, r, c), lambda i: (i, 0, 0))
    return pl.pallas_call(
        _ssd_tables_kernel,
        grid=(N_GROUPS // gb,),
        in_specs=[blk3(3, STATE_W), blk3(SSM_GROUP, STATE_W), blk3(SSM_GROUP, STATE_W), blk3(1, SSD_W)],
        out_specs=[blk3(SSD_W, SSD_W), blk3(SSD_W, STATE_W), blk3(SSD_W, STATE_W),
                   pl.BlockSpec((8, gb, SUBLANES, STATE_W), lambda i: (0, i, 0, 0)),
                   pl.BlockSpec((2, gb, 1, STATE_W), lambda i: (0, i, 0, 0))],
        out_shape=[jax.ShapeDtypeStruct((N_GROUPS, SSD_W, SSD_W), BF16),
                   jax.ShapeDtypeStruct((N_GROUPS, SSD_W, STATE_W), BF16),
                   jax.ShapeDtypeStruct((N_GROUPS, SSD_W, STATE_W), BF16),
                   jax.ShapeDtypeStruct((8, N_GROUPS, SUBLANES, STATE_W), F32),
                   jax.ShapeDtypeStruct((2, N_GROUPS, 1, STATE_W), F32)],
        compiler_params=pltpu.CompilerParams(dimension_semantics=("arbitrary",)),
        name="ssd_tables",
    )(prm, bt, cc, dtile)


def _ssd_tables_kernel(prm_ref, bt_ref, cc_ref, dt_ref, m_ref, wd_ref, rt_ref, tab_ref, abar_ref):
    t = SSD_T
    lane = lax.broadcasted_iota(jnp.int32, (1, STATE_W), 1)
    sgn = jnp.where(lane >= SSM_STATE, 1.0, -1.0).astype(F32)
    conj = -sgn
    n_small = lax.broadcasted_iota(jnp.int32, (3 * SUBLANES, 1), 0).astype(F32)
    sub_i = lax.broadcasted_iota(jnp.int32, (SUBLANES, 1), 0)
    n_big = float(t) * (sub_i + 1).astype(F32)
    row = lax.broadcasted_iota(jnp.int32, (SSD_W, SSD_W), 0)
    col = lax.broadcasted_iota(jnp.int32, (SSD_W, SSD_W), 1)
    diag = row == col
    kcol = lax.broadcasted_iota(jnp.int32, (SSM_GROUP, SSD_W), 1)
    for gi in range(prm_ref.shape[0]):
        a_re, a_im = prm_ref[gi, 0:1, :], prm_ref[gi, 1:2, :]
        dt = jnp.exp(prm_ref[gi, 2:3, :])
        lr, li = dt * a_re, dt * a_im

        def power(n):
            mag = jnp.exp(n * lr)
            return mag * jnp.cos(n * li), mag * jnp.sin(n * li) * sgn

        pr_s, pis_s = power(n_small)
        pr_b, pis_b = power(n_big)

        def times_powers(z, zs, exps):
            return jnp.concatenate([z * pr_s[e:e + 1, :] + zs * pis_s[e:e + 1, :] for e in exps], axis=0)

        ar, ais = pr_s[1:2, :], pis_s[1:2, :]
        den = a_re * a_re + a_im * a_im
        nr, ni = ar - 1.0, ais * sgn
        fr, fis = (nr * a_re + ni * a_im) / den, (ni * a_re - nr * a_im) / den * sgn
        bt = bt_ref[gi]
        bb = bt * fr + pltpu.roll(bt, SSM_STATE, 1) * fis
        bbs = pltpu.roll(bb, SSM_STATE, 1)
        cc = cc_ref[gi]
        ccs = pltpu.roll(cc, SSM_STATE, 1)
        steps = range(t)
        y = times_powers(cc, ccs, list(steps))
        kt = lax.dot_general(bb * conj, y, (((1,), (1,)), ((), ())), precision=lax.Precision.HIGHEST,
                             preferred_element_type=F32)
        m = jnp.concatenate(
            [kt if s == 0 else jnp.where(kcol >= SSM_GROUP * s, pltpu.roll(kt, SSM_GROUP * s, 1), 0.0)
             for s in steps], axis=0)
        m_ref[gi] = (m + jnp.where(diag, dt_ref[gi], 0.0)).astype(BF16)
        wd_ref[gi] = times_powers(bb, bbs, [t - 1 - s for s in steps]).astype(BF16)
        rt_ref[gi] = (times_powers(cc, ccs, [s + 1 for s in steps]) * conj).astype(BF16)
        for lvl in range(3):
            d = 1 << lvl
            tab_ref[2 * lvl, gi] = jnp.where(sub_i >= d, pr_b[d - 1:d, :], 0.0)
            tab_ref[2 * lvl + 1, gi] = jnp.where(sub_i >= d, pis_b[d - 1:d, :], 0.0)
        tab_ref[6, gi] = pr_b
        tab_ref[7, gi] = pis_b
        abar_ref[0, gi] = pr_b[0:1, :]
        abar_ref[1, gi] = pis_b[0:1, :]


def _rope_freq():
    half = HEAD_DIM // 2
    return ROPE_THETA ** (-jnp.asarray(np.arange(LANES) % half, F32) / half)


def _rope_tables(pos):
    sign = jnp.asarray(np.where(np.arange(LANES) % HEAD_DIM < HEAD_DIM // 2, -1.0, 1.0), F32)
    ang = pos.astype(F32)[:, None] * _rope_freq()[None, :]
    return jnp.cos(ang), jnp.sin(ang) * sign[None, :]


def _rope_base(rows):
    freq = _rope_freq()[None, :]
    ang = jnp.arange(rows, dtype=F32)[:, None] * freq
    return jnp.cos(ang), jnp.sin(ang), freq


def _const(shape):
    nd = len(shape)
    return pl.BlockSpec(shape, lambda *_: (0,) * nd, pipeline_mode=pl.Buffered(1))


def _whole(shape):
    nd = len(shape)
    return pl.BlockSpec(shape, lambda *_: (0,) * nd)


def _out_call(x, a, yg, prev, w, *, tb, sb, strided, cs, pad):
    nb, length, _ = x.shape
    nk = tb // SSD_T
    grid = (nb, length // tb)
    in_specs = [
        pl.BlockSpec((1, tb, D_MODEL), lambda b, i: (b, i, 0)),
        pl.BlockSpec((1, tb, D_ATTN), lambda b, i: (b, i, 0)),
        pl.BlockSpec((N_GROUPS, 1, nk, SSD_W), lambda b, i: (0, b, i, 0)),
    ]
    args = [x, a, yg]
    if prev is not None:
        in_specs.append(pl.BlockSpec((1, pad, 2 * D_FF), lambda b, i: (b, 0, 0)))
        args.append(prev)
    weights = [w['w_glu'], w['onorm_a'], w['onorm_s'], w['w_out'], w['norm2'], w['w_up'], w['conv_w'],
               w['conv_b'], w['w_down'], w['final_g']]
    in_specs += [_const(t.shape) for t in weights]
    kern = functools.partial(_out_kernel, strided=strided, cs=cs, has_prev=prev is not None)
    if strided:
        ext_shape = (2, pad + sb // SUBLANES, 2 * D_FF)
        cst_block, cst_map = (1, 2, pad, 2 * D_FF), (lambda b, i: (b, 0, 0, 0))
    else:
        ext_shape = (pad + sb, 2 * D_FF)
        cst_block, cst_map = (1, pad, 2 * D_FF), (lambda b, i: (b, 0, 0))
    return pl.pallas_call(
        kern,
        grid=grid,
        in_specs=in_specs,
        out_specs=[pl.BlockSpec((1, tb, D_MODEL), lambda b, i: (b, i, 0)),
                   pl.BlockSpec(cst_block, cst_map)],
        out_shape=[jax.ShapeDtypeStruct((nb, length, D_MODEL), F32),
                   jax.ShapeDtypeStruct((nb,) + cst_block[1:], F32)],
        scratch_shapes=[pltpu.VMEM((tb // sb, D_SSM // LANES, sb, LANES), F32),
                        pltpu.VMEM(ext_shape, F32),
                        pltpu.VMEM((D_MODEL // LANES, sb, LANES), F32),
                        pltpu.VMEM((D_MODEL // LANES, sb, LANES), F32)],
        compiler_params=pltpu.CompilerParams(dimension_semantics=("arbitrary", "arbitrary"),
                                             vmem_limit_bytes=VMEM_LIMIT),
        name="layer_out",
    )(*args, *weights)


def kernel(x_prompt, x_sample, cache_k, cache_v, state_ssm_re, state_ssm_im, state_conv, norm1_g, w_in, attn_sinks, ssm_A_re, ssm_A_im, ssm_log_dt, ssm_B_re, ssm_B_im, ssm_C_re, ssm_C_im, ssm_D, w_glu, onorm_attn_g, onorm_ssm_g, w_out, norm2_g, w_up, conv_w, conv_b, w_down, final_g):
    assert norm1_g.shape[0] == 1, "one layer"
    nb, length, _ = x_prompt.shape
    ns, nt, _ = x_sample.shape
    n_cache = cache_k.shape[2]
    assert nt == SSD_T and length % TB_IN == 0 and length % TB_OUT == 0 and (length // SSD_T) % NC_SSD == 0

    m_mat, wd_mat, r_mat, tab, abar = _ssd_tables(ssm_A_re[0], ssm_A_im[0], ssm_log_dt[0], ssm_B_re[0],
                                                  ssm_B_im[0], ssm_C_re[0], ssm_C_im[0], ssm_D[0])
    sinks = attn_sinks[0]
    g1 = norm1_g
    win = w_in[0].astype(BF16)
    w = dict(w_glu=w_glu[0].astype(BF16), onorm_a=onorm_attn_g, onorm_s=onorm_ssm_g, w_out=w_out[0].astype(BF16),
             norm2=norm2_g, conv_w=conv_w[0], conv_b=conv_b, final_g=final_g[None, :])
    smem = pl.BlockSpec(memory_space=pltpu.SMEM)

    cos_p, sin_p, freq = _rope_base(TB_IN)
    nk = TB_IN // SSD_T
    a_p, ug_p, k_p, v_p = pl.pallas_call(
        functools.partial(_prompt_in_kernel, sb=SUB_IN),
        grid=(nb, length // TB_IN),
        in_specs=[smem,
                  pl.BlockSpec((1, TB_IN, D_MODEL), lambda b, i: (b, i, 0)),
                  _const((1, D_MODEL)), _const((D_MODEL, D_IN)),
                  _const((TB_IN, LANES)), _const((TB_IN, LANES)), _const((1, LANES))],
        out_specs=[pl.BlockSpec((1, TB_IN, D_ATTN), lambda b, i: (b, i, 0)),
                   pl.BlockSpec((N_GROUPS, 1, nk, SSD_W), lambda b, i: (0, b, i, 0)),
                   pl.BlockSpec((1, WINDOW, KV_W), lambda b, i: (b, 0, 0)),
                   pl.BlockSpec((1, WINDOW, KV_W), lambda b, i: (b, 0, 0))],
        out_shape=[jax.ShapeDtypeStruct((nb, length, D_ATTN), F32),
                   jax.ShapeDtypeStruct((N_GROUPS, nb, length // SSD_T, SSD_W), BF16),
                   jax.ShapeDtypeStruct((nb, WINDOW, KV_W), F32),
                   jax.ShapeDtypeStruct((nb, WINDOW, KV_W), F32)],
        scratch_shapes=[pltpu.VMEM((TB_IN, D_ATTN), F32),
                        pltpu.VMEM((2 * N_KV_HEADS, WINDOW + TB_IN, LANES), BF16),
                        pltpu.VMEM((WINDOW + TB_IN, LANES), F32),
                        pltpu.VMEM((2, N_KV_HEADS, 2, LANES, WINDOW + TB_IN), BF16),
                        pltpu.VMEM((D_SSM // LANES, TB_IN, LANES), F32)],
        compiler_params=pltpu.CompilerParams(dimension_semantics=("arbitrary", "arbitrary"),
                                             vmem_limit_bytes=VMEM_LIMIT),
        name="prompt_in",
    )(sinks, x_prompt, g1, win, cos_p, sin_p, freq)

    n_ssd = length // SSD_T // NC_SSD
    up_rows, down_rows = D_MODEL // n_ssd, D_FF // n_ssd
    yg_p, sfin_p, w['w_up'], w['w_down'] = pl.pallas_call(
        _prompt_ssd_kernel,
        grid=(n_ssd,),
        in_specs=[pl.BlockSpec((N_GROUPS, nb, NC_SSD, SSD_W), lambda i: (0, 0, i, 0)),
                  _const(m_mat.shape), _const(wd_mat.shape), _const(r_mat.shape), _const(tab.shape),
                  _const(abar.shape),
                  pl.BlockSpec((up_rows, 2 * D_FF), lambda i: (i, 0)),
                  pl.BlockSpec((down_rows, D_MODEL), lambda i: (i, 0))],
        out_specs=[pl.BlockSpec((N_GROUPS, nb, NC_SSD, SSD_W), lambda i: (0, 0, i, 0)),
                   pl.BlockSpec((nb, N_GROUPS, 1, STATE_W), lambda i: (0, 0, 0, 0)),
                   pl.BlockSpec((up_rows, 2 * D_FF), lambda i: (i, 0)),
                   pl.BlockSpec((down_rows, D_MODEL), lambda i: (i, 0))],
        out_shape=[jax.ShapeDtypeStruct((N_GROUPS, nb, length // SSD_T, SSD_W), BF16),
                   jax.ShapeDtypeStruct((nb, N_GROUPS, 1, STATE_W), F32),
                   jax.ShapeDtypeStruct((D_MODEL, 2 * D_FF), BF16),
                   jax.ShapeDtypeStruct((D_FF, D_MODEL), BF16)],
        scratch_shapes=[pltpu.VMEM((N_GROUPS, nb * NC_SSD, STATE_W), F32),
                        pltpu.VMEM((N_GROUPS, nb * NC_SSD, STATE_W), F32),
                        pltpu.VMEM((nb, N_GROUPS, 1, STATE_W), F32)],
        compiler_params=pltpu.CompilerParams(dimension_semantics=("arbitrary",), vmem_limit_bytes=VMEM_LIMIT),
        name="prompt_ssd",
    )(ug_p, m_mat, wd_mat, r_mat, tab, abar, w_up[0], w_down[0])

    y_p, cst_p = _out_call(x_prompt, a_p, yg_p, None, w, tb=TB_OUT, sb=SUB_OUT, strided=True, cs=1, pad=SUBLANES)

    rows = ns * nt
    xs = x_sample.transpose(1, 0, 2).reshape(rows, D_MODEL)
    cos_s, sin_s = _rope_tables(PAST_LEN + jnp.arange(rows) // ns)
    kc = cache_k[0].reshape(ns * n_cache, KV_W)
    vc = cache_v[0].reshape(ns * n_cache, KV_W)
    h0 = jnp.concatenate([state_ssm_re[0], state_ssm_im[0]], axis=-1).transpose(1, 0, 2)
    dec = functools.partial(_decode_in_kernel, n_streams=ns, n_steps=nt, n_cache=n_cache)
    dec_in = [xs, g1, win, cos_s, sin_s, kc, vc, h0, m_mat, wd_mat, r_mat, abar]
    a_s, kn, vn, yg_s, sn = pl.pallas_call(
        dec,
        grid=(1,),
        in_specs=[smem] + [_const(t.shape) for t in dec_in],
        out_specs=[_whole((rows, D_ATTN)), _whole((rows, KV_W)), _whole((rows, KV_W)),
                   _whole((N_GROUPS, ns, SSD_W)), _whole((N_GROUPS, ns, STATE_W))],
        out_shape=[jax.ShapeDtypeStruct((rows, D_ATTN), F32), jax.ShapeDtypeStruct((rows, KV_W), F32),
                   jax.ShapeDtypeStruct((rows, KV_W), F32), jax.ShapeDtypeStruct((N_GROUPS, ns, SSD_W), BF16),
                   jax.ShapeDtypeStruct((N_GROUPS, ns, STATE_W), F32)],
        scratch_shapes=[pltpu.VMEM((D_SSM // LANES, rows, LANES), F32), pltpu.VMEM((N_GROUPS, ns, SSD_W), BF16)],
        compiler_params=pltpu.CompilerParams(dimension_semantics=("arbitrary",), vmem_limit_bytes=VMEM_LIMIT),
        name="decode_in",
    )(sinks, *dec_in)

    cs = ns
    prev = state_conv[0].transpose(1, 0, 2).reshape(1, (CONV_W - 1) * cs, 2 * D_FF)
    y_s, cst_s = _out_call(xs[None], a_s[None], yg_s[:, None], prev, w, tb=rows, sb=rows, strided=False, cs=cs,
                           pad=(CONV_W - 1) * cs)

    kv5 = lambda t, n: t.reshape(1, n, -1, N_KV_HEADS, HEAD_DIM)
    y_sample = y_s.reshape(nt, ns, D_MODEL).transpose(1, 0, 2)
    kn_b = kn.reshape(nt, ns, KV_W).transpose(1, 0, 2)
    vn_b = vn.reshape(nt, ns, KV_W).transpose(1, 0, 2)
    k_new = jnp.concatenate([cache_k[0].reshape(ns, n_cache, KV_W), kn_b], axis=1)[:, -n_cache:]
    v_new = jnp.concatenate([cache_v[0].reshape(ns, n_cache, KV_W), vn_b], axis=1)[:, -n_cache:]
    conv_p = cst_p[:, :, SUBLANES - 1, :][None]
    conv_s = cst_s.reshape(CONV_W - 1, ns, 2 * D_FF).transpose(1, 0, 2)[None]
    sn_b = sn.transpose(1, 0, 2)
    return (y_p, y_sample,
            kv5(k_p, nb), kv5(v_p, nb),
            sfin_p[None, :, :, 0, :SSM_STATE], sfin_p[None, :, :, 0, SSM_STATE:], conv_p,
            kv5(k_new, ns), kv5(v_new, ns),
            sn_b[None, :, :, :SSM_STATE], sn_b[None, :, :, SSM_STATE:], conv_s)
```

```python
import functools
import math

import numpy as np
import jax
import jax.numpy as jnp
from jax import lax
from jax.experimental import pallas as pl
from jax.experimental.pallas import tpu as pltpu

F32 = jnp.float32
BF16 = jnp.bfloat16

D_MODEL = 1024
CHUNK = 64
D_ATTN = 512
D_SSM = 512
HEAD_DIM = 64
N_HEADS = 8
N_KV_HEADS = 2
KV_W = N_KV_HEADS * HEAD_DIM
WINDOW = 128
N_WIN_CHUNKS = WINDOW // CHUNK
ROPE_THETA = 10000.0
SSM_GROUP = 16
N_GROUPS = D_SSM // SSM_GROUP
SSM_STATE = 64
D_FF = 2816
CONV_W = 3
EPS = 1e-6
D_IN = D_ATTN + 2 * KV_W + D_SSM
PAST_LEN = 2048

SSD_T = 16
LANES = 128
SUBLANES = 8
GROUPS_PER_VREG = LANES // SSM_GROUP
SSD_W = SSD_T * SSM_GROUP
STATE_W = 2 * SSM_STATE

TB_IN = 2048
SUB_IN = 512
TB_OUT = 512
SUB_OUT = 256
NC_SSD = 64
FF_TILE = 256
GROUP_UNROLL = 32
DECODE_UNROLL = 32
TABLE_GROUPS = 8
VMEM_LIMIT = 56 * 1024 * 1024


def _log2(n):
    assert n > 0 and n & (n - 1) == 0, n
    return n.bit_length() - 1


def _pdiv(x, n):
    return lax.shift_right_arithmetic(x, jnp.int32(_log2(n)))


def _pmod(x, n):
    return x & (n - 1)


def _rms(x, g):
    return x * lax.rsqrt(jnp.mean(x * x, axis=-1, keepdims=True) + EPS) * g


def _rope(xb, cos, sin_signed, first_half):
    partner = jnp.where(first_half, pltpu.roll(xb, LANES - HEAD_DIM // 2, 1), pltpu.roll(xb, HEAD_DIM // 2, 1))
    return xb * cos + partner * sin_signed


def _group_transpose8(xs, lane_group):
    xs = list(xs)
    for d in (4, 2, 1):
        keep = (lane_group & d) == 0
        nxt = list(xs)
        for i in range(GROUPS_PER_VREG):
            if i & d:
                continue
            a, b = xs[i], xs[i | d]
            nxt[i] = jnp.where(keep, a, pltpu.roll(b, SSM_GROUP * d, 1))
            nxt[i | d] = jnp.where(keep, pltpu.roll(a, LANES - SSM_GROUP * d, 1), b)
        xs = nxt
    return xs


def _row_sel(s, nk, strided, row0=0):
    if strided:
        return pl.ds(row0 + s, nk, stride=SSD_T)
    return pl.ds(row0 + s * nk, nk)


def _to_groups(u_ref, put, nk, strided, row0=0):
    lane_group = _pdiv(lax.broadcasted_iota(jnp.int32, (nk, LANES), 1), SSM_GROUP)
    for j in range(D_SSM // LANES):
        for half in range(SSD_T // GROUPS_PER_VREG):
            xs = [u_ref[j, _row_sel(GROUPS_PER_VREG * half + sl, nk, strided, row0), :]
                  for sl in range(GROUPS_PER_VREG)]
            ws = _group_transpose8(xs, lane_group)
            for gq in range(GROUPS_PER_VREG):
                put(GROUPS_PER_VREG * j + gq, half, ws[gq].astype(BF16))


def _from_groups(get, y_ref, nk, strided):
    lane_group = _pdiv(lax.broadcasted_iota(jnp.int32, (nk, LANES), 1), SSM_GROUP)
    for j in range(D_SSM // LANES):
        for half in range(SSD_T // GROUPS_PER_VREG):
            ws = [get(GROUPS_PER_VREG * j + gq, half).astype(F32) for gq in range(GROUPS_PER_VREG)]
            xs = _group_transpose8(ws, lane_group)
            for sl in range(GROUPS_PER_VREG):
                y_ref[j, _row_sel(GROUPS_PER_VREG * half + sl, nk, strided), :] = xs[sl]


def _cmul(re_full, im_signed, z):
    return re_full * z + im_signed * pltpu.roll(z, SSM_STATE, z.ndim - 1)


def _in_proj(x, g1_ref, win_ref, cos_ref, sin_ref):
    rows = x.shape[0]
    hn = _rms(x, g1_ref[...]).astype(BF16)
    proj = jnp.dot(hn, win_ref[...], preferred_element_type=F32)
    cos = cos_ref[...]
    sin = sin_ref[...]
    lane = lax.broadcasted_iota(jnp.int32, (rows, LANES), 1)
    first_half = _pmod(lane, HEAD_DIM) < (HEAD_DIM // 2)
    scale = HEAD_DIM ** -0.5
    qs = [_rope(proj[:, LANES * j:LANES * (j + 1)], cos, sin, first_half) * scale for j in range(D_ATTN // LANES)]
    k = _rope(proj[:, D_ATTN:D_ATTN + KV_W], cos, sin, first_half)
    v = proj[:, D_ATTN + KV_W:D_ATTN + 2 * KV_W]
    u = proj[:, D_ATTN + 2 * KV_W:]
    return qs, k, v, u


def _kv_variants(k, v):
    lane = lax.broadcasted_iota(jnp.int32, k.shape, 1)
    lo = lane < HEAD_DIM
    out = []
    for t in (k, v):
        tr = pltpu.roll(t, HEAD_DIM, 1)
        zero = jnp.zeros_like(t)
        out += [jnp.where(lo, t, zero), jnp.where(lo, zero, tr), jnp.where(lo, tr, zero), jnp.where(lo, zero, t)]
    return [o.astype(BF16) for o in out]


def _k_variants(k):
    lo = lax.broadcasted_iota(jnp.int32, k.shape, 1) < HEAD_DIM
    kr = pltpu.roll(k, HEAD_DIM, 1)
    zero = jnp.zeros_like(k)
    out = [jnp.where(lo, k, zero), jnp.where(lo, zero, kr), jnp.where(lo, kr, zero), jnp.where(lo, zero, k)]
    return [o.astype(BF16) for o in out]


def _softmax_pv_t(s, vt_win, sink_row):
    m = jnp.maximum(jnp.max(s, axis=0, keepdims=True), sink_row)
    e = jnp.exp(s - m)
    den = jnp.sum(e, axis=0, keepdims=True) + jnp.exp(sink_row - m)
    o = jnp.dot(vt_win, e.astype(BF16), preferred_element_type=F32)
    return o * (1.0 / den)


def _attend(a_bf, kvar, vvar, valid, sink_col):
    s = lax.dot_general(a_bf, kvar, (((1,), (1,)), ((), ())), preferred_element_type=F32)
    for cond in valid:
        s = jnp.where(cond, s, -jnp.inf)
    m = jnp.maximum(jnp.max(s, axis=-1, keepdims=True), sink_col)
    e = jnp.exp(s - m)
    den = jnp.sum(e, axis=-1, keepdims=True) + jnp.exp(sink_col - m)
    o = jnp.dot(e.astype(BF16), vvar, preferred_element_type=F32)
    return o * (1.0 / den)


def _prompt_in_kernel(sinks_ref, x_ref, g1_ref, win_ref, cos_ref, sin_ref, freq_ref,
                      a_ref, ug_ref, kp_ref, vp_ref, q_s, k_s, vb_s, vt_s, u_s, *, sb):
    tb = x_ref.shape[1]
    n_sub = tb // sb
    nks = sb // SSD_T
    blk = pl.program_id(1)
    nbuf = WINDOW + tb
    cps = sb // CHUNK
    n_chunks = tb // CHUNK

    @pl.when(blk == 0)
    def _():
        k_s[:, 0:WINDOW, :] = jnp.zeros((4, WINDOW, LANES), BF16)
        vb_s[0:WINDOW, :] = jnp.zeros((WINDOW, LANES), F32)

    @pl.when(blk > 0)
    def _():
        k_s[:, 0:WINDOW, :] = k_s[:, tb:nbuf, :]
        vb_s[0:WINDOW, :] = vb_s[tb:nbuf, :]

    lane = lax.broadcasted_iota(jnp.int32, (sb, LANES), 1)
    first_half = _pmod(lane, HEAD_DIM) < (HEAD_DIM // 2)
    scale = HEAD_DIM ** -0.5
    ang0 = (blk * tb).astype(F32) * freq_ref[...]
    cos0, sin0 = jnp.cos(ang0), jnp.sin(ang0)
    half_sign = jnp.where(first_half[0:1, :], -1.0, 1.0).astype(F32)

    def project(s):
        rows = slice(s * sb, (s + 1) * sb)
        brows = slice(WINDOW + s * sb, WINDOW + (s + 1) * sb)
        state = {}

        def piece(c0):
            if 'hn' not in state:
                state['hn'] = _rms(x_ref[0, rows, :], g1_ref[...]).astype(BF16)
                cb, sn = cos_ref[rows, :], sin_ref[rows, :]
                state['cos'] = cos0 * cb - sin0 * sn
                state['sin'] = (sin0 * cb + cos0 * sn) * half_sign
            return jnp.dot(state['hn'], win_ref[:, c0:c0 + 2 * LANES], preferred_element_type=F32)

        def q_piece(j):
            def run():
                p = piece(2 * LANES * j)
                for h in range(2):
                    q_s[rows, LANES * (2 * j + h):LANES * (2 * j + h + 1)] = _rope(
                        p[:, LANES * h:LANES * (h + 1)], state['cos'], state['sin'], first_half) * scale
            return run

        def kv_piece():
            p = piece(D_ATTN)
            k = _rope(p[:, :KV_W], state['cos'], state['sin'], first_half)
            v = p[:, KV_W:]
            for idx, arr in enumerate(_k_variants(k)):
                k_s[idx, brows, :] = arr
            vb_s[brows, :] = v
            if s == n_sub - 1:
                kp_ref[0] = k[sb - WINDOW:, :]
                vp_ref[0] = v[sb - WINDOW:, :]

        def u_piece(j):
            def run():
                p = piece(D_ATTN + 2 * KV_W + 2 * LANES * j)
                for h in range(2):
                    u_s[2 * j + h, rows, :] = p[:, LANES * h:LANES * (h + 1)]
            return run

        return [kv_piece, q_piece(0), q_piece(1), u_piece(0), u_piece(1)]

    vt_tail = {}

    def build_vt(s):
        lo = 0 if s == 0 else WINDOW + s * sb
        hi = WINDOW + (s + 1) * sb
        piece = vb_s[lo:hi, :].T
        src = piece if s == 0 else jnp.concatenate([vt_tail['v'], piece], axis=1)
        lo1 = lo if s == 0 else lo - LANES
        shifted = pltpu.roll(src, src.shape[1] - CHUNK, 1)
        vt_tail['v'] = piece[:, piece.shape[1] - LANES:]
        for sh, arr, b0 in ((0, piece, lo), (1, shifted, lo1)):
            w = arr.shape[1]
            zrows = jnp.zeros((HEAD_DIM, w), F32)
            for g in range(N_KV_HEADS):
                rows = arr[HEAD_DIM * g:HEAD_DIM * (g + 1), :]
                vt_s[sh, g, 0, :, b0:b0 + w] = jnp.concatenate([rows, zrows], axis=0).astype(BF16)
                vt_s[sh, g, 1, :, b0:b0 + w] = jnp.concatenate([zrows, rows], axis=0).astype(BF16)

    def regroup(s):
        def put(g, half, val):
            ug_ref[g, 0, s * nks:(s + 1) * nks, LANES * half:LANES * (half + 1)] = val
        _to_groups(u_s, put, nks, strided=True, row0=s * sb)

    nkeys = WINDOW + CHUNK
    key_row = lax.broadcasted_iota(jnp.int32, (nkeys, 1), 0)
    q_lane = lax.broadcasted_iota(jnp.int32, (1, LANES), 1)

    def scores(i):
        r0 = i * CHUNK
        valid = key_row >= (WINDOW - r0 - blk * tb) if i < N_WIN_CHUNKS else None
        out = []
        for g in range(N_KV_HEADS):
            qa = q_s[r0:r0 + CHUNK, 2 * LANES * g:2 * LANES * g + LANES]
            qb = q_s[r0:r0 + CHUNK, 2 * LANES * g + LANES:2 * LANES * (g + 1)]
            a_bf = jnp.concatenate([qa, qb], axis=0).astype(BF16)
            for par in range(2):
                s = lax.dot_general(k_s[2 * g + par, r0:r0 + nkeys, :], a_bf, (((1,), (1,)), ((), ())),
                                    preferred_element_type=F32)
                out.append(s if valid is None else jnp.where(valid, s, -jnp.inf))
        return out

    for item in project(0):
        item()
    build_vt(0)
    staging = []
    for s in range(1, n_sub):
        staging += [(s, item) for item in project(s)] + [(s, functools.partial(build_vt, s))]
    regroups = [functools.partial(regroup, s) for s in range(n_sub)]
    pending = scores(0)
    for i in range(n_chunks):
        if i + 1 < n_chunks and (i + 1) % cps == 0:
            while staging and staging[0][0] <= (i + 1) // cps:
                staging.pop(0)[1]()
        nxt = scores(i + 1) if i + 1 < n_chunks else None
        if staging:
            staging.pop(0)[1]()
        elif regroups:
            regroups.pop(0)()
        r0 = i * CHUNK
        sh = i % 2
        base = r0 - sh * CHUNK
        for g in range(N_KV_HEADS):
            acc = None
            for par in range(2):
                h0 = 4 * g + par
                sink_row = jnp.where(q_lane < CHUNK, sinks_ref[h0], sinks_ref[h0 + 2])
                o = _softmax_pv_t(pending[2 * g + par], vt_s[sh, g, par, :, base:base + nkeys], sink_row)
                acc = o if acc is None else acc + o
            at = acc.T
            a_ref[0, r0:r0 + CHUNK, 2 * LANES * g:2 * LANES * g + LANES] = at[:CHUNK]
            a_ref[0, r0:r0 + CHUNK, 2 * LANES * g + LANES:2 * LANES * (g + 1)] = at[CHUNK:]
        pending = nxt
    for item in regroups:
        item()


def _prompt_ssd_kernel(ug_ref, m_ref, wd_ref, r_ref, tab_ref, abar_ref, wup_f_ref, wdown_f_ref,
                       yg_ref, sfin_ref, wup_b_ref, wdown_b_ref, d_s, sp_s, carry_s):
    nb = ug_ref.shape[1]
    nc = ug_ref.shape[2]
    step = pl.program_id(0)
    wup_b_ref[...] = wup_f_ref[...].astype(BF16)
    wdown_b_ref[...] = wdown_f_ref[...].astype(BF16)

    @pl.when(step == 0)
    def _():
        carry_s[...] = jnp.zeros(carry_s.shape, F32)

    def state_in(g, c):
        ub = ug_ref[g].reshape(nb * nc, SSD_W)
        d_s[g] = jnp.dot(ub, wd_ref[g], preferred_element_type=F32)
        return c

    lax.fori_loop(0, N_GROUPS, state_in, 0, unroll=GROUP_UNROLL)

    hp = N_GROUPS // 2
    lo_half = lax.broadcasted_iota(jnp.int32, (1, 1, STATE_W), 2) < SSM_STATE

    def split(z):
        a, b = z[:hp], z[hp:]
        return (jnp.where(lo_half, a, pltpu.roll(b, SSM_STATE, 2)),
                jnp.where(lo_half, pltpu.roll(a, SSM_STATE, 2), b))

    def merge(re, im):
        return jnp.concatenate([jnp.where(lo_half, re, pltpu.roll(im, SSM_STATE, 2)),
                                jnp.where(lo_half, pltpu.roll(re, SSM_STATE, 2), im)], axis=0)

    def pair_tables(tre, tim):
        return jnp.where(lo_half, tre[:hp], tre[hp:]), jnp.where(lo_half, -tim[:hp], tim[hp:])

    pw = [pair_tables(tab_ref[2 * i], tab_ref[2 * i + 1]) for i in range(4)]
    ar, ai = pair_tables(abar_ref[0], abar_ref[1])
    kio = lax.broadcasted_iota(jnp.int32, (N_GROUPS, nc, STATE_W), 1)
    for b in range(nb):
        dg = d_s[:, b * nc:(b + 1) * nc, :]
        s_in = carry_s[b]
        xr_all, xi_all = split(jnp.where(kio == 0, s_in, pltpu.roll(dg, 1, 1)))
        cr = ci = jnp.zeros((hp, 1, STATE_W), F32)
        for j in range(nc // SUBLANES):
            xr = xr_all[:, SUBLANES * j:SUBLANES * (j + 1), :]
            xi = xi_all[:, SUBLANES * j:SUBLANES * (j + 1), :]
            for lvl in range(3):
                pr, pi = pw[lvl]
                sr, si = pltpu.roll(xr, 1 << lvl, 1), pltpu.roll(xi, 1 << lvl, 1)
                xr, xi = xr + pr * sr - pi * si, xi + pr * si + pi * sr
            qr, qi = pw[3]
            crb, cib = jnp.broadcast_to(cr, xr.shape), jnp.broadcast_to(ci, xi.shape)
            hr, hi = xr + qr * crb - qi * cib, xi + qr * cib + qi * crb
            sp_s[:, b * nc + SUBLANES * j:b * nc + SUBLANES * (j + 1), :] = merge(hr, hi)
            cr, ci = hr[:, SUBLANES - 1:SUBLANES, :], hi[:, SUBLANES - 1:SUBLANES, :]
        carry_s[b] = merge(ar * cr - ai * ci, ar * ci + ai * cr) + dg[:, nc - 1:nc, :]

    def readout(g, c):
        ub = ug_ref[g].reshape(nb * nc, SSD_W)
        y = jnp.dot(ub, m_ref[g], preferred_element_type=F32)
        y = y + lax.dot_general(sp_s[g].astype(BF16), r_ref[g], (((1,), (1,)), ((), ())),
                                preferred_element_type=F32)
        yg_ref[g] = y.reshape(nb, nc, SSD_W).astype(BF16)
        return c

    lax.fori_loop(0, N_GROUPS, readout, 0, unroll=GROUP_UNROLL)
    sfin_ref[...] = carry_s[...]


def _out_kernel(*refs, strided, cs, has_prev):
    if has_prev:
        x_ref, a_ref, yg_ref, prev_ref = refs[:4]
        refs = refs[4:]
    else:
        x_ref, a_ref, yg_ref = refs[:3]
        prev_ref = None
        refs = refs[3:]
    (wglu_ref, ga_ref, gs_ref, wout_ref, g2_ref, wup_ref, cw_ref, cb_ref, wdown_ref, gf_ref,
     y_ref, cst_ref, ys_s, ext_s, px_s, py_s) = refs
    tb = x_ref.shape[1]
    n_sub, _, sb, _ = ys_s.shape
    nks = sb // SSD_T
    planes = strided
    pr = sb // SUBLANES
    pad = SUBLANES if planes else ext_s.shape[0] - sb
    blk = pl.program_id(1)
    assert n_sub * sb == tb and (not planes or (cs == 1 and not has_prev))

    @pl.when(blk == 0)
    def _():
        if planes:
            ext_s[:, 0:pad, :] = jnp.zeros((2, pad, 2 * D_FF), F32)
        elif has_prev:
            ext_s[0:pad, :] = prev_ref[0]
        else:
            ext_s[0:pad, :] = jnp.zeros((pad, 2 * D_FF), F32)

    def mix(i):
        rows = slice(i * sb, (i + 1) * sb)
        _from_groups(lambda g, half: yg_ref[g, 0, i * nks:(i + 1) * nks, LANES * half:LANES * (half + 1)],
                     ys_s.at[i], nks, strided)
        yv = jnp.concatenate([ys_s[i, j] for j in range(D_SSM // LANES)], axis=-1)
        z = 0.5 * yv * (1.0 + jnp.tanh(math.sqrt(2.0 / math.pi) * (yv + 0.044715 * (yv * yv * yv))))
        gate = jnp.dot(z.astype(BF16), wglu_ref[...], preferred_element_type=F32)
        s_out = z * jax.nn.sigmoid(gate)
        na = _rms(a_ref[0, rows, :], ga_ref[...]).astype(BF16)
        ns = _rms(s_out, gs_ref[...]).astype(BF16)
        x1 = x_ref[0, rows, :] + jnp.dot(na, wout_ref[0:D_ATTN, :], preferred_element_type=F32)
        return x1 + jnp.dot(ns, wout_ref[D_ATTN:, :], preferred_element_type=F32)

    def tile_cols(j, half):
        return slice(half * D_FF + j * FF_TILE, half * D_FF + (j + 1) * FF_TILE)

    def to_planes(x):
        for c in range(D_MODEL // LANES):
            px_s[c] = x[:, LANES * c:LANES * (c + 1)]
        return jnp.concatenate(
            [jnp.concatenate([px_s[c, pl.ds(r, pr, stride=SUBLANES), :] for r in range(SUBLANES)], axis=0)
             for c in range(D_MODEL // LANES)], axis=1)

    def store_rows(i, y):
        rows = slice(i * sb, (i + 1) * sb)
        if not planes:
            y_ref[0, rows, :] = y
            return
        for c in range(D_MODEL // LANES):
            for r in range(SUBLANES):
                py_s[c, pl.ds(r, pr, stride=SUBLANES), :] = y[r * pr:(r + 1) * pr, LANES * c:LANES * (c + 1)]
            y_ref[0, rows, LANES * c:LANES * (c + 1)] = py_s[c]

    def time_shifts(up, cols):
        if not planes:
            ext_s[pad:pad + sb, cols] = up
            sh2 = ext_s[pad - 2 * cs:pad - 2 * cs + sb, cols]
            sh1 = ext_s[pad - cs:pad - cs + sb, cols]
            ext_s[0:pad, cols] = ext_s[sb:sb + pad, cols]
            return sh2, sh1
        for p in range(2):
            ext_s[p, pad:pad + pr, cols] = up[(6 + p) * pr:(7 + p) * pr, :]
        s6 = ext_s[0, pad - 1:pad - 1 + pr, cols]
        s7 = ext_s[1, pad - 1:pad - 1 + pr, cols]
        ext_s[:, 0:pad, cols] = ext_s[:, pr:pr + pad, cols]
        sh1 = jnp.concatenate([s7, up[:7 * pr, :]], axis=0)
        sh2 = jnp.concatenate([s6, s7, up[:6 * pr, :]], axis=0)
        return sh2, sh1

    def conv_ffn(i, x1):
        if planes:
            x1 = to_planes(x1)
        h2 = _rms(x1, g2_ref[...]).astype(BF16)

        def up_proj(j):
            return [jnp.dot(h2, wup_ref[:, tile_cols(j, half)], preferred_element_type=F32) for half in range(2)]

        def down_proj(act, j):
            return jnp.dot(act, wdown_ref[j * FF_TILE:(j + 1) * FF_TILE, :], preferred_element_type=F32)

        n_tiles = D_FF // FF_TILE
        acc = jnp.zeros((sb, D_MODEL), F32)
        ups = up_proj(0)
        act_prev = None
        for j in range(n_tiles):
            nxt = up_proj(j + 1) if j + 1 < n_tiles else None
            if act_prev is not None:
                acc = acc + down_proj(act_prev, j - 1)
            parts = []
            for half in range(2):
                cols = tile_cols(j, half)
                sh2, sh1 = time_shifts(ups[half], cols)
                parts.append(sh2 * cw_ref[0:1, cols] + sh1 * cw_ref[1:2, cols]
                             + ups[half] * cw_ref[2:3, cols] + cb_ref[:, cols])
            cg, cv = parts
            act_prev = (cg * jax.nn.sigmoid(cg) * cv).astype(BF16)
            ups = nxt
        acc = acc + down_proj(act_prev, n_tiles - 1)
        store_rows(i, _rms(x1 + acc, gf_ref[...]))

    mixed = mix(0)
    for i in range(n_sub):
        nxt_mixed = mix(i + 1) if i + 1 < n_sub else None
        conv_ffn(i, mixed)
        mixed = nxt_mixed
    if planes:
        cst_ref[0] = ext_s[:, 0:pad, :]
    else:
        cst_ref[0] = ext_s[0:pad, :]


def _decode_in_kernel(sinks_ref, x_ref, g1_ref, win_ref, cos_ref, sin_ref, kc_ref, vc_ref, h0_ref,
                      m_ref, wd_ref, r_ref, abar_ref,
                      a_ref, kn_ref, vn_ref, yg_ref, sn_ref, u_s, ug_s, *, n_streams, n_steps, n_cache):
    rows = n_streams * n_steps
    qs, k, v, u = _in_proj(x_ref[...], g1_ref, win_ref, cos_ref, sin_ref)
    kn_ref[...] = k
    vn_ref[...] = v
    for j in range(D_SSM // LANES):
        u_s[j] = u[:, LANES * j:LANES * (j + 1)]
    ncache_rows = n_streams * n_cache
    nkeys = ncache_rows + rows
    kall = jnp.concatenate([kc_ref[...], k], axis=0)
    vall = jnp.concatenate([vc_ref[...], v], axis=0)
    variants = _kv_variants(kall, vall)

    colv = lax.broadcasted_iota(jnp.int32, (1, nkeys), 1)
    is_new = colv >= ncache_rows
    cnew = colv - ncache_rows
    k_stream = jnp.where(is_new, _pmod(cnew, n_streams), _pdiv(colv, n_cache))
    k_pos = jnp.where(is_new, PAST_LEN + _pdiv(cnew, n_streams), PAST_LEN - n_cache + _pmod(colv, n_cache))
    k_chunk = _pdiv(k_pos, CHUNK)
    rowv = _pmod(lax.broadcasted_iota(jnp.int32, (2 * rows, 1), 0), rows)
    q_stream = _pmod(rowv, n_streams)
    q_chunk = _pdiv(PAST_LEN + _pdiv(rowv, n_streams), CHUNK)
    ok_col = jnp.where(k_pos >= 0, k_stream, -1)
    d_chunk = q_chunk - k_chunk
    valid = [q_stream == ok_col,
             lax.bitcast_convert_type(d_chunk, jnp.uint32) <= jnp.uint32(N_WIN_CHUNKS)]
    top = lax.broadcasted_iota(jnp.int32, (2 * rows, 1), 0) < rows

    for g in range(N_KV_HEADS):
        a_bf = jnp.concatenate([qs[2 * g], qs[2 * g + 1]], axis=0).astype(BF16)
        acc = jnp.zeros((2 * rows, LANES), F32)
        for par in range(2):
            h0 = 4 * g + par
            sink_col = jnp.where(top, sinks_ref[h0], sinks_ref[h0 + 2])
            acc = acc + _attend(a_bf, variants[2 * g + par], variants[4 + 2 * g + par], valid, sink_col)
        a_ref[:, 2 * LANES * g:2 * LANES * g + LANES] = acc[:rows]
        a_ref[:, 2 * LANES * g + LANES:2 * LANES * (g + 1)] = acc[rows:]

    def put(g, half, val):
        ug_s[g, :, LANES * half:LANES * (half + 1)] = val

    _to_groups(u_s, put, n_streams, strided=False)
    def group_body(g, c):
        ub = ug_s[g]
        h0g = h0_ref[g]
        d = jnp.dot(ub, wd_ref[g], preferred_element_type=F32)
        sn_ref[g] = _cmul(abar_ref[0, g], abar_ref[1, g], h0g) + d
        y = jnp.dot(ub, m_ref[g], preferred_element_type=F32)
        y = y + lax.dot_general(h0g.astype(BF16), r_ref[g], (((1,), (1,)), ((), ())),
                                preferred_element_type=F32)
        yg_ref[g] = y.astype(BF16)
        return c

    lax.fori_loop(0, N_GROUPS, group_body, 0, unroll=DECODE_UNROLL)


def _ssd_tables(a_re, a_im, log_dt, b_re, b_im, c_re, c_im, d_skip):
    dup = lambda x: jnp.concatenate([x, x], axis=-1)
    prm = jnp.stack([dup(a_re), dup(a_im), jnp.broadcast_to(log_dt[:, None], (N_GROUPS, STATE_W))], axis=1)
    bt = jnp.concatenate([b_re.transpose(0, 2, 1), b_im.transpose(0, 2, 1)], axis=-1)
    cc = jnp.concatenate([c_re, c_im], axis=-1)
    dtile = jnp.tile(d_skip, (1, SSD_T))[:, None, :]
    gb = TABLE_GROUPS
    blk3 = lambda r, c: pl.BlockSpec((gb, r, c), lambda i: (i, 0, 0))
    return pl.pallas_call(
        _ssd_tables_kernel,
        grid=(N_GROUPS // gb,),
        in_specs=[blk3(3, STATE_W), blk3(SSM_GROUP, STATE_W), blk3(SSM_GROUP, STATE_W), blk3(1, SSD_W)],
        out_specs=[blk3(SSD_W, SSD_W), blk3(SSD_W, STATE_W), blk3(SSD_W, STATE_W),
                   pl.BlockSpec((8, gb, SUBLANES, STATE_W), lambda i: (0, i, 0, 0)),
                   pl.BlockSpec((2, gb, 1, STATE_W), lambda i: (0, i, 0, 0))],
        out_shape=[jax.ShapeDtypeStruct((N_GROUPS, SSD_W, SSD_W), BF16),
                   jax.ShapeDtypeStruct((N_GROUPS, SSD_W, STATE_W), BF16),
                   jax.ShapeDtypeStruct((N_GROUPS, SSD_W, STATE_W), BF16),
                   jax.ShapeDtypeStruct((8, N_GROUPS, SUBLANES, STATE_W), F32),
                   jax.ShapeDtypeStruct((2, N_GROUPS, 1, STATE_W), F32)],
        compiler_params=pltpu.CompilerParams(dimension_semantics=("arbitrary",)),
        name="ssd_tables",
    )(prm, bt, cc, dtile)


def _ssd_tables_kernel(prm_ref, bt_ref, cc_ref, dt_ref, m_ref, wd_ref, rt_ref, tab_ref, abar_ref):
    t = SSD_T
    lane = lax.broadcasted_iota(jnp.int32, (1, STATE_W), 1)
    sgn = jnp.where(lane >= SSM_STATE, 1.0, -1.0).astype(F32)
    conj = -sgn
    n_small = lax.broadcasted_iota(jnp.int32, (3 * SUBLANES, 1), 0).astype(F32)
    sub_i = lax.broadcasted_iota(jnp.int32, (SUBLANES, 1), 0)
    n_big = float(t) * (sub_i + 1).astype(F32)
    row = lax.broadcasted_iota(jnp.int32, (SSD_W, SSD_W), 0)
    col = lax.broadcasted_iota(jnp.int32, (SSD_W, SSD_W), 1)
    diag = row == col
    kcol = lax.broadcasted_iota(jnp.int32, (SSM_GROUP, SSD_W), 1)
    for gi in range(prm_ref.shape[0]):
        a_re, a_im = prm_ref[gi, 0:1, :], prm_ref[gi, 1:2, :]
        dt = jnp.exp(prm_ref[gi, 2:3, :])
        lr, li = dt * a_re, dt * a_im

        def power(n):
            mag = jnp.exp(n * lr)
            return mag * jnp.cos(n * li), mag * jnp.sin(n * li) * sgn

        pr_s, pis_s = power(n_small)
        pr_b, pis_b = power(n_big)

        def times_powers(z, zs, exps):
            return jnp.concatenate([z * pr_s[e:e + 1, :] + zs * pis_s[e:e + 1, :] for e in exps], axis=0)

        ar, ais = pr_s[1:2, :], pis_s[1:2, :]
        den = a_re * a_re + a_im * a_im
        nr, ni = ar - 1.0, ais * sgn
        fr, fis = (nr * a_re + ni * a_im) / den, (ni * a_re - nr * a_im) / den * sgn
        bt = bt_ref[gi]
        bb = bt * fr + pltpu.roll(bt, SSM_STATE, 1) * fis
        bbs = pltpu.roll(bb, SSM_STATE, 1)
        cc = cc_ref[gi]
        ccs = pltpu.roll(cc, SSM_STATE, 1)
        steps = range(t)
        y = times_powers(cc, ccs, list(steps))
        kt = lax.dot_general(bb * conj, y, (((1,), (1,)), ((), ())), precision=lax.Precision.HIGHEST,
                             preferred_element_type=F32)
        m = jnp.concatenate(
            [kt if s == 0 else jnp.where(kcol >= SSM_GROUP * s, pltpu.roll(kt, SSM_GROUP * s, 1), 0.0)
             for s in steps], axis=0)
        m_ref[gi] = (m + jnp.where(diag, dt_ref[gi], 0.0)).astype(BF16)
        wd_ref[gi] = times_powers(bb, bbs, [t - 1 - s for s in steps]).astype(BF16)
        rt_ref[gi] = (times_powers(cc, ccs, [s + 1 for s in steps]) * conj).astype(BF16)
        for lvl in range(3):
            d = 1 << lvl
            tab_ref[2 * lvl, gi] = jnp.where(sub_i >= d, pr_b[d - 1:d, :], 0.0)
            tab_ref[2 * lvl + 1, gi] = jnp.where(sub_i >= d, pis_b[d - 1:d, :], 0.0)
        tab_ref[6, gi] = pr_b
        tab_ref[7, gi] = pis_b
        abar_ref[0, gi] = pr_b[0:1, :]
        abar_ref[1, gi] = pis_b[0:1, :]


def _rope_freq():
    half = HEAD_DIM // 2
    return ROPE_THETA ** (-jnp.asarray(np.arange(LANES) % half, F32) / half)


def _rope_tables(pos):
    sign = jnp.asarray(np.where(np.arange(LANES) % HEAD_DIM < HEAD_DIM // 2, -1.0, 1.0), F32)
    ang = pos.astype(F32)[:, None] * _rope_freq()[None, :]
    return jnp.cos(ang), jnp.sin(ang) * sign[None, :]


def _rope_base(rows):
    freq = _rope_freq()[None, :]
    ang = jnp.arange(rows, dtype=F32)[:, None] * freq
    return jnp.cos(ang), jnp.sin(ang), freq


def _const(shape):
    nd = len(shape)
    return pl.BlockSpec(shape, lambda *_: (0,) * nd, pipeline_mode=pl.Buffered(1))


def _whole(shape):
    nd = len(shape)
    return pl.BlockSpec(shape, lambda *_: (0,) * nd)


def _out_call(x, a, yg, prev, w, *, tb, sb, strided, cs, pad):
    nb, length, _ = x.shape
    nk = tb // SSD_T
    grid = (nb, length // tb)
    in_specs = [
        pl.BlockSpec((1, tb, D_MODEL), lambda b, i: (b, i, 0)),
        pl.BlockSpec((1, tb, D_ATTN), lambda b, i: (b, i, 0)),
        pl.BlockSpec((N_GROUPS, 1, nk, SSD_W), lambda b, i: (0, b, i, 0)),
    ]
    args = [x, a, yg]
    if prev is not None:
        in_specs.append(pl.BlockSpec((1, pad, 2 * D_FF), lambda b, i: (b, 0, 0)))
        args.append(prev)
    weights = [w['w_glu'], w['onorm_a'], w['onorm_s'], w['w_out'], w['norm2'], w['w_up'], w['conv_w'],
               w['conv_b'], w['w_down'], w['final_g']]
    in_specs += [_const(t.shape) for t in weights]
    kern = functools.partial(_out_kernel, strided=strided, cs=cs, has_prev=prev is not None)
    if strided:
        ext_shape = (2, pad + sb // SUBLANES, 2 * D_FF)
        cst_block, cst_map = (1, 2, pad, 2 * D_FF), (lambda b, i: (b, 0, 0, 0))
    else:
        ext_shape = (pad + sb, 2 * D_FF)
        cst_block, cst_map = (1, pad, 2 * D_FF), (lambda b, i: (b, 0, 0))
    return pl.pallas_call(
        kern,
        grid=grid,
        in_specs=in_specs,
        out_specs=[pl.BlockSpec((1, tb, D_MODEL), lambda b, i: (b, i, 0)),
                   pl.BlockSpec(cst_block, cst_map)],
        out_shape=[jax.ShapeDtypeStruct((nb, length, D_MODEL), F32),
                   jax.ShapeDtypeStruct((nb,) + cst_block[1:], F32)],
        scratch_shapes=[pltpu.VMEM((tb // sb, D_SSM // LANES, sb, LANES), F32),
                        pltpu.VMEM(ext_shape, F32),
                        pltpu.VMEM((D_MODEL // LANES, sb, LANES), F32),
                        pltpu.VMEM((D_MODEL // LANES, sb, LANES), F32)],
        compiler_params=pltpu.CompilerParams(dimension_semantics=("arbitrary", "arbitrary"),
                                             vmem_limit_bytes=VMEM_LIMIT),
        name="layer_out",
    )(*args, *weights)


def kernel(x_prompt, x_sample, cache_k, cache_v, state_ssm_re, state_ssm_im, state_conv, norm1_g, w_in, attn_sinks, ssm_A_re, ssm_A_im, ssm_log_dt, ssm_B_re, ssm_B_im, ssm_C_re, ssm_C_im, ssm_D, w_glu, onorm_attn_g, onorm_ssm_g, w_out, norm2_g, w_up, conv_w, conv_b, w_down, final_g):
    assert norm1_g.shape[0] == 1, "one layer"
    nb, length, _ = x_prompt.shape
    ns, nt, _ = x_sample.shape
    n_cache = cache_k.shape[2]
    assert nt == SSD_T and length % TB_IN == 0 and length % TB_OUT == 0 and (length // SSD_T) % NC_SSD == 0

    m_mat, wd_mat, r_mat, tab, abar = _ssd_tables(ssm_A_re[0], ssm_A_im[0], ssm_log_dt[0], ssm_B_re[0],
                                                  ssm_B_im[0], ssm_C_re[0], ssm_C_im[0], ssm_D[0])
    sinks = attn_sinks[0]
    g1 = norm1_g
    win = w_in[0].astype(BF16)
    w = dict(w_glu=w_glu[0].astype(BF16), onorm_a=onorm_attn_g, onorm_s=onorm_ssm_g, w_out=w_out[0].astype(BF16),
             norm2=norm2_g, conv_w=conv_w[0], conv_b=conv_b, final_g=final_g[None, :])
    smem = pl.BlockSpec(memory_space=pltpu.SMEM)

    cos_p, sin_p, freq = _rope_base(TB_IN)
    nk = TB_IN // SSD_T
    a_p, ug_p, k_p, v_p = pl.pallas_call(
        functools.partial(_prompt_in_kernel, sb=SUB_IN),
        grid=(nb, length // TB_IN),
        in_specs=[smem,
                  pl.BlockSpec((1, TB_IN, D_MODEL), lambda b, i: (b, i, 0)),
                  _const((1, D_MODEL)), _const((D_MODEL, D_IN)),
                  _const((TB_IN, LANES)), _const((TB_IN, LANES)), _const((1, LANES))],
        out_specs=[pl.BlockSpec((1, TB_IN, D_ATTN), lambda b, i: (b, i, 0)),
                   pl.BlockSpec((N_GROUPS, 1, nk, SSD_W), lambda b, i: (0, b, i, 0)),
                   pl.BlockSpec((1, WINDOW, KV_W), lambda b, i: (b, 0, 0)),
                   pl.BlockSpec((1, WINDOW, KV_W), lambda b, i: (b, 0, 0))],
        out_shape=[jax.ShapeDtypeStruct((nb, length, D_ATTN), F32),
                   jax.ShapeDtypeStruct((N_GROUPS, nb, length // SSD_T, SSD_W), BF16),
                   jax.ShapeDtypeStruct((nb, WINDOW, KV_W), F32),
                   jax.ShapeDtypeStruct((nb, WINDOW, KV_W), F32)],
        scratch_shapes=[pltpu.VMEM((TB_IN, D_ATTN), F32),
                        pltpu.VMEM((2 * N_KV_HEADS, WINDOW + TB_IN, LANES), BF16),
                        pltpu.VMEM((WINDOW + TB_IN, LANES), F32),
                        pltpu.VMEM((2, N_KV_HEADS, 2, LANES, WINDOW + TB_IN), BF16),
                        pltpu.VMEM((D_SSM // LANES, TB_IN, LANES), F32)],
        compiler_params=pltpu.CompilerParams(dimension_semantics=("arbitrary", "arbitrary"),
                                             vmem_limit_bytes=VMEM_LIMIT),
        name="prompt_in",
    )(sinks, x_prompt, g1, win, cos_p, sin_p, freq)

    n_ssd = length // SSD_T // NC_SSD
    up_rows, down_rows = D_MODEL // n_ssd, D_FF // n_ssd
    yg_p, sfin_p, w['w_up'], w['w_down'] = pl.pallas_call(
        _prompt_ssd_kernel,
        grid=(n_ssd,),
        in_specs=[pl.BlockSpec((N_GROUPS, nb, NC_SSD, SSD_W), lambda i: (0, 0, i, 0)),
                  _const(m_mat.shape), _const(wd_mat.shape), _const(r_mat.shape), _const(tab.shape),
                  _const(abar.shape),
                  pl.BlockSpec((up_rows, 2 * D_FF), lambda i: (i, 0)),
                  pl.BlockSpec((down_rows, D_MODEL), lambda i: (i, 0))],
        out_specs=[pl.BlockSpec((N_GROUPS, nb, NC_SSD, SSD_W), lambda i: (0, 0, i, 0)),
                   pl.BlockSpec((nb, N_GROUPS, 1, STATE_W), lambda i: (0, 0, 0, 0)),
                   pl.BlockSpec((up_rows, 2 * D_FF), lambda i: (i, 0)),
                   pl.BlockSpec((down_rows, D_MODEL), lambda i: (i, 0))],
        out_shape=[jax.ShapeDtypeStruct((N_GROUPS, nb, length // SSD_T, SSD_W), BF16),
                   jax.ShapeDtypeStruct((nb, N_GROUPS, 1, STATE_W), F32),
                   jax.ShapeDtypeStruct((D_MODEL, 2 * D_FF), BF16),
                   jax.ShapeDtypeStruct((D_FF, D_MODEL), BF16)],
        scratch_shapes=[pltpu.VMEM((N_GROUPS, nb * NC_SSD, STATE_W), F32),
                        pltpu.VMEM((N_GROUPS, nb * NC_SSD, STATE_W), F32),
                        pltpu.VMEM((nb, N_GROUPS, 1, STATE_W), F32)],
        compiler_params=pltpu.CompilerParams(dimension_semantics=("arbitrary",), vmem_limit_bytes=VMEM_LIMIT),
        name="prompt_ssd",
    )(ug_p, m_mat, wd_mat, r_mat, tab, abar, w_up[0], w_down[0])

    y_p, cst_p = _out_call(x_prompt, a_p, yg_p, None, w, tb=TB_OUT, sb=SUB_OUT, strided=True, cs=1, pad=SUBLANES)

    rows = ns * nt
    xs = x_sample.transpose(1, 0, 2).reshape(rows, D_MODEL)
    cos_s, sin_s = _rope_tables(PAST_LEN + jnp.arange(rows) // ns)
    kc = cache_k[0].reshape(ns * n_cache, KV_W)
    vc = cache_v[0].reshape(ns * n_cache, KV_W)
    h0 = jnp.concatenate([state_ssm_re[0], state_ssm_im[0]], axis=-1).transpose(1, 0, 2)
    dec = functools.partial(_decode_in_kernel, n_streams=ns, n_steps=nt, n_cache=n_cache)
    dec_in = [xs, g1, win, cos_s, sin_s, kc, vc, h0, m_mat, wd_mat, r_mat, abar]
    a_s, kn, vn, yg_s, sn = pl.pallas_call(
        dec,
        grid=(1,),
        in_specs=[smem] + [_const(t.shape) for t in dec_in],
        out_specs=[_whole((rows, D_ATTN)), _whole((rows, KV_W)), _whole((rows, KV_W)),
                   _whole((N_GROUPS, ns, SSD_W)), _whole((N_GROUPS, ns, STATE_W))],
        out_shape=[jax.ShapeDtypeStruct((rows, D_ATTN), F32), jax.ShapeDtypeStruct((rows, KV_W), F32),
                   jax.ShapeDtypeStruct((rows, KV_W), F32), jax.ShapeDtypeStruct((N_GROUPS, ns, SSD_W), BF16),
                   jax.ShapeDtypeStruct((N_GROUPS, ns, STATE_W), F32)],
        scratch_shapes=[pltpu.VMEM((D_SSM // LANES, rows, LANES), F32), pltpu.VMEM((N_GROUPS, ns, SSD_W), BF16)],
        compiler_params=pltpu.CompilerParams(dimension_semantics=("arbitrary",), vmem_limit_bytes=VMEM_LIMIT),
        name="decode_in",
    )(sinks, *dec_in)

    cs = ns
    prev = state_conv[0].transpose(1, 0, 2).reshape(1, (CONV_W - 1) * cs, 2 * D_FF)
    y_s, cst_s = _out_call(xs[None], a_s[None], yg_s[:, None], prev, w, tb=rows, sb=rows, strided=False, cs=cs,
                           pad=(CONV_W - 1) * cs)

    kv5 = lambda t, n: t.reshape(1, n, -1, N_KV_HEADS, HEAD_DIM)
    y_sample = y_s.reshape(nt, ns, D_MODEL).transpose(1, 0, 2)
    kn_b = kn.reshape(nt, ns, KV_W).transpose(1, 0, 2)
    vn_b = vn.reshape(nt, ns, KV_W).transpose(1, 0, 2)
    k_new = jnp.concatenate([cache_k[0].reshape(ns, n_cache, KV_W), kn_b], axis=1)[:, -n_cache:]
    v_new = jnp.concatenate([cache_v[0].reshape(ns, n_cache, KV_W), vn_b], axis=1)[:, -n_cache:]
    conv_p = cst_p[:, :, SUBLANES - 1, :][None]
    conv_s = cst_s.reshape(CONV_W - 1, ns, 2 * D_FF).transpose(1, 0, 2)[None]
    sn_b = sn.transpose(1, 0, 2)
    return (y_p, y_sample,
            kv5(k_p, nb), kv5(v_p, nb),
            sfin_p[None, :, :, 0, :SSM_STATE], sfin_p[None, :, :, 0, SSM_STATE:], conv_p,
            kv5(k_new, ns), kv5(v_new, ns),
            sn_b[None, :, :, :SSM_STATE], sn_b[None, :, :, SSM_STATE:], conv_s)
```
